```python
import math
import jax, jax.numpy as jnp
from jax import lax
import numpy as np

D_MODEL = 1024
BATCH = 8
SEQ = 4096
DEPTH = 4

N_MIXERS = 2
N_SSD = (DEPTH + 1) // 2
N_FOX = DEPTH // 2
EPS = 1e-6

SSD_EXPAND = 2
SSD_D_INNER = SSD_EXPAND * D_MODEL
SSD_HEAD_DIM = 64
SSD_HEADS = SSD_D_INNER // SSD_HEAD_DIM
SSD_GROUPS = 4
SSD_HPG = SSD_HEADS // SSD_GROUPS
SSD_STATE = 128
SSD_CONV = 4
SSD_CHUNK = 128
SSD_CONV_DIM = SSD_D_INNER + 2 * SSD_GROUPS * SSD_STATE
SSD_IN_DIM = SSD_D_INNER + SSD_CONV_DIM + SSD_HEADS

FOX_HEAD_DIM = 64
FOX_HEADS = D_MODEL // FOX_HEAD_DIM
FOX_D = FOX_HEADS * FOX_HEAD_DIM
FOX_IN_DIM = 4 * FOX_D + FOX_HEADS
Q_BLOCK = 128

D_FF = 2816
FFN_CONV = 3

kernel_name = 'hybrid_ssd_fox_convglu_trunk'


def rmsnorm(x, g):
    xf = x.astype(jnp.float32)
    y = xf * lax.rsqrt(jnp.mean(xf * xf, axis=-1, keepdims=True) + EPS)
    return (y * g.astype(jnp.float32)).astype(x.dtype)


def causal_dwconv(x, w, b):
    k_w, c = w.shape
    y = lax.conv_general_dilated(x, w[:, None, :].astype(x.dtype), window_strides=(1,),
                                 padding=[(k_w - 1, 0)],
                                 dimension_numbers=('NWC', 'WIO', 'NWC'),
                                 feature_group_count=c)
    return y + b.astype(x.dtype)


def ssd_chunked(xs, dt, a, bm, cm):
    bsz, s_len = xs.shape[0], xs.shape[1]
    nc, l_c = s_len // SSD_CHUNK, SSD_CHUNK
    x = xs.reshape(bsz, nc, l_c, SSD_GROUPS, SSD_HPG, SSD_HEAD_DIM)
    dtc = dt.reshape(bsz, nc, l_c, SSD_GROUPS, SSD_HPG)
    bc = bm.reshape(bsz, nc, l_c, SSD_GROUPS, SSD_STATE)
    cc = cm.reshape(bsz, nc, l_c, SSD_GROUPS, SSD_STATE)
    da = jnp.moveaxis(dtc * a.reshape(SSD_GROUPS, SSD_HPG), 2, -1)
    a_cs = jnp.cumsum(da, axis=-1)
    xdt = x * dtc[..., None]
    causal = jnp.tril(jnp.ones((l_c, l_c), dtype=bool))
    decay = jnp.exp(jnp.where(causal, a_cs[..., :, None] - a_cs[..., None, :], -jnp.inf))
    cb = jnp.einsum('bclgn,bcsgn->bcgls', cc, bc)
    y_diag = jnp.einsum('bcgrls,bcsgrp->bclgrp', cb[:, :, :, None] * decay, xdt)
    decay_states = jnp.exp(a_cs[..., -1:] - a_cs)
    states = jnp.einsum('bclgn,bcgrl,bclgrp->bcgrpn', bc, decay_states, xdt).astype(jnp.float32)
    chunk_decay = jnp.exp(a_cs[..., -1])

    def step(h, inp):
        st, dec = inp
        return h * dec[..., None, None] + st, h

    h0 = jnp.zeros((bsz, SSD_GROUPS, SSD_HPG, SSD_HEAD_DIM, SSD_STATE), jnp.float32)
    _, prev = lax.scan(step, h0, (jnp.moveaxis(states, 1, 0), jnp.moveaxis(chunk_decay, 1, 0)))
    prev = jnp.moveaxis(prev, 0, 1)
    y_off = jnp.einsum('bclgn,bcgrpn,bcgrl->bclgrp', cc, prev, jnp.exp(a_cs))
    return (y_diag + y_off).reshape(bsz, s_len, SSD_HEADS, SSD_HEAD_DIM).astype(xs.dtype)


def mamba2_mixer(h, w_in, conv_w, conv_b, dt_bias, a_log, d_skip, norm_g, w_out):
    bsz, s_len, _ = h.shape
    proj = h @ w_in
    z = proj[..., :SSD_D_INNER]
    xbc = proj[..., SSD_D_INNER:SSD_D_INNER + SSD_CONV_DIM]
    dt_raw = proj[..., SSD_D_INNER + SSD_CONV_DIM:]
    xbc = jax.nn.silu(causal_dwconv(xbc, conv_w, conv_b))
    gn = SSD_GROUPS * SSD_STATE
    xs = xbc[..., :SSD_D_INNER].reshape(bsz, s_len, SSD_HEADS, SSD_HEAD_DIM)
    bm = xbc[..., SSD_D_INNER:SSD_D_INNER + gn].reshape(bsz, s_len, SSD_GROUPS, SSD_STATE)
    cm = xbc[..., SSD_D_INNER + gn:].reshape(bsz, s_len, SSD_GROUPS, SSD_STATE)
    dt = jax.nn.softplus((dt_raw + dt_bias).astype(jnp.float32))
    a = -jnp.exp(a_log.astype(jnp.float32))
    y = ssd_chunked(xs, dt, a, bm, cm) + xs * d_skip[:, None]
    y = y.reshape(bsz, s_len, SSD_D_INNER)
    yz = (y * jax.nn.silu(z)).astype(jnp.float32).reshape(bsz, s_len, SSD_GROUPS, SSD_D_INNER // SSD_GROUPS)
    yz = yz * lax.rsqrt(jnp.mean(yz * yz, axis=-1, keepdims=True) + EPS)
    y = (yz.reshape(bsz, s_len, SSD_D_INNER) * norm_g.astype(jnp.float32)).astype(h.dtype)
    return y @ w_out


def fox_attention(h, w_in, b_f, q_norm_g, k_norm_g, w_out):
    bsz, s_len, _ = h.shape
    proj = h @ w_in
    q = rmsnorm(proj[..., :FOX_D].reshape(bsz, s_len, FOX_HEADS, FOX_HEAD_DIM), q_norm_g)
    k = rmsnorm(proj[..., FOX_D:2 * FOX_D].reshape(bsz, s_len, FOX_HEADS, FOX_HEAD_DIM), k_norm_g)
    v = proj[..., 2 * FOX_D:3 * FOX_D].reshape(bsz, s_len, FOX_HEADS, FOX_HEAD_DIM)
    gate = proj[..., 3 * FOX_D:4 * FOX_D]
    log_f = jax.nn.log_sigmoid((proj[..., 4 * FOX_D:] + b_f).astype(jnp.float32))
    cum = jnp.transpose(jnp.cumsum(log_f, axis=1), (0, 2, 1))
    q, k, v = (jnp.transpose(t, (0, 2, 1, 3)) for t in (q, k, v))
    scale = FOX_HEAD_DIM ** -0.5
    outs = []
    for i in range(s_len // Q_BLOCK):
        qs, qe = i * Q_BLOCK, (i + 1) * Q_BLOCK
        sc = jnp.einsum('bhqd,bhkd->bhqk', q[:, :, qs:qe], k[:, :, :qe]).astype(jnp.float32) * scale
        sc = sc + cum[:, :, qs:qe, None] - cum[:, :, None, :qe]
        mask = (qs + jnp.arange(Q_BLOCK))[:, None] >= jnp.arange(qe)[None, :]
        p = jax.nn.softmax(jnp.where(mask, sc, -jnp.inf), axis=-1)
        outs.append(jnp.einsum('bhqk,bhkd->bhqd', p.astype(v.dtype), v[:, :, :qe]))
    o = jnp.transpose(jnp.concatenate(outs, axis=2), (0, 2, 1, 3)).reshape(bsz, s_len, FOX_D)
    return (o * jax.nn.sigmoid(gate)) @ w_out


def conv_glu_ffn(h, w_up, conv_w, conv_b, w_down):
    u = h @ w_up
    gate = causal_dwconv(u[..., :D_FF], conv_w, conv_b)
    return (jax.nn.silu(gate) * u[..., D_FF:]) @ w_down


def _fwd_setup_inputs(seed: int = 0) -> dict:
    key = jax.random.key(seed)
    ks = jax.random.split(key, 24)
    f32 = jnp.float32

    def nrm(k, shape, scale):
        return jax.random.normal(k, shape, f32) * scale

    res_scale = (2 * DEPTH) ** -0.5
    dt_init = jnp.exp(jax.random.uniform(ks[4], (N_SSD, SSD_HEADS), f32, math.log(1e-3), math.log(1e-1)))
    return {
        'x': nrm(ks[0], (BATCH, SEQ, D_MODEL), 1.0),
        'mix_norm_g': 1.0 + nrm(ks[1], (DEPTH, D_MODEL), 0.02),
        'ffn_norm_g': 1.0 + nrm(ks[2], (DEPTH, D_MODEL), 0.02),
        'ssd_w_in': nrm(ks[3], (N_SSD, D_MODEL, SSD_IN_DIM), D_MODEL ** -0.5),
        'ssd_conv_w': nrm(ks[5], (N_SSD, SSD_CONV, SSD_CONV_DIM), SSD_CONV ** -0.5),
        'ssd_conv_b': nrm(ks[6], (N_SSD, SSD_CONV_DIM), 0.02),
        'ssd_dt_bias': dt_init + jnp.log(-jnp.expm1(-dt_init)),
        'ssd_a_log': jnp.log(jax.random.uniform(ks[7], (N_SSD, SSD_HEADS), f32, 1.0, 16.0)),
        'ssd_d': 1.0 + nrm(ks[8], (N_SSD, SSD_HEADS), 0.1),
        'ssd_norm_g': 1.0 + nrm(ks[9], (N_SSD, SSD_D_INNER), 0.02),
        'ssd_w_out': nrm(ks[10], (N_SSD, SSD_D_INNER, D_MODEL), SSD_D_INNER ** -0.5 * res_scale),
        'fox_w_in': nrm(ks[11], (N_FOX, D_MODEL, FOX_IN_DIM), D_MODEL ** -0.5),
        'fox_b_f': jax.random.uniform(ks[12], (N_FOX, FOX_HEADS), f32, 2.0, 6.0),
        'fox_q_norm_g': 1.0 + nrm(ks[13], (N_FOX, FOX_HEAD_DIM), 0.02),
        'fox_k_norm_g': 1.0 + nrm(ks[14], (N_FOX, FOX_HEAD_DIM), 0.02),
        'fox_w_out': nrm(ks[15], (N_FOX, FOX_D, D_MODEL), FOX_D ** -0.5 * res_scale),
        'ffn_w_up': nrm(ks[16], (DEPTH, D_MODEL, 2 * D_FF), D_MODEL ** -0.5),
        'ffn_conv_w': nrm(ks[17], (DEPTH, FFN_CONV, D_FF), FFN_CONV ** -0.5),
        'ffn_conv_b': nrm(ks[18], (DEPTH, D_FF), 0.02),
        'ffn_w_down': nrm(ks[19], (DEPTH, D_FF, D_MODEL), D_FF ** -0.5 * res_scale),
        'final_norm_g': 1.0 + nrm(ks[20], (D_MODEL,), 0.02),
    }


def _fwd_reference(x, mix_norm_g, ffn_norm_g,
              ssd_w_in, ssd_conv_w, ssd_conv_b, ssd_dt_bias, ssd_a_log, ssd_d, ssd_norm_g, ssd_w_out,
              fox_w_in, fox_b_f, fox_q_norm_g, fox_k_norm_g, fox_w_out,
              ffn_w_up, ffn_conv_w, ffn_conv_b, ffn_w_down, final_norm_g):
    h = x
    for i in range(DEPTH):
        hn = rmsnorm(h, mix_norm_g[i])
        j = i // N_MIXERS
        if i % N_MIXERS == 0:
            h = h + mamba2_mixer(hn, ssd_w_in[j], ssd_conv_w[j], ssd_conv_b[j], ssd_dt_bias[j],
                                 ssd_a_log[j], ssd_d[j], ssd_norm_g[j], ssd_w_out[j])
        else:
            h = h + fox_attention(hn, fox_w_in[j], fox_b_f[j], fox_q_norm_g[j], fox_k_norm_g[j], fox_w_out[j])
        h = h + conv_glu_ffn(rmsnorm(h, ffn_norm_g[i]), ffn_w_up[i], ffn_conv_w[i], ffn_conv_b[i], ffn_w_down[i])
    return rmsnorm(h, final_norm_g)


import jax as _jax
import jax.numpy as _jnp

TWIN_FORMAT = 'train_step'
FWD_PARAMS = ['x', 'mix_norm_g', 'ffn_norm_g', 'ssd_w_in', 'ssd_conv_w', 'ssd_conv_b', 'ssd_dt_bias', 'ssd_a_log', 'ssd_d', 'ssd_norm_g', 'ssd_w_out', 'fox_w_in', 'fox_b_f', 'fox_q_norm_g', 'fox_k_norm_g', 'fox_w_out', 'ffn_w_up', 'ffn_conv_w', 'ffn_conv_b', 'ffn_w_down', 'final_norm_g']
TWIN_WEIGHTS = ['mix_norm_g', 'ffn_norm_g', 'ssd_w_in', 'ssd_conv_w', 'ssd_conv_b', 'ssd_dt_bias', 'ssd_a_log', 'ssd_d', 'ssd_norm_g', 'ssd_w_out', 'fox_w_in', 'fox_b_f', 'fox_q_norm_g', 'fox_k_norm_g', 'fox_w_out', 'ffn_w_up', 'ffn_conv_w', 'ffn_conv_b', 'ffn_w_down', 'final_norm_g']
TWIN_DIFF_INPUT = 'x'
TWIN_INPUTS = ['x', 'mix_norm_g', 'ffn_norm_g', 'ssd_w_in', 'ssd_conv_w', 'ssd_conv_b', 'ssd_dt_bias', 'ssd_a_log', 'ssd_d', 'ssd_norm_g', 'ssd_w_out', 'fox_w_in', 'fox_b_f', 'fox_q_norm_g', 'fox_k_norm_g', 'fox_w_out', 'ffn_w_up', 'ffn_conv_w', 'ffn_conv_b', 'ffn_w_down', 'final_norm_g', 'loss_target', 'm_mix_norm_g', 'm_ffn_norm_g', 'm_ssd_w_in', 'm_ssd_conv_w', 'm_ssd_conv_b', 'm_ssd_dt_bias', 'm_ssd_a_log', 'm_ssd_d', 'm_ssd_norm_g', 'm_ssd_w_out', 'm_fox_w_in', 'm_fox_b_f', 'm_fox_q_norm_g', 'm_fox_k_norm_g', 'm_fox_w_out', 'm_ffn_w_up', 'm_ffn_conv_w', 'm_ffn_conv_b', 'm_ffn_w_down', 'm_final_norm_g', 'v_mix_norm_g', 'v_ffn_norm_g', 'v_ssd_w_in', 'v_ssd_conv_w', 'v_ssd_conv_b', 'v_ssd_dt_bias', 'v_ssd_a_log', 'v_ssd_d', 'v_ssd_norm_g', 'v_ssd_w_out', 'v_fox_w_in', 'v_fox_b_f', 'v_fox_q_norm_g', 'v_fox_k_norm_g', 'v_fox_w_out', 'v_ffn_w_up', 'v_ffn_conv_w', 'v_ffn_conv_b', 'v_ffn_w_down', 'v_final_norm_g']
TWIN_OUTPUTS = ['loss', 'grad_x', 'grad_mix_norm_g', 'grad_ffn_norm_g', 'grad_ssd_w_in', 'grad_ssd_conv_w', 'grad_ssd_conv_b', 'grad_ssd_dt_bias', 'grad_ssd_a_log', 'grad_ssd_d', 'grad_ssd_norm_g', 'grad_ssd_w_out', 'grad_fox_w_in', 'grad_fox_b_f', 'grad_fox_q_norm_g', 'grad_fox_k_norm_g', 'grad_fox_w_out', 'grad_ffn_w_up', 'grad_ffn_conv_w', 'grad_ffn_conv_b', 'grad_ffn_w_down', 'grad_final_norm_g', 'delta_mix_norm_g', 'delta_ffn_norm_g', 'delta_ssd_w_in', 'delta_ssd_conv_w', 'delta_ssd_conv_b', 'delta_ssd_dt_bias', 'delta_ssd_a_log', 'delta_ssd_d', 'delta_ssd_norm_g', 'delta_ssd_w_out', 'delta_fox_w_in', 'delta_fox_b_f', 'delta_fox_q_norm_g', 'delta_fox_k_norm_g', 'delta_fox_w_out', 'delta_ffn_w_up', 'delta_ffn_conv_w', 'delta_ffn_conv_b', 'delta_ffn_w_down', 'delta_final_norm_g', 'new_m_mix_norm_g', 'new_m_ffn_norm_g', 'new_m_ssd_w_in', 'new_m_ssd_conv_w', 'new_m_ssd_conv_b', 'new_m_ssd_dt_bias', 'new_m_ssd_a_log', 'new_m_ssd_d', 'new_m_ssd_norm_g', 'new_m_ssd_w_out', 'new_m_fox_w_in', 'new_m_fox_b_f', 'new_m_fox_q_norm_g', 'new_m_fox_k_norm_g', 'new_m_fox_w_out', 'new_m_ffn_w_up', 'new_m_ffn_conv_w', 'new_m_ffn_conv_b', 'new_m_ffn_w_down', 'new_m_final_norm_g', 'new_v_mix_norm_g', 'new_v_ffn_norm_g', 'new_v_ssd_w_in', 'new_v_ssd_conv_w', 'new_v_ssd_conv_b', 'new_v_ssd_dt_bias', 'new_v_ssd_a_log', 'new_v_ssd_d', 'new_v_ssd_norm_g', 'new_v_ssd_w_out', 'new_v_fox_w_in', 'new_v_fox_b_f', 'new_v_fox_q_norm_g', 'new_v_fox_k_norm_g', 'new_v_fox_w_out', 'new_v_ffn_w_up', 'new_v_ffn_conv_w', 'new_v_ffn_conv_b', 'new_v_ffn_w_down', 'new_v_final_norm_g']
TWIN_LEAF_KINDS = {'loss': 'loss', 'grad_x': 'grad_x', 'grad_mix_norm_g': 'grad_w', 'grad_ffn_norm_g': 'grad_w', 'grad_ssd_w_in': 'grad_w', 'grad_ssd_conv_w': 'grad_w', 'grad_ssd_conv_b': 'grad_w', 'grad_ssd_dt_bias': 'grad_w', 'grad_ssd_a_log': 'grad_w', 'grad_ssd_d': 'grad_w', 'grad_ssd_norm_g': 'grad_w', 'grad_ssd_w_out': 'grad_w', 'grad_fox_w_in': 'grad_w', 'grad_fox_b_f': 'grad_w', 'grad_fox_q_norm_g': 'grad_w', 'grad_fox_k_norm_g': 'grad_w', 'grad_fox_w_out': 'grad_w', 'grad_ffn_w_up': 'grad_w', 'grad_ffn_conv_w': 'grad_w', 'grad_ffn_conv_b': 'grad_w', 'grad_ffn_w_down': 'grad_w', 'grad_final_norm_g': 'grad_w', 'delta_mix_norm_g': 'delta_w', 'delta_ffn_norm_g': 'delta_w', 'delta_ssd_w_in': 'delta_w', 'delta_ssd_conv_w': 'delta_w', 'delta_ssd_conv_b': 'delta_w', 'delta_ssd_dt_bias': 'delta_w', 'delta_ssd_a_log': 'delta_w', 'delta_ssd_d': 'delta_w', 'delta_ssd_norm_g': 'delta_w', 'delta_ssd_w_out': 'delta_w', 'delta_fox_w_in': 'delta_w', 'delta_fox_b_f': 'delta_w', 'delta_fox_q_norm_g': 'delta_w', 'delta_fox_k_norm_g': 'delta_w', 'delta_fox_w_out': 'delta_w', 'delta_ffn_w_up': 'delta_w', 'delta_ffn_conv_w': 'delta_w', 'delta_ffn_conv_b': 'delta_w', 'delta_ffn_w_down': 'delta_w', 'delta_final_norm_g': 'delta_w', 'new_m_mix_norm_g': 'new_m', 'new_m_ffn_norm_g': 'new_m', 'new_m_ssd_w_in': 'new_m', 'new_m_ssd_conv_w': 'new_m', 'new_m_ssd_conv_b': 'new_m', 'new_m_ssd_dt_bias': 'new_m', 'new_m_ssd_a_log': 'new_m', 'new_m_ssd_d': 'new_m', 'new_m_ssd_norm_g': 'new_m', 'new_m_ssd_w_out': 'new_m', 'new_m_fox_w_in': 'new_m', 'new_m_fox_b_f': 'new_m', 'new_m_fox_q_norm_g': 'new_m', 'new_m_fox_k_norm_g': 'new_m', 'new_m_fox_w_out': 'new_m', 'new_m_ffn_w_up': 'new_m', 'new_m_ffn_conv_w': 'new_m', 'new_m_ffn_conv_b': 'new_m', 'new_m_ffn_w_down': 'new_m', 'new_m_final_norm_g': 'new_m', 'new_v_mix_norm_g': 'new_v', 'new_v_ffn_norm_g': 'new_v', 'new_v_ssd_w_in': 'new_v', 'new_v_ssd_conv_w': 'new_v', 'new_v_ssd_conv_b': 'new_v', 'new_v_ssd_dt_bias': 'new_v', 'new_v_ssd_a_log': 'new_v', 'new_v_ssd_d': 'new_v', 'new_v_ssd_norm_g': 'new_v', 'new_v_ssd_w_out': 'new_v', 'new_v_fox_w_in': 'new_v', 'new_v_fox_b_f': 'new_v', 'new_v_fox_q_norm_g': 'new_v', 'new_v_fox_k_norm_g': 'new_v', 'new_v_fox_w_out': 'new_v', 'new_v_ffn_w_up': 'new_v', 'new_v_ffn_conv_w': 'new_v', 'new_v_ffn_conv_b': 'new_v', 'new_v_ffn_w_down': 'new_v', 'new_v_final_norm_g': 'new_v'}


def _forward(args):
    return _fwd_reference(*[args[k] for k in FWD_PARAMS])


def _output_shape():
    def fwd():
        inp = _fwd_setup_inputs(0)
        return _fwd_reference(*[inp[k] for k in FWD_PARAMS])
    out = _jax.eval_shape(fwd)
    return out.shape, out.dtype

N_MICROBATCH = 1
ADAM_LR = 0.001
ADAM_B1 = 0.9
ADAM_B2 = 0.999
ADAM_EPS = 1e-08
ADAM_WD = 0.01
ADAM_STEP = 10
PER_EXAMPLE_BATCH_AXIS = {'x': 0, 'loss_target': 0}
SHARED_INPUTS = []
_WEIGHT_DTYPES = {'mix_norm_g': _jnp.float32, 'ffn_norm_g': _jnp.float32, 'ssd_w_in': _jnp.float32, 'ssd_conv_w': _jnp.float32, 'ssd_conv_b': _jnp.float32, 'ssd_dt_bias': _jnp.float32, 'ssd_a_log': _jnp.float32, 'ssd_d': _jnp.float32, 'ssd_norm_g': _jnp.float32, 'ssd_w_out': _jnp.float32, 'fox_w_in': _jnp.float32, 'fox_b_f': _jnp.float32, 'fox_q_norm_g': _jnp.float32, 'fox_k_norm_g': _jnp.float32, 'fox_w_out': _jnp.float32, 'ffn_w_up': _jnp.float32, 'ffn_conv_w': _jnp.float32, 'ffn_conv_b': _jnp.float32, 'ffn_w_down': _jnp.float32, 'final_norm_g': _jnp.float32}
MOMENT_SCALE = {'mix_norm_g': 6.650589e-02, 'ffn_norm_g': 5.186599e-02, 'ssd_w_in': 4.127947e-02, 'ssd_conv_w': 3.829965e-02, 'ssd_conv_b': 5.101295e-02, 'ssd_dt_bias': 9.214406e-02, 'ssd_a_log': 1.278004e-01, 'ssd_d': 2.654298e-01, 'ssd_norm_g': 4.618386e-02, 'ssd_w_out': 1.764598e-01, 'fox_w_in': 7.124061e-03, 'fox_b_f': 4.280513e-02, 'fox_q_norm_g': 2.921506e-02, 'fox_k_norm_g': 2.928471e-02, 'fox_w_out': 2.602554e-02, 'ffn_w_up': 2.193296e-02, 'ffn_conv_w': 2.228040e-02, 'ffn_conv_b': 2.105851e-02, 'ffn_w_down': 1.012562e-01, 'final_norm_g': 3.202910e+01}


def _to_microbatches(a, axis):
    t = _jnp.moveaxis(a, axis, 0)
    t = t.reshape((N_MICROBATCH, t.shape[0] // N_MICROBATCH) + t.shape[1:])
    return _jnp.moveaxis(t, 1, axis + 1)


def setup_inputs(seed: int = 0) -> dict:
    inp = _fwd_setup_inputs(seed)
    key = _jax.random.fold_in(_jax.random.key(seed), 7919)
    shape, _ = _output_shape()
    out = dict(inp)
    out["loss_target"] = _jax.random.normal(_jax.random.fold_in(key, 0), shape, _jnp.float32)
    for i, name in enumerate(TWIN_WEIGHTS):
        w = inp[name].astype(_jnp.float32)
        if MOMENT_SCALE is None:
            s = _jnp.sqrt(_jnp.mean(_jnp.square(w)) + 1e-30)
        else:
            s = MOMENT_SCALE[name]
        km, kv = _jax.random.split(_jax.random.fold_in(key, i + 1))
        out[name] = w
        out["m_" + name] = s * _jax.random.normal(km, w.shape, _jnp.float32)
        out["v_" + name] = (s * s) * _jax.random.uniform(kv, w.shape, _jnp.float32, 0.5, 1.5)
    if N_MICROBATCH > 1:
        for name, axis in PER_EXAMPLE_BATCH_AXIS.items():
            out[name] = _to_microbatches(out[name], axis)
    return {'x': out['x'], 'mix_norm_g': out['mix_norm_g'], 'ffn_norm_g': out['ffn_norm_g'], 'ssd_w_in': out['ssd_w_in'], 'ssd_conv_w': out['ssd_conv_w'], 'ssd_conv_b': out['ssd_conv_b'], 'ssd_dt_bias': out['ssd_dt_bias'], 'ssd_a_log': out['ssd_a_log'], 'ssd_d': out['ssd_d'], 'ssd_norm_g': out['ssd_norm_g'], 'ssd_w_out': out['ssd_w_out'], 'fox_w_in': out['fox_w_in'], 'fox_b_f': out['fox_b_f'], 'fox_q_norm_g': out['fox_q_norm_g'], 'fox_k_norm_g': out['fox_k_norm_g'], 'fox_w_out': out['fox_w_out'], 'ffn_w_up': out['ffn_w_up'], 'ffn_conv_w': out['ffn_conv_w'], 'ffn_conv_b': out['ffn_conv_b'], 'ffn_w_down': out['ffn_w_down'], 'final_norm_g': out['final_norm_g'], 'loss_target': out['loss_target'], 'm_mix_norm_g': out['m_mix_norm_g'], 'm_ffn_norm_g': out['m_ffn_norm_g'], 'm_ssd_w_in': out['m_ssd_w_in'], 'm_ssd_conv_w': out['m_ssd_conv_w'], 'm_ssd_conv_b': out['m_ssd_conv_b'], 'm_ssd_dt_bias': out['m_ssd_dt_bias'], 'm_ssd_a_log': out['m_ssd_a_log'], 'm_ssd_d': out['m_ssd_d'], 'm_ssd_norm_g': out['m_ssd_norm_g'], 'm_ssd_w_out': out['m_ssd_w_out'], 'm_fox_w_in': out['m_fox_w_in'], 'm_fox_b_f': out['m_fox_b_f'], 'm_fox_q_norm_g': out['m_fox_q_norm_g'], 'm_fox_k_norm_g': out['m_fox_k_norm_g'], 'm_fox_w_out': out['m_fox_w_out'], 'm_ffn_w_up': out['m_ffn_w_up'], 'm_ffn_conv_w': out['m_ffn_conv_w'], 'm_ffn_conv_b': out['m_ffn_conv_b'], 'm_ffn_w_down': out['m_ffn_w_down'], 'm_final_norm_g': out['m_final_norm_g'], 'v_mix_norm_g': out['v_mix_norm_g'], 'v_ffn_norm_g': out['v_ffn_norm_g'], 'v_ssd_w_in': out['v_ssd_w_in'], 'v_ssd_conv_w': out['v_ssd_conv_w'], 'v_ssd_conv_b': out['v_ssd_conv_b'], 'v_ssd_dt_bias': out['v_ssd_dt_bias'], 'v_ssd_a_log': out['v_ssd_a_log'], 'v_ssd_d': out['v_ssd_d'], 'v_ssd_norm_g': out['v_ssd_norm_g'], 'v_ssd_w_out': out['v_ssd_w_out'], 'v_fox_w_in': out['v_fox_w_in'], 'v_fox_b_f': out['v_fox_b_f'], 'v_fox_q_norm_g': out['v_fox_q_norm_g'], 'v_fox_k_norm_g': out['v_fox_k_norm_g'], 'v_fox_w_out': out['v_fox_w_out'], 'v_ffn_w_up': out['v_ffn_w_up'], 'v_ffn_conv_w': out['v_ffn_conv_w'], 'v_ffn_conv_b': out['v_ffn_conv_b'], 'v_ffn_w_down': out['v_ffn_w_down'], 'v_final_norm_g': out['v_final_norm_g']}


def _loss(weights, diff, rest, loss_target):
    with _jax.named_scope("forward"):
        args = {**rest, TWIN_DIFF_INPUT: diff, **{k: w.astype(_WEIGHT_DTYPES[k]) for k, w in weights.items()}}
        y = _forward(args)
    with _jax.named_scope("loss_head"):
        err = _jnp.square(y.astype(_jnp.float32) - loss_target)
        return 0.5 * _jnp.sum(_jnp.mean(err, axis=-1)) if err.ndim else 0.5 * err


def _adamw(w, g, m, v):
    m = ADAM_B1 * m + (1.0 - ADAM_B1) * g
    v = ADAM_B2 * v + (1.0 - ADAM_B2) * _jnp.square(g)
    m_hat = m / (1.0 - ADAM_B1 ** ADAM_STEP)
    v_hat = v / (1.0 - ADAM_B2 ** ADAM_STEP)
    delta = -ADAM_LR * (m_hat / (_jnp.sqrt(v_hat) + ADAM_EPS) + ADAM_WD * w)
    return delta, m, v


def reference(x, mix_norm_g, ffn_norm_g, ssd_w_in, ssd_conv_w, ssd_conv_b, ssd_dt_bias, ssd_a_log, ssd_d, ssd_norm_g, ssd_w_out, fox_w_in, fox_b_f, fox_q_norm_g, fox_k_norm_g, fox_w_out, ffn_w_up, ffn_conv_w, ffn_conv_b, ffn_w_down, final_norm_g, loss_target, m_mix_norm_g, m_ffn_norm_g, m_ssd_w_in, m_ssd_conv_w, m_ssd_conv_b, m_ssd_dt_bias, m_ssd_a_log, m_ssd_d, m_ssd_norm_g, m_ssd_w_out, m_fox_w_in, m_fox_b_f, m_fox_q_norm_g, m_fox_k_norm_g, m_fox_w_out, m_ffn_w_up, m_ffn_conv_w, m_ffn_conv_b, m_ffn_w_down, m_final_norm_g, v_mix_norm_g, v_ffn_norm_g, v_ssd_w_in, v_ssd_conv_w, v_ssd_conv_b, v_ssd_dt_bias, v_ssd_a_log, v_ssd_d, v_ssd_norm_g, v_ssd_w_out, v_fox_w_in, v_fox_b_f, v_fox_q_norm_g, v_fox_k_norm_g, v_fox_w_out, v_ffn_w_up, v_ffn_conv_w, v_ffn_conv_b, v_ffn_w_down, v_final_norm_g):
    given = dict(x=x, mix_norm_g=mix_norm_g, ffn_norm_g=ffn_norm_g, ssd_w_in=ssd_w_in, ssd_conv_w=ssd_conv_w, ssd_conv_b=ssd_conv_b, ssd_dt_bias=ssd_dt_bias, ssd_a_log=ssd_a_log, ssd_d=ssd_d, ssd_norm_g=ssd_norm_g, ssd_w_out=ssd_w_out, fox_w_in=fox_w_in, fox_b_f=fox_b_f, fox_q_norm_g=fox_q_norm_g, fox_k_norm_g=fox_k_norm_g, fox_w_out=fox_w_out, ffn_w_up=ffn_w_up, ffn_conv_w=ffn_conv_w, ffn_conv_b=ffn_conv_b, ffn_w_down=ffn_w_down, final_norm_g=final_norm_g, loss_target=loss_target, m_mix_norm_g=m_mix_norm_g, m_ffn_norm_g=m_ffn_norm_g, m_ssd_w_in=m_ssd_w_in, m_ssd_conv_w=m_ssd_conv_w, m_ssd_conv_b=m_ssd_conv_b, m_ssd_dt_bias=m_ssd_dt_bias, m_ssd_a_log=m_ssd_a_log, m_ssd_d=m_ssd_d, m_ssd_norm_g=m_ssd_norm_g, m_ssd_w_out=m_ssd_w_out, m_fox_w_in=m_fox_w_in, m_fox_b_f=m_fox_b_f, m_fox_q_norm_g=m_fox_q_norm_g, m_fox_k_norm_g=m_fox_k_norm_g, m_fox_w_out=m_fox_w_out, m_ffn_w_up=m_ffn_w_up, m_ffn_conv_w=m_ffn_conv_w, m_ffn_conv_b=m_ffn_conv_b, m_ffn_w_down=m_ffn_w_down, m_final_norm_g=m_final_norm_g, v_mix_norm_g=v_mix_norm_g, v_ffn_norm_g=v_ffn_norm_g, v_ssd_w_in=v_ssd_w_in, v_ssd_conv_w=v_ssd_conv_w, v_ssd_conv_b=v_ssd_conv_b, v_ssd_dt_bias=v_ssd_dt_bias, v_ssd_a_log=v_ssd_a_log, v_ssd_d=v_ssd_d, v_ssd_norm_g=v_ssd_norm_g, v_ssd_w_out=v_ssd_w_out, v_fox_w_in=v_fox_w_in, v_fox_b_f=v_fox_b_f, v_fox_q_norm_g=v_fox_q_norm_g, v_fox_k_norm_g=v_fox_k_norm_g, v_fox_w_out=v_fox_w_out, v_ffn_w_up=v_ffn_w_up, v_ffn_conv_w=v_ffn_conv_w, v_ffn_conv_b=v_ffn_conv_b, v_ffn_w_down=v_ffn_w_down, v_final_norm_g=v_final_norm_g)
    weights = {n: given[n] for n in TWIN_WEIGHTS}
    shared = {n: given[n] for n in SHARED_INPUTS}
    per_example = {n: given[n] for n in ['x']}
    grad_fn = _jax.value_and_grad(_loss, argnums=(0, 1))

    def one_microbatch(ex, loss_target):
        ex = dict(ex)
        diff = ex.pop(TWIN_DIFF_INPUT)
        return grad_fn(weights, diff, {**shared, **ex}, loss_target)

    if N_MICROBATCH == 1:
        loss, (grad_w, grad_x) = one_microbatch(per_example, given["loss_target"])
    else:
        def body(carry, xs):
            loss_sum, grad_sum = carry
            l_k, (gw_k, gx_k) = one_microbatch(xs[0], xs[1])
            with _jax.named_scope("update"):
                return (loss_sum + l_k, _jax.tree.map(_jnp.add, grad_sum, gw_k)), gx_k

        init = (_jnp.zeros((), _jnp.float32), _jax.tree.map(_jnp.zeros_like, weights))
        (loss, grad_w), grad_x = _jax.lax.scan(body, init, (per_example, given["loss_target"]))
    with _jax.named_scope("update"):
        delta_w, new_m, new_v = {}, {}, {}
        for n in TWIN_WEIGHTS:
            delta_w[n], new_m[n], new_v[n] = _adamw(weights[n], grad_w[n], given["m_" + n], given["v_" + n])
    return (loss, grad_x, *[grad_w[n] for n in TWIN_WEIGHTS], *[delta_w[n] for n in TWIN_WEIGHTS],
            *[new_m[n] for n in TWIN_WEIGHTS], *[new_v[n] for n in TWIN_WEIGHTS])
```

```python
import functools
import math

import jax
import jax.numpy as jnp
from jax import lax
from jax.experimental import pallas as pl
from jax.experimental.pallas import tpu as pltpu

F32 = jnp.float32
MXU_DT = jnp.bfloat16
VMEM_LIMIT_BYTES = 56 * 1024 * 1024
LANES = 128
N_DEV = 8
MESH_IDS = pl.DeviceIdType.MESH

EPS = 1e-6
D_MODEL = 1024
SSD_D_INNER = 2048
SSD_HEAD_DIM = 64
SSD_HEADS = 32
SSD_GROUPS = 4
SSD_HPG = 8
SSD_STATE = 128
SSD_CONV = 4
SSD_CHUNK = 128
SSD_CONV_DIM = 3072
SSD_ZX = SSD_D_INNER + SSD_CONV_DIM
FOX_HEAD_DIM = 64
FOX_HEADS = 16
FOX_D = 1024
D_FF = 2816
FFN_CONV = 3
ADAM_LR, ADAM_B1, ADAM_B2, ADAM_EPS, ADAM_WD, ADAM_STEP = 0.001, 0.9, 0.999, 1e-08, 0.01, 10


def _params(*sem):
    return pltpu.CompilerParams(dimension_semantics=sem or None, vmem_limit_bytes=VMEM_LIMIT_BYTES)


def _sds(shape, dtype=F32):
    return jax.ShapeDtypeStruct(tuple(shape), dtype)


def _col_tile(n, cap=1536):
    best = None
    for t in range(LANES, min(n, cap) + 1, LANES):
        if n % t == 0:
            best = t
    assert best is not None, n
    return best


def _sigmoid(x):
    return 1.0 / (1.0 + jnp.exp(-x))


def _matmul(a, b, *, ta=False, tb=False, res=None, out_dtype=F32, tm=512, tn=None, name):
    (kdim, m) = a.shape if ta else a.shape[::-1]
    (n, k2) = b.shape if tb else b.shape[::-1]
    assert kdim == k2, (a.shape, b.shape, ta, tb)
    tm = min(tm, m)
    if m % tm:
        tm = _col_tile(m, tm)
    tn = tn or _col_tile(n)
    assert m % tm == 0 and n % tn == 0, (m, tm, n, tn)
    dims = (((0 if ta else 1,), (1 if tb else 0,)), ((), ()))

    def body(*refs):
        a_ref, b_ref = refs[0], refs[1]
        o_ref = refs[-1]
        acc = lax.dot_general(a_ref[...].astype(MXU_DT), b_ref[...].astype(MXU_DT), dims,
                              preferred_element_type=F32)
        if res is not None:
            acc = acc + refs[2][...].astype(F32)
        o_ref[...] = acc.astype(o_ref.dtype)

    a_spec = pl.BlockSpec((kdim, tm), lambda i, j: (0, i)) if ta else pl.BlockSpec((tm, kdim), lambda i, j: (i, 0))
    b_spec = pl.BlockSpec((tn, kdim), lambda i, j: (j, 0)) if tb else pl.BlockSpec((kdim, tn), lambda i, j: (0, j))
    o_spec = pl.BlockSpec((tm, tn), lambda i, j: (i, j))
    ins, specs = [a, b], [a_spec, b_spec]
    if res is not None:
        ins.append(res)
        specs.append(o_spec)
    return pl.pallas_call(
        body, name=name, grid=(m // tm, n // tn), in_specs=specs, out_specs=o_spec,
        out_shape=_sds((m, n), out_dtype), compiler_params=_params("parallel", "parallel"),
    )(*ins)


def _rowwise(fn, rows, consts, out_rows, out_sums, *, tr=256, name):
    rows = [r if isinstance(r, tuple) else (r, 0, r.shape[1]) for r in rows]
    s = rows[0][0].shape[0]
    tr = min(tr, s)
    assert s % tr == 0
    n_in, n_c, n_or = len(rows), len(consts), len(out_rows)

    def body(*refs):
        ins = [r[...] for r in refs[:n_in + n_c]]
        outs = fn(*ins)
        o_refs = refs[n_in + n_c:]
        for o_ref, val in zip(o_refs[:n_or], outs[:n_or]):
            o_ref[...] = val.astype(o_ref.dtype)
        if out_sums:
            first = pl.program_id(0) == 0

            @pl.when(first)
            def _():
                for o_ref, val in zip(o_refs[n_or:], outs[n_or:]):
                    o_ref[...] = val.astype(o_ref.dtype)

            @pl.when(jnp.logical_not(first))
            def _():
                for o_ref, val in zip(o_refs[n_or:], outs[n_or:]):
                    o_ref[...] += val.astype(o_ref.dtype)

    in_specs = [pl.BlockSpec((tr, width), functools.partial(lambda i, cb: (i, cb), cb=cb)) for _, cb, width in rows]
    in_specs += [pl.BlockSpec(c.shape, lambda i: (0, 0)) for c in consts]
    out_specs = [pl.BlockSpec((tr, o.shape[1]), lambda i: (i, 0)) for o in out_rows]
    out_specs += [pl.BlockSpec(o.shape, lambda i: (0, 0)) for o in out_sums]
    return pl.pallas_call(
        body, name=name, grid=(s // tr,), in_specs=in_specs, out_specs=out_specs,
        out_shape=list(out_rows) + list(out_sums),
        compiler_params=_params("arbitrary" if out_sums else "parallel"),
    )(*[r[0] for r in rows], *consts)


def _rms_fwd(h, g, name):
    def fn(x, gv):
        r = lax.rsqrt(jnp.mean(x * x, axis=-1, keepdims=True) + EPS)
        return (x * r * gv,)
    return _rowwise(fn, [h], [g.reshape(1, -1)], [_sds(h.shape, MXU_DT)], [], name=name)[0]


def _rms_bwd(h, g, dy, dres, name):
    def fn(x, dyv, dr, gv):
        r = lax.rsqrt(jnp.mean(x * x, axis=-1, keepdims=True) + EPS)
        dyg = dyv * gv
        dx = r * dyg - x * (r * r * r) * jnp.mean(x * dyg, axis=-1, keepdims=True)
        return dr + dx, jnp.sum(dyv * x * r, axis=0, keepdims=True)
    return _rowwise(fn, [h, dy, dres], [g.reshape(1, -1)], [_sds(h.shape)], [_sds((1, h.shape[1]))], name=name)


def _loss_head(h, g, target, name):
    c = h.shape[1]

    def fn(x, t, gv):
        r = lax.rsqrt(jnp.mean(x * x, axis=-1, keepdims=True) + EPS)
        y = x * r * gv
        err = y - t
        dyv = err * (1.0 / c)
        dyg = dyv * gv
        dx = r * dyg - x * (r * r * r) * jnp.mean(x * dyg, axis=-1, keepdims=True)
        loss = 0.5 * jnp.sum(jnp.mean(err * err, axis=-1, keepdims=True), axis=0, keepdims=True)
        return dx, jnp.sum(dyv * x * r, axis=0, keepdims=True), loss
    return _rowwise(fn, [h, target], [g.reshape(1, -1)], [_sds(h.shape)], [_sds((1, c)), _sds((1, 1))], name=name)


PAD_ROWS = 8
ROW_TILE = 512


def _shifted_conv(xp_ref, w, r0, tr, kw):
    acc = None
    for k in range(kw):
        xk = xp_ref[pl.ds(PAD_ROWS + r0 - (kw - 1) + k, tr), :]
        term = xk * w[k:k + 1, :]
        acc = term if acc is None else acc + term
    return acc


def _convglu_fwd(u, conv_w, conv_b, name):
    s = u.shape[0]
    nt = D_FF // LANES
    tr = min(ROW_TILE, s)

    def body(ug_ref, uv_ref, w_ref, b_ref, act_ref, xp_ref):
        xp_ref[pl.ds(0, PAD_ROWS), :] = jnp.zeros((PAD_ROWS, LANES), F32)
        xp_ref[pl.ds(PAD_ROWS, s), :] = ug_ref[...]
        w = w_ref[...]
        b = b_ref[...]
        for r0 in range(0, s, tr):
            gate = _shifted_conv(xp_ref, w, r0, tr, FFN_CONV) + b
            act = gate * _sigmoid(gate) * uv_ref[pl.ds(r0, tr), :]
            act_ref[pl.ds(r0, tr), :] = act.astype(act_ref.dtype)

    return pl.pallas_call(
        body, name=name, grid=(nt,),
        in_specs=[pl.BlockSpec((s, LANES), lambda j: (0, j)), pl.BlockSpec((s, LANES), lambda j: (0, nt + j)),
                  pl.BlockSpec((FFN_CONV, LANES), lambda j: (0, j)), pl.BlockSpec((1, LANES), lambda j: (0, j))],
        out_specs=pl.BlockSpec((s, LANES), lambda j: (0, j)),
        out_shape=_sds((s, D_FF), MXU_DT),
        scratch_shapes=[pltpu.VMEM((s + PAD_ROWS, LANES), F32)],
        compiler_params=_params("parallel"),
    )(u, u, conv_w, conv_b.reshape(1, -1))


def _convglu_bwd(u, dact, conv_w, conv_b, name):
    s = u.shape[0]
    nt = D_FF // LANES
    tr = min(ROW_TILE, s)
    kw = FFN_CONV

    def body(ug_ref, uv_ref, da_ref, w_ref, b_ref, dug_ref, duv_ref, dw_ref, db_ref, xp_ref, dgp_ref):
        xp_ref[pl.ds(0, PAD_ROWS), :] = jnp.zeros((PAD_ROWS, LANES), F32)
        xp_ref[pl.ds(PAD_ROWS, s), :] = ug_ref[...]
        dgp_ref[pl.ds(s, PAD_ROWS), :] = jnp.zeros((PAD_ROWS, LANES), F32)
        w = w_ref[...]
        b = b_ref[...]
        dw = [jnp.zeros((1, LANES), F32) for _ in range(kw)]
        db = jnp.zeros((1, LANES), F32)
        for r0 in range(0, s, tr):
            gate = _shifted_conv(xp_ref, w, r0, tr, kw) + b
            sg = _sigmoid(gate)
            da = da_ref[pl.ds(r0, tr), :].astype(F32)
            duv_ref[pl.ds(r0, tr), :] = (da * gate * sg).astype(duv_ref.dtype)
            dgate = da * uv_ref[pl.ds(r0, tr), :] * (sg * (1.0 + gate * (1.0 - sg)))
            dgp_ref[pl.ds(r0, tr), :] = dgate
            db = db + jnp.sum(dgate, axis=0, keepdims=True)
            for k in range(kw):
                xk = xp_ref[pl.ds(PAD_ROWS + r0 - (kw - 1) + k, tr), :]
                dw[k] = dw[k] + jnp.sum(dgate * xk, axis=0, keepdims=True)
        for r0 in range(0, s, tr):
            acc = None
            for k in range(kw):
                term = dgp_ref[pl.ds(r0 + (kw - 1) - k, tr), :] * w[k:k + 1, :]
                acc = term if acc is None else acc + term
            dug_ref[pl.ds(r0, tr), :] = acc.astype(dug_ref.dtype)
        for k in range(kw):
            dw_ref[pl.ds(k, 1), :] = dw[k]
        db_ref[...] = db

    col = lambda j: (0, j)
    return pl.pallas_call(
        body, name=name, grid=(nt,),
        in_specs=[pl.BlockSpec((s, LANES), col), pl.BlockSpec((s, LANES), lambda j: (0, nt + j)),
                  pl.BlockSpec((s, LANES), col), pl.BlockSpec((kw, LANES), col), pl.BlockSpec((1, LANES), col)],
        out_specs=[pl.BlockSpec((s, LANES), col), pl.BlockSpec((s, LANES), col),
                   pl.BlockSpec((kw, LANES), col), pl.BlockSpec((1, LANES), col)],
        out_shape=[_sds((s, D_FF), MXU_DT), _sds((s, D_FF), MXU_DT), _sds((kw, D_FF)), _sds((1, D_FF))],
        scratch_shapes=[pltpu.VMEM((s + PAD_ROWS, LANES), F32), pltpu.VMEM((s + PAD_ROWS, LANES), F32)],
        compiler_params=_params("parallel"),
    )(u, u, dact, conv_w, conv_b.reshape(1, -1))


def _ffn_fwd(h, norm_g, w_up, conv_w, conv_b, w_down, tag):
    hf = _rms_fwd(h, norm_g, f"ffn_norm_fwd")
    u = _matmul(hf, w_up, name="ffn_up")
    act = _convglu_fwd(u, conv_w, conv_b, "convglu_fwd")
    h_out = _matmul(act, w_down, res=h, name="ffn_down")
    return h_out, (hf, u, act)


def _ffn_bwd(h, dh, saved, norm_g, w_up, conv_w, conv_b, w_down):
    hf, u, act = saved
    dh_b = dh.astype(MXU_DT)
    dact = _matmul(dh_b, w_down, tb=True, name="ffn_down_dx")
    dw_down = _matmul(act, dh_b, ta=True, out_dtype=MXU_DT, name="ffn_down_dw")
    dug, duv, dconv_w, dconv_b = _convglu_bwd(u, dact, conv_w, conv_b, "convglu_bwd")
    dhf = _matmul(dug, w_up[:, :D_FF], tb=True, name="ffn_up_dx_gate")
    dhf = _matmul(duv, w_up[:, D_FF:], tb=True, res=dhf, name="ffn_up_dx_val")
    dw_up = jnp.concatenate([_matmul(hf, dug, ta=True, out_dtype=MXU_DT, name="ffn_up_dw_gate"),
                             _matmul(hf, duv, ta=True, out_dtype=MXU_DT, name="ffn_up_dw_val")], axis=1)
    dh_in, dnorm_g = _rms_bwd(h, norm_g, dhf, dh, "ffn_norm_bwd")
    return dh_in, dict(norm_g=dnorm_g, w_up=dw_up, conv_w=dconv_w, conv_b=dconv_b, w_down=dw_down)


def _dwconv_silu_fwd(proj, col0, n_ch, conv_w, conv_b, name):
    s = proj.shape[0]
    nt, t0, kw = n_ch // LANES, col0 // LANES, conv_w.shape[0]
    tr = min(ROW_TILE, s)

    def body(x_ref, w_ref, b_ref, o_ref, xp_ref):
        xp_ref[pl.ds(0, PAD_ROWS), :] = jnp.zeros((PAD_ROWS, LANES), F32)
        xp_ref[pl.ds(PAD_ROWS, s), :] = x_ref[...]
        w = w_ref[...]
        b = b_ref[...]
        for r0 in range(0, s, tr):
            pre = _shifted_conv(xp_ref, w, r0, tr, kw) + b
            o_ref[pl.ds(r0, tr), :] = pre * _sigmoid(pre)

    col = lambda j: (0, j)
    return pl.pallas_call(
        body, name=name, grid=(nt,),
        in_specs=[pl.BlockSpec((s, LANES), lambda j: (0, t0 + j)), pl.BlockSpec((kw, LANES), col),
                  pl.BlockSpec((1, LANES), col)],
        out_specs=pl.BlockSpec((s, LANES), col), out_shape=_sds((s, n_ch)),
        scratch_shapes=[pltpu.VMEM((s + PAD_ROWS, LANES), F32)],
        compiler_params=_params("parallel"),
    )(proj, conv_w, conv_b.reshape(1, -1))


def _dwconv_silu_bwd(proj, col0, n_ch, dout, conv_w, conv_b, name):
    s = proj.shape[0]
    nt, t0, kw = n_ch // LANES, col0 // LANES, conv_w.shape[0]
    tr = min(ROW_TILE, s)

    def body(x_ref, do_ref, w_ref, b_ref, dx_ref, dw_ref, db_ref, xp_ref, dgp_ref):
        xp_ref[pl.ds(0, PAD_ROWS), :] = jnp.zeros((PAD_ROWS, LANES), F32)
        xp_ref[pl.ds(PAD_ROWS, s), :] = x_ref[...]
        dgp_ref[pl.ds(s, PAD_ROWS), :] = jnp.zeros((PAD_ROWS, LANES), F32)
        w = w_ref[...]
        b = b_ref[...]
        dw = [jnp.zeros((1, LANES), F32) for _ in range(kw)]
        db = jnp.zeros((1, LANES), F32)
        for r0 in range(0, s, tr):
            pre = _shifted_conv(xp_ref, w, r0, tr, kw) + b
            sg = _sigmoid(pre)
            dpre = do_ref[pl.ds(r0, tr), :] * (sg * (1.0 + pre * (1.0 - sg)))
            dgp_ref[pl.ds(r0, tr), :] = dpre
            db = db + jnp.sum(dpre, axis=0, keepdims=True)
            for k in range(kw):
                xk = xp_ref[pl.ds(PAD_ROWS + r0 - (kw - 1) + k, tr), :]
                dw[k] = dw[k] + jnp.sum(dpre * xk, axis=0, keepdims=True)
        for r0 in range(0, s, tr):
            acc = None
            for k in range(kw):
                term = dgp_ref[pl.ds(r0 + (kw - 1) - k, tr), :] * w[k:k + 1, :]
                acc = term if acc is None else acc + term
            dx_ref[pl.ds(r0, tr), :] = acc.astype(dx_ref.dtype)
        for k in range(kw):
            dw_ref[pl.ds(k, 1), :] = dw[k]
        db_ref[...] = db

    col = lambda j: (0, j)
    return pl.pallas_call(
        body, name=name, grid=(nt,),
        in_specs=[pl.BlockSpec((s, LANES), lambda j: (0, t0 + j)), pl.BlockSpec((s, LANES), col),
                  pl.BlockSpec((kw, LANES), col), pl.BlockSpec((1, LANES), col)],
        out_specs=[pl.BlockSpec((s, LANES), col), pl.BlockSpec((kw, LANES), col), pl.BlockSpec((1, LANES), col)],
        out_shape=[_sds((s, n_ch), MXU_DT), _sds((kw, n_ch)), _sds((1, n_ch))],
        scratch_shapes=[pltpu.VMEM((s + PAD_ROWS, LANES), F32), pltpu.VMEM((s + PAD_ROWS, LANES), F32)],
        compiler_params=_params("parallel"),
    )(proj, dout, conv_w, conv_b.reshape(1, -1))


HIGHEST = lax.Precision.HIGHEST
PAIRS = SSD_HPG // 2
PAIR_W = 2 * SSD_HEAD_DIM
GROUP_W = SSD_HPG * SSD_HEAD_DIM


def _iota2(shape, axis):
    return lax.broadcasted_iota(jnp.int32, shape, axis)


def _lane_pad(v):
    return jnp.pad(v.reshape(1, -1), ((0, 0), (0, LANES - v.shape[0])))


def _heads_to_groups(a):
    s = a.shape[0]
    return a[:, :SSD_HEADS].reshape(s, SSD_GROUPS, SSD_HPG).transpose(1, 0, 2)


def _groups_to_heads(a):
    s = a.shape[1]
    return jnp.pad(a.transpose(1, 0, 2).reshape(s, SSD_HEADS), ((0, 0), (0, LANES - SSD_HEADS)))


def _ssd_prep(dt_raw, dt_bias, a_log, name):
    s = dt_raw.shape[0]
    lc = SSD_CHUNK

    def body(x_ref, b_ref, al_ref, dt_ref, acs_ref, acst_ref):
        x = x_ref[...] + b_ref[...]
        dt = jnp.maximum(x, 0.0) + jnp.log1p(jnp.exp(-jnp.abs(x)))
        da = dt * (-jnp.exp(al_ref[...]))
        lower = (_iota2((lc, lc), 0) >= _iota2((lc, lc), 1)).astype(F32)
        upper = (_iota2((lc, lc), 0) <= _iota2((lc, lc), 1)).astype(F32)
        dt_ref[...] = dt
        acs_ref[...] = jnp.dot(lower, da, precision=HIGHEST, preferred_element_type=F32)
        acst_ref[...] = lax.dot_general(da, upper, (((0,), (0,)), ((), ())), precision=HIGHEST,
                                        preferred_element_type=F32)

    row = pl.BlockSpec((lc, LANES), lambda c: (c, 0))
    one = pl.BlockSpec((1, LANES), lambda c: (0, 0))
    return pl.pallas_call(
        body, name=name, grid=(s // lc,), in_specs=[row, one, one],
        out_specs=[row, row, pl.BlockSpec((LANES, lc), lambda c: (0, c))],
        out_shape=[_sds((s, LANES)), _sds((s, LANES)), _sds((LANES, s))],
        compiler_params=_params("parallel"),
    )(dt_raw, dt_bias, a_log)


def _pair_cols(v, p, lo):
    return jnp.where(lo, v[:, 2 * p:2 * p + 1], v[:, 2 * p + 1:2 * p + 2])


def _decay_matrix(acs, acst, h, tri):
    return jnp.exp(jnp.where(tri, acs[:, h:h + 1] - acst[h:h + 1, :], -jnp.inf))


def _dot(a, b, ca, cb):
    return lax.dot_general(a, b, (((ca,), (cb,)), ((), ())), preferred_element_type=F32)


def _ssd_scan_fwd(xbc, dtg, acsg, acstg, d_skip, name):
    s = xbc.shape[0]
    lc, nc = SSD_CHUNK, s // SSD_CHUNK
    xt, bt = GROUP_W // LANES, SSD_D_INNER // LANES

    def body(x_ref, b_ref, c_ref, dt_ref, acs_ref, acst_ref, dsk_ref, y_ref, hp_ref, st_ref):
        @pl.when(pl.program_id(1) == 0)
        def _():
            st_ref[...] = jnp.zeros(st_ref.shape, F32)

        bm, cmb = b_ref[...], c_ref[...].astype(MXU_DT)
        dt, acs, acst = dt_ref[...], acs_ref[...], acst_ref[...]
        cb = _dot(cmb, bm.astype(MXU_DT), 1, 1)
        tri = _iota2((lc, lc), 0) >= _iota2((lc, lc), 1)
        lo = _iota2((lc, PAIR_W), 1) < SSD_HEAD_DIM
        a_last = acs[lc - 1:lc, :]
        e_acs, e_ds, e_cd = jnp.exp(acs), jnp.exp(a_last - acs), jnp.exp(a_last)
        for p in range(PAIRS):
            sl = pl.ds(p * PAIR_W, PAIR_W)
            xp = x_ref[:, sl]
            ub = (xp * _pair_cols(dt, p, lo)).astype(MXU_DT)
            m0 = (cb * _decay_matrix(acs, acst, 2 * p, tri)).astype(MXU_DT)
            m1 = (cb * _decay_matrix(acs, acst, 2 * p + 1, tri)).astype(MXU_DT)
            ht = st_ref[p]
            hp_ref[p] = ht
            y = jnp.where(lo, _dot(m0, ub, 1, 0), _dot(m1, ub, 1, 0))
            y = y + _dot(cmb, ht.astype(MXU_DT), 1, 0) * _pair_cols(e_acs, p, lo)
            y_ref[:, sl] = y + xp * dsk_ref[:, sl]
            bd0 = (bm * e_ds[:, 2 * p:2 * p + 1]).astype(MXU_DT)
            bd1 = (bm * e_ds[:, 2 * p + 1:2 * p + 2]).astype(MXU_DT)
            st_ref[p] = ht * _pair_cols(e_cd, p, lo[:1]) + jnp.where(lo, _dot(bd0, ub, 0, 0), _dot(bd1, ub, 0, 0))

    small = pl.BlockSpec((None, lc, SSD_HPG), lambda g, c: (g, c, 0))
    return pl.pallas_call(
        body, name=name, grid=(SSD_GROUPS, nc),
        in_specs=[pl.BlockSpec((lc, GROUP_W), lambda g, c: (c, g)),
                  pl.BlockSpec((lc, LANES), lambda g, c: (c, bt + g)),
                  pl.BlockSpec((lc, LANES), lambda g, c: (c, bt + SSD_GROUPS + g)),
                  small, small, pl.BlockSpec((None, SSD_HPG, lc), lambda g, c: (g, 0, c)),
                  pl.BlockSpec((1, GROUP_W), lambda g, c: (0, g))],
        out_specs=[pl.BlockSpec((lc, GROUP_W), lambda g, c: (c, g)),
                   pl.BlockSpec((None, PAIRS, SSD_STATE, PAIR_W), lambda g, c: (c, g, 0, 0))],
        out_shape=[_sds((s, SSD_D_INNER)), _sds((nc, SSD_GROUPS * PAIRS, SSD_STATE, PAIR_W))],
        scratch_shapes=[pltpu.VMEM((PAIRS, SSD_STATE, PAIR_W), F32)],
        compiler_params=_params("parallel", "arbitrary"),
    )(xbc, xbc, xbc, dtg, acsg, acstg, d_skip)


def _ssd_scan_bwd(xbc, dtg, acsg, acstg, d_skip, dy, hprev, name):
    s = xbc.shape[0]
    lc, nc = SSD_CHUNK, s // SSD_CHUNK
    bt = SSD_D_INNER // LANES

    def body(x_ref, b_ref, c_ref, dt_ref, acs_ref, acst_ref, dsk_ref, dy_ref, hp_ref, hn_ref,
             dx_ref, db_ref, dc_ref, daq_ref, dar_ref, ddtx_ref, dd_ref, dst_ref, ta_ref, tx_ref):
        @pl.when(pl.program_id(1) == 0)
        def _():
            dst_ref[...] = jnp.zeros(dst_ref.shape, F32)
            dd_ref[...] = jnp.zeros(dd_ref.shape, F32)

        bm, cmb = b_ref[...], c_ref[...].astype(MXU_DT)
        bmb = bm.astype(MXU_DT)
        dt, acs, acst = dt_ref[...], acs_ref[...], acst_ref[...]
        cb = _dot(cmb, bmb, 1, 1)
        tri = _iota2((lc, lc), 0) >= _iota2((lc, lc), 1)
        lo = _iota2((lc, PAIR_W), 1) < SSD_HEAD_DIM
        a_last = acs[lc - 1:lc, :]
        e_acs, e_ds, e_cd = jnp.exp(acs), jnp.exp(a_last - acs), jnp.exp(a_last)
        dcb = jnp.zeros((lc, lc), F32)
        dc_x = jnp.zeros((lc, SSD_STATE), F32)
        db_x = jnp.zeros((lc, SSD_STATE), F32)
        da_in = jnp.zeros((lc, LANES), F32)
        da_out = jnp.zeros((SSD_HPG, lc), F32)
        head_col = _iota2((lc, LANES), 1)
        head_row = _iota2((SSD_HPG, lc), 0)
        last = _iota2((SSD_HPG, lc), 1) == lc - 1
        for p in range(PAIRS):
            sl = pl.ds(p * PAIR_W, PAIR_W)
            xp, dyp, dsk = x_ref[:, sl], dy_ref[:, sl], dsk_ref[:, sl]
            dtp = _pair_cols(dt, p, lo)
            u = xp * dtp
            ub, dyb = u.astype(MXU_DT), dyp.astype(MXU_DT)
            lmat = (_decay_matrix(acs, acst, 2 * p, tri), _decay_matrix(acs, acst, 2 * p + 1, tri))
            m0, m1 = (cb * lmat[0]).astype(MXU_DT), (cb * lmat[1]).astype(MXU_DT)
            ea, dsl = _pair_cols(e_acs, p, lo), _pair_cols(e_ds, p, lo)
            dht, ht = dst_ref[p], hp_ref[p]
            dhtb, htb = dht.astype(MXU_DT), ht.astype(MXU_DT)
            bd0 = (bm * e_ds[:, 2 * p:2 * p + 1]).astype(MXU_DT)
            bd1 = (bm * e_ds[:, 2 * p + 1:2 * p + 2]).astype(MXU_DT)
            du_state = jnp.where(lo, _dot(bd0, dhtb, 1, 0), _dot(bd1, dhtb, 1, 0))
            du = jnp.where(lo, _dot(m0, dyb, 0, 0), _dot(m1, dyb, 0, 0)) + du_state
            y_off = _dot(cmb, htb, 1, 0) * ea
            ta_ref[:, sl] = dyp * y_off - u * du_state
            tx_ref[:, sl] = du * xp
            dx_ref[:, sl] = dtp * du + dsk * dyp
            dd_ref[:, sl] += jnp.sum(dyp * xp, axis=0, keepdims=True)
            dy_h = (jnp.where(lo, dyp, 0.0).astype(MXU_DT), jnp.where(lo, 0.0, dyp).astype(MXU_DT))
            carry = jnp.sum(dht * hn_ref[p], axis=0, keepdims=True)
            for hh in range(2):
                h = 2 * p + hh
                dml = _dot(dy_h[hh], ub, 1, 1) * lmat[hh]
                dcb = dcb + dml
                flow = cb * dml
                da_in = da_in + jnp.where(head_col == h, jnp.sum(flow, axis=1, keepdims=True), 0.0)
                through = jnp.sum(jnp.where(lo[:1] == (hh == 0), carry, 0.0), axis=1, keepdims=True)
                da_out = da_out + jnp.where(head_row == h, jnp.sum(flow, axis=0, keepdims=True)
                                            - jnp.where(last, through, 0.0), 0.0)
            dye = (dyp * ea).astype(MXU_DT)
            dc_x = dc_x + _dot(dye, htb, 1, 1)
            db_x = db_x + _dot((u * dsl).astype(MXU_DT), dhtb, 1, 1)
            dst_ref[p] = dht * _pair_cols(e_cd, p, lo[:1]) + _dot(cmb, dye, 0, 0)
        dcbb = dcb.astype(MXU_DT)
        dc_ref[...] = _dot(dcbb, bmb, 1, 0) + dc_x
        db_ref[...] = _dot(dcbb, cmb, 0, 0) + db_x
        seg_lo = _iota2((GROUP_W, LANES), 1) * SSD_HEAD_DIM
        chan = _iota2((GROUP_W, LANES), 0)
        seg = jnp.logical_and(chan >= seg_lo, chan < seg_lo + SSD_HEAD_DIM).astype(F32)
        da_in = da_in + jnp.dot(ta_ref[...], seg, precision=HIGHEST, preferred_element_type=F32)
        daq_ref[...] = da_in[:, :SSD_HPG]
        dar_ref[...] = da_out
        ddtx_ref[...] = jnp.dot(tx_ref[...], seg, precision=HIGHEST, preferred_element_type=F32)[:, :SSD_HPG]

    rev = lambda c: nc - 1 - c
    small = pl.BlockSpec((None, lc, SSD_HPG), lambda g, c: (g, rev(c), 0))
    small_t = pl.BlockSpec((None, SSD_HPG, lc), lambda g, c: (g, 0, rev(c)))
    wide = pl.BlockSpec((lc, GROUP_W), lambda g, c: (rev(c), g))
    state = lambda at: pl.BlockSpec((None, PAIRS, SSD_STATE, PAIR_W), lambda g, c: (at(c), g, 0, 0))
    return pl.pallas_call(
        body, name=name, grid=(SSD_GROUPS, nc),
        in_specs=[pl.BlockSpec((lc, GROUP_W), lambda g, c: (rev(c), g)),
                  pl.BlockSpec((lc, LANES), lambda g, c: (rev(c), bt + g)),
                  pl.BlockSpec((lc, LANES), lambda g, c: (rev(c), bt + SSD_GROUPS + g)),
                  small, small, small_t, pl.BlockSpec((1, GROUP_W), lambda g, c: (0, g)), wide,
                  state(rev), state(lambda c: jnp.minimum(rev(c) + 1, nc - 1))],
        out_specs=[wide, pl.BlockSpec((lc, LANES), lambda g, c: (rev(c), g)),
                   pl.BlockSpec((lc, LANES), lambda g, c: (rev(c), g)), small, small_t, small,
                   pl.BlockSpec((1, GROUP_W), lambda g, c: (0, g))],
        out_shape=[_sds((s, SSD_D_INNER)), _sds((s, SSD_GROUPS * SSD_STATE)), _sds((s, SSD_GROUPS * SSD_STATE)),
                   _sds((SSD_GROUPS, s, SSD_HPG)), _sds((SSD_GROUPS, SSD_HPG, s)), _sds((SSD_GROUPS, s, SSD_HPG)),
                   _sds((1, SSD_D_INNER))],
        scratch_shapes=[pltpu.VMEM((PAIRS, SSD_STATE, PAIR_W), F32), pltpu.VMEM((lc, GROUP_W), F32),
                        pltpu.VMEM((lc, GROUP_W), F32)],
        compiler_params=_params("parallel", "arbitrary"),
    )(xbc, xbc, xbc, dtg, acsg, acstg, d_skip, dy, hprev, hprev)


def _ssd_post(da_in, da_out, ddtx, dt, dt_raw, dt_bias, a_log, name):
    s = da_in.shape[0]
    lc = SSD_CHUNK

    def body(dain_ref, daout_ref, ddtx_ref, dt_ref, x_ref, b_ref, al_ref, ddr_ref, dal_ref, dbias_ref):
        @pl.when(pl.program_id(0) == 0)
        def _():
            dal_ref[...] = jnp.zeros(dal_ref.shape, F32)
            dbias_ref[...] = jnp.zeros(dbias_ref.shape, F32)

        upper = (_iota2((lc, lc), 0) <= _iota2((lc, lc), 1)).astype(F32)
        dda = jnp.dot(upper, dain_ref[...] - daout_ref[...], precision=HIGHEST, preferred_element_type=F32)
        a = -jnp.exp(al_ref[...])
        ddt = dda * a + ddtx_ref[...]
        dal_ref[...] += jnp.sum(dda * dt_ref[...], axis=0, keepdims=True) * a
        ddr = ddt * _sigmoid(x_ref[...] + b_ref[...])
        ddr_ref[...] = ddr.astype(ddr_ref.dtype)
        dbias_ref[...] += jnp.sum(ddr, axis=0, keepdims=True)

    row = pl.BlockSpec((lc, LANES), lambda i: (i, 0))
    one = pl.BlockSpec((1, LANES), lambda i: (0, 0))
    return pl.pallas_call(
        body, name=name, grid=(s // lc,), in_specs=[row, row, row, row, row, one, one], out_specs=[row, one, one],
        out_shape=[_sds((s, LANES), MXU_DT), _sds((1, LANES)), _sds((1, LANES))],
        compiler_params=_params("arbitrary"),
    )(da_in, da_out, ddtx, dt, dt_raw, dt_bias, a_log)


NORM_GROUP_W = SSD_D_INNER // SSD_GROUPS


def _group_rstd(yz):
    return [lax.rsqrt(jnp.mean(jnp.square(yz[:, g * NORM_GROUP_W:(g + 1) * NORM_GROUP_W]), axis=-1, keepdims=True) + EPS)
            for g in range(SSD_GROUPS)]


def _gated_norm_fwd(y, proj, norm_g, name):
    def fn(yv, z, gv):
        yz = yv * (z * _sigmoid(z))
        parts = [yz[:, g * NORM_GROUP_W:(g + 1) * NORM_GROUP_W] * r for g, r in enumerate(_group_rstd(yz))]
        return (jnp.concatenate(parts, axis=1) * gv,)
    return _rowwise(fn, [y, (proj, 0, SSD_D_INNER)], [norm_g.reshape(1, -1)], [_sds(y.shape, MXU_DT)], [],
                    name=name)[0]


def _gated_norm_bwd(y, proj, norm_g, dout, name):
    def fn(yv, z, do, gv):
        sg = _sigmoid(z)
        sz = z * sg
        yz = yv * sz
        dog = do * gv
        dyz, dg = [], []
        for g, r in enumerate(_group_rstd(yz)):
            cols = slice(g * NORM_GROUP_W, (g + 1) * NORM_GROUP_W)
            yzg, dogg = yz[:, cols], dog[:, cols]
            dyz.append(r * dogg - yzg * (r * r * r) * jnp.mean(yzg * dogg, axis=-1, keepdims=True))
            dg.append(jnp.sum(do[:, cols] * yzg * r, axis=0, keepdims=True))
        dyz = jnp.concatenate(dyz, axis=1)
        return dyz * sz, dyz * yv * (sg * (1.0 + z * (1.0 - sg))), jnp.concatenate(dg, axis=1)
    return _rowwise(fn, [y, (proj, 0, SSD_D_INNER), dout], [norm_g.reshape(1, -1)],
                    [_sds(y.shape), _sds(y.shape, MXU_DT)], [_sds((1, y.shape[1]))], name=name)


def _ssd_fwd(h, p):
    hn = _rms_fwd(h, p["norm_g"], "mix_norm_fwd")
    proj = _matmul(hn, p["w_zx"], name="ssd_in_zx")
    dt_raw = _matmul(hn, p["w_dt"], name="ssd_in_dt")
    xbc = _dwconv_silu_fwd(proj, SSD_D_INNER, SSD_CONV_DIM, p["conv_w"], p["conv_b"], "ssd_conv_fwd")
    dt, acs, acst = _ssd_prep(dt_raw, _lane_pad(p["dt_bias"]), _lane_pad(p["a_log"]), "ssd_prep")
    dtg, acsg = _heads_to_groups(dt), _heads_to_groups(acs)
    acstg = acst[:SSD_HEADS].reshape(SSD_GROUPS, SSD_HPG, -1)
    d_skip = jnp.repeat(p["d"], SSD_HEAD_DIM).reshape(1, -1)
    y, hprev = _ssd_scan_fwd(xbc, dtg, acsg, acstg, d_skip, "ssd_scan_fwd")
    yn = _gated_norm_fwd(y, proj, p["gnorm_g"], "ssd_gnorm_fwd")
    h_out = _matmul(yn, p["w_out"], res=h, name="ssd_out")
    return h_out, (hn, proj, dt_raw, xbc, dt, dtg, acsg, acstg, d_skip, y, hprev, yn)


def _ssd_bwd(h, dh, saved, p):
    hn, proj, dt_raw, xbc, dt, dtg, acsg, acstg, d_skip, y, hprev, yn = saved
    dh_b = dh.astype(MXU_DT)
    dyn = _matmul(dh_b, p["w_out"], tb=True, name="ssd_out_dx")
    dw_out = _matmul(yn, dh_b, ta=True, out_dtype=MXU_DT, name="ssd_out_dw")
    dy, dz, dgnorm = _gated_norm_bwd(y, proj, p["gnorm_g"], dyn, "ssd_gnorm_bwd")
    dx, dbm, dcm, daq, dar, ddtx, dd = _ssd_scan_bwd(xbc, dtg, acsg, acstg, d_skip, dy, hprev, "ssd_scan_bwd")
    dxbc = jnp.concatenate([dx, dbm, dcm], axis=1)
    dpre, dconv_w, dconv_b = _dwconv_silu_bwd(proj, SSD_D_INNER, SSD_CONV_DIM, dxbc, p["conv_w"], p["conv_b"],
                                              "ssd_conv_bwd")
    ddr, dalog, dbias = _ssd_post(_groups_to_heads(daq), _groups_to_heads(dar.transpose(0, 2, 1)),
                                  _groups_to_heads(ddtx), dt, dt_raw,
                                  _lane_pad(p["dt_bias"]), _lane_pad(p["a_log"]), "ssd_post")
    w_z, w_x = p["w_zx"][:, :SSD_D_INNER], p["w_zx"][:, SSD_D_INNER:]
    dhn = _matmul(dz, w_z, tb=True, name="ssd_in_dx_z")
    dhn = _matmul(dpre, w_x, tb=True, res=dhn, name="ssd_in_dx_x")
    dhn = _matmul(ddr, p["w_dt"], tb=True, res=dhn, name="ssd_in_dx_dt")
    dw_in = jnp.concatenate([_matmul(hn, dz, ta=True, out_dtype=MXU_DT, name="ssd_in_dw_z"),
                             _matmul(hn, dpre, ta=True, out_dtype=MXU_DT, name="ssd_in_dw_x"),
                             _matmul(hn, ddr, ta=True, out_dtype=MXU_DT, name="ssd_in_dw_dt")[:, :SSD_HEADS]], axis=1)
    dh_in, dnorm_g = _rms_bwd(h, p["norm_g"], dhn, dh, "mix_norm_bwd")
    grads = dict(norm_g=dnorm_g, w_in=dw_in, conv_w=dconv_w, conv_b=dconv_b, dt_bias=dbias[:, :SSD_HEADS],
                 a_log=dalog[:, :SSD_HEADS], d=dd.reshape(SSD_HEADS, SSD_HEAD_DIM).sum(axis=1).reshape(1, -1),
                 gnorm_g=dgnorm, w_out=dw_out)
    return dh_in, grads


FOX_PAIRS = FOX_HEADS // 2
ATT_TQ = 256
ATT_TK = 256
NEG_BIG = -1e30
FOX_SCALE = FOX_HEAD_DIM ** -0.5


def _head_sum_matrix():
    return (_iota2((LANES, LANES), 0) < FOX_HEAD_DIM) == (_iota2((LANES, LANES), 1) < FOX_HEAD_DIM)


def _head_sums(x):
    bd = _head_sum_matrix().astype(F32)
    parts = [jnp.dot(x[:, j * LANES:(j + 1) * LANES], bd, precision=HIGHEST, preferred_element_type=F32)
             for j in range(x.shape[1] // LANES)]
    return parts[0] if len(parts) == 1 else jnp.concatenate(parts, axis=1)


def _fox_prep_fwd(proj, f_raw, qg, kg, b_f, name):
    s = proj.shape[0]
    tr = min(ROW_TILE, s)

    def body(q_ref, k_ref, f_ref, qg_ref, kg_ref, b_ref, qn_ref, kn_ref, cum_ref, carry_ref):
        @pl.when(pl.program_id(0) == 0)
        def _():
            carry_ref[...] = jnp.zeros(carry_ref.shape, F32)

        for x_ref, g_ref, o_ref in ((q_ref, qg_ref, qn_ref), (k_ref, kg_ref, kn_ref)):
            x = x_ref[...]
            r = lax.rsqrt(_head_sums(x * x) * (1.0 / FOX_HEAD_DIM) + EPS)
            o_ref[...] = (x * r * g_ref[...]).astype(o_ref.dtype)
        x = f_ref[...] + b_ref[...]
        lf = jnp.minimum(x, 0.0) - jnp.log1p(jnp.exp(-jnp.abs(x)))
        lower = (_iota2((tr, tr), 0) >= _iota2((tr, tr), 1)).astype(F32)
        cum_ref[...] = jnp.dot(lower, lf, precision=HIGHEST, preferred_element_type=F32) + carry_ref[...]
        carry_ref[...] += jnp.sum(lf, axis=0, keepdims=True)

    wide = lambda cb: pl.BlockSpec((tr, FOX_D), lambda i: (i, cb))
    row = pl.BlockSpec((tr, LANES), lambda i: (i, 0))
    one = lambda n: pl.BlockSpec((1, n), lambda i: (0, 0))
    return pl.pallas_call(
        body, name=name, grid=(s // tr,), in_specs=[wide(0), wide(1), row, one(FOX_D), one(FOX_D), one(LANES)],
        out_specs=[wide(0), wide(0), row],
        out_shape=[_sds((s, FOX_D), MXU_DT), _sds((s, FOX_D), MXU_DT), _sds((s, LANES))],
        scratch_shapes=[pltpu.VMEM((1, LANES), F32)],
        compiler_params=_params("arbitrary"),
    )(proj, proj, f_raw, qg, kg, b_f)


def _fox_prep_bwd(proj, f_raw, qg, kg, b_f, dqn, dkn, dcum_q, dcum_k, name):
    s = proj.shape[0]
    tr = min(ROW_TILE, s)
    nb = s // tr

    def body(q_ref, k_ref, f_ref, qg_ref, kg_ref, b_ref, dqn_ref, dkn_ref, dcq_ref, dck_ref,
             dq_ref, dk_ref, df_ref, dqg_ref, dkg_ref, db_ref, carry_ref):
        @pl.when(pl.program_id(0) == 0)
        def _():
            carry_ref[...] = jnp.zeros(carry_ref.shape, F32)
            dqg_ref[...] = jnp.zeros(dqg_ref.shape, F32)
            dkg_ref[...] = jnp.zeros(dkg_ref.shape, F32)
            db_ref[...] = jnp.zeros(db_ref.shape, F32)

        for x_ref, g_ref, dy_ref, dx_ref, dg_ref in ((q_ref, qg_ref, dqn_ref, dq_ref, dqg_ref),
                                                     (k_ref, kg_ref, dkn_ref, dk_ref, dkg_ref)):
            x, dy = x_ref[...], dy_ref[...]
            r = lax.rsqrt(_head_sums(x * x) * (1.0 / FOX_HEAD_DIM) + EPS)
            dyg = dy * g_ref[...]
            dx = r * dyg - x * (r * r * r) * (_head_sums(x * dyg) * (1.0 / FOX_HEAD_DIM))
            dx_ref[...] = dx.astype(dx_ref.dtype)
            dg_ref[...] += jnp.sum(dy * x * r, axis=0, keepdims=True)
        dc = dcq_ref[...] + dck_ref[...]
        upper = (_iota2((tr, tr), 0) <= _iota2((tr, tr), 1)).astype(F32)
        dlf = jnp.dot(upper, dc, precision=HIGHEST, preferred_element_type=F32) + carry_ref[...]
        carry_ref[...] += jnp.sum(dc, axis=0, keepdims=True)
        df = dlf * _sigmoid(-(f_ref[...] + b_ref[...]))
        df_ref[...] = df.astype(df_ref.dtype)
        db_ref[...] += jnp.sum(df, axis=0, keepdims=True)

    wide = lambda cb: pl.BlockSpec((tr, FOX_D), lambda i: (nb - 1 - i, cb))
    row = pl.BlockSpec((tr, LANES), lambda i: (nb - 1 - i, 0))
    one = lambda n: pl.BlockSpec((1, n), lambda i: (0, 0))
    return pl.pallas_call(
        body, name=name, grid=(nb,),
        in_specs=[wide(0), wide(1), row, one(FOX_D), one(FOX_D), one(LANES), wide(0), wide(0), row, row],
        out_specs=[wide(0), wide(0), row, one(FOX_D), one(FOX_D), one(LANES)],
        out_shape=[_sds((s, FOX_D), MXU_DT), _sds((s, FOX_D), MXU_DT), _sds((s, LANES), MXU_DT),
                   _sds((1, FOX_D)), _sds((1, FOX_D)), _sds((1, LANES))],
        scratch_shapes=[pltpu.VMEM((1, LANES), F32)],
        compiler_params=_params("arbitrary"),
    )(proj, proj, f_raw, qg, kg, b_f, dqn, dkn, dcum_q, dcum_k)


def _cum_per_pair(cum):
    s = cum.shape[0]
    c = cum[:, :FOX_HEADS]
    cols = c.reshape(s, FOX_PAIRS, 2).transpose(1, 0, 2)
    rows = c.T.reshape(FOX_PAIRS, 2, s // ATT_TK, ATT_TK).transpose(0, 2, 1, 3)
    return cols, rows


def _fox_attn_fwd(qn, kn, proj, cumc, cumr, name):
    s = qn.shape[0]
    tq, tk = ATT_TQ, ATT_TK
    assert tq == tk and s % tq == 0
    vt, gt = 2 * FOX_D // LANES, 3 * FOX_D // LANES

    def body(q_ref, k_ref, v_ref, g_ref, cc_ref, cr_ref, o_ref, og_ref, lse_ref):
        qi = pl.program_id(1)
        lo = _iota2((tq, LANES), 1) < FOX_HEAD_DIM
        q = q_ref[...]
        qh = (jnp.where(lo, q, jnp.zeros_like(q)), jnp.where(lo, jnp.zeros_like(q), q))
        cc = cc_ref[...]
        qpos = qi * tq + _iota2((tq, tk), 0)
        kcol = _iota2((tq, tk), 1)

        def kv_step(j, carry):
            rows = pl.ds(pl.multiple_of(j * tk, tk), tk)
            kb, vb = k_ref[rows, :], v_ref[rows, :].astype(MXU_DT)
            cr = cr_ref[j]
            mask = qpos >= j * tk + kcol
            out = []
            for hh in range(2):
                m, l, acc = carry[3 * hh:3 * hh + 3]
                sc = _dot(qh[hh], kb, 1, 1) * FOX_SCALE + cc[:, hh:hh + 1] - cr[hh:hh + 1, :]
                sc = jnp.where(mask, sc, NEG_BIG)
                m_new = jnp.maximum(m, jnp.max(sc, axis=1, keepdims=True))
                pr = jnp.exp(sc - m_new)
                alpha = jnp.exp(m - m_new)
                out += [m_new, alpha * l + jnp.sum(pr, axis=1, keepdims=True),
                        alpha * acc + _dot(pr.astype(MXU_DT), vb, 1, 0)]
            return tuple(out)

        init = (jnp.full((tq, 1), NEG_BIG, F32), jnp.zeros((tq, 1), F32), jnp.zeros((tq, LANES), F32)) * 2
        m0, l0, a0, m1, l1, a1 = lax.fori_loop(0, qi + 1, kv_step, init)
        o = jnp.where(lo, a0 / l0, a1 / l1)
        o_ref[...] = o
        og_ref[...] = (o * _sigmoid(g_ref[...])).astype(og_ref.dtype)
        lse_ref[...] = jnp.where(lo, m0 + jnp.log(l0), m1 + jnp.log(l1))

    qblk = pl.BlockSpec((tq, LANES), lambda p, i: (i, p))
    seq = lambda t0: pl.BlockSpec((s, LANES), lambda p, i: (0, t0 + p))
    return pl.pallas_call(
        body, name=name, grid=(FOX_PAIRS, s // tq),
        in_specs=[qblk, seq(0), seq(vt), pl.BlockSpec((tq, LANES), lambda p, i: (i, gt + p)),
                  pl.BlockSpec((None, tq, 2), lambda p, i: (p, i, 0)),
                  pl.BlockSpec((None, s // tk, 2, tk), lambda p, i: (p, 0, 0, 0))],
        out_specs=[qblk, qblk, qblk],
        out_shape=[_sds((s, FOX_D)), _sds((s, FOX_D), MXU_DT), _sds((s, FOX_D))],
        compiler_params=_params("parallel", "parallel"),
    )(qn, kn, proj, proj, cumc, cumr)


def _fox_gate_bwd(dog, o, proj, name):
    def fn(dogv, ov, gate):
        sg = _sigmoid(gate)
        do = dogv * sg
        return do, dogv * ov * sg * (1.0 - sg), _head_sums(do * ov)
    return _rowwise(fn, [dog, o, (proj, 3, FOX_D)], [], [_sds(o.shape, MXU_DT), _sds(o.shape, MXU_DT), _sds(o.shape)],
                    [], name=name)


def _fox_attn_bwd(qn, kn, proj, do, lse, delta, cumc, cumr, name):
    s = qn.shape[0]
    tq, tk = ATT_TQ, ATT_TK
    nq, nk = s // tq, s // tk
    vt = 2 * FOX_D // LANES

    def body(q_ref, do_ref, lse_ref, dl_ref, cc_ref, k_ref, v_ref, cr_ref, dq_ref, drs_ref, dk_ref, dv_ref, dcr_ref):
        kj = pl.program_id(1)

        @pl.when(kj == 0)
        def _():
            dq_ref[...] = jnp.zeros(dq_ref.shape, F32)
            drs_ref[...] = jnp.zeros(drs_ref.shape, F32)

        lo = _iota2((tk, LANES), 1) < FOX_HEAD_DIM
        kb, vb = k_ref[...], v_ref[...].astype(MXU_DT)
        kh = (jnp.where(lo, kb, jnp.zeros_like(kb)), jnp.where(lo, jnp.zeros_like(kb), kb))
        cr = cr_ref[...]
        kpos = kj * tk + _iota2((tq, tk), 1)
        qrow = _iota2((tq, tk), 0)

        def q_step(i, carry):
            dk, dv, dc0, dc1 = carry
            rows = pl.ds(pl.multiple_of(i * tq, tq), tq)
            qb, dob = q_ref[rows, :], do_ref[rows, :]
            lse_b, dl_b, cc = lse_ref[rows, :], dl_ref[rows, :], cc_ref[rows, :]
            mask = i * tq + qrow >= kpos
            dq = jnp.zeros((tq, LANES), F32)
            dcs, drs = [], []
            for hh in range(2):
                lane0 = hh * FOX_HEAD_DIM
                q_h = jnp.where(lo, qb, jnp.zeros_like(qb)) if hh == 0 else jnp.where(lo, jnp.zeros_like(qb), qb)
                do_h = jnp.where(lo, dob, jnp.zeros_like(dob)) if hh == 0 else jnp.where(lo, jnp.zeros_like(dob), dob)
                sc = _dot(q_h, kb, 1, 1) * FOX_SCALE + cc[:, hh:hh + 1] - cr[hh:hh + 1, :]
                pr = jnp.where(mask, jnp.exp(sc - lse_b[:, lane0:lane0 + 1]), 0.0)
                prb = pr.astype(MXU_DT)
                dv = dv + _dot(prb, do_h, 0, 0)
                ds = pr * (_dot(do_h, vb, 1, 1) - dl_b[:, lane0:lane0 + 1])
                dcs.append(jnp.sum(ds, axis=0, keepdims=True))
                drs.append(jnp.sum(ds, axis=1, keepdims=True))
                dsb = ds.astype(MXU_DT)
                dq = dq + _dot(dsb, kh[hh], 1, 0)
                dk = dk + _dot(dsb, q_h, 0, 0)
            dq_ref[rows, :] += dq * FOX_SCALE
            drs_ref[rows, :] += jnp.where(lo, drs[0], drs[1])
            return dk, dv, dc0 - dcs[0], dc1 - dcs[1]

        zero = jnp.zeros((tk, LANES), F32)
        dk, dv, dc0, dc1 = lax.fori_loop(kj, nq, q_step, (zero, zero, jnp.zeros((1, tk), F32), jnp.zeros((1, tk), F32)))
        dk_ref[...] = dk * FOX_SCALE
        dv_ref[...] = dv.astype(dv_ref.dtype)
        dcr_ref[0:1, :] = dc0
        dcr_ref[1:2, :] = dc1

    seq = lambda t0: pl.BlockSpec((s, LANES), lambda p, j: (0, t0 + p))
    kblk = lambda t0: pl.BlockSpec((tk, LANES), lambda p, j: (j, t0 + p))
    crb = pl.BlockSpec((None, None, 2, tk), lambda p, j: (p, j, 0, 0))
    return pl.pallas_call(
        body, name=name, grid=(FOX_PAIRS, nk),
        in_specs=[seq(0), seq(0), seq(0), seq(0), pl.BlockSpec((None, s, 2), lambda p, j: (p, 0, 0)),
                  kblk(0), kblk(vt), crb],
        out_specs=[seq(0), seq(0), kblk(0), kblk(0), crb],
        out_shape=[_sds((s, FOX_D)), _sds((s, FOX_D)), _sds((s, FOX_D)), _sds((s, FOX_D), MXU_DT),
                   _sds((FOX_PAIRS, nk, 2, tk))],
        compiler_params=_params("parallel", "arbitrary"),
    )(qn, do, lse, delta, cumc, kn, proj, cumr)


def _fox_fwd(h, p):
    hn = _rms_fwd(h, p["norm_g"], "mix_norm_fwd")
    proj = _matmul(hn, p["w_qkvg"], name="fox_in_qkvg")
    f_raw = _matmul(hn, p["w_f"], name="fox_in_f")
    qg = jnp.tile(p["q_norm_g"], FOX_HEADS).reshape(1, -1)
    kg = jnp.tile(p["k_norm_g"], FOX_HEADS).reshape(1, -1)
    qn, kn, cum = _fox_prep_fwd(proj, f_raw, qg, kg, _lane_pad(p["b_f"]), "fox_prep_fwd")
    cumc, cumr = _cum_per_pair(cum)
    o, og, lse = _fox_attn_fwd(qn, kn, proj, cumc, cumr, "fox_attn_fwd")
    h_out = _matmul(og, p["w_out"], res=h, name="fox_out")
    return h_out, (hn, proj, f_raw, qg, kg, qn, kn, cumc, cumr, o, og, lse)


def _fox_bwd(h, dh, saved, p):
    hn, proj, f_raw, qg, kg, qn, kn, cumc, cumr, o, og, lse = saved
    s = h.shape[0]
    dh_b = dh.astype(MXU_DT)
    dog = _matmul(dh_b, p["w_out"], tb=True, name="fox_out_dx")
    dw_out = _matmul(og, dh_b, ta=True, out_dtype=MXU_DT, name="fox_out_dw")
    do, dgate, delta = _fox_gate_bwd(dog, o, proj, "fox_gate_bwd")
    dqn, drs, dkn, dv, dcr = _fox_attn_bwd(qn, kn, proj, do, lse, delta, cumc, cumr, "fox_attn_bwd")
    head_cols = lambda a: jnp.pad(a, ((0, 0), (0, LANES - FOX_HEADS)))
    dcum_q = head_cols(drs.reshape(s, FOX_HEADS, FOX_HEAD_DIM)[:, :, 0])
    dcum_k = head_cols(dcr.transpose(0, 2, 1, 3).reshape(FOX_HEADS, s).T)
    dq, dk, df, dqg, dkg, dbf = _fox_prep_bwd(proj, f_raw, qg, kg, _lane_pad(p["b_f"]), dqn, dkn, dcum_q, dcum_k,
                                              "fox_prep_bwd")
    dproj = jnp.concatenate([dq, dk, dv, dgate], axis=1)
    dhn = _matmul(dproj, p["w_qkvg"], tb=True, name="fox_in_dx_qkvg")
    dhn = _matmul(df, p["w_f"], tb=True, res=dhn, name="fox_in_dx_f")
    dw_in = jnp.concatenate([_matmul(hn, dproj, ta=True, out_dtype=MXU_DT, name="fox_in_dw_qkvg"),
                             _matmul(hn, df, ta=True, out_dtype=MXU_DT, name="fox_in_dw_f")[:, :FOX_HEADS]], axis=1)
    dh_in, dnorm_g = _rms_bwd(h, p["norm_g"], dhn, dh, "mix_norm_bwd")
    fold = lambda g: g.reshape(FOX_HEADS, FOX_HEAD_DIM).sum(axis=0).reshape(1, -1)
    grads = dict(norm_g=dnorm_g, w_in=dw_in, b_f=dbf[:, :FOX_HEADS], q_norm_g=fold(dqg), k_norm_g=fold(dkg),
                 w_out=dw_out)
    return dh_in, grads


HBM_SPEC = pl.BlockSpec(memory_space=pl.ANY)


def _my_index():
    return 4 * lax.axis_index("x") + 2 * lax.axis_index("y") + lax.axis_index("c")


def _peer(k):
    x, y, c = lax.axis_index("x"), lax.axis_index("y"), lax.axis_index("c")
    flip = lambda v, bit: 1 - v if bit else v
    return (flip(x, k & 4), flip(y, k & 2), flip(c, k & 1))


def _exchange(arrays, scatter, name):
    n = len(arrays)

    def body(*refs):
        ins, outs = refs[:n], refs[n:2 * n]
        send_sems, recv_sems, local_sems = refs[2 * n:]
        me = _my_index()
        local = []
        for a in range(n):
            local.append(pltpu.make_async_copy(ins[a].at[me] if scatter else ins[a], outs[a].at[me], local_sems.at[a]))
            local[-1].start()

        def copy(a, k, slot):
            return pltpu.make_async_remote_copy(
                src_ref=ins[a].at[jnp.bitwise_xor(me, k)] if scatter else ins[a], dst_ref=outs[a].at[slot],
                send_sem=send_sems.at[a, k - 1], recv_sem=recv_sems.at[a, k - 1],
                device_id=_peer(k), device_id_type=MESH_IDS)

        pairs = [(a, k) for a in range(n) for k in range(1, N_DEV)]
        for a, k in pairs:
            copy(a, k, me).start()
        for a, k in pairs:
            copy(a, k, jnp.bitwise_xor(me, k)).wait_recv()
        for a, k in pairs:
            copy(a, k, me).wait_send()
        for cp in local:
            cp.wait()

    out_shape = [_sds(a.shape if scatter else (N_DEV,) + a.shape, a.dtype) for a in arrays]
    return pl.pallas_call(
        body, name=name, in_specs=[HBM_SPEC] * n, out_specs=[HBM_SPEC] * n, out_shape=out_shape,
        scratch_shapes=[pltpu.SemaphoreType.DMA((n, N_DEV - 1)), pltpu.SemaphoreType.DMA((n, N_DEV - 1)),
                        pltpu.SemaphoreType.DMA((n,))],
    )(*arrays)


def _allreduce_small(buf, name):
    def body(in_ref, all_ref, sum_ref, send_sems, recv_sems):
        me = _my_index()
        all_ref[me] = in_ref[...]

        def copy(k, slot):
            return pltpu.make_async_remote_copy(
                src_ref=in_ref, dst_ref=all_ref.at[slot], send_sem=send_sems.at[k - 1], recv_sem=recv_sems.at[k - 1],
                device_id=_peer(k), device_id_type=MESH_IDS)

        for k in range(1, N_DEV):
            copy(k, me).start()
        for k in range(1, N_DEV):
            copy(k, jnp.bitwise_xor(me, k)).wait_recv()
        for k in range(1, N_DEV):
            copy(k, me).wait_send()
        acc = all_ref[0]
        for j in range(1, N_DEV):
            acc = acc + all_ref[j]
        sum_ref[...] = acc

    vmem = pl.BlockSpec(memory_space=pltpu.VMEM)
    return pl.pallas_call(
        body, name=name, in_specs=[vmem], out_specs=[vmem, vmem],
        out_shape=[_sds((N_DEV,) + buf.shape), _sds(buf.shape)],
        scratch_shapes=[pltpu.SemaphoreType.DMA((N_DEV - 1,)), pltpu.SemaphoreType.DMA((N_DEV - 1,))],
        compiler_params=pltpu.CompilerParams(vmem_limit_bytes=VMEM_LIMIT_BYTES),
    )(buf)[1]


def _adamw_math(w, g, m, v):
    m = ADAM_B1 * m + (1.0 - ADAM_B1) * g
    v = ADAM_B2 * v + (1.0 - ADAM_B2) * (g * g)
    m_hat = m / (1.0 - ADAM_B1 ** ADAM_STEP)
    v_hat = v / (1.0 - ADAM_B2 ** ADAM_STEP)
    return -ADAM_LR * (m_hat / (jnp.sqrt(v_hat) + ADAM_EPS) + ADAM_WD * w), m, v


def _row_tile(rows, cap=256, mult=16):
    best = None
    for t in range(mult, min(rows, cap) + 1, mult):
        if rows % t == 0:
            best = t
    assert best is not None, rows
    return best


def _adamw_sharded(w, m, v, partials, name):
    rows, cols = w.shape
    tr = _row_tile(rows)

    def body(w_ref, m_ref, v_ref, p_ref, g_ref, d_ref, nm_ref, nv_ref):
        g = p_ref[0].astype(F32)
        for j in range(1, N_DEV):
            g = g + p_ref[j].astype(F32)
        delta, m_new, v_new = _adamw_math(w_ref[...], g, m_ref[...], v_ref[...])
        g_ref[...], d_ref[...], nm_ref[...], nv_ref[...] = g, delta, m_new, v_new

    blk = pl.BlockSpec((tr, cols), lambda i: (i, 0))
    return pl.pallas_call(
        body, name=name, grid=(rows // tr,),
        in_specs=[blk, blk, blk, pl.BlockSpec((N_DEV, tr, cols), lambda i: (0, i, 0))],
        out_specs=[blk] * 4, out_shape=[_sds(w.shape)] * 4, compiler_params=_params("parallel"),
    )(w, m, v, partials)


def _adamw_small(w, g, m, v, name):
    def body(w_ref, g_ref, m_ref, v_ref, d_ref, nm_ref, nv_ref):
        d_ref[...], nm_ref[...], nv_ref[...] = _adamw_math(w_ref[...], g_ref[...], m_ref[...], v_ref[...])

    return pl.pallas_call(body, name=name, out_shape=[_sds(w.shape)] * 3,
                          compiler_params=_params())(w, g, m, v)


def _pack(arrays):
    flat = jnp.concatenate([a.reshape(-1).astype(F32) for a in arrays])
    pad = -flat.shape[0] % (8 * LANES)
    return jnp.pad(flat, (0, pad)).reshape(-1, LANES)


def _unpack(buf, shapes):
    flat, out, off = buf.reshape(-1), [], 0
    for shp in shapes:
        size = math.prod(shp)
        out.append(flat[off:off + size].reshape(shp))
        off += size
    return out


WEIGHTS = ["mix_norm_g", "ffn_norm_g", "ssd_w_in", "ssd_conv_w", "ssd_conv_b", "ssd_dt_bias", "ssd_a_log", "ssd_d",
           "ssd_norm_g", "ssd_w_out", "fox_w_in", "fox_b_f", "fox_q_norm_g", "fox_k_norm_g", "fox_w_out", "ffn_w_up",
           "ffn_conv_w", "ffn_conv_b", "ffn_w_down", "final_norm_g"]
BIG = ["ssd_w_in", "ssd_w_out", "fox_w_in", "fox_w_out", "ffn_w_up", "ffn_w_down"]
COLUMN_SHARDED = ["ssd_w_in", "fox_w_in", "ffn_w_up"]
CONV = ["ssd_conv_w", "ffn_conv_w"]
REPLICATED = [n for n in WEIGHTS if n not in BIG + CONV]
DEPTH = 4


def _full_weight(gathered, on_columns):
    _, nl, r, c = gathered.shape
    if on_columns:
        return gathered.transpose(1, 2, 0, 3).reshape(nl, r, N_DEV * c)
    return gathered.transpose(1, 0, 2, 3).reshape(nl, N_DEV * r, c)


def _to_shards(full, on_columns):
    nl, r, c = full.shape
    if on_columns:
        return full.reshape(nl, r, N_DEV, c // N_DEV).transpose(2, 0, 1, 3)
    return full.reshape(nl, N_DEV, r // N_DEV, c).transpose(1, 0, 2, 3)


def _pad_cols(w):
    return jnp.pad(w, ((0, 0), (0, LANES - w.shape[1])))


def kernel(x, mix_norm_g, ffn_norm_g, ssd_w_in, ssd_conv_w, ssd_conv_b, ssd_dt_bias, ssd_a_log, ssd_d, ssd_norm_g, ssd_w_out, fox_w_in, fox_b_f, fox_q_norm_g, fox_k_norm_g, fox_w_out, ffn_w_up, ffn_conv_w, ffn_conv_b, ffn_w_down, final_norm_g, loss_target, m_mix_norm_g, m_ffn_norm_g, m_ssd_w_in, m_ssd_conv_w, m_ssd_conv_b, m_ssd_dt_bias, m_ssd_a_log, m_ssd_d, m_ssd_norm_g, m_ssd_w_out, m_fox_w_in, m_fox_b_f, m_fox_q_norm_g, m_fox_k_norm_g, m_fox_w_out, m_ffn_w_up, m_ffn_conv_w, m_ffn_conv_b, m_ffn_w_down, m_final_norm_g, v_mix_norm_g, v_ffn_norm_g, v_ssd_w_in, v_ssd_conv_w, v_ssd_conv_b, v_ssd_dt_bias, v_ssd_a_log, v_ssd_d, v_ssd_norm_g, v_ssd_w_out, v_fox_w_in, v_fox_b_f, v_fox_q_norm_g, v_fox_k_norm_g, v_fox_w_out, v_ffn_w_up, v_ffn_conv_w, v_ffn_conv_b, v_ffn_w_down, v_final_norm_g):
    given = dict(locals())
    w = {n: given[n] for n in WEIGHTS}
    mom = {n: given["m_" + n] for n in WEIGHTS}
    var = {n: given["v_" + n] for n in WEIGHTS}
    me = _my_index()

    gathered = _exchange([w[n].astype(MXU_DT) for n in BIG] + [w[n] for n in CONV], False, "gather_weights")
    full = {n: _full_weight(g, n in COLUMN_SHARDED) for n, g in zip(BIG, gathered)}
    conv_full = {n: _full_weight(g, True) for n, g in zip(CONV, gathered[len(BIG):])}

    def ssd_params(i, j):
        w_in = full["ssd_w_in"][j]
        return dict(norm_g=w["mix_norm_g"][i], w_zx=w_in[:, :SSD_ZX], w_dt=_pad_cols(w_in[:, SSD_ZX:]),
                    conv_w=conv_full["ssd_conv_w"][j], conv_b=w["ssd_conv_b"][j], dt_bias=w["ssd_dt_bias"][j],
                    a_log=w["ssd_a_log"][j], d=w["ssd_d"][j], gnorm_g=w["ssd_norm_g"][j], w_out=full["ssd_w_out"][j])

    def fox_params(i, j):
        w_in = full["fox_w_in"][j]
        return dict(norm_g=w["mix_norm_g"][i], w_qkvg=w_in[:, :4 * FOX_D], w_f=_pad_cols(w_in[:, 4 * FOX_D:]),
                    b_f=w["fox_b_f"][j], q_norm_g=w["fox_q_norm_g"][j], k_norm_g=w["fox_k_norm_g"][j],
                    w_out=full["fox_w_out"][j])

    def ffn_params(i):
        return (w["ffn_norm_g"][i], full["ffn_w_up"][i], conv_full["ffn_conv_w"][i], w["ffn_conv_b"][i],
                full["ffn_w_down"][i])

    h = x[0]
    tape = []
    for i in range(DEPTH):
        j = i // 2
        if i % 2 == 0:
            mp = ssd_params(i, j)
            h_mid, mix_saved = _ssd_fwd(h, mp)
        else:
            mp = fox_params(i, j)
            h_mid, mix_saved = _fox_fwd(h, mp)
        fp = ffn_params(i)
        h_out, ffn_saved = _ffn_fwd(h_mid, *fp, "ffn")
        tape.append((h, mp, mix_saved, h_mid, fp, ffn_saved))
        h = h_out
    dh, dfinal_g, loss_part = _loss_head(h, w["final_norm_g"], loss_target[0], "loss_head")

    grads = {n: [None] * w[n].shape[0] for n in WEIGHTS if n != "final_norm_g"}
    for i in reversed(range(DEPTH)):
        j = i // 2
        h_in, mp, mix_saved, h_mid, fp, ffn_saved = tape[i]
        dh, g = _ffn_bwd(h_mid, dh, ffn_saved, *fp)
        grads["ffn_norm_g"][i], grads["ffn_w_up"][i], grads["ffn_conv_w"][i] = g["norm_g"][0], g["w_up"], g["conv_w"]
        grads["ffn_conv_b"][i], grads["ffn_w_down"][i] = g["conv_b"][0], g["w_down"]
        if i % 2 == 0:
            dh, g = _ssd_bwd(h_in, dh, mix_saved, mp)
            for key, name in (("w_in", "ssd_w_in"), ("conv_w", "ssd_conv_w"), ("w_out", "ssd_w_out")):
                grads[name][j] = g[key]
            for key, name in (("conv_b", "ssd_conv_b"), ("dt_bias", "ssd_dt_bias"), ("a_log", "ssd_a_log"),
                              ("d", "ssd_d"), ("gnorm_g", "ssd_norm_g")):
                grads[name][j] = g[key][0]
        else:
            dh, g = _fox_bwd(h_in, dh, mix_saved, mp)
            grads["fox_w_in"][j], grads["fox_w_out"][j] = g["w_in"], g["w_out"]
            for key, name in (("b_f", "fox_b_f"), ("q_norm_g", "fox_q_norm_g"), ("k_norm_g", "fox_k_norm_g")):
                grads[name][j] = g[key][0]
        grads["mix_norm_g"][i] = g["norm_g"][0]
    grads = {n: jnp.stack(v) for n, v in grads.items()}
    grads["final_norm_g"] = dfinal_g[0]

    small_names = REPLICATED + CONV
    summed = _unpack(_allreduce_small(_pack([grads[n] for n in small_names] + [loss_part]), "allreduce_small"),
                     [grads[n].shape for n in small_names] + [(1, 1)])
    loss = summed[-1][0, 0]
    g_small = dict(zip(small_names, summed[:-1]))
    for n in CONV:
        width = w[n].shape[-1]
        g_small[n] = lax.dynamic_slice_in_dim(g_small[n], me * width, width, axis=2)
    pk = lambda d: _pack([d[n] for n in small_names])
    d_small, m_small, v_small = _adamw_small(pk(w), pk(g_small), pk(mom), pk(var), "adamw_small")
    shapes = [w[n].shape for n in small_names]
    out_g, out_d, out_m, out_v = dict(g_small), {}, {}, {}
    for dst, buf in ((out_d, d_small), (out_m, m_small), (out_v, v_small)):
        dst.update(zip(small_names, _unpack(buf, shapes)))

    partials = _exchange([_to_shards(grads[n], n in COLUMN_SHARDED) for n in BIG], True, "scatter_gradients")
    for n, part in zip(BIG, partials):
        shp = w[n].shape
        flat = lambda a: a.reshape(shp[0] * shp[1], shp[2])
        res = _adamw_sharded(flat(w[n]), flat(mom[n]), flat(var[n]), part.reshape(N_DEV, shp[0] * shp[1], shp[2]),
                             "adamw_" + n)
        out_g[n], out_d[n], out_m[n], out_v[n] = [r.reshape(shp) for r in res]

    return (loss, dh[None], *[out_g[n] for n in WEIGHTS], *[out_d[n] for n in WEIGHTS],
            *[out_m[n] for n in WEIGHTS], *[out_v[n] for n in WEIGHTS])
```

```python
import functools
import math

import jax
import jax.numpy as jnp
from jax import lax
from jax.experimental import pallas as pl
from jax.experimental.pallas import tpu as pltpu

F32 = jnp.float32
MXU_DT = jnp.bfloat16
VMEM_LIMIT_BYTES = 56 * 1024 * 1024
LANES = 128
N_DEV = 8
MESH_IDS = pl.DeviceIdType.MESH

EPS = 1e-6
D_MODEL = 1024
SSD_D_INNER = 2048
SSD_HEAD_DIM = 64
SSD_HEADS = 32
SSD_GROUPS = 4
SSD_HPG = 8
SSD_STATE = 128
SSD_CONV = 4
SSD_CHUNK = 128
SSD_CONV_DIM = 3072
SSD_ZX = SSD_D_INNER + SSD_CONV_DIM
FOX_HEAD_DIM = 64
FOX_HEADS = 16
FOX_D = 1024
D_FF = 2816
FFN_CONV = 3
ADAM_LR, ADAM_B1, ADAM_B2, ADAM_EPS, ADAM_WD, ADAM_STEP = 0.001, 0.9, 0.999, 1e-08, 0.01, 10


def _params(*sem):
    return pltpu.CompilerParams(dimension_semantics=sem or None, vmem_limit_bytes=VMEM_LIMIT_BYTES)


def _sds(shape, dtype=F32):
    return jax.ShapeDtypeStruct(tuple(shape), dtype)


def _col_tile(n, cap=1536):
    best = None
    for t in range(LANES, min(n, cap) + 1, LANES):
        if n % t == 0:
            best = t
    assert best is not None, n
    return best


def _sigmoid(x):
    return 1.0 / (1.0 + jnp.exp(-x))


def _matmul(a, b, *, ta=False, tb=False, res=None, out_dtype=F32, tm=512, tn=None, name):
    (kdim, m) = a.shape if ta else a.shape[::-1]
    (n, k2) = b.shape if tb else b.shape[::-1]
    assert kdim == k2, (a.shape, b.shape, ta, tb)
    tm = min(tm, m)
    if m % tm:
        tm = _col_tile(m, tm)
    tn = tn or _col_tile(n)
    assert m % tm == 0 and n % tn == 0, (m, tm, n, tn)
    dims = (((0 if ta else 1,), (1 if tb else 0,)), ((), ()))

    def body(*refs):
        a_ref, b_ref = refs[0], refs[1]
        o_ref = refs[-1]
        acc = lax.dot_general(a_ref[...].astype(MXU_DT), b_ref[...].astype(MXU_DT), dims,
                              preferred_element_type=F32)
        if res is not None:
            acc = acc + refs[2][...].astype(F32)
        o_ref[...] = acc.astype(o_ref.dtype)

    a_spec = pl.BlockSpec((kdim, tm), lambda i, j: (0, i)) if ta else pl.BlockSpec((tm, kdim), lambda i, j: (i, 0))
    b_spec = pl.BlockSpec((tn, kdim), lambda i, j: (j, 0)) if tb else pl.BlockSpec((kdim, tn), lambda i, j: (0, j))
    o_spec = pl.BlockSpec((tm, tn), lambda i, j: (i, j))
    ins, specs = [a, b], [a_spec, b_spec]
    if res is not None:
        ins.append(res)
        specs.append(o_spec)
    return pl.pallas_call(
        body, name=name, grid=(m // tm, n // tn), in_specs=specs, out_specs=o_spec,
        out_shape=_sds((m, n), out_dtype), compiler_params=_params("parallel", "parallel"),
    )(*ins)


def _rowwise(fn, rows, consts, out_rows, out_sums, *, tr=256, name):
    rows = [r if isinstance(r, tuple) else (r, 0, r.shape[1]) for r in rows]
    s = rows[0][0].shape[0]
    tr = min(tr, s)
    assert s % tr == 0
    n_in, n_c, n_or = len(rows), len(consts), len(out_rows)

    def body(*refs):
        ins = [r[...] for r in refs[:n_in + n_c]]
        outs = fn(*ins)
        o_refs = refs[n_in + n_c:]
        for o_ref, val in zip(o_refs[:n_or], outs[:n_or]):
            o_ref[...] = val.astype(o_ref.dtype)
        if out_sums:
            first = pl.program_id(0) == 0

            @pl.when(first)
            def _():
                for o_ref, val in zip(o_refs[n_or:], outs[n_or:]):
                    o_ref[...] = val.astype(o_ref.dtype)

            @pl.when(jnp.logical_not(first))
            def _():
                for o_ref, val in zip(o_refs[n_or:], outs[n_or:]):
                    o_ref[...] += val.astype(o_ref.dtype)

    in_specs = [pl.BlockSpec((tr, width), functools.partial(lambda i, cb: (i, cb), cb=cb)) for _, cb, width in rows]
    in_specs += [pl.BlockSpec(c.shape, lambda i: (0, 0)) for c in consts]
    out_specs = [pl.BlockSpec((tr, o.shape[1]), lambda i: (i, 0)) for o in out_rows]
    out_specs += [pl.BlockSpec(o.shape, lambda i: (0, 0)) for o in out_sums]
    return pl.pallas_call(
        body, name=name, grid=(s // tr,), in_specs=in_specs, out_specs=out_specs,
        out_shape=list(out_rows) + list(out_sums),
        compiler_params=_params("arbitrary" if out_sums else "parallel"),
    )(*[r[0] for r in rows], *consts)


def _rms_fwd(h, g, name):
    def fn(x, gv):
        r = lax.rsqrt(jnp.mean(x * x, axis=-1, keepdims=True) + EPS)
        return (x * r * gv,)
    return _rowwise(fn, [h], [g.reshape(1, -1)], [_sds(h.shape, MXU_DT)], [], name=name)[0]


def _rms_bwd(h, g, dy, dres, name):
    def fn(x, dyv, dr, gv):
        r = lax.rsqrt(jnp.mean(x * x, axis=-1, keepdims=True) + EPS)
        dyg = dyv * gv
        dx = dr + r * dyg - x * (r * r * r) * jnp.mean(x * dyg, axis=-1, keepdims=True)
        return dx, dx, jnp.sum(dyv * x * r, axis=0, keepdims=True)
    return _rowwise(fn, [h, dy, dres], [g.reshape(1, -1)], [_sds(h.shape), _sds(h.shape, MXU_DT)],
                    [_sds((1, h.shape[1]))], name=name)


def _loss_head(h, g, target, name):
    c = h.shape[1]

    def fn(x, t, gv):
        r = lax.rsqrt(jnp.mean(x * x, axis=-1, keepdims=True) + EPS)
        y = x * r * gv
        err = y - t
        dyv = err * (1.0 / c)
        dyg = dyv * gv
        dx = r * dyg - x * (r * r * r) * jnp.mean(x * dyg, axis=-1, keepdims=True)
        loss = 0.5 * jnp.sum(jnp.mean(err * err, axis=-1, keepdims=True), axis=0, keepdims=True)
        return dx, dx, jnp.sum(dyv * x * r, axis=0, keepdims=True), loss
    return _rowwise(fn, [h, target], [g.reshape(1, -1)], [_sds(h.shape), _sds(h.shape, MXU_DT)],
                    [_sds((1, c)), _sds((1, 1))], name=name)


PAD_ROWS = 8
ROW_TILE = 512


def _shifted_conv(xp_ref, w, r0, tr, kw):
    acc = None
    for k in range(kw):
        xk = xp_ref[pl.ds(PAD_ROWS + r0 - (kw - 1) + k, tr), :]
        term = xk * w[k:k + 1, :]
        acc = term if acc is None else acc + term
    return acc


def _convglu_fwd(u, conv_w, conv_b, name):
    s = u.shape[0]
    nt = D_FF // LANES
    tr = min(ROW_TILE, s)

    def body(ug_ref, uv_ref, w_ref, b_ref, act_ref, xp_ref):
        xp_ref[pl.ds(0, PAD_ROWS), :] = jnp.zeros((PAD_ROWS, LANES), F32)
        xp_ref[pl.ds(PAD_ROWS, s), :] = ug_ref[...]
        w = w_ref[...]
        b = b_ref[...]
        for r0 in range(0, s, tr):
            gate = _shifted_conv(xp_ref, w, r0, tr, FFN_CONV) + b
            act = gate * _sigmoid(gate) * uv_ref[pl.ds(r0, tr), :]
            act_ref[pl.ds(r0, tr), :] = act.astype(act_ref.dtype)

    return pl.pallas_call(
        body, name=name, grid=(nt,),
        in_specs=[pl.BlockSpec((s, LANES), lambda j: (0, j)), pl.BlockSpec((s, LANES), lambda j: (0, nt + j)),
                  pl.BlockSpec((FFN_CONV, LANES), lambda j: (0, j)), pl.BlockSpec((1, LANES), lambda j: (0, j))],
        out_specs=pl.BlockSpec((s, LANES), lambda j: (0, j)),
        out_shape=_sds((s, D_FF), MXU_DT),
        scratch_shapes=[pltpu.VMEM((s + PAD_ROWS, LANES), F32)],
        compiler_params=_params("parallel"),
    )(u, u, conv_w, conv_b.reshape(1, -1))


def _convglu_bwd(u, dact, conv_w, conv_b, name):
    s = u.shape[0]
    nt = D_FF // LANES
    tr = min(ROW_TILE, s)
    kw = FFN_CONV

    def body(ug_ref, uv_ref, da_ref, w_ref, b_ref, dug_ref, duv_ref, dw_ref, db_ref, xp_ref, dgp_ref):
        xp_ref[pl.ds(0, PAD_ROWS), :] = jnp.zeros((PAD_ROWS, LANES), F32)
        xp_ref[pl.ds(PAD_ROWS, s), :] = ug_ref[...]
        dgp_ref[pl.ds(s, PAD_ROWS), :] = jnp.zeros((PAD_ROWS, LANES), F32)
        w = w_ref[...]
        b = b_ref[...]
        dw = [jnp.zeros((1, LANES), F32) for _ in range(kw)]
        db = jnp.zeros((1, LANES), F32)
        for r0 in range(0, s, tr):
            gate = _shifted_conv(xp_ref, w, r0, tr, kw) + b
            sg = _sigmoid(gate)
            da = da_ref[pl.ds(r0, tr), :].astype(F32)
            duv_ref[pl.ds(r0, tr), :] = (da * gate * sg).astype(duv_ref.dtype)
            dgate = da * uv_ref[pl.ds(r0, tr), :] * (sg * (1.0 + gate * (1.0 - sg)))
            dgp_ref[pl.ds(r0, tr), :] = dgate
            db = db + jnp.sum(dgate, axis=0, keepdims=True)
            for k in range(kw):
                xk = xp_ref[pl.ds(PAD_ROWS + r0 - (kw - 1) + k, tr), :]
                dw[k] = dw[k] + jnp.sum(dgate * xk, axis=0, keepdims=True)
        for r0 in range(0, s, tr):
            acc = None
            for k in range(kw):
                term = dgp_ref[pl.ds(r0 + (kw - 1) - k, tr), :] * w[k:k + 1, :]
                acc = term if acc is None else acc + term
            dug_ref[pl.ds(r0, tr), :] = acc.astype(dug_ref.dtype)
        for k in range(kw):
            dw_ref[pl.ds(k, 1), :] = dw[k]
        db_ref[...] = db

    col = lambda j: (0, j)
    return pl.pallas_call(
        body, name=name, grid=(nt,),
        in_specs=[pl.BlockSpec((s, LANES), col), pl.BlockSpec((s, LANES), lambda j: (0, nt + j)),
                  pl.BlockSpec((s, LANES), col), pl.BlockSpec((kw, LANES), col), pl.BlockSpec((1, LANES), col)],
        out_specs=[pl.BlockSpec((s, LANES), col), pl.BlockSpec((s, LANES), col),
                   pl.BlockSpec((kw, LANES), col), pl.BlockSpec((1, LANES), col)],
        out_shape=[_sds((s, D_FF), MXU_DT), _sds((s, D_FF), MXU_DT), _sds((kw, D_FF)), _sds((1, D_FF))],
        scratch_shapes=[pltpu.VMEM((s + PAD_ROWS, LANES), F32), pltpu.VMEM((s + PAD_ROWS, LANES), F32)],
        compiler_params=_params("parallel"),
    )(u, u, dact, conv_w, conv_b.reshape(1, -1))


def _ffn_fwd(h, norm_g, w_up, conv_w, conv_b, w_down, tag):
    hf = _rms_fwd(h, norm_g, f"ffn_norm_fwd")
    u = _matmul(hf, w_up, name="ffn_up")
    act = _convglu_fwd(u, conv_w, conv_b, "convglu_fwd")
    h_out = _matmul(act, w_down, res=h, name="ffn_down")
    return h_out, (hf, u, act)


def _ffn_bwd(h, dh, saved, norm_g, w_up, conv_w, conv_b, w_down):
    hf, u, act = saved
    dh, dh_b = dh
    dact = _matmul(dh_b, w_down, tb=True, name="ffn_down_dx")
    dw_down = _matmul(act, dh_b, ta=True, out_dtype=MXU_DT, name="ffn_down_dw")
    dug, duv, dconv_w, dconv_b = _convglu_bwd(u, dact, conv_w, conv_b, "convglu_bwd")
    dhf = _matmul(dug, w_up[:, :D_FF], tb=True, name="ffn_up_dx_gate")
    dhf = _matmul(duv, w_up[:, D_FF:], tb=True, res=dhf, name="ffn_up_dx_val")
    dw_up = jnp.concatenate([_matmul(hf, dug, ta=True, out_dtype=MXU_DT, name="ffn_up_dw_gate"),
                             _matmul(hf, duv, ta=True, out_dtype=MXU_DT, name="ffn_up_dw_val")], axis=1)
    *dh_in, dnorm_g = _rms_bwd(h, norm_g, dhf, dh, "ffn_norm_bwd")
    return dh_in, dict(norm_g=dnorm_g, w_up=dw_up, conv_w=dconv_w, conv_b=dconv_b, w_down=dw_down)


def _dwconv_silu_fwd(proj, col0, n_ch, conv_w, conv_b, name):
    s = proj.shape[0]
    nt, t0, kw = n_ch // LANES, col0 // LANES, conv_w.shape[0]
    tr = min(ROW_TILE, s)

    def body(x_ref, w_ref, b_ref, o_ref, xp_ref):
        xp_ref[pl.ds(0, PAD_ROWS), :] = jnp.zeros((PAD_ROWS, LANES), F32)
        xp_ref[pl.ds(PAD_ROWS, s), :] = x_ref[...]
        w = w_ref[...]
        b = b_ref[...]
        for r0 in range(0, s, tr):
            pre = _shifted_conv(xp_ref, w, r0, tr, kw) + b
            o_ref[pl.ds(r0, tr), :] = pre * _sigmoid(pre)

    col = lambda j: (0, j)
    return pl.pallas_call(
        body, name=name, grid=(nt,),
        in_specs=[pl.BlockSpec((s, LANES), lambda j: (0, t0 + j)), pl.BlockSpec((kw, LANES), col),
                  pl.BlockSpec((1, LANES), col)],
        out_specs=pl.BlockSpec((s, LANES), col), out_shape=_sds((s, n_ch)),
        scratch_shapes=[pltpu.VMEM((s + PAD_ROWS, LANES), F32)],
        compiler_params=_params("parallel"),
    )(proj, conv_w, conv_b.reshape(1, -1))


def _dwconv_silu_bwd(proj, col0, n_ch, dout, conv_w, conv_b, name):
    s = proj.shape[0]
    nt, t0, kw = n_ch // LANES, col0 // LANES, conv_w.shape[0]
    tr = min(ROW_TILE, s)

    def body(x_ref, do_ref, w_ref, b_ref, dx_ref, dw_ref, db_ref, xp_ref, dgp_ref):
        xp_ref[pl.ds(0, PAD_ROWS), :] = jnp.zeros((PAD_ROWS, LANES), F32)
        xp_ref[pl.ds(PAD_ROWS, s), :] = x_ref[...]
        dgp_ref[pl.ds(s, PAD_ROWS), :] = jnp.zeros((PAD_ROWS, LANES), F32)
        w = w_ref[...]
        b = b_ref[...]
        dw = [jnp.zeros((1, LANES), F32) for _ in range(kw)]
        db = jnp.zeros((1, LANES), F32)
        for r0 in range(0, s, tr):
            pre = _shifted_conv(xp_ref, w, r0, tr, kw) + b
            sg = _sigmoid(pre)
            dpre = do_ref[pl.ds(r0, tr), :] * (sg * (1.0 + pre * (1.0 - sg)))
            dgp_ref[pl.ds(r0, tr), :] = dpre
            db = db + jnp.sum(dpre, axis=0, keepdims=True)
            for k in range(kw):
                xk = xp_ref[pl.ds(PAD_ROWS + r0 - (kw - 1) + k, tr), :]
                dw[k] = dw[k] + jnp.sum(dpre * xk, axis=0, keepdims=True)
        for r0 in range(0, s, tr):
            acc = None
            for k in range(kw):
                term = dgp_ref[pl.ds(r0 + (kw - 1) - k, tr), :] * w[k:k + 1, :]
                acc = term if acc is None else acc + term
            dx_ref[pl.ds(r0, tr), :] = acc.astype(dx_ref.dtype)
        for k in range(kw):
            dw_ref[pl.ds(k, 1), :] = dw[k]
        db_ref[...] = db

    col = lambda j: (0, j)
    return pl.pallas_call(
        body, name=name, grid=(nt,),
        in_specs=[pl.BlockSpec((s, LANES), lambda j: (0, t0 + j)), pl.BlockSpec((s, LANES), col),
                  pl.BlockSpec((kw, LANES), col), pl.BlockSpec((1, LANES), col)],
        out_specs=[pl.BlockSpec((s, LANES), col), pl.BlockSpec((kw, LANES), col), pl.BlockSpec((1, LANES), col)],
        out_shape=[_sds((s, n_ch), MXU_DT), _sds((kw, n_ch)), _sds((1, n_ch))],
        scratch_shapes=[pltpu.VMEM((s + PAD_ROWS, LANES), F32), pltpu.VMEM((s + PAD_ROWS, LANES), F32)],
        compiler_params=_params("parallel"),
    )(proj, dout, conv_w, conv_b.reshape(1, -1))


HIGHEST = lax.Precision.HIGHEST
PAIRS = SSD_HPG // 2
PAIR_W = 2 * SSD_HEAD_DIM
GROUP_W = SSD_HPG * SSD_HEAD_DIM


def _iota2(shape, axis):
    return lax.broadcasted_iota(jnp.int32, shape, axis)


def _lane_pad(v):
    return jnp.pad(v.reshape(1, -1), ((0, 0), (0, LANES - v.shape[0])))


def _heads_to_groups(a):
    s = a.shape[0]
    return a[:, :SSD_HEADS].reshape(s, SSD_GROUPS, SSD_HPG).transpose(1, 0, 2)


def _groups_to_heads(a):
    s = a.shape[1]
    return jnp.pad(a.transpose(1, 0, 2).reshape(s, SSD_HEADS), ((0, 0), (0, LANES - SSD_HEADS)))


def _ssd_prep(dt_raw, dt_bias, a_log, name):
    s = dt_raw.shape[0]
    lc = SSD_CHUNK

    def body(x_ref, b_ref, al_ref, dt_ref, acs_ref, acst_ref):
        x = x_ref[...] + b_ref[...]
        dt = jnp.maximum(x, 0.0) + jnp.log1p(jnp.exp(-jnp.abs(x)))
        da = dt * (-jnp.exp(al_ref[...]))
        lower = (_iota2((lc, lc), 0) >= _iota2((lc, lc), 1)).astype(F32)
        upper = (_iota2((lc, lc), 0) <= _iota2((lc, lc), 1)).astype(F32)
        dt_ref[...] = dt
        acs_ref[...] = jnp.dot(lower, da, precision=HIGHEST, preferred_element_type=F32)
        acst_ref[...] = lax.dot_general(da, upper, (((0,), (0,)), ((), ())), precision=HIGHEST,
                                        preferred_element_type=F32)

    row = pl.BlockSpec((lc, LANES), lambda c: (c, 0))
    one = pl.BlockSpec((1, LANES), lambda c: (0, 0))
    return pl.pallas_call(
        body, name=name, grid=(s // lc,), in_specs=[row, one, one],
        out_specs=[row, row, pl.BlockSpec((LANES, lc), lambda c: (0, c))],
        out_shape=[_sds((s, LANES)), _sds((s, LANES)), _sds((LANES, s))],
        compiler_params=_params("parallel"),
    )(dt_raw, dt_bias, a_log)


def _pair_cols(v, p, lo):
    return jnp.where(lo, v[:, 2 * p:2 * p + 1], v[:, 2 * p + 1:2 * p + 2])


def _decay_matrix(acs, acst, h, tri):
    return jnp.exp(jnp.where(tri, acs[:, h:h + 1] - acst[h:h + 1, :], -jnp.inf))


def _dot(a, b, ca, cb):
    return lax.dot_general(a, b, (((ca,), (cb,)), ((), ())), preferred_element_type=F32)


def _ssd_scan_fwd(xbc, dtg, acsg, acstg, d_skip, name):
    s = xbc.shape[0]
    lc, nc = SSD_CHUNK, s // SSD_CHUNK
    xt, bt = GROUP_W // LANES, SSD_D_INNER // LANES

    def body(x_ref, b_ref, c_ref, dt_ref, acs_ref, acst_ref, dsk_ref, y_ref, hp_ref, st_ref):
        @pl.when(pl.program_id(1) == 0)
        def _():
            st_ref[...] = jnp.zeros(st_ref.shape, F32)

        bm, cmb = b_ref[...], c_ref[...].astype(MXU_DT)
        dt, acs, acst = dt_ref[...], acs_ref[...], acst_ref[...]
        cb = _dot(cmb, bm.astype(MXU_DT), 1, 1)
        tri = _iota2((lc, lc), 0) >= _iota2((lc, lc), 1)
        lo = _iota2((lc, PAIR_W), 1) < SSD_HEAD_DIM
        a_last = acs[lc - 1:lc, :]
        e_acs, e_ds, e_cd = jnp.exp(acs), jnp.exp(a_last - acs), jnp.exp(a_last)
        for p in range(PAIRS):
            sl = pl.ds(p * PAIR_W, PAIR_W)
            xp = x_ref[:, sl]
            ub = (xp * _pair_cols(dt, p, lo)).astype(MXU_DT)
            m0 = (cb * _decay_matrix(acs, acst, 2 * p, tri)).astype(MXU_DT)
            m1 = (cb * _decay_matrix(acs, acst, 2 * p + 1, tri)).astype(MXU_DT)
            ht = st_ref[p]
            hp_ref[p] = ht
            y = jnp.where(lo, _dot(m0, ub, 1, 0), _dot(m1, ub, 1, 0))
            y = y + _dot(cmb, ht.astype(MXU_DT), 1, 0) * _pair_cols(e_acs, p, lo)
            y_ref[:, sl] = y + xp * dsk_ref[:, sl]
            bd0 = (bm * e_ds[:, 2 * p:2 * p + 1]).astype(MXU_DT)
            bd1 = (bm * e_ds[:, 2 * p + 1:2 * p + 2]).astype(MXU_DT)
            st_ref[p] = ht * _pair_cols(e_cd, p, lo[:1]) + jnp.where(lo, _dot(bd0, ub, 0, 0), _dot(bd1, ub, 0, 0))

    small = pl.BlockSpec((None, lc, SSD_HPG), lambda g, c: (g, c, 0))
    return pl.pallas_call(
        body, name=name, grid=(SSD_GROUPS, nc),
        in_specs=[pl.BlockSpec((lc, GROUP_W), lambda g, c: (c, g)),
                  pl.BlockSpec((lc, LANES), lambda g, c: (c, bt + g)),
                  pl.BlockSpec((lc, LANES), lambda g, c: (c, bt + SSD_GROUPS + g)),
                  small, small, pl.BlockSpec((None, SSD_HPG, lc), lambda g, c: (g, 0, c)),
                  pl.BlockSpec((1, GROUP_W), lambda g, c: (0, g))],
        out_specs=[pl.BlockSpec((lc, GROUP_W), lambda g, c: (c, g)),
                   pl.BlockSpec((None, PAIRS, SSD_STATE, PAIR_W), lambda g, c: (c, g, 0, 0))],
        out_shape=[_sds((s, SSD_D_INNER)), _sds((nc, SSD_GROUPS * PAIRS, SSD_STATE, PAIR_W))],
        scratch_shapes=[pltpu.VMEM((PAIRS, SSD_STATE, PAIR_W), F32)],
        compiler_params=_params("parallel", "arbitrary"),
    )(xbc, xbc, xbc, dtg, acsg, acstg, d_skip)


def _ssd_scan_bwd(xbc, dtg, acsg, acstg, d_skip, dy, hprev, name):
    s = xbc.shape[0]
    lc, nc = SSD_CHUNK, s // SSD_CHUNK
    bt = SSD_D_INNER // LANES

    def body(x_ref, b_ref, c_ref, dt_ref, acs_ref, acst_ref, dsk_ref, dy_ref, hp_ref, hn_ref,
             dx_ref, db_ref, dc_ref, daq_ref, dar_ref, ddtx_ref, dd_ref, dst_ref, ta_ref, tx_ref):
        @pl.when(pl.program_id(1) == 0)
        def _():
            dst_ref[...] = jnp.zeros(dst_ref.shape, F32)
            dd_ref[...] = jnp.zeros(dd_ref.shape, F32)

        bm, cmb = b_ref[...], c_ref[...].astype(MXU_DT)
        bmb = bm.astype(MXU_DT)
        dt, acs, acst = dt_ref[...], acs_ref[...], acst_ref[...]
        cb = _dot(cmb, bmb, 1, 1)
        tri = _iota2((lc, lc), 0) >= _iota2((lc, lc), 1)
        lo = _iota2((lc, PAIR_W), 1) < SSD_HEAD_DIM
        a_last = acs[lc - 1:lc, :]
        e_acs, e_ds, e_cd = jnp.exp(acs), jnp.exp(a_last - acs), jnp.exp(a_last)
        dcb = jnp.zeros((lc, lc), F32)
        dc_x = jnp.zeros((lc, SSD_STATE), F32)
        db_x = jnp.zeros((lc, SSD_STATE), F32)
        da_in = jnp.zeros((lc, LANES), F32)
        da_out = jnp.zeros((SSD_HPG, lc), F32)
        head_col = _iota2((lc, LANES), 1)
        head_row = _iota2((SSD_HPG, lc), 0)
        last = _iota2((SSD_HPG, lc), 1) == lc - 1
        for p in range(PAIRS):
            sl = pl.ds(p * PAIR_W, PAIR_W)
            xp, dyp, dsk = x_ref[:, sl], dy_ref[:, sl], dsk_ref[:, sl]
            dtp = _pair_cols(dt, p, lo)
            u = xp * dtp
            ub, dyb = u.astype(MXU_DT), dyp.astype(MXU_DT)
            lmat = (_decay_matrix(acs, acst, 2 * p, tri), _decay_matrix(acs, acst, 2 * p + 1, tri))
            m0, m1 = (cb * lmat[0]).astype(MXU_DT), (cb * lmat[1]).astype(MXU_DT)
            ea, dsl = _pair_cols(e_acs, p, lo), _pair_cols(e_ds, p, lo)
            dht, ht = dst_ref[p], hp_ref[p]
            dhtb, htb = dht.astype(MXU_DT), ht.astype(MXU_DT)
            bd0 = (bm * e_ds[:, 2 * p:2 * p + 1]).astype(MXU_DT)
            bd1 = (bm * e_ds[:, 2 * p + 1:2 * p + 2]).astype(MXU_DT)
            du_state = jnp.where(lo, _dot(bd0, dhtb, 1, 0), _dot(bd1, dhtb, 1, 0))
            du = jnp.where(lo, _dot(m0, dyb, 0, 0), _dot(m1, dyb, 0, 0)) + du_state
            y_off = _dot(cmb, htb, 1, 0) * ea
            ta_ref[:, sl] = dyp * y_off - u * du_state
            tx_ref[:, sl] = du * xp
            dx_ref[:, sl] = dtp * du + dsk * dyp
            dd_ref[:, sl] += jnp.sum(dyp * xp, axis=0, keepdims=True)
            dy_h = (jnp.where(lo, dyp, 0.0).astype(MXU_DT), jnp.where(lo, 0.0, dyp).astype(MXU_DT))
            carry = jnp.sum(dht * hn_ref[p], axis=0, keepdims=True)
            for hh in range(2):
                h = 2 * p + hh
                dml = _dot(dy_h[hh], ub, 1, 1) * lmat[hh]
                dcb = dcb + dml
                flow = cb * dml
                da_in = da_in + jnp.where(head_col == h, jnp.sum(flow, axis=1, keepdims=True), 0.0)
                through = jnp.sum(jnp.where(lo[:1] == (hh == 0), carry, 0.0), axis=1, keepdims=True)
                da_out = da_out + jnp.where(head_row == h, jnp.sum(flow, axis=0, keepdims=True)
                                            - jnp.where(last, through, 0.0), 0.0)
            dye = (dyp * ea).astype(MXU_DT)
            dc_x = dc_x + _dot(dye, htb, 1, 1)
            db_x = db_x + _dot((u * dsl).astype(MXU_DT), dhtb, 1, 1)
            dst_ref[p] = dht * _pair_cols(e_cd, p, lo[:1]) + _dot(cmb, dye, 0, 0)
        dcbb = dcb.astype(MXU_DT)
        dc_ref[...] = _dot(dcbb, bmb, 1, 0) + dc_x
        db_ref[...] = _dot(dcbb, cmb, 0, 0) + db_x
        seg_lo = _iota2((GROUP_W, LANES), 1) * SSD_HEAD_DIM
        chan = _iota2((GROUP_W, LANES), 0)
        seg = jnp.logical_and(chan >= seg_lo, chan < seg_lo + SSD_HEAD_DIM).astype(F32)
        da_in = da_in + jnp.dot(ta_ref[...], seg, precision=HIGHEST, preferred_element_type=F32)
        daq_ref[...] = da_in[:, :SSD_HPG]
        dar_ref[...] = da_out
        ddtx_ref[...] = jnp.dot(tx_ref[...], seg, precision=HIGHEST, preferred_element_type=F32)[:, :SSD_HPG]

    rev = lambda c: nc - 1 - c
    small = pl.BlockSpec((None, lc, SSD_HPG), lambda g, c: (g, rev(c), 0))
    small_t = pl.BlockSpec((None, SSD_HPG, lc), lambda g, c: (g, 0, rev(c)))
    wide = pl.BlockSpec((lc, GROUP_W), lambda g, c: (rev(c), g))
    state = lambda at: pl.BlockSpec((None, PAIRS, SSD_STATE, PAIR_W), lambda g, c: (at(c), g, 0, 0))
    return pl.pallas_call(
        body, name=name, grid=(SSD_GROUPS, nc),
        in_specs=[pl.BlockSpec((lc, GROUP_W), lambda g, c: (rev(c), g)),
                  pl.BlockSpec((lc, LANES), lambda g, c: (rev(c), bt + g)),
                  pl.BlockSpec((lc, LANES), lambda g, c: (rev(c), bt + SSD_GROUPS + g)),
                  small, small, small_t, pl.BlockSpec((1, GROUP_W), lambda g, c: (0, g)), wide,
                  state(rev), state(lambda c: jnp.minimum(rev(c) + 1, nc - 1))],
        out_specs=[wide, pl.BlockSpec((lc, LANES), lambda g, c: (rev(c), g)),
                   pl.BlockSpec((lc, LANES), lambda g, c: (rev(c), g)), small, small_t, small,
                   pl.BlockSpec((1, GROUP_W), lambda g, c: (0, g))],
        out_shape=[_sds((s, SSD_D_INNER)), _sds((s, SSD_GROUPS * SSD_STATE)), _sds((s, SSD_GROUPS * SSD_STATE)),
                   _sds((SSD_GROUPS, s, SSD_HPG)), _sds((SSD_GROUPS, SSD_HPG, s)), _sds((SSD_GROUPS, s, SSD_HPG)),
                   _sds((1, SSD_D_INNER))],
        scratch_shapes=[pltpu.VMEM((PAIRS, SSD_STATE, PAIR_W), F32), pltpu.VMEM((lc, GROUP_W), F32),
                        pltpu.VMEM((lc, GROUP_W), F32)],
        compiler_params=_params("parallel", "arbitrary"),
    )(xbc, xbc, xbc, dtg, acsg, acstg, d_skip, dy, hprev, hprev)


def _ssd_post(da_in, da_out, ddtx, dt, dt_raw, dt_bias, a_log, name):
    s = da_in.shape[0]
    lc = SSD_CHUNK

    def body(dain_ref, daout_ref, ddtx_ref, dt_ref, x_ref, b_ref, al_ref, ddr_ref, dal_ref, dbias_ref):
        @pl.when(pl.program_id(0) == 0)
        def _():
            dal_ref[...] = jnp.zeros(dal_ref.shape, F32)
            dbias_ref[...] = jnp.zeros(dbias_ref.shape, F32)

        upper = (_iota2((lc, lc), 0) <= _iota2((lc, lc), 1)).astype(F32)
        dda = jnp.dot(upper, dain_ref[...] - daout_ref[...], precision=HIGHEST, preferred_element_type=F32)
        a = -jnp.exp(al_ref[...])
        ddt = dda * a + ddtx_ref[...]
        dal_ref[...] += jnp.sum(dda * dt_ref[...], axis=0, keepdims=True) * a
        ddr = ddt * _sigmoid(x_ref[...] + b_ref[...])
        ddr_ref[...] = ddr.astype(ddr_ref.dtype)
        dbias_ref[...] += jnp.sum(ddr, axis=0, keepdims=True)

    row = pl.BlockSpec((lc, LANES), lambda i: (i, 0))
    one = pl.BlockSpec((1, LANES), lambda i: (0, 0))
    return pl.pallas_call(
        body, name=name, grid=(s // lc,), in_specs=[row, row, row, row, row, one, one], out_specs=[row, one, one],
        out_shape=[_sds((s, LANES), MXU_DT), _sds((1, LANES)), _sds((1, LANES))],
        compiler_params=_params("arbitrary"),
    )(da_in, da_out, ddtx, dt, dt_raw, dt_bias, a_log)


NORM_GROUP_W = SSD_D_INNER // SSD_GROUPS


def _group_rstd(yz):
    return [lax.rsqrt(jnp.mean(jnp.square(yz[:, g * NORM_GROUP_W:(g + 1) * NORM_GROUP_W]), axis=-1, keepdims=True) + EPS)
            for g in range(SSD_GROUPS)]


def _gated_norm_fwd(y, proj, norm_g, name):
    def fn(yv, z, gv):
        yz = yv * (z * _sigmoid(z))
        parts = [yz[:, g * NORM_GROUP_W:(g + 1) * NORM_GROUP_W] * r for g, r in enumerate(_group_rstd(yz))]
        return (jnp.concatenate(parts, axis=1) * gv,)
    return _rowwise(fn, [y, (proj, 0, SSD_D_INNER)], [norm_g.reshape(1, -1)], [_sds(y.shape, MXU_DT)], [],
                    name=name)[0]


def _gated_norm_bwd(y, proj, norm_g, dout, name):
    def fn(yv, z, do, gv):
        sg = _sigmoid(z)
        sz = z * sg
        yz = yv * sz
        dog = do * gv
        dyz, dg = [], []
        for g, r in enumerate(_group_rstd(yz)):
            cols = slice(g * NORM_GROUP_W, (g + 1) * NORM_GROUP_W)
            yzg, dogg = yz[:, cols], dog[:, cols]
            dyz.append(r * dogg - yzg * (r * r * r) * jnp.mean(yzg * dogg, axis=-1, keepdims=True))
            dg.append(jnp.sum(do[:, cols] * yzg * r, axis=0, keepdims=True))
        dyz = jnp.concatenate(dyz, axis=1)
        return dyz * sz, dyz * yv * (sg * (1.0 + z * (1.0 - sg))), jnp.concatenate(dg, axis=1)
    return _rowwise(fn, [y, (proj, 0, SSD_D_INNER), dout], [norm_g.reshape(1, -1)],
                    [_sds(y.shape), _sds(y.shape, MXU_DT)], [_sds((1, y.shape[1]))], name=name)


def _ssd_fwd(h, p):
    hn = _rms_fwd(h, p["norm_g"], "mix_norm_fwd")
    proj = _matmul(hn, p["w_zx"], name="ssd_in_zx")
    dt_raw = _matmul(hn, p["w_dt"], name="ssd_in_dt")
    xbc = _dwconv_silu_fwd(proj, SSD_D_INNER, SSD_CONV_DIM, p["conv_w"], p["conv_b"], "ssd_conv_fwd")
    dt, acs, acst = _ssd_prep(dt_raw, _lane_pad(p["dt_bias"]), _lane_pad(p["a_log"]), "ssd_prep")
    dtg, acsg = _heads_to_groups(dt), _heads_to_groups(acs)
    acstg = acst[:SSD_HEADS].reshape(SSD_GROUPS, SSD_HPG, -1)
    d_skip = jnp.repeat(p["d"], SSD_HEAD_DIM).reshape(1, -1)
    y, hprev = _ssd_scan_fwd(xbc, dtg, acsg, acstg, d_skip, "ssd_scan_fwd")
    yn = _gated_norm_fwd(y, proj, p["gnorm_g"], "ssd_gnorm_fwd")
    h_out = _matmul(yn, p["w_out"], res=h, name="ssd_out")
    return h_out, (hn, proj, dt_raw, xbc, dt, dtg, acsg, acstg, d_skip, y, hprev, yn)


def _ssd_bwd(h, dh, saved, p):
    hn, proj, dt_raw, xbc, dt, dtg, acsg, acstg, d_skip, y, hprev, yn = saved
    dh, dh_b = dh
    dyn = _matmul(dh_b, p["w_out"], tb=True, name="ssd_out_dx")
    dw_out = _matmul(yn, dh_b, ta=True, out_dtype=MXU_DT, name="ssd_out_dw")
    dy, dz, dgnorm = _gated_norm_bwd(y, proj, p["gnorm_g"], dyn, "ssd_gnorm_bwd")
    dx, dbm, dcm, daq, dar, ddtx, dd = _ssd_scan_bwd(xbc, dtg, acsg, acstg, d_skip, dy, hprev, "ssd_scan_bwd")
    dxbc = jnp.concatenate([dx, dbm, dcm], axis=1)
    dpre, dconv_w, dconv_b = _dwconv_silu_bwd(proj, SSD_D_INNER, SSD_CONV_DIM, dxbc, p["conv_w"], p["conv_b"],
                                              "ssd_conv_bwd")
    ddr, dalog, dbias = _ssd_post(_groups_to_heads(daq), _groups_to_heads(dar.transpose(0, 2, 1)),
                                  _groups_to_heads(ddtx), dt, dt_raw,
                                  _lane_pad(p["dt_bias"]), _lane_pad(p["a_log"]), "ssd_post")
    w_z, w_x = p["w_zx"][:, :SSD_D_INNER], p["w_zx"][:, SSD_D_INNER:]
    dhn = _matmul(dz, w_z, tb=True, name="ssd_in_dx_z")
    dhn = _matmul(dpre, w_x, tb=True, res=dhn, name="ssd_in_dx_x")
    dhn = _matmul(ddr, p["w_dt"], tb=True, res=dhn, name="ssd_in_dx_dt")
    dw_in = jnp.concatenate([_matmul(hn, dz, ta=True, out_dtype=MXU_DT, name="ssd_in_dw_z"),
                             _matmul(hn, dpre, ta=True, out_dtype=MXU_DT, name="ssd_in_dw_x"),
                             _matmul(hn, ddr, ta=True, out_dtype=MXU_DT, name="ssd_in_dw_dt")[:, :SSD_HEADS]], axis=1)
    *dh_in, dnorm_g = _rms_bwd(h, p["norm_g"], dhn, dh, "mix_norm_bwd")
    grads = dict(norm_g=dnorm_g, w_in=dw_in, conv_w=dconv_w, conv_b=dconv_b, dt_bias=dbias[:, :SSD_HEADS],
                 a_log=dalog[:, :SSD_HEADS], d=dd.reshape(SSD_HEADS, SSD_HEAD_DIM).sum(axis=1).reshape(1, -1),
                 gnorm_g=dgnorm, w_out=dw_out)
    return dh_in, grads


FOX_PAIRS = FOX_HEADS // 2
ATT_TQ = 256
ATT_TK = 512
NEG_BIG = -1e30
FOX_SCALE = FOX_HEAD_DIM ** -0.5
FOX_AUG_D = FOX_HEADS * LANES
QSIDE = FOX_HEAD_DIM
KSIDE = FOX_HEAD_DIM + 3


def _split3(x):
    a = x.astype(MXU_DT).astype(F32)
    b = (x - a).astype(MXU_DT).astype(F32)
    return a, b, (x - a - b).astype(MXU_DT).astype(F32)


def _head_tiles(pair_tile):
    return pair_tile, pltpu.roll(pair_tile, FOX_HEAD_DIM, 1)


def _fill_lanes(base, lane, first, values):
    for i, v in enumerate(values):
        base = jnp.where(lane == first + i, v, base)
    return base


def _pair_tile(lane, tile0, tile1):
    return jnp.where(lane < FOX_HEAD_DIM, tile0, pltpu.roll(tile1, FOX_HEAD_DIM, 1))


def _compact_heads(a, lane):
    return jnp.concatenate([_pair_tile(lane, a[:, 2 * j * LANES:(2 * j + 1) * LANES],
                                       a[:, (2 * j + 1) * LANES:(2 * j + 2) * LANES]) for j in range(FOX_PAIRS)], axis=1)


def _head_sum_matrix():
    return (_iota2((LANES, LANES), 0) < FOX_HEAD_DIM) == (_iota2((LANES, LANES), 1) < FOX_HEAD_DIM)


def _head_sums(x):
    bd = _head_sum_matrix().astype(F32)
    parts = [jnp.dot(x[:, j * LANES:(j + 1) * LANES], bd, precision=HIGHEST, preferred_element_type=F32)
             for j in range(x.shape[1] // LANES)]
    return parts[0] if len(parts) == 1 else jnp.concatenate(parts, axis=1)


def _fox_prep_fwd(proj, f_raw, qg, kg, b_f, name):
    s = proj.shape[0]
    tr = min(ATT_TQ, s)

    def body(q_ref, k_ref, v_ref, f_ref, qg_ref, kg_ref, b_ref, qa_ref, ka_ref, va_ref, carry_ref):
        @pl.when(pl.program_id(0) == 0)
        def _():
            carry_ref[...] = jnp.zeros(carry_ref.shape, F32)

        normed = []
        for x_ref, g_ref in ((q_ref, qg_ref), (k_ref, kg_ref)):
            x = x_ref[...]
            r = lax.rsqrt(_head_sums(x * x) * (1.0 / FOX_HEAD_DIM) + EPS)
            normed.append(x * r * g_ref[...])
        qn, kn, v = normed[0] * FOX_SCALE, normed[1], v_ref[...]
        x = f_ref[...] + b_ref[...]
        lf = jnp.minimum(x, 0.0) - jnp.log1p(jnp.exp(-jnp.abs(x)))
        lower = (_iota2((tr, tr), 0) >= _iota2((tr, tr), 1)).astype(F32)
        cum = jnp.dot(lower, lf, precision=HIGHEST, preferred_element_type=F32) + carry_ref[...]
        carry_ref[...] += jnp.sum(lf, axis=0, keepdims=True)
        first = _iota2((LANES, FOX_D), 0) * FOX_HEAD_DIM
        chan = _iota2((LANES, FOX_D), 1)
        spread = jnp.logical_and(chan >= first, chan < first + FOX_HEAD_DIM).astype(F32)
        cum = jnp.dot(cum, spread, precision=HIGHEST, preferred_element_type=F32)
        lane = _iota2((tr, LANES), 1)
        ones = jnp.where(jnp.logical_and(lane >= QSIDE, lane < KSIDE + 3), 1.0, 0.0)
        for j in range(FOX_PAIRS):
            cols = slice(j * LANES, (j + 1) * LANES)
            tiles = zip(_head_tiles(qn[:, cols]), _head_tiles(kn[:, cols]), _head_tiles(v[:, cols]),
                        reversed(_head_tiles(cum[:, cols])))
            for hh, (qt, kt, vt, ct) in enumerate(tiles):
                out = slice((2 * j + hh) * LANES, (2 * j + hh + 1) * LANES)
                c3 = _split3(ct)
                head = lane < FOX_HEAD_DIM
                qa_ref[:, out] = _fill_lanes(jnp.where(head, qt, ones), lane, QSIDE, c3).astype(qa_ref.dtype)
                ka_ref[:, out] = _fill_lanes(jnp.where(head, kt, ones), lane, KSIDE, [-c for c in c3]).astype(ka_ref.dtype)
                va_ref[:, out] = jnp.where(head, vt, jnp.where(lane < KSIDE, 1.0, 0.0)).astype(va_ref.dtype)

    wide = lambda cb: pl.BlockSpec((tr, FOX_D), lambda i: (i, cb))
    aug = pl.BlockSpec((tr, FOX_AUG_D), lambda i: (i, 0))
    one = lambda n: pl.BlockSpec((1, n), lambda i: (0, 0))
    return pl.pallas_call(
        body, name=name, grid=(s // tr,),
        in_specs=[wide(0), wide(1), wide(2), pl.BlockSpec((tr, LANES), lambda i: (i, 0)), one(FOX_D), one(FOX_D),
                  one(LANES)],
        out_specs=[aug, aug, aug], out_shape=[_sds((s, FOX_AUG_D), MXU_DT)] * 3,
        scratch_shapes=[pltpu.VMEM((1, LANES), F32)],
        compiler_params=_params("arbitrary"),
    )(proj, proj, proj, f_raw, qg, kg, b_f)


def _fox_prep_bwd(proj, f_raw, qg, kg, b_f, dqa, dka, row_sums, col_sums, name):
    s = proj.shape[0]
    tr = min(ATT_TQ, s)
    nb = s // tr

    def body(q_ref, k_ref, f_ref, qg_ref, kg_ref, b_ref, dqa_ref, dka_ref, dcq_ref, dck_ref,
             dq_ref, dk_ref, df_ref, dqg_ref, dkg_ref, db_ref, carry_ref):
        @pl.when(pl.program_id(0) == 0)
        def _():
            carry_ref[...] = jnp.zeros(carry_ref.shape, F32)
            dqg_ref[...] = jnp.zeros(dqg_ref.shape, F32)
            dkg_ref[...] = jnp.zeros(dkg_ref.shape, F32)
            db_ref[...] = jnp.zeros(db_ref.shape, F32)

        lane = _iota2((tr, LANES), 1)
        for x_ref, g_ref, dt_ref, scale, dx_ref, dg_ref in ((q_ref, qg_ref, dqa_ref, FOX_SCALE, dq_ref, dqg_ref),
                                                            (k_ref, kg_ref, dka_ref, 1.0, dk_ref, dkg_ref)):
            x, dy = x_ref[...], _compact_heads(dt_ref[...], lane) * scale
            r = lax.rsqrt(_head_sums(x * x) * (1.0 / FOX_HEAD_DIM) + EPS)
            dyg = dy * g_ref[...]
            dx = r * dyg - x * (r * r * r) * (_head_sums(x * dyg) * (1.0 / FOX_HEAD_DIM))
            dx_ref[...] = dx.astype(dx_ref.dtype)
            dg_ref[...] += jnp.sum(dy * x * r, axis=0, keepdims=True)
        dc = dcq_ref[...] - dck_ref[...]
        upper = (_iota2((tr, tr), 0) <= _iota2((tr, tr), 1)).astype(F32)
        dlf = jnp.dot(upper, dc, precision=HIGHEST, preferred_element_type=F32) + carry_ref[...]
        carry_ref[...] += jnp.sum(dc, axis=0, keepdims=True)
        df = dlf * _sigmoid(-(f_ref[...] + b_ref[...]))
        df_ref[...] = df.astype(df_ref.dtype)
        db_ref[...] += jnp.sum(df, axis=0, keepdims=True)

    wide = lambda cb: pl.BlockSpec((tr, FOX_D), lambda i: (nb - 1 - i, cb))
    aug = pl.BlockSpec((tr, FOX_AUG_D), lambda i: (nb - 1 - i, 0))
    row = pl.BlockSpec((tr, LANES), lambda i: (nb - 1 - i, 0))
    one = lambda n: pl.BlockSpec((1, n), lambda i: (0, 0))
    return pl.pallas_call(
        body, name=name, grid=(nb,),
        in_specs=[wide(0), wide(1), row, one(FOX_D), one(FOX_D), one(LANES), aug, aug, row, row],
        out_specs=[wide(0), wide(0), row, one(FOX_D), one(FOX_D), one(LANES)],
        out_shape=[_sds((s, FOX_D), MXU_DT), _sds((s, FOX_D), MXU_DT), _sds((s, LANES), MXU_DT),
                   _sds((1, FOX_D)), _sds((1, FOX_D)), _sds((1, LANES))],
        scratch_shapes=[pltpu.VMEM((1, LANES), F32)],
        compiler_params=_params("arbitrary"),
    )(proj, proj, f_raw, qg, kg, b_f, dqa, dka, row_sums, col_sums)


def _fox_attn_fwd(qa, ka, va, proj, name):
    s = qa.shape[0]
    tq, tk = min(ATT_TQ, s), min(ATT_TK, s)
    assert s % tq == 0 and s % tk == 0
    gt = 3 * FOX_D // LANES
    head_lanes = [slice(hh * LANES, (hh + 1) * LANES) for hh in range(2)]

    def body(qa_ref, ka_ref, va_ref, g_ref, o_ref, og_ref, qb_ref):
        qi = pl.program_id(1)
        lane = _iota2((tq, LANES), 1)
        ahead = _iota2((tq, tk), 0) - _iota2((tq, tk), 1)
        q = [qa_ref[:, hs] for hs in head_lanes]

        def kv_step(j, carry, masked):
            rows = pl.ds(pl.multiple_of(j * tk, tk), tk)
            out = []
            for hh, hs in enumerate(head_lanes):
                m, acc = carry[2 * hh:2 * hh + 2]
                sc = _dot(q[hh], ka_ref[rows, hs], 1, 1)
                if masked:
                    sc = jnp.where(ahead >= j * tk - qi * tq, sc, NEG_BIG)
                m_new = jnp.maximum(m, jnp.max(sc, axis=1, keepdims=True))
                pr = jnp.exp(sc - m_new).astype(MXU_DT)
                out += [m_new, jnp.exp(m - m_new) * acc + _dot(pr, va_ref[rows, hs], 1, 0)]
            return tuple(out)

        n_clear = lax.div(qi * tq, tk)
        n_all = lax.div((qi + 1) * tq + tk - 1, tk)
        init = (jnp.full((tq, 1), NEG_BIG, F32), jnp.zeros((tq, LANES), F32)) * 2
        carry = lax.fori_loop(0, n_clear, functools.partial(kv_step, masked=False), init)
        carry = lax.fori_loop(n_clear, n_all, functools.partial(kv_step, masked=True), carry)
        heads = []
        for hh, hs in enumerate(head_lanes):
            m, acc = carry[2 * hh:2 * hh + 2]
            l = acc[:, QSIDE:QSIDE + 1]
            heads.append(acc / l)
            qf = q[hh].astype(F32)
            bias = qf[:, QSIDE:QSIDE + 1] + qf[:, QSIDE + 1:QSIDE + 2] + qf[:, QSIDE + 2:QSIDE + 3]
            qb_ref[:, hs] = _fill_lanes(qf, lane, QSIDE, _split3(bias - (m + jnp.log(l)))).astype(qb_ref.dtype)
        o = _pair_tile(lane, heads[0], heads[1])
        o_ref[...] = o
        og_ref[...] = (o * _sigmoid(g_ref[...])).astype(og_ref.dtype)

    blk2 = pl.BlockSpec((tq, 2 * LANES), lambda p, i: (i, p))
    seq2 = pl.BlockSpec((s, 2 * LANES), lambda p, i: (0, p))
    blk = pl.BlockSpec((tq, LANES), lambda p, i: (i, p))
    return pl.pallas_call(
        body, name=name, grid=(FOX_PAIRS, s // tq),
        in_specs=[blk2, seq2, seq2, pl.BlockSpec((tq, LANES), lambda p, i: (i, gt + p))],
        out_specs=[blk, blk, blk2],
        out_shape=[_sds((s, FOX_D)), _sds((s, FOX_D), MXU_DT), _sds((s, FOX_AUG_D), MXU_DT)],
        compiler_params=_params("parallel", "parallel"),
    )(qa, ka, va, proj)


def _fox_gate_bwd(dog, o, proj, name):
    def fn(dogv, ov, gate):
        sg = _sigmoid(gate)
        do = dogv * sg
        delta = _head_sums(do * ov)
        lane = _iota2((do.shape[0], LANES), 1)
        tiles = []
        for j in range(FOX_PAIRS):
            cols = slice(j * LANES, (j + 1) * LANES)
            for dt, dl in zip(_head_tiles(do[:, cols]), reversed(_head_tiles(delta[:, cols]))):
                tiles.append(_fill_lanes(jnp.where(lane < FOX_HEAD_DIM, dt, 0.0), lane, QSIDE,
                                         [-d for d in _split3(dl)]))
        return dogv * ov * sg * (1.0 - sg), jnp.concatenate(tiles, axis=1)
    return _rowwise(fn, [dog, o, (proj, 3, FOX_D)], [], [_sds(o.shape, MXU_DT), _sds((o.shape[0], FOX_AUG_D), MXU_DT)],
                    [], name=name)


def _fox_attn_bwd(qb, ka, va, doa, name):
    s = qb.shape[0]
    tq, tk = min(ATT_TQ, s), min(ATT_TK, s)
    nq, nk = s // tq, s // tk
    head_lanes = [slice(hh * LANES, (hh + 1) * LANES) for hh in range(2)]

    def body(qb_ref, doa_ref, ka_ref, va_ref, dqa_ref, dka_ref, dv_ref):
        kj = pl.program_id(1)

        @pl.when(kj == 0)
        def _():
            dqa_ref[...] = jnp.zeros(dqa_ref.shape, F32)

        ahead = _iota2((tq, tk), 0) - _iota2((tq, tk), 1)
        kb = [ka_ref[:, hs] for hs in head_lanes]
        vb = [va_ref[:, hs] for hs in head_lanes]

        def q_step(i, carry, masked):
            rows = pl.ds(pl.multiple_of(i * tq, tq), tq)
            out = []
            for hh, hs in enumerate(head_lanes):
                dk, dv = carry[2 * hh:2 * hh + 2]
                q, do = qb_ref[rows, hs], doa_ref[rows, hs]
                pr = jnp.exp(_dot(q, kb[hh], 1, 1))
                if masked:
                    pr = jnp.where(ahead >= kj * tk - i * tq, pr, 0.0)
                dv = dv + _dot(pr.astype(MXU_DT), do, 0, 0)
                ds = (pr * _dot(do, vb[hh], 1, 1)).astype(MXU_DT)
                dqa_ref[rows, hs] += _dot(ds, kb[hh], 1, 0)
                out += [dk + _dot(ds, q, 0, 0), dv]
            return tuple(out)

        first = lax.div(kj * tk, tq)
        n_masked = lax.div((kj + 1) * tk + tq - 1, tq)
        carry = lax.fori_loop(first, n_masked, functools.partial(q_step, masked=True),
                              (jnp.zeros((tk, LANES), F32),) * 4)
        dk0, dv0, dk1, dv1 = lax.fori_loop(n_masked, nq, functools.partial(q_step, masked=False), carry)
        dka_ref[:, head_lanes[0]] = dk0
        dka_ref[:, head_lanes[1]] = dk1
        dv_ref[...] = _pair_tile(_iota2((tk, LANES), 1), dv0, dv1).astype(dv_ref.dtype)

    seq2 = pl.BlockSpec((s, 2 * LANES), lambda p, j: (0, p))
    blk2 = pl.BlockSpec((tk, 2 * LANES), lambda p, j: (j, p))
    return pl.pallas_call(
        body, name=name, grid=(FOX_PAIRS, nk), in_specs=[seq2, seq2, blk2, blk2],
        out_specs=[seq2, blk2, pl.BlockSpec((tk, LANES), lambda p, j: (j, p))],
        out_shape=[_sds((s, FOX_AUG_D)), _sds((s, FOX_AUG_D)), _sds((s, FOX_D), MXU_DT)],
        compiler_params=_params("parallel", "arbitrary"),
    )(qb, doa, ka, va)


def _fox_fwd(h, p):
    hn = _rms_fwd(h, p["norm_g"], "mix_norm_fwd")
    proj = _matmul(hn, p["w_qkvg"], name="fox_in_qkvg")
    f_raw = _matmul(hn, p["w_f"], name="fox_in_f")
    qg = jnp.tile(p["q_norm_g"], FOX_HEADS).reshape(1, -1)
    kg = jnp.tile(p["k_norm_g"], FOX_HEADS).reshape(1, -1)
    qa, ka, va = _fox_prep_fwd(proj, f_raw, qg, kg, _lane_pad(p["b_f"]), "fox_prep_fwd")
    o, og, qb = _fox_attn_fwd(qa, ka, va, proj, "fox_attn_fwd")
    h_out = _matmul(og, p["w_out"], res=h, name="fox_out")
    return h_out, (hn, proj, f_raw, qg, kg, ka, va, qb, o, og)


def _fox_bwd(h, dh, saved, p):
    hn, proj, f_raw, qg, kg, ka, va, qb, o, og = saved
    s = h.shape[0]
    dh, dh_b = dh
    dog = _matmul(dh_b, p["w_out"], tb=True, name="fox_out_dx")
    dw_out = _matmul(og, dh_b, ta=True, out_dtype=MXU_DT, name="fox_out_dw")
    dgate, doa = _fox_gate_bwd(dog, o, proj, "fox_gate_bwd")
    dqa, dka, dv = _fox_attn_bwd(qb, ka, va, doa, "fox_attn_bwd")
    head_cols = lambda a, lane: jnp.pad(a.reshape(s, FOX_HEADS, LANES)[:, :, lane], ((0, 0), (0, LANES - FOX_HEADS)))
    dq, dk, df, dqg, dkg, dbf = _fox_prep_bwd(proj, f_raw, qg, kg, _lane_pad(p["b_f"]), dqa, dka,
                                              head_cols(dqa, QSIDE), head_cols(dka, KSIDE), "fox_prep_bwd")
    dproj = jnp.concatenate([dq, dk, dv, dgate], axis=1)
    dhn = _matmul(dproj, p["w_qkvg"], tb=True, name="fox_in_dx_qkvg")
    dhn = _matmul(df, p["w_f"], tb=True, res=dhn, name="fox_in_dx_f")
    dw_in = jnp.concatenate([_matmul(hn, dproj, ta=True, out_dtype=MXU_DT, name="fox_in_dw_qkvg"),
                             _matmul(hn, df, ta=True, out_dtype=MXU_DT, name="fox_in_dw_f")[:, :FOX_HEADS]], axis=1)
    *dh_in, dnorm_g = _rms_bwd(h, p["norm_g"], dhn, dh, "mix_norm_bwd")
    fold = lambda g: g.reshape(FOX_HEADS, FOX_HEAD_DIM).sum(axis=0).reshape(1, -1)
    grads = dict(norm_g=dnorm_g, w_in=dw_in, b_f=dbf[:, :FOX_HEADS], q_norm_g=fold(dqg), k_norm_g=fold(dkg),
                 w_out=dw_out)
    return dh_in, grads


HBM_SPEC = pl.BlockSpec(memory_space=pl.ANY)


def _my_index():
    return 4 * lax.axis_index("x") + 2 * lax.axis_index("y") + lax.axis_index("c")


def _peer(k):
    x, y, c = lax.axis_index("x"), lax.axis_index("y"), lax.axis_index("c")
    flip = lambda v, bit: 1 - v if bit else v
    return (flip(x, k & 4), flip(y, k & 2), flip(c, k & 1))


def _exchange(arrays, scatter, name):
    n = len(arrays)

    def body(*refs):
        ins, outs = refs[:n], refs[n:2 * n]
        send_sems, recv_sems, local_sems = refs[2 * n:]
        me = _my_index()
        local = []
        for a in range(n):
            local.append(pltpu.make_async_copy(ins[a].at[me] if scatter else ins[a], outs[a].at[me], local_sems.at[a]))
            local[-1].start()

        def copy(a, k, slot):
            return pltpu.make_async_remote_copy(
                src_ref=ins[a].at[jnp.bitwise_xor(me, k)] if scatter else ins[a], dst_ref=outs[a].at[slot],
                send_sem=send_sems.at[a, k - 1], recv_sem=recv_sems.at[a, k - 1],
                device_id=_peer(k), device_id_type=MESH_IDS)

        pairs = [(a, k) for a in range(n) for k in range(1, N_DEV)]
        for a, k in pairs:
            copy(a, k, me).start()
        for a, k in pairs:
            copy(a, k, jnp.bitwise_xor(me, k)).wait_recv()
        for a, k in pairs:
            copy(a, k, me).wait_send()
        for cp in local:
            cp.wait()

    out_shape = [_sds(a.shape if scatter else (N_DEV,) + a.shape, a.dtype) for a in arrays]
    return pl.pallas_call(
        body, name=name, in_specs=[HBM_SPEC] * n, out_specs=[HBM_SPEC] * n, out_shape=out_shape,
        scratch_shapes=[pltpu.SemaphoreType.DMA((n, N_DEV - 1)), pltpu.SemaphoreType.DMA((n, N_DEV - 1)),
                        pltpu.SemaphoreType.DMA((n,))],
    )(*arrays)


def _allreduce_small(buf, name):
    def body(in_ref, all_ref, sum_ref, send_sems, recv_sems):
        me = _my_index()
        all_ref[me] = in_ref[...]

        def copy(k, slot):
            return pltpu.make_async_remote_copy(
                src_ref=in_ref, dst_ref=all_ref.at[slot], send_sem=send_sems.at[k - 1], recv_sem=recv_sems.at[k - 1],
                device_id=_peer(k), device_id_type=MESH_IDS)

        for k in range(1, N_DEV):
            copy(k, me).start()
        for k in range(1, N_DEV):
            copy(k, jnp.bitwise_xor(me, k)).wait_recv()
        for k in range(1, N_DEV):
            copy(k, me).wait_send()
        acc = all_ref[0]
        for j in range(1, N_DEV):
            acc = acc + all_ref[j]
        sum_ref[...] = acc

    vmem = pl.BlockSpec(memory_space=pltpu.VMEM)
    return pl.pallas_call(
        body, name=name, in_specs=[vmem], out_specs=[vmem, vmem],
        out_shape=[_sds((N_DEV,) + buf.shape), _sds(buf.shape)],
        scratch_shapes=[pltpu.SemaphoreType.DMA((N_DEV - 1,)), pltpu.SemaphoreType.DMA((N_DEV - 1,))],
        compiler_params=pltpu.CompilerParams(vmem_limit_bytes=VMEM_LIMIT_BYTES),
    )(buf)[1]


def _adamw_math(w, g, m, v):
    m = ADAM_B1 * m + (1.0 - ADAM_B1) * g
    v = ADAM_B2 * v + (1.0 - ADAM_B2) * (g * g)
    m_hat = m / (1.0 - ADAM_B1 ** ADAM_STEP)
    v_hat = v / (1.0 - ADAM_B2 ** ADAM_STEP)
    return -ADAM_LR * (m_hat / (jnp.sqrt(v_hat) + ADAM_EPS) + ADAM_WD * w), m, v


def _row_tile(rows, cap=256, mult=16):
    best = None
    for t in range(mult, min(rows, cap) + 1, mult):
        if rows % t == 0:
            best = t
    assert best is not None, rows
    return best


def _adamw_sharded(w, m, v, partials, name):
    rows, cols = w.shape
    tr = _row_tile(rows)

    def body(w_ref, m_ref, v_ref, p_ref, g_ref, d_ref, nm_ref, nv_ref):
        g = p_ref[0].astype(F32)
        for j in range(1, N_DEV):
            g = g + p_ref[j].astype(F32)
        delta, m_new, v_new = _adamw_math(w_ref[...], g, m_ref[...], v_ref[...])
        g_ref[...], d_ref[...], nm_ref[...], nv_ref[...] = g, delta, m_new, v_new

    blk = pl.BlockSpec((tr, cols), lambda i: (i, 0))
    return pl.pallas_call(
        body, name=name, grid=(rows // tr,),
        in_specs=[blk, blk, blk, pl.BlockSpec((N_DEV, tr, cols), lambda i: (0, i, 0))],
        out_specs=[blk] * 4, out_shape=[_sds(w.shape)] * 4, compiler_params=_params("parallel"),
    )(w, m, v, partials)


def _adamw_small(w, g, m, v, name):
    def body(w_ref, g_ref, m_ref, v_ref, d_ref, nm_ref, nv_ref):
        d_ref[...], nm_ref[...], nv_ref[...] = _adamw_math(w_ref[...], g_ref[...], m_ref[...], v_ref[...])

    return pl.pallas_call(body, name=name, out_shape=[_sds(w.shape)] * 3,
                          compiler_params=_params())(w, g, m, v)


def _pack(arrays):
    flat = jnp.concatenate([a.reshape(-1).astype(F32) for a in arrays])
    pad = -flat.shape[0] % (8 * LANES)
    return jnp.pad(flat, (0, pad)).reshape(-1, LANES)


def _unpack(buf, shapes):
    flat, out, off = buf.reshape(-1), [], 0
    for shp in shapes:
        size = math.prod(shp)
        out.append(flat[off:off + size].reshape(shp))
        off += size
    return out


WEIGHTS = ["mix_norm_g", "ffn_norm_g", "ssd_w_in", "ssd_conv_w", "ssd_conv_b", "ssd_dt_bias", "ssd_a_log", "ssd_d",
           "ssd_norm_g", "ssd_w_out", "fox_w_in", "fox_b_f", "fox_q_norm_g", "fox_k_norm_g", "fox_w_out", "ffn_w_up",
           "ffn_conv_w", "ffn_conv_b", "ffn_w_down", "final_norm_g"]
BIG = ["ssd_w_in", "ssd_w_out", "fox_w_in", "fox_w_out", "ffn_w_up", "ffn_w_down"]
COLUMN_SHARDED = ["ssd_w_in", "fox_w_in", "ffn_w_up"]
CONV = ["ssd_conv_w", "ffn_conv_w"]
REPLICATED = [n for n in WEIGHTS if n not in BIG + CONV]
DEPTH = 4


def _full_weight(gathered, on_columns):
    _, nl, r, c = gathered.shape
    if on_columns:
        return gathered.transpose(1, 2, 0, 3).reshape(nl, r, N_DEV * c)
    return gathered.transpose(1, 0, 2, 3).reshape(nl, N_DEV * r, c)


def _to_shards(full, on_columns):
    nl, r, c = full.shape
    if on_columns:
        return full.reshape(nl, r, N_DEV, c // N_DEV).transpose(2, 0, 1, 3)
    return full.reshape(nl, N_DEV, r // N_DEV, c).transpose(1, 0, 2, 3)


def _pad_cols(w):
    return jnp.pad(w, ((0, 0), (0, LANES - w.shape[1])))


def kernel(x, mix_norm_g, ffn_norm_g, ssd_w_in, ssd_conv_w, ssd_conv_b, ssd_dt_bias, ssd_a_log, ssd_d, ssd_norm_g, ssd_w_out, fox_w_in, fox_b_f, fox_q_norm_g, fox_k_norm_g, fox_w_out, ffn_w_up, ffn_conv_w, ffn_conv_b, ffn_w_down, final_norm_g, loss_target, m_mix_norm_g, m_ffn_norm_g, m_ssd_w_in, m_ssd_conv_w, m_ssd_conv_b, m_ssd_dt_bias, m_ssd_a_log, m_ssd_d, m_ssd_norm_g, m_ssd_w_out, m_fox_w_in, m_fox_b_f, m_fox_q_norm_g, m_fox_k_norm_g, m_fox_w_out, m_ffn_w_up, m_ffn_conv_w, m_ffn_conv_b, m_ffn_w_down, m_final_norm_g, v_mix_norm_g, v_ffn_norm_g, v_ssd_w_in, v_ssd_conv_w, v_ssd_conv_b, v_ssd_dt_bias, v_ssd_a_log, v_ssd_d, v_ssd_norm_g, v_ssd_w_out, v_fox_w_in, v_fox_b_f, v_fox_q_norm_g, v_fox_k_norm_g, v_fox_w_out, v_ffn_w_up, v_ffn_conv_w, v_ffn_conv_b, v_ffn_w_down, v_final_norm_g):
    given = dict(locals())
    w = {n: given[n] for n in WEIGHTS}
    mom = {n: given["m_" + n] for n in WEIGHTS}
    var = {n: given["v_" + n] for n in WEIGHTS}
    me = _my_index()

    gathered = _exchange([w[n].astype(MXU_DT) for n in BIG] + [w[n] for n in CONV], False, "gather_weights")
    full = {n: _full_weight(g, n in COLUMN_SHARDED) for n, g in zip(BIG, gathered)}
    conv_full = {n: _full_weight(g, True) for n, g in zip(CONV, gathered[len(BIG):])}

    def ssd_params(i, j):
        w_in = full["ssd_w_in"][j]
        return dict(norm_g=w["mix_norm_g"][i], w_zx=w_in[:, :SSD_ZX], w_dt=_pad_cols(w_in[:, SSD_ZX:]),
                    conv_w=conv_full["ssd_conv_w"][j], conv_b=w["ssd_conv_b"][j], dt_bias=w["ssd_dt_bias"][j],
                    a_log=w["ssd_a_log"][j], d=w["ssd_d"][j], gnorm_g=w["ssd_norm_g"][j], w_out=full["ssd_w_out"][j])

    def fox_params(i, j):
        w_in = full["fox_w_in"][j]
        return dict(norm_g=w["mix_norm_g"][i], w_qkvg=w_in[:, :4 * FOX_D], w_f=_pad_cols(w_in[:, 4 * FOX_D:]),
                    b_f=w["fox_b_f"][j], q_norm_g=w["fox_q_norm_g"][j], k_norm_g=w["fox_k_norm_g"][j],
                    w_out=full["fox_w_out"][j])

    def ffn_params(i):
        return (w["ffn_norm_g"][i], full["ffn_w_up"][i], conv_full["ffn_conv_w"][i], w["ffn_conv_b"][i],
                full["ffn_w_down"][i])

    h = x[0]
    tape = []
    for i in range(DEPTH):
        j = i // 2
        if i % 2 == 0:
            mp = ssd_params(i, j)
            h_mid, mix_saved = _ssd_fwd(h, mp)
        else:
            mp = fox_params(i, j)
            h_mid, mix_saved = _fox_fwd(h, mp)
        fp = ffn_params(i)
        h_out, ffn_saved = _ffn_fwd(h_mid, *fp, "ffn")
        tape.append((h, mp, mix_saved, h_mid, fp, ffn_saved))
        h = h_out
    *dh, dfinal_g, loss_part = _loss_head(h, w["final_norm_g"], loss_target[0], "loss_head")

    grads = {n: [None] * w[n].shape[0] for n in WEIGHTS if n != "final_norm_g"}
    for i in reversed(range(DEPTH)):
        j = i // 2
        h_in, mp, mix_saved, h_mid, fp, ffn_saved = tape[i]
        dh, g = _ffn_bwd(h_mid, dh, ffn_saved, *fp)
        grads["ffn_norm_g"][i], grads["ffn_w_up"][i], grads["ffn_conv_w"][i] = g["norm_g"][0], g["w_up"], g["conv_w"]
        grads["ffn_conv_b"][i], grads["ffn_w_down"][i] = g["conv_b"][0], g["w_down"]
        if i % 2 == 0:
            dh, g = _ssd_bwd(h_in, dh, mix_saved, mp)
            for key, name in (("w_in", "ssd_w_in"), ("conv_w", "ssd_conv_w"), ("w_out", "ssd_w_out")):
                grads[name][j] = g[key]
            for key, name in (("conv_b", "ssd_conv_b"), ("dt_bias", "ssd_dt_bias"), ("a_log", "ssd_a_log"),
                              ("d", "ssd_d"), ("gnorm_g", "ssd_norm_g")):
                grads[name][j] = g[key][0]
        else:
            dh, g = _fox_bwd(h_in, dh, mix_saved, mp)
            grads["fox_w_in"][j], grads["fox_w_out"][j] = g["w_in"], g["w_out"]
            for key, name in (("b_f", "fox_b_f"), ("q_norm_g", "fox_q_norm_g"), ("k_norm_g", "fox_k_norm_g")):
                grads[name][j] = g[key][0]
        grads["mix_norm_g"][i] = g["norm_g"][0]
    grads = {n: jnp.stack(v) for n, v in grads.items()}
    grads["final_norm_g"] = dfinal_g[0]

    small_names = REPLICATED + CONV
    summed = _unpack(_allreduce_small(_pack([grads[n] for n in small_names] + [loss_part]), "allreduce_small"),
                     [grads[n].shape for n in small_names] + [(1, 1)])
    loss = summed[-1][0, 0]
    g_small = dict(zip(small_names, summed[:-1]))
    for n in CONV:
        width = w[n].shape[-1]
        g_small[n] = lax.dynamic_slice_in_dim(g_small[n], me * width, width, axis=2)
    pk = lambda d: _pack([d[n] for n in small_names])
    d_small, m_small, v_small = _adamw_small(pk(w), pk(g_small), pk(mom), pk(var), "adamw_small")
    shapes = [w[n].shape for n in small_names]
    out_g, out_d, out_m, out_v = dict(g_small), {}, {}, {}
    for dst, buf in ((out_d, d_small), (out_m, m_small), (out_v, v_small)):
        dst.update(zip(small_names, _unpack(buf, shapes)))

    partials = _exchange([_to_shards(grads[n], n in COLUMN_SHARDED) for n in BIG], True, "scatter_gradients")
    for n, part in zip(BIG, partials):
        shp = w[n].shape
        flat = lambda a: a.reshape(shp[0] * shp[1], shp[2])
        res = _adamw_sharded(flat(w[n]), flat(mom[n]), flat(var[n]), part.reshape(N_DEV, shp[0] * shp[1], shp[2]),
                             "adamw_" + n)
        out_g[n], out_d[n], out_m[n], out_v[n] = [r.reshape(shp) for r in res]

    return (loss, dh[0][None], *[out_g[n] for n in WEIGHTS], *[out_d[n] for n in WEIGHTS],
            *[out_m[n] for n in WEIGHTS], *[out_v[n] for n in WEIGHTS])
```

```python
import functools
import math

import jax
import jax.numpy as jnp
from jax import lax
from jax.experimental import pallas as pl
from jax.experimental.pallas import tpu as pltpu

F32 = jnp.float32
MXU_DT = jnp.bfloat16
VMEM_LIMIT_BYTES = 56 * 1024 * 1024
LANES = 128
N_DEV = 8
MESH_IDS = pl.DeviceIdType.MESH

EPS = 1e-6
D_MODEL = 1024
SSD_D_INNER = 2048
SSD_HEAD_DIM = 64
SSD_HEADS = 32
SSD_GROUPS = 4
SSD_HPG = 8
SSD_STATE = 128
SSD_CONV = 4
SSD_CHUNK = 128
SSD_CONV_DIM = 3072
SSD_ZX = SSD_D_INNER + SSD_CONV_DIM
FOX_HEAD_DIM = 64
FOX_HEADS = 16
FOX_D = 1024
D_FF = 2816
FFN_CONV = 3
ADAM_LR, ADAM_B1, ADAM_B2, ADAM_EPS, ADAM_WD, ADAM_STEP = 0.001, 0.9, 0.999, 1e-08, 0.01, 10


def _params(*sem):
    return pltpu.CompilerParams(dimension_semantics=sem or None, vmem_limit_bytes=VMEM_LIMIT_BYTES)


def _sds(shape, dtype=F32):
    return jax.ShapeDtypeStruct(tuple(shape), dtype)


def _col_tile(n, cap=1536):
    best = None
    for t in range(LANES, min(n, cap) + 1, LANES):
        if n % t == 0:
            best = t
    assert best is not None, n
    return best


def _sigmoid(x):
    return 1.0 / (1.0 + jnp.exp(-x))


def _matmul(a, b, *, ta=False, tb=False, res=None, out_dtype=F32, tm=512, tn=None, name):
    (kdim, m) = a.shape if ta else a.shape[::-1]
    (n, k2) = b.shape if tb else b.shape[::-1]
    assert kdim == k2, (a.shape, b.shape, ta, tb)
    tm = min(tm, m)
    if m % tm:
        tm = _col_tile(m, tm)
    tn = tn or _col_tile(n)
    assert m % tm == 0 and n % tn == 0, (m, tm, n, tn)
    dims = (((0 if ta else 1,), (1 if tb else 0,)), ((), ()))

    def body(*refs):
        a_ref, b_ref = refs[0], refs[1]
        o_ref = refs[-1]
        acc = lax.dot_general(a_ref[...].astype(MXU_DT), b_ref[...].astype(MXU_DT), dims,
                              preferred_element_type=F32)
        if res is not None:
            acc = acc + refs[2][...].astype(F32)
        o_ref[...] = acc.astype(o_ref.dtype)

    a_spec = pl.BlockSpec((kdim, tm), lambda i, j: (0, i)) if ta else pl.BlockSpec((tm, kdim), lambda i, j: (i, 0))
    b_spec = pl.BlockSpec((tn, kdim), lambda i, j: (j, 0)) if tb else pl.BlockSpec((kdim, tn), lambda i, j: (0, j))
    o_spec = pl.BlockSpec((tm, tn), lambda i, j: (i, j))
    ins, specs = [a, b], [a_spec, b_spec]
    if res is not None:
        ins.append(res)
        specs.append(o_spec)
    return pl.pallas_call(
        body, name=name, grid=(m // tm, n // tn), in_specs=specs, out_specs=o_spec,
        out_shape=_sds((m, n), out_dtype), compiler_params=_params("parallel", "parallel"),
    )(*ins)


def _rowwise(fn, rows, consts, out_rows, out_sums, *, tr=256, name):
    rows = [r if isinstance(r, tuple) else (r, 0, r.shape[1]) for r in rows]
    s = rows[0][0].shape[0]
    tr = min(tr, s)
    assert s % tr == 0
    n_in, n_c, n_or = len(rows), len(consts), len(out_rows)

    def body(*refs):
        ins = [r[...] for r in refs[:n_in + n_c]]
        outs = fn(*ins)
        o_refs = refs[n_in + n_c:]
        for o_ref, val in zip(o_refs[:n_or], outs[:n_or]):
            o_ref[...] = val.astype(o_ref.dtype)
        if out_sums:
            first = pl.program_id(0) == 0

            @pl.when(first)
            def _():
                for o_ref, val in zip(o_refs[n_or:], outs[n_or:]):
                    o_ref[...] = val.astype(o_ref.dtype)

            @pl.when(jnp.logical_not(first))
            def _():
                for o_ref, val in zip(o_refs[n_or:], outs[n_or:]):
                    o_ref[...] += val.astype(o_ref.dtype)

    in_specs = [pl.BlockSpec((tr, width), functools.partial(lambda i, cb: (i, cb), cb=cb)) for _, cb, width in rows]
    in_specs += [pl.BlockSpec(c.shape, lambda i: (0, 0)) for c in consts]
    out_specs = [pl.BlockSpec((tr, o.shape[1]), lambda i: (i, 0)) for o in out_rows]
    out_specs += [pl.BlockSpec(o.shape, lambda i: (0, 0)) for o in out_sums]
    return pl.pallas_call(
        body, name=name, grid=(s // tr,), in_specs=in_specs, out_specs=out_specs,
        out_shape=list(out_rows) + list(out_sums),
        compiler_params=_params("arbitrary" if out_sums else "parallel"),
    )(*[r[0] for r in rows], *consts)


def _rms_fwd(h, g, name):
    def fn(x, gv):
        r = lax.rsqrt(jnp.mean(x * x, axis=-1, keepdims=True) + EPS)
        return (x * r * gv,)
    return _rowwise(fn, [h], [g.reshape(1, -1)], [_sds(h.shape, MXU_DT)], [], name=name)[0]


def _rms_bwd(h, g, dy, dres, name):
    def fn(x, dyv, dr, gv):
        r = lax.rsqrt(jnp.mean(x * x, axis=-1, keepdims=True) + EPS)
        dyg = dyv * gv
        dx = dr + r * dyg - x * (r * r * r) * jnp.mean(x * dyg, axis=-1, keepdims=True)
        return dx, dx, jnp.sum(dyv * x * r, axis=0, keepdims=True)
    return _rowwise(fn, [h, dy, dres], [g.reshape(1, -1)], [_sds(h.shape), _sds(h.shape, MXU_DT)],
                    [_sds((1, h.shape[1]))], name=name)


def _loss_head(h, g, target, name):
    c = h.shape[1]

    def fn(x, t, gv):
        r = lax.rsqrt(jnp.mean(x * x, axis=-1, keepdims=True) + EPS)
        y = x * r * gv
        err = y - t
        dyv = err * (1.0 / c)
        dyg = dyv * gv
        dx = r * dyg - x * (r * r * r) * jnp.mean(x * dyg, axis=-1, keepdims=True)
        loss = 0.5 * jnp.sum(jnp.mean(err * err, axis=-1, keepdims=True), axis=0, keepdims=True)
        return dx, dx, jnp.sum(dyv * x * r, axis=0, keepdims=True), loss
    return _rowwise(fn, [h, target], [g.reshape(1, -1)], [_sds(h.shape), _sds(h.shape, MXU_DT)],
                    [_sds((1, c)), _sds((1, 1))], name=name)


PAD_ROWS = 8
ROW_TILE = 512


def _shifted_conv(xp_ref, w, r0, tr, kw):
    acc = None
    for k in range(kw):
        xk = xp_ref[pl.ds(PAD_ROWS + r0 - (kw - 1) + k, tr), :]
        term = xk * w[k:k + 1, :]
        acc = term if acc is None else acc + term
    return acc


def _convglu_fwd(u, conv_w, conv_b, name):
    s = u.shape[0]
    nt = D_FF // LANES
    tr = min(ROW_TILE, s)

    def body(ug_ref, uv_ref, w_ref, b_ref, act_ref, xp_ref):
        xp_ref[pl.ds(0, PAD_ROWS), :] = jnp.zeros((PAD_ROWS, LANES), F32)
        xp_ref[pl.ds(PAD_ROWS, s), :] = ug_ref[...]
        w = w_ref[...]
        b = b_ref[...]
        for r0 in range(0, s, tr):
            gate = _shifted_conv(xp_ref, w, r0, tr, FFN_CONV) + b
            act = gate * _sigmoid(gate) * uv_ref[pl.ds(r0, tr), :]
            act_ref[pl.ds(r0, tr), :] = act.astype(act_ref.dtype)

    return pl.pallas_call(
        body, name=name, grid=(nt,),
        in_specs=[pl.BlockSpec((s, LANES), lambda j: (0, j)), pl.BlockSpec((s, LANES), lambda j: (0, nt + j)),
                  pl.BlockSpec((FFN_CONV, LANES), lambda j: (0, j)), pl.BlockSpec((1, LANES), lambda j: (0, j))],
        out_specs=pl.BlockSpec((s, LANES), lambda j: (0, j)),
        out_shape=_sds((s, D_FF), MXU_DT),
        scratch_shapes=[pltpu.VMEM((s + PAD_ROWS, LANES), F32)],
        compiler_params=_params("parallel"),
    )(u, u, conv_w, conv_b.reshape(1, -1))


def _convglu_bwd(u, dact, conv_w, conv_b, name):
    s = u.shape[0]
    nt = D_FF // LANES
    tr = min(ROW_TILE, s)
    kw = FFN_CONV

    def body(ug_ref, uv_ref, da_ref, w_ref, b_ref, dug_ref, duv_ref, dw_ref, db_ref, xp_ref, dgp_ref):
        xp_ref[pl.ds(0, PAD_ROWS), :] = jnp.zeros((PAD_ROWS, LANES), F32)
        xp_ref[pl.ds(PAD_ROWS, s), :] = ug_ref[...]
        dgp_ref[pl.ds(s, PAD_ROWS), :] = jnp.zeros((PAD_ROWS, LANES), F32)
        w = w_ref[...]
        b = b_ref[...]
        dw = [jnp.zeros((1, LANES), F32) for _ in range(kw)]
        db = jnp.zeros((1, LANES), F32)
        for r0 in range(0, s, tr):
            gate = _shifted_conv(xp_ref, w, r0, tr, kw) + b
            sg = _sigmoid(gate)
            da = da_ref[pl.ds(r0, tr), :].astype(F32)
            duv_ref[pl.ds(r0, tr), :] = (da * gate * sg).astype(duv_ref.dtype)
            dgate = da * uv_ref[pl.ds(r0, tr), :] * (sg * (1.0 + gate * (1.0 - sg)))
            dgp_ref[pl.ds(r0, tr), :] = dgate
            db = db + jnp.sum(dgate, axis=0, keepdims=True)
            for k in range(kw):
                xk = xp_ref[pl.ds(PAD_ROWS + r0 - (kw - 1) + k, tr), :]
                dw[k] = dw[k] + jnp.sum(dgate * xk, axis=0, keepdims=True)
        for r0 in range(0, s, tr):
            acc = None
            for k in range(kw):
                term = dgp_ref[pl.ds(r0 + (kw - 1) - k, tr), :] * w[k:k + 1, :]
                acc = term if acc is None else acc + term
            dug_ref[pl.ds(r0, tr), :] = acc.astype(dug_ref.dtype)
        for k in range(kw):
            dw_ref[pl.ds(k, 1), :] = dw[k]
        db_ref[...] = db

    col = lambda j: (0, j)
    return pl.pallas_call(
        body, name=name, grid=(nt,),
        in_specs=[pl.BlockSpec((s, LANES), col), pl.BlockSpec((s, LANES), lambda j: (0, nt + j)),
                  pl.BlockSpec((s, LANES), col), pl.BlockSpec((kw, LANES), col), pl.BlockSpec((1, LANES), col)],
        out_specs=[pl.BlockSpec((s, LANES), col), pl.BlockSpec((s, LANES), col),
                   pl.BlockSpec((kw, LANES), col), pl.BlockSpec((1, LANES), col)],
        out_shape=[_sds((s, D_FF), MXU_DT), _sds((s, D_FF), MXU_DT), _sds((kw, D_FF)), _sds((1, D_FF))],
        scratch_shapes=[pltpu.VMEM((s + PAD_ROWS, LANES), F32), pltpu.VMEM((s + PAD_ROWS, LANES), F32)],
        compiler_params=_params("parallel"),
    )(u, u, dact, conv_w, conv_b.reshape(1, -1))


def _ffn_fwd(h, norm_g, w_up, conv_w, conv_b, w_down, tag):
    hf = _rms_fwd(h, norm_g, f"ffn_norm_fwd")
    u = _matmul(hf, w_up, name="ffn_up")
    act = _convglu_fwd(u, conv_w, conv_b, "convglu_fwd")
    h_out = _matmul(act, w_down, res=h, name="ffn_down")
    return h_out, (hf, u, act)


def _ffn_bwd(h, dh, saved, norm_g, w_up, conv_w, conv_b, w_down):
    hf, u, act = saved
    dh, dh_b = dh
    dact = _matmul(dh_b, w_down, tb=True, name="ffn_down_dx")
    dw_down = _matmul(act, dh_b, ta=True, out_dtype=MXU_DT, name="ffn_down_dw")
    dug, duv, dconv_w, dconv_b = _convglu_bwd(u, dact, conv_w, conv_b, "convglu_bwd")
    dhf = _matmul(dug, w_up[:, :D_FF], tb=True, name="ffn_up_dx_gate")
    dhf = _matmul(duv, w_up[:, D_FF:], tb=True, res=dhf, name="ffn_up_dx_val")
    dw_up = jnp.concatenate([_matmul(hf, dug, ta=True, out_dtype=MXU_DT, name="ffn_up_dw_gate"),
                             _matmul(hf, duv, ta=True, out_dtype=MXU_DT, name="ffn_up_dw_val")], axis=1)
    *dh_in, dnorm_g = _rms_bwd(h, norm_g, dhf, dh, "ffn_norm_bwd")
    return dh_in, dict(norm_g=dnorm_g, w_up=dw_up, conv_w=dconv_w, conv_b=dconv_b, w_down=dw_down)


def _dwconv_silu_fwd(proj, col0, n_ch, conv_w, conv_b, name):
    s = proj.shape[0]
    nt, t0, kw = n_ch // LANES, col0 // LANES, conv_w.shape[0]
    tr = min(ROW_TILE, s)

    def body(x_ref, w_ref, b_ref, o_ref, xp_ref):
        xp_ref[pl.ds(0, PAD_ROWS), :] = jnp.zeros((PAD_ROWS, LANES), F32)
        xp_ref[pl.ds(PAD_ROWS, s), :] = x_ref[...]
        w = w_ref[...]
        b = b_ref[...]
        for r0 in range(0, s, tr):
            pre = _shifted_conv(xp_ref, w, r0, tr, kw) + b
            o_ref[pl.ds(r0, tr), :] = pre * _sigmoid(pre)

    col = lambda j: (0, j)
    return pl.pallas_call(
        body, name=name, grid=(nt,),
        in_specs=[pl.BlockSpec((s, LANES), lambda j: (0, t0 + j)), pl.BlockSpec((kw, LANES), col),
                  pl.BlockSpec((1, LANES), col)],
        out_specs=pl.BlockSpec((s, LANES), col), out_shape=_sds((s, n_ch)),
        scratch_shapes=[pltpu.VMEM((s + PAD_ROWS, LANES), F32)],
        compiler_params=_params("parallel"),
    )(proj, conv_w, conv_b.reshape(1, -1))


def _dwconv_silu_bwd(proj, col0, n_ch, dout, conv_w, conv_b, name):
    s = proj.shape[0]
    nt, t0, kw = n_ch // LANES, col0 // LANES, conv_w.shape[0]
    tr = min(ROW_TILE, s)

    def body(x_ref, do_ref, w_ref, b_ref, dx_ref, dw_ref, db_ref, xp_ref, dgp_ref):
        xp_ref[pl.ds(0, PAD_ROWS), :] = jnp.zeros((PAD_ROWS, LANES), F32)
        xp_ref[pl.ds(PAD_ROWS, s), :] = x_ref[...]
        dgp_ref[pl.ds(s, PAD_ROWS), :] = jnp.zeros((PAD_ROWS, LANES), F32)
        w = w_ref[...]
        b = b_ref[...]
        dw = [jnp.zeros((1, LANES), F32) for _ in range(kw)]
        db = jnp.zeros((1, LANES), F32)
        for r0 in range(0, s, tr):
            pre = _shifted_conv(xp_ref, w, r0, tr, kw) + b
            sg = _sigmoid(pre)
            dpre = do_ref[pl.ds(r0, tr), :] * (sg * (1.0 + pre * (1.0 - sg)))
            dgp_ref[pl.ds(r0, tr), :] = dpre
            db = db + jnp.sum(dpre, axis=0, keepdims=True)
            for k in range(kw):
                xk = xp_ref[pl.ds(PAD_ROWS + r0 - (kw - 1) + k, tr), :]
                dw[k] = dw[k] + jnp.sum(dpre * xk, axis=0, keepdims=True)
        for r0 in range(0, s, tr):
            acc = None
            for k in range(kw):
                term = dgp_ref[pl.ds(r0 + (kw - 1) - k, tr), :] * w[k:k + 1, :]
                acc = term if acc is None else acc + term
            dx_ref[pl.ds(r0, tr), :] = acc.astype(dx_ref.dtype)
        for k in range(kw):
            dw_ref[pl.ds(k, 1), :] = dw[k]
        db_ref[...] = db

    col = lambda j: (0, j)
    return pl.pallas_call(
        body, name=name, grid=(nt,),
        in_specs=[pl.BlockSpec((s, LANES), lambda j: (0, t0 + j)), pl.BlockSpec((s, LANES), col),
                  pl.BlockSpec((kw, LANES), col), pl.BlockSpec((1, LANES), col)],
        out_specs=[pl.BlockSpec((s, LANES), col), pl.BlockSpec((kw, LANES), col), pl.BlockSpec((1, LANES), col)],
        out_shape=[_sds((s, n_ch), MXU_DT), _sds((kw, n_ch)), _sds((1, n_ch))],
        scratch_shapes=[pltpu.VMEM((s + PAD_ROWS, LANES), F32), pltpu.VMEM((s + PAD_ROWS, LANES), F32)],
        compiler_params=_params("parallel"),
    )(proj, dout, conv_w, conv_b.reshape(1, -1))


HIGHEST = lax.Precision.HIGHEST
PAIRS = SSD_HPG // 2
PAIR_W = 2 * SSD_HEAD_DIM
GROUP_W = SSD_HPG * SSD_HEAD_DIM


def _iota2(shape, axis):
    return lax.broadcasted_iota(jnp.int32, shape, axis)


def _lane_pad(v):
    return jnp.pad(v.reshape(1, -1), ((0, 0), (0, LANES - v.shape[0])))


def _heads_to_groups(a):
    s = a.shape[0]
    return a[:, :SSD_HEADS].reshape(s, SSD_GROUPS, SSD_HPG).transpose(1, 0, 2)


def _groups_to_heads(a):
    s = a.shape[1]
    return jnp.pad(a.transpose(1, 0, 2).reshape(s, SSD_HEADS), ((0, 0), (0, LANES - SSD_HEADS)))


def _ssd_prep(dt_raw, dt_bias, a_log, name):
    s = dt_raw.shape[0]
    lc = SSD_CHUNK

    def body(x_ref, b_ref, al_ref, dt_ref, acs_ref, acst_ref):
        x = x_ref[...] + b_ref[...]
        dt = jnp.maximum(x, 0.0) + jnp.log1p(jnp.exp(-jnp.abs(x)))
        da = dt * (-jnp.exp(al_ref[...]))
        lower = (_iota2((lc, lc), 0) >= _iota2((lc, lc), 1)).astype(F32)
        upper = (_iota2((lc, lc), 0) <= _iota2((lc, lc), 1)).astype(F32)
        dt_ref[...] = dt
        acs_ref[...] = jnp.dot(lower, da, precision=HIGHEST, preferred_element_type=F32)
        acst_ref[...] = lax.dot_general(da, upper, (((0,), (0,)), ((), ())), precision=HIGHEST,
                                        preferred_element_type=F32)

    row = pl.BlockSpec((lc, LANES), lambda c: (c, 0))
    one = pl.BlockSpec((1, LANES), lambda c: (0, 0))
    return pl.pallas_call(
        body, name=name, grid=(s // lc,), in_specs=[row, one, one],
        out_specs=[row, row, pl.BlockSpec((LANES, lc), lambda c: (0, c))],
        out_shape=[_sds((s, LANES)), _sds((s, LANES)), _sds((LANES, s))],
        compiler_params=_params("parallel"),
    )(dt_raw, dt_bias, a_log)


def _pair_cols(v, p, lo):
    return jnp.where(lo, v[:, 2 * p:2 * p + 1], v[:, 2 * p + 1:2 * p + 2])


def _decay_matrix(acs, acst, h, tri):
    return jnp.exp(jnp.where(tri, acs[:, h:h + 1] - acst[h:h + 1, :], -jnp.inf))


def _dot(a, b, ca, cb):
    return lax.dot_general(a, b, (((ca,), (cb,)), ((), ())), preferred_element_type=F32)


def _ssd_scan_fwd(xbc, dtg, acsg, acstg, d_skip, name):
    s = xbc.shape[0]
    lc, nc = SSD_CHUNK, s // SSD_CHUNK
    xt, bt = GROUP_W // LANES, SSD_D_INNER // LANES

    def body(x_ref, b_ref, c_ref, dt_ref, acs_ref, acst_ref, dsk_ref, y_ref, hp_ref, st_ref):
        @pl.when(pl.program_id(1) == 0)
        def _():
            st_ref[...] = jnp.zeros(st_ref.shape, F32)

        bm, cmb = b_ref[...], c_ref[...].astype(MXU_DT)
        dt, acs, acst = dt_ref[...], acs_ref[...], acst_ref[...]
        cb = _dot(cmb, bm.astype(MXU_DT), 1, 1)
        tri = _iota2((lc, lc), 0) >= _iota2((lc, lc), 1)
        lo = _iota2((lc, PAIR_W), 1) < SSD_HEAD_DIM
        a_last = acs[lc - 1:lc, :]
        e_acs, e_ds, e_cd = jnp.exp(acs), jnp.exp(a_last - acs), jnp.exp(a_last)
        for p in range(PAIRS):
            sl = pl.ds(p * PAIR_W, PAIR_W)
            xp = x_ref[:, sl]
            ub = (xp * _pair_cols(dt, p, lo)).astype(MXU_DT)
            m0 = (cb * _decay_matrix(acs, acst, 2 * p, tri)).astype(MXU_DT)
            m1 = (cb * _decay_matrix(acs, acst, 2 * p + 1, tri)).astype(MXU_DT)
            ht = st_ref[p]
            hp_ref[p] = ht
            y = jnp.where(lo, _dot(m0, ub, 1, 0), _dot(m1, ub, 1, 0))
            y = y + _dot(cmb, ht.astype(MXU_DT), 1, 0) * _pair_cols(e_acs, p, lo)
            y_ref[:, sl] = y + xp * dsk_ref[:, sl]
            bd0 = (bm * e_ds[:, 2 * p:2 * p + 1]).astype(MXU_DT)
            bd1 = (bm * e_ds[:, 2 * p + 1:2 * p + 2]).astype(MXU_DT)
            st_ref[p] = ht * _pair_cols(e_cd, p, lo[:1]) + jnp.where(lo, _dot(bd0, ub, 0, 0), _dot(bd1, ub, 0, 0))

    small = pl.BlockSpec((None, lc, SSD_HPG), lambda g, c: (g, c, 0))
    return pl.pallas_call(
        body, name=name, grid=(SSD_GROUPS, nc),
        in_specs=[pl.BlockSpec((lc, GROUP_W), lambda g, c: (c, g)),
                  pl.BlockSpec((lc, LANES), lambda g, c: (c, bt + g)),
                  pl.BlockSpec((lc, LANES), lambda g, c: (c, bt + SSD_GROUPS + g)),
                  small, small, pl.BlockSpec((None, SSD_HPG, lc), lambda g, c: (g, 0, c)),
                  pl.BlockSpec((1, GROUP_W), lambda g, c: (0, g))],
        out_specs=[pl.BlockSpec((lc, GROUP_W), lambda g, c: (c, g)),
                   pl.BlockSpec((None, PAIRS, SSD_STATE, PAIR_W), lambda g, c: (c, g, 0, 0))],
        out_shape=[_sds((s, SSD_D_INNER)), _sds((nc, SSD_GROUPS * PAIRS, SSD_STATE, PAIR_W))],
        scratch_shapes=[pltpu.VMEM((PAIRS, SSD_STATE, PAIR_W), F32)],
        compiler_params=_params("parallel", "arbitrary"),
    )(xbc, xbc, xbc, dtg, acsg, acstg, d_skip)


def _ssd_scan_bwd(xbc, dtg, acsg, acstg, d_skip, dy, hprev, name):
    s = xbc.shape[0]
    lc, nc = SSD_CHUNK, s // SSD_CHUNK
    bt = SSD_D_INNER // LANES

    def body(x_ref, b_ref, c_ref, dt_ref, acs_ref, acst_ref, dsk_ref, dy_ref, hp_ref, hn_ref,
             dx_ref, db_ref, dc_ref, daq_ref, dar_ref, ddtx_ref, dd_ref, dst_ref, ta_ref, tx_ref):
        @pl.when(pl.program_id(1) == 0)
        def _():
            dst_ref[...] = jnp.zeros(dst_ref.shape, F32)
            dd_ref[...] = jnp.zeros(dd_ref.shape, F32)

        bm, cmb = b_ref[...], c_ref[...].astype(MXU_DT)
        bmb = bm.astype(MXU_DT)
        dt, acs, acst = dt_ref[...], acs_ref[...], acst_ref[...]
        cb = _dot(cmb, bmb, 1, 1)
        tri = _iota2((lc, lc), 0) >= _iota2((lc, lc), 1)
        lo = _iota2((lc, PAIR_W), 1) < SSD_HEAD_DIM
        a_last = acs[lc - 1:lc, :]
        e_acs, e_ds, e_cd = jnp.exp(acs), jnp.exp(a_last - acs), jnp.exp(a_last)
        dcb = jnp.zeros((lc, lc), F32)
        dc_x = jnp.zeros((lc, SSD_STATE), F32)
        db_x = jnp.zeros((lc, SSD_STATE), F32)
        da_in = jnp.zeros((lc, LANES), F32)
        da_out = jnp.zeros((SSD_HPG, lc), F32)
        head_col = _iota2((lc, LANES), 1)
        head_row = _iota2((SSD_HPG, lc), 0)
        last = _iota2((SSD_HPG, lc), 1) == lc - 1
        for p in range(PAIRS):
            sl = pl.ds(p * PAIR_W, PAIR_W)
            xp, dyp, dsk = x_ref[:, sl], dy_ref[:, sl], dsk_ref[:, sl]
            dtp = _pair_cols(dt, p, lo)
            u = xp * dtp
            ub, dyb = u.astype(MXU_DT), dyp.astype(MXU_DT)
            lmat = (_decay_matrix(acs, acst, 2 * p, tri), _decay_matrix(acs, acst, 2 * p + 1, tri))
            m0, m1 = (cb * lmat[0]).astype(MXU_DT), (cb * lmat[1]).astype(MXU_DT)
            ea, dsl = _pair_cols(e_acs, p, lo), _pair_cols(e_ds, p, lo)
            dht, ht = dst_ref[p], hp_ref[p]
            dhtb, htb = dht.astype(MXU_DT), ht.astype(MXU_DT)
            bd0 = (bm * e_ds[:, 2 * p:2 * p + 1]).astype(MXU_DT)
            bd1 = (bm * e_ds[:, 2 * p + 1:2 * p + 2]).astype(MXU_DT)
            du_state = jnp.where(lo, _dot(bd0, dhtb, 1, 0), _dot(bd1, dhtb, 1, 0))
            du = jnp.where(lo, _dot(m0, dyb, 0, 0), _dot(m1, dyb, 0, 0)) + du_state
            y_off = _dot(cmb, htb, 1, 0) * ea
            ta_ref[:, sl] = dyp * y_off - u * du_state
            tx_ref[:, sl] = du * xp
            dx_ref[:, sl] = dtp * du + dsk * dyp
            dd_ref[:, sl] += jnp.sum(dyp * xp, axis=0, keepdims=True)
            dy_h = (jnp.where(lo, dyp, 0.0).astype(MXU_DT), jnp.where(lo, 0.0, dyp).astype(MXU_DT))
            carry = jnp.sum(dht * hn_ref[p], axis=0, keepdims=True)
            for hh in range(2):
                h = 2 * p + hh
                dml = _dot(dy_h[hh], ub, 1, 1) * lmat[hh]
                dcb = dcb + dml
                flow = cb * dml
                da_in = da_in + jnp.where(head_col == h, jnp.sum(flow, axis=1, keepdims=True), 0.0)
                through = jnp.sum(jnp.where(lo[:1] == (hh == 0), carry, 0.0), axis=1, keepdims=True)
                da_out = da_out + jnp.where(head_row == h, jnp.sum(flow, axis=0, keepdims=True)
                                            - jnp.where(last, through, 0.0), 0.0)
            dye = (dyp * ea).astype(MXU_DT)
            dc_x = dc_x + _dot(dye, htb, 1, 1)
            db_x = db_x + _dot((u * dsl).astype(MXU_DT), dhtb, 1, 1)
            dst_ref[p] = dht * _pair_cols(e_cd, p, lo[:1]) + _dot(cmb, dye, 0, 0)
        dcbb = dcb.astype(MXU_DT)
        dc_ref[...] = _dot(dcbb, bmb, 1, 0) + dc_x
        db_ref[...] = _dot(dcbb, cmb, 0, 0) + db_x
        seg_lo = _iota2((GROUP_W, LANES), 1) * SSD_HEAD_DIM
        chan = _iota2((GROUP_W, LANES), 0)
        seg = jnp.logical_and(chan >= seg_lo, chan < seg_lo + SSD_HEAD_DIM).astype(F32)
        da_in = da_in + jnp.dot(ta_ref[...], seg, precision=HIGHEST, preferred_element_type=F32)
        daq_ref[...] = da_in[:, :SSD_HPG]
        dar_ref[...] = da_out
        ddtx_ref[...] = jnp.dot(tx_ref[...], seg, precision=HIGHEST, preferred_element_type=F32)[:, :SSD_HPG]

    rev = lambda c: nc - 1 - c
    small = pl.BlockSpec((None, lc, SSD_HPG), lambda g, c: (g, rev(c), 0))
    small_t = pl.BlockSpec((None, SSD_HPG, lc), lambda g, c: (g, 0, rev(c)))
    wide = pl.BlockSpec((lc, GROUP_W), lambda g, c: (rev(c), g))
    state = lambda at: pl.BlockSpec((None, PAIRS, SSD_STATE, PAIR_W), lambda g, c: (at(c), g, 0, 0))
    return pl.pallas_call(
        body, name=name, grid=(SSD_GROUPS, nc),
        in_specs=[pl.BlockSpec((lc, GROUP_W), lambda g, c: (rev(c), g)),
                  pl.BlockSpec((lc, LANES), lambda g, c: (rev(c), bt + g)),
                  pl.BlockSpec((lc, LANES), lambda g, c: (rev(c), bt + SSD_GROUPS + g)),
                  small, small, small_t, pl.BlockSpec((1, GROUP_W), lambda g, c: (0, g)), wide,
                  state(rev), state(lambda c: jnp.minimum(rev(c) + 1, nc - 1))],
        out_specs=[wide, pl.BlockSpec((lc, LANES), lambda g, c: (rev(c), g)),
                   pl.BlockSpec((lc, LANES), lambda g, c: (rev(c), g)), small, small_t, small,
                   pl.BlockSpec((1, GROUP_W), lambda g, c: (0, g))],
        out_shape=[_sds((s, SSD_D_INNER)), _sds((s, SSD_GROUPS * SSD_STATE)), _sds((s, SSD_GROUPS * SSD_STATE)),
                   _sds((SSD_GROUPS, s, SSD_HPG)), _sds((SSD_GROUPS, SSD_HPG, s)), _sds((SSD_GROUPS, s, SSD_HPG)),
                   _sds((1, SSD_D_INNER))],
        scratch_shapes=[pltpu.VMEM((PAIRS, SSD_STATE, PAIR_W), F32), pltpu.VMEM((lc, GROUP_W), F32),
                        pltpu.VMEM((lc, GROUP_W), F32)],
        compiler_params=_params("parallel", "arbitrary"),
    )(xbc, xbc, xbc, dtg, acsg, acstg, d_skip, dy, hprev, hprev)


def _ssd_post(da_in, da_out, ddtx, dt, dt_raw, dt_bias, a_log, name):
    s = da_in.shape[0]
    lc = SSD_CHUNK

    def body(dain_ref, daout_ref, ddtx_ref, dt_ref, x_ref, b_ref, al_ref, ddr_ref, dal_ref, dbias_ref):
        @pl.when(pl.program_id(0) == 0)
        def _():
            dal_ref[...] = jnp.zeros(dal_ref.shape, F32)
            dbias_ref[...] = jnp.zeros(dbias_ref.shape, F32)

        upper = (_iota2((lc, lc), 0) <= _iota2((lc, lc), 1)).astype(F32)
        dda = jnp.dot(upper, dain_ref[...] - daout_ref[...], precision=HIGHEST, preferred_element_type=F32)
        a = -jnp.exp(al_ref[...])
        ddt = dda * a + ddtx_ref[...]
        dal_ref[...] += jnp.sum(dda * dt_ref[...], axis=0, keepdims=True) * a
        ddr = ddt * _sigmoid(x_ref[...] + b_ref[...])
        ddr_ref[...] = ddr.astype(ddr_ref.dtype)
        dbias_ref[...] += jnp.sum(ddr, axis=0, keepdims=True)

    row = pl.BlockSpec((lc, LANES), lambda i: (i, 0))
    one = pl.BlockSpec((1, LANES), lambda i: (0, 0))
    return pl.pallas_call(
        body, name=name, grid=(s // lc,), in_specs=[row, row, row, row, row, one, one], out_specs=[row, one, one],
        out_shape=[_sds((s, LANES), MXU_DT), _sds((1, LANES)), _sds((1, LANES))],
        compiler_params=_params("arbitrary"),
    )(da_in, da_out, ddtx, dt, dt_raw, dt_bias, a_log)


NORM_GROUP_W = SSD_D_INNER // SSD_GROUPS


def _group_rstd(yz):
    return [lax.rsqrt(jnp.mean(jnp.square(yz[:, g * NORM_GROUP_W:(g + 1) * NORM_GROUP_W]), axis=-1, keepdims=True) + EPS)
            for g in range(SSD_GROUPS)]


def _gated_norm_fwd(y, proj, norm_g, name):
    def fn(yv, z, gv):
        yz = yv * (z * _sigmoid(z))
        parts = [yz[:, g * NORM_GROUP_W:(g + 1) * NORM_GROUP_W] * r for g, r in enumerate(_group_rstd(yz))]
        return (jnp.concatenate(parts, axis=1) * gv,)
    return _rowwise(fn, [y, (proj, 0, SSD_D_INNER)], [norm_g.reshape(1, -1)], [_sds(y.shape, MXU_DT)], [],
                    name=name)[0]


def _gated_norm_bwd(y, proj, norm_g, dout, name):
    def fn(yv, z, do, gv):
        sg = _sigmoid(z)
        sz = z * sg
        yz = yv * sz
        dog = do * gv
        dyz, dg = [], []
        for g, r in enumerate(_group_rstd(yz)):
            cols = slice(g * NORM_GROUP_W, (g + 1) * NORM_GROUP_W)
            yzg, dogg = yz[:, cols], dog[:, cols]
            dyz.append(r * dogg - yzg * (r * r * r) * jnp.mean(yzg * dogg, axis=-1, keepdims=True))
            dg.append(jnp.sum(do[:, cols] * yzg * r, axis=0, keepdims=True))
        dyz = jnp.concatenate(dyz, axis=1)
        return dyz * sz, dyz * yv * (sg * (1.0 + z * (1.0 - sg))), jnp.concatenate(dg, axis=1)
    return _rowwise(fn, [y, (proj, 0, SSD_D_INNER), dout], [norm_g.reshape(1, -1)],
                    [_sds(y.shape), _sds(y.shape, MXU_DT)], [_sds((1, y.shape[1]))], name=name)


def _ssd_fwd(h, p):
    hn = _rms_fwd(h, p["norm_g"], "mix_norm_fwd")
    proj = _matmul(hn, p["w_zx"], name="ssd_in_zx")
    dt_raw = _matmul(hn, p["w_dt"], name="ssd_in_dt")
    xbc = _dwconv_silu_fwd(proj, SSD_D_INNER, SSD_CONV_DIM, p["conv_w"], p["conv_b"], "ssd_conv_fwd")
    dt, acs, acst = _ssd_prep(dt_raw, _lane_pad(p["dt_bias"]), _lane_pad(p["a_log"]), "ssd_prep")
    dtg, acsg = _heads_to_groups(dt), _heads_to_groups(acs)
    acstg = acst[:SSD_HEADS].reshape(SSD_GROUPS, SSD_HPG, -1)
    d_skip = jnp.repeat(p["d"], SSD_HEAD_DIM).reshape(1, -1)
    y, hprev = _ssd_scan_fwd(xbc, dtg, acsg, acstg, d_skip, "ssd_scan_fwd")
    yn = _gated_norm_fwd(y, proj, p["gnorm_g"], "ssd_gnorm_fwd")
    h_out = _matmul(yn, p["w_out"], res=h, name="ssd_out")
    return h_out, (hn, proj, dt_raw, xbc, dt, dtg, acsg, acstg, d_skip, y, hprev, yn)


def _ssd_bwd(h, dh, saved, p):
    hn, proj, dt_raw, xbc, dt, dtg, acsg, acstg, d_skip, y, hprev, yn = saved
    dh, dh_b = dh
    dyn = _matmul(dh_b, p["w_out"], tb=True, name="ssd_out_dx")
    dw_out = _matmul(yn, dh_b, ta=True, out_dtype=MXU_DT, name="ssd_out_dw")
    dy, dz, dgnorm = _gated_norm_bwd(y, proj, p["gnorm_g"], dyn, "ssd_gnorm_bwd")
    dx, dbm, dcm, daq, dar, ddtx, dd = _ssd_scan_bwd(xbc, dtg, acsg, acstg, d_skip, dy, hprev, "ssd_scan_bwd")
    dxbc = jnp.concatenate([dx, dbm, dcm], axis=1)
    dpre, dconv_w, dconv_b = _dwconv_silu_bwd(proj, SSD_D_INNER, SSD_CONV_DIM, dxbc, p["conv_w"], p["conv_b"],
                                              "ssd_conv_bwd")
    ddr, dalog, dbias = _ssd_post(_groups_to_heads(daq), _groups_to_heads(dar.transpose(0, 2, 1)),
                                  _groups_to_heads(ddtx), dt, dt_raw,
                                  _lane_pad(p["dt_bias"]), _lane_pad(p["a_log"]), "ssd_post")
    w_z, w_x = p["w_zx"][:, :SSD_D_INNER], p["w_zx"][:, SSD_D_INNER:]
    dhn = _matmul(dz, w_z, tb=True, name="ssd_in_dx_z")
    dhn = _matmul(dpre, w_x, tb=True, res=dhn, name="ssd_in_dx_x")
    dhn = _matmul(ddr, p["w_dt"], tb=True, res=dhn, name="ssd_in_dx_dt")
    dw_in = jnp.concatenate([_matmul(hn, dz, ta=True, out_dtype=MXU_DT, name="ssd_in_dw_z"),
                             _matmul(hn, dpre, ta=True, out_dtype=MXU_DT, name="ssd_in_dw_x"),
                             _matmul(hn, ddr, ta=True, out_dtype=MXU_DT, name="ssd_in_dw_dt")[:, :SSD_HEADS]], axis=1)
    *dh_in, dnorm_g = _rms_bwd(h, p["norm_g"], dhn, dh, "mix_norm_bwd")
    grads = dict(norm_g=dnorm_g, w_in=dw_in, conv_w=dconv_w, conv_b=dconv_b, dt_bias=dbias[:, :SSD_HEADS],
                 a_log=dalog[:, :SSD_HEADS], d=dd.reshape(SSD_HEADS, SSD_HEAD_DIM).sum(axis=1).reshape(1, -1),
                 gnorm_g=dgnorm, w_out=dw_out)
    return dh_in, grads


FOX_PAIRS = FOX_HEADS // 2
ATT_TQ = 256
ATT_TK = 512
NEG_BIG = -1e30
FOX_SCALE = FOX_HEAD_DIM ** -0.5
FOX_AUG_D = FOX_HEADS * LANES
QSIDE = FOX_HEAD_DIM
KSIDE = FOX_HEAD_DIM + 3


def _split3(x):
    a = x.astype(MXU_DT).astype(F32)
    b = (x - a).astype(MXU_DT).astype(F32)
    return a, b, (x - a - b).astype(MXU_DT).astype(F32)


def _head_tiles(pair_tile):
    return pair_tile, pltpu.roll(pair_tile, FOX_HEAD_DIM, 1)


def _fill_lanes(base, lane, first, values):
    for i, v in enumerate(values):
        base = jnp.where(lane == first + i, v, base)
    return base


def _pair_tile(lane, tile0, tile1):
    return jnp.where(lane < FOX_HEAD_DIM, tile0, pltpu.roll(tile1, FOX_HEAD_DIM, 1))


def _compact_heads(a, lane):
    return jnp.concatenate([_pair_tile(lane, a[:, 2 * j * LANES:(2 * j + 1) * LANES],
                                       a[:, (2 * j + 1) * LANES:(2 * j + 2) * LANES]) for j in range(FOX_PAIRS)], axis=1)


def _head_sum_matrix():
    return (_iota2((LANES, LANES), 0) < FOX_HEAD_DIM) == (_iota2((LANES, LANES), 1) < FOX_HEAD_DIM)


def _head_sums(x):
    bd = _head_sum_matrix().astype(F32)
    parts = [jnp.dot(x[:, j * LANES:(j + 1) * LANES], bd, precision=HIGHEST, preferred_element_type=F32)
             for j in range(x.shape[1] // LANES)]
    return parts[0] if len(parts) == 1 else jnp.concatenate(parts, axis=1)


def _fox_prep_fwd(proj, f_raw, qg, kg, b_f, name):
    s = proj.shape[0]
    tr = min(ATT_TQ, s)

    def body(q_ref, k_ref, v_ref, f_ref, qg_ref, kg_ref, b_ref, qa_ref, ka_ref, va_ref, carry_ref):
        @pl.when(pl.program_id(0) == 0)
        def _():
            carry_ref[...] = jnp.zeros(carry_ref.shape, F32)

        normed = []
        for x_ref, g_ref in ((q_ref, qg_ref), (k_ref, kg_ref)):
            x = x_ref[...]
            r = lax.rsqrt(_head_sums(x * x) * (1.0 / FOX_HEAD_DIM) + EPS)
            normed.append(x * r * g_ref[...])
        qn, kn, v = normed[0] * FOX_SCALE, normed[1], v_ref[...]
        x = f_ref[...] + b_ref[...]
        lf = jnp.minimum(x, 0.0) - jnp.log1p(jnp.exp(-jnp.abs(x)))
        lower = (_iota2((tr, tr), 0) >= _iota2((tr, tr), 1)).astype(F32)
        cum = jnp.dot(lower, lf, precision=HIGHEST, preferred_element_type=F32) + carry_ref[...]
        carry_ref[...] += jnp.sum(lf, axis=0, keepdims=True)
        first = _iota2((LANES, FOX_D), 0) * FOX_HEAD_DIM
        chan = _iota2((LANES, FOX_D), 1)
        spread = jnp.logical_and(chan >= first, chan < first + FOX_HEAD_DIM).astype(F32)
        cum = jnp.dot(cum, spread, precision=HIGHEST, preferred_element_type=F32)
        lane = _iota2((tr, LANES), 1)
        ones = jnp.where(jnp.logical_and(lane >= QSIDE, lane < KSIDE + 3), 1.0, 0.0)
        for j in range(FOX_PAIRS):
            cols = slice(j * LANES, (j + 1) * LANES)
            tiles = zip(_head_tiles(qn[:, cols]), _head_tiles(kn[:, cols]), _head_tiles(v[:, cols]),
                        reversed(_head_tiles(cum[:, cols])))
            for hh, (qt, kt, vt, ct) in enumerate(tiles):
                out = slice((2 * j + hh) * LANES, (2 * j + hh + 1) * LANES)
                c3 = _split3(ct)
                head = lane < FOX_HEAD_DIM
                qa_ref[:, out] = _fill_lanes(jnp.where(head, qt, ones), lane, QSIDE, c3).astype(qa_ref.dtype)
                ka_ref[:, out] = _fill_lanes(jnp.where(head, kt, ones), lane, KSIDE, [-c for c in c3]).astype(ka_ref.dtype)
                va_ref[:, out] = jnp.where(head, vt, jnp.where(lane < KSIDE, 1.0, 0.0)).astype(va_ref.dtype)

    wide = lambda cb: pl.BlockSpec((tr, FOX_D), lambda i: (i, cb))
    aug = pl.BlockSpec((tr, FOX_AUG_D), lambda i: (i, 0))
    one = lambda n: pl.BlockSpec((1, n), lambda i: (0, 0))
    return pl.pallas_call(
        body, name=name, grid=(s // tr,),
        in_specs=[wide(0), wide(1), wide(2), pl.BlockSpec((tr, LANES), lambda i: (i, 0)), one(FOX_D), one(FOX_D),
                  one(LANES)],
        out_specs=[aug, aug, aug], out_shape=[_sds((s, FOX_AUG_D), MXU_DT)] * 3,
        scratch_shapes=[pltpu.VMEM((1, LANES), F32)],
        compiler_params=_params("arbitrary"),
    )(proj, proj, proj, f_raw, qg, kg, b_f)


def _fox_prep_bwd(proj, f_raw, qg, kg, b_f, dqa, dka, row_sums, col_sums, name):
    s = proj.shape[0]
    tr = min(ATT_TQ, s)
    nb = s // tr

    def body(q_ref, k_ref, f_ref, qg_ref, kg_ref, b_ref, dqa_ref, dka_ref, dcq_ref, dck_ref,
             dq_ref, dk_ref, df_ref, dqg_ref, dkg_ref, db_ref, carry_ref):
        @pl.when(pl.program_id(0) == 0)
        def _():
            carry_ref[...] = jnp.zeros(carry_ref.shape, F32)
            dqg_ref[...] = jnp.zeros(dqg_ref.shape, F32)
            dkg_ref[...] = jnp.zeros(dkg_ref.shape, F32)
            db_ref[...] = jnp.zeros(db_ref.shape, F32)

        lane = _iota2((tr, LANES), 1)
        for x_ref, g_ref, dt_ref, scale, dx_ref, dg_ref in ((q_ref, qg_ref, dqa_ref, FOX_SCALE, dq_ref, dqg_ref),
                                                            (k_ref, kg_ref, dka_ref, 1.0, dk_ref, dkg_ref)):
            x, dy = x_ref[...], _compact_heads(dt_ref[...], lane) * scale
            r = lax.rsqrt(_head_sums(x * x) * (1.0 / FOX_HEAD_DIM) + EPS)
            dyg = dy * g_ref[...]
            dx = r * dyg - x * (r * r * r) * (_head_sums(x * dyg) * (1.0 / FOX_HEAD_DIM))
            dx_ref[...] = dx.astype(dx_ref.dtype)
            dg_ref[...] += jnp.sum(dy * x * r, axis=0, keepdims=True)
        dc = dcq_ref[...] - dck_ref[...]
        upper = (_iota2((tr, tr), 0) <= _iota2((tr, tr), 1)).astype(F32)
        dlf = jnp.dot(upper, dc, precision=HIGHEST, preferred_element_type=F32) + carry_ref[...]
        carry_ref[...] += jnp.sum(dc, axis=0, keepdims=True)
        df = dlf * _sigmoid(-(f_ref[...] + b_ref[...]))
        df_ref[...] = df.astype(df_ref.dtype)
        db_ref[...] += jnp.sum(df, axis=0, keepdims=True)

    wide = lambda cb: pl.BlockSpec((tr, FOX_D), lambda i: (nb - 1 - i, cb))
    aug = pl.BlockSpec((tr, FOX_AUG_D), lambda i: (nb - 1 - i, 0))
    row = pl.BlockSpec((tr, LANES), lambda i: (nb - 1 - i, 0))
    one = lambda n: pl.BlockSpec((1, n), lambda i: (0, 0))
    return pl.pallas_call(
        body, name=name, grid=(nb,),
        in_specs=[wide(0), wide(1), row, one(FOX_D), one(FOX_D), one(LANES), aug, aug, row, row],
        out_specs=[wide(0), wide(0), row, one(FOX_D), one(FOX_D), one(LANES)],
        out_shape=[_sds((s, FOX_D), MXU_DT), _sds((s, FOX_D), MXU_DT), _sds((s, LANES), MXU_DT),
                   _sds((1, FOX_D)), _sds((1, FOX_D)), _sds((1, LANES))],
        scratch_shapes=[pltpu.VMEM((1, LANES), F32)],
        compiler_params=_params("arbitrary"),
    )(proj, proj, f_raw, qg, kg, b_f, dqa, dka, row_sums, col_sums)


def _fox_attn_fwd(qa, ka, va, proj, name):
    s = qa.shape[0]
    tq, tk = min(ATT_TQ, s), min(ATT_TK, s)
    assert s % tq == 0 and s % tk == 0
    gt = 3 * FOX_D // LANES
    head_lanes = [slice(hh * LANES, (hh + 1) * LANES) for hh in range(2)]

    def body(qa_ref, ka_ref, va_ref, g_ref, o_ref, og_ref, qb_ref):
        qi = pl.program_id(1)
        lane = _iota2((tq, LANES), 1)
        ahead = _iota2((tq, tk), 0) - _iota2((tq, tk), 1)
        q = [qa_ref[:, hs] for hs in head_lanes]

        def kv_step(j, carry, masked):
            rows = pl.ds(pl.multiple_of(j * tk, tk), tk)
            out = []
            for hh, hs in enumerate(head_lanes):
                m, acc = carry[2 * hh:2 * hh + 2]
                sc = _dot(q[hh], ka_ref[rows, hs], 1, 1)
                if masked:
                    sc = jnp.where(ahead >= j * tk - qi * tq, sc, NEG_BIG)
                m_new = jnp.maximum(m, jnp.max(sc, axis=1, keepdims=True))
                pr = jnp.exp(sc - m_new).astype(MXU_DT)
                out += [m_new, jnp.exp(m - m_new) * acc + _dot(pr, va_ref[rows, hs], 1, 0)]
            return tuple(out)

        n_clear = lax.div(qi * tq, tk)
        n_all = lax.div((qi + 1) * tq + tk - 1, tk)
        init = (jnp.full((tq, 1), NEG_BIG, F32), jnp.zeros((tq, LANES), F32)) * 2
        carry = lax.fori_loop(0, n_clear, functools.partial(kv_step, masked=False), init)
        carry = lax.fori_loop(n_clear, n_all, functools.partial(kv_step, masked=True), carry)
        heads = []
        for hh, hs in enumerate(head_lanes):
            m, acc = carry[2 * hh:2 * hh + 2]
            l = acc[:, QSIDE:QSIDE + 1]
            heads.append(acc / l)
            qf = q[hh].astype(F32)
            bias = qf[:, QSIDE:QSIDE + 1] + qf[:, QSIDE + 1:QSIDE + 2] + qf[:, QSIDE + 2:QSIDE + 3]
            qb_ref[:, hs] = _fill_lanes(qf, lane, QSIDE, _split3(bias - (m + jnp.log(l)))).astype(qb_ref.dtype)
        o = _pair_tile(lane, heads[0], heads[1])
        o_ref[...] = o
        og_ref[...] = (o * _sigmoid(g_ref[...])).astype(og_ref.dtype)

    blk2 = pl.BlockSpec((tq, 2 * LANES), lambda p, i: (i, p))
    seq2 = pl.BlockSpec((s, 2 * LANES), lambda p, i: (0, p))
    blk = pl.BlockSpec((tq, LANES), lambda p, i: (i, p))
    return pl.pallas_call(
        body, name=name, grid=(FOX_PAIRS, s // tq),
        in_specs=[blk2, seq2, seq2, pl.BlockSpec((tq, LANES), lambda p, i: (i, gt + p))],
        out_specs=[blk, blk, blk2],
        out_shape=[_sds((s, FOX_D)), _sds((s, FOX_D), MXU_DT), _sds((s, FOX_AUG_D), MXU_DT)],
        compiler_params=_params("parallel", "parallel"),
    )(qa, ka, va, proj)


def _fox_gate_bwd(dog, o, proj, name):
    def fn(dogv, ov, gate):
        sg = _sigmoid(gate)
        do = dogv * sg
        delta = _head_sums(do * ov)
        lane = _iota2((do.shape[0], LANES), 1)
        tiles = []
        for j in range(FOX_PAIRS):
            cols = slice(j * LANES, (j + 1) * LANES)
            for dt, dl in zip(_head_tiles(do[:, cols]), reversed(_head_tiles(delta[:, cols]))):
                tiles.append(_fill_lanes(jnp.where(lane < FOX_HEAD_DIM, dt, 0.0), lane, QSIDE,
                                         [-d for d in _split3(dl)]))
        return dogv * ov * sg * (1.0 - sg), jnp.concatenate(tiles, axis=1)
    return _rowwise(fn, [dog, o, (proj, 3, FOX_D)], [], [_sds(o.shape, MXU_DT), _sds((o.shape[0], FOX_AUG_D), MXU_DT)],
                    [], name=name)


def _fox_attn_bwd(qb, ka, va, doa, name):
    s = qb.shape[0]
    tq, tk = min(ATT_TQ, s), min(ATT_TK, s)
    nq, nk = s // tq, s // tk
    head_lanes = [slice(hh * LANES, (hh + 1) * LANES) for hh in range(2)]

    def body(qb_ref, doa_ref, ka_ref, va_ref, dqa_ref, dka_ref, dv_ref):
        kj = pl.program_id(1)

        @pl.when(kj == 0)
        def _():
            dqa_ref[...] = jnp.zeros(dqa_ref.shape, F32)

        ahead = _iota2((tq, tk), 0) - _iota2((tq, tk), 1)
        kb = [ka_ref[:, hs] for hs in head_lanes]
        vb = [va_ref[:, hs] for hs in head_lanes]

        def q_step(i, carry, masked):
            rows = pl.ds(pl.multiple_of(i * tq, tq), tq)
            out = []
            for hh, hs in enumerate(head_lanes):
                dk, dv = carry[2 * hh:2 * hh + 2]
                q, do = qb_ref[rows, hs], doa_ref[rows, hs]
                pr = jnp.exp(_dot(q, kb[hh], 1, 1))
                if masked:
                    pr = jnp.where(ahead >= kj * tk - i * tq, pr, 0.0)
                dv = dv + _dot(pr.astype(MXU_DT), do, 0, 0)
                ds = (pr * _dot(do, vb[hh], 1, 1)).astype(MXU_DT)
                dqa_ref[rows, hs] += _dot(ds, kb[hh], 1, 0)
                out += [dk + _dot(ds, q, 0, 0), dv]
            return tuple(out)

        first = lax.div(kj * tk, tq)
        n_masked = lax.div((kj + 1) * tk + tq - 1, tq)
        carry = lax.fori_loop(first, n_masked, functools.partial(q_step, masked=True),
                              (jnp.zeros((tk, LANES), F32),) * 4)
        dk0, dv0, dk1, dv1 = lax.fori_loop(n_masked, nq, functools.partial(q_step, masked=False), carry)
        dka_ref[:, head_lanes[0]] = dk0
        dka_ref[:, head_lanes[1]] = dk1
        dv_ref[...] = _pair_tile(_iota2((tk, LANES), 1), dv0, dv1).astype(dv_ref.dtype)

    seq2 = pl.BlockSpec((s, 2 * LANES), lambda p, j: (0, p))
    blk2 = pl.BlockSpec((tk, 2 * LANES), lambda p, j: (j, p))
    return pl.pallas_call(
        body, name=name, grid=(FOX_PAIRS, nk), in_specs=[seq2, seq2, blk2, blk2],
        out_specs=[seq2, blk2, pl.BlockSpec((tk, LANES), lambda p, j: (j, p))],
        out_shape=[_sds((s, FOX_AUG_D)), _sds((s, FOX_AUG_D)), _sds((s, FOX_D), MXU_DT)],
        compiler_params=_params("parallel", "arbitrary"),
    )(qb, doa, ka, va)


def _fox_fwd(h, p):
    hn = _rms_fwd(h, p["norm_g"], "mix_norm_fwd")
    proj = _matmul(hn, p["w_qkvg"], name="fox_in_qkvg")
    f_raw = _matmul(hn, p["w_f"], name="fox_in_f")
    qg = jnp.tile(p["q_norm_g"], FOX_HEADS).reshape(1, -1)
    kg = jnp.tile(p["k_norm_g"], FOX_HEADS).reshape(1, -1)
    qa, ka, va = _fox_prep_fwd(proj, f_raw, qg, kg, _lane_pad(p["b_f"]), "fox_prep_fwd")
    o, og, qb = _fox_attn_fwd(qa, ka, va, proj, "fox_attn_fwd")
    h_out = _matmul(og, p["w_out"], res=h, name="fox_out")
    return h_out, (hn, proj, f_raw, qg, kg, ka, va, qb, o, og)


def _fox_bwd(h, dh, saved, p):
    hn, proj, f_raw, qg, kg, ka, va, qb, o, og = saved
    s = h.shape[0]
    dh, dh_b = dh
    dog = _matmul(dh_b, p["w_out"], tb=True, name="fox_out_dx")
    dw_out = _matmul(og, dh_b, ta=True, out_dtype=MXU_DT, name="fox_out_dw")
    dgate, doa = _fox_gate_bwd(dog, o, proj, "fox_gate_bwd")
    dqa, dka, dv = _fox_attn_bwd(qb, ka, va, doa, "fox_attn_bwd")
    head_cols = lambda a, lane: jnp.pad(a.reshape(s, FOX_HEADS, LANES)[:, :, lane], ((0, 0), (0, LANES - FOX_HEADS)))
    dq, dk, df, dqg, dkg, dbf = _fox_prep_bwd(proj, f_raw, qg, kg, _lane_pad(p["b_f"]), dqa, dka,
                                              head_cols(dqa, QSIDE), head_cols(dka, KSIDE), "fox_prep_bwd")
    dproj = jnp.concatenate([dq, dk, dv, dgate], axis=1)
    dhn = _matmul(dproj, p["w_qkvg"], tb=True, name="fox_in_dx_qkvg")
    dhn = _matmul(df, p["w_f"], tb=True, res=dhn, name="fox_in_dx_f")
    dw_in = jnp.concatenate([_matmul(hn, dproj, ta=True, out_dtype=MXU_DT, name="fox_in_dw_qkvg"),
                             _matmul(hn, df, ta=True, out_dtype=MXU_DT, name="fox_in_dw_f")[:, :FOX_HEADS]], axis=1)
    *dh_in, dnorm_g = _rms_bwd(h, p["norm_g"], dhn, dh, "mix_norm_bwd")
    fold = lambda g: g.reshape(FOX_HEADS, FOX_HEAD_DIM).sum(axis=0).reshape(1, -1)
    grads = dict(norm_g=dnorm_g, w_in=dw_in, b_f=dbf[:, :FOX_HEADS], q_norm_g=fold(dqg), k_norm_g=fold(dkg),
                 w_out=dw_out)
    return dh_in, grads


def _my_index():
    return 4 * lax.axis_index("x") + 2 * lax.axis_index("y") + lax.axis_index("c")


def _peer(k):
    x, y, c = lax.axis_index("x"), lax.axis_index("y"), lax.axis_index("c")
    flip = lambda v, bit: 1 - v if bit else v
    return (flip(x, k & 4), flip(y, k & 2), flip(c, k & 1))


SEM_SPEC = pl.BlockSpec(memory_space=pltpu.SEMAPHORE)
DATAFLOW = pltpu.SideEffectType.DATAFLOW_SIDE_EFFECTING


def _at(ref, idx):
    return ref.at[tuple(idx)] if idx else ref


def _copies_start(name, srcs, lands, groups):
    ns, nl = len(srcs), len(lands)
    items = [item for group in groups for item in group]

    def body(*refs):
        src_refs, land_refs = refs[:ns], refs[ns:ns + nl]
        sems = refs[ns + nl:ns + nl + 2 * len(items)]
        me = _my_index()
        for t, (i, src_slot, j, dst_slot, _) in enumerate(items):
            for k in range(1, N_DEV):
                pltpu.make_async_remote_copy(
                    src_ref=_at(src_refs[i], src_slot(me, k)), dst_ref=_at(land_refs[j], dst_slot(me, k)),
                    send_sem=sems[2 * t], recv_sem=sems[2 * t + 1], device_id=_peer(k),
                    device_id_type=MESH_IDS).start()
        refs[-1][...] = jnp.zeros(refs[-1].shape, F32)

    hbm = pl.BlockSpec(memory_space=pltpu.HBM)
    bufs = [pltpu.with_memory_space_constraint(a, pltpu.HBM) for a in list(srcs) + list(lands)]
    n_sems = 2 * len(items)
    out = pl.pallas_call(
        body, name=name, in_specs=[hbm] * (ns + nl),
        out_specs=(*[SEM_SPEC] * n_sems, *[hbm] * (ns + nl), pl.BlockSpec(memory_space=pltpu.VMEM)),
        out_shape=(*[pltpu.SemaphoreType.DMA(())] * n_sems, *[pltpu.HBM(a.shape, a.dtype) for a in bufs],
                   _sds((8, LANES))),
        input_output_aliases={i: n_sems + i for i in range(ns + nl)},
        compiler_params=pltpu.CompilerParams(has_side_effects=DATAFLOW),
    )(*bufs)
    sems, t = [], 0
    for group in groups:
        sems.append([(out[2 * (t + u)], out[2 * (t + u) + 1]) for u in range(len(group))])
        t += len(group)
    return sems, list(out[n_sems:n_sems + ns]), list(out[n_sems + ns:n_sems + ns + nl]), out[-1]


def _copies_wait(name, keep, lands, sems, group, after):
    nk, nl, n = len(keep), len(lands), len(group)

    def body(*refs):
        land_refs = refs[nk:nk + nl]
        sem_refs = refs[nk + nl:nk + nl + 2 * n]
        me = _my_index()
        copies = []
        for t, (_, _, j, _, seven) in enumerate(group):
            blocks = _at(land_refs[j], seven(me))
            copies.append(pltpu.make_async_remote_copy(src_ref=blocks, dst_ref=blocks, send_sem=sem_refs[2 * t],
                                                       recv_sem=sem_refs[2 * t + 1], device_id=_peer(1),
                                                       device_id_type=MESH_IDS))
        for cp in copies:
            cp.wait_recv()
        for cp in copies:
            cp.wait_send()

    hbm = pl.BlockSpec(memory_space=pltpu.HBM)
    bufs = list(keep) + list(lands)
    out = pl.pallas_call(
        body, name=name, in_specs=[hbm] * (nk + nl) + [SEM_SPEC] * (2 * n) + [pl.BlockSpec(memory_space=pl.ANY)],
        out_specs=[hbm] * (nk + nl), out_shape=[pltpu.HBM(a.shape, a.dtype) for a in bufs],
        input_output_aliases={i: i for i in range(nk + nl)},
        compiler_params=pltpu.CompilerParams(has_side_effects=DATAFLOW),
    )(*bufs, *[s for pair in sems for s in pair], after)
    return list(out[nk:])


def _allreduce_small(buf, name):
    def body(in_ref, all_ref, sum_ref, send_sems, recv_sems):
        me = _my_index()
        all_ref[me] = in_ref[...]

        def copy(k, slot):
            return pltpu.make_async_remote_copy(
                src_ref=in_ref, dst_ref=all_ref.at[slot], send_sem=send_sems.at[k - 1], recv_sem=recv_sems.at[k - 1],
                device_id=_peer(k), device_id_type=MESH_IDS)

        for k in range(1, N_DEV):
            copy(k, me).start()
        for k in range(1, N_DEV):
            copy(k, jnp.bitwise_xor(me, k)).wait_recv()
        for k in range(1, N_DEV):
            copy(k, me).wait_send()
        acc = all_ref[0]
        for j in range(1, N_DEV):
            acc = acc + all_ref[j]
        sum_ref[...] = acc

    vmem = pl.BlockSpec(memory_space=pltpu.VMEM)
    return pl.pallas_call(
        body, name=name, in_specs=[vmem], out_specs=[vmem, vmem],
        out_shape=[_sds((N_DEV,) + buf.shape), _sds(buf.shape)],
        scratch_shapes=[pltpu.SemaphoreType.DMA((N_DEV - 1,)), pltpu.SemaphoreType.DMA((N_DEV - 1,))],
        compiler_params=pltpu.CompilerParams(vmem_limit_bytes=VMEM_LIMIT_BYTES),
    )(buf)[1]


def _adamw_math(w, g, m, v):
    m = ADAM_B1 * m + (1.0 - ADAM_B1) * g
    v = ADAM_B2 * v + (1.0 - ADAM_B2) * (g * g)
    m_hat = m / (1.0 - ADAM_B1 ** ADAM_STEP)
    v_hat = v / (1.0 - ADAM_B2 ** ADAM_STEP)
    return -ADAM_LR * (m_hat / (jnp.sqrt(v_hat) + ADAM_EPS) + ADAM_WD * w), m, v


def _row_tile(rows, cap=256, mult=16):
    best = None
    for t in range(mult, min(rows, cap) + 1, mult):
        if rows % t == 0:
            best = t
    assert best is not None, rows
    return best


def _adamw_sharded(w, m, v, partials, name):
    rows, cols = w.shape
    tr = _row_tile(rows)

    def body(w_ref, m_ref, v_ref, p_ref, g_ref, d_ref, nm_ref, nv_ref):
        g = p_ref[0].astype(F32)
        for j in range(1, N_DEV):
            g = g + p_ref[j].astype(F32)
        delta, m_new, v_new = _adamw_math(w_ref[...], g, m_ref[...], v_ref[...])
        g_ref[...], d_ref[...], nm_ref[...], nv_ref[...] = g, delta, m_new, v_new

    blk = pl.BlockSpec((tr, cols), lambda i: (i, 0))
    return pl.pallas_call(
        body, name=name, grid=(rows // tr,),
        in_specs=[blk, blk, blk, pl.BlockSpec((N_DEV, tr, cols), lambda i: (0, i, 0))],
        out_specs=[blk] * 4, out_shape=[_sds(w.shape)] * 4, compiler_params=_params("parallel"),
    )(w, m, v, partials)


def _adamw_small(w, g, m, v, name):
    def body(w_ref, g_ref, m_ref, v_ref, d_ref, nm_ref, nv_ref):
        d_ref[...], nm_ref[...], nv_ref[...] = _adamw_math(w_ref[...], g_ref[...], m_ref[...], v_ref[...])

    return pl.pallas_call(body, name=name, out_shape=[_sds(w.shape)] * 3,
                          compiler_params=_params())(w, g, m, v)


def _pack(arrays):
    flat = jnp.concatenate([a.reshape(-1).astype(F32) for a in arrays])
    pad = -flat.shape[0] % (8 * LANES)
    return jnp.pad(flat, (0, pad)).reshape(-1, LANES)


def _unpack(buf, shapes):
    flat, out, off = buf.reshape(-1), [], 0
    for shp in shapes:
        size = math.prod(shp)
        out.append(flat[off:off + size].reshape(shp))
        off += size
    return out


WEIGHTS = ["mix_norm_g", "ffn_norm_g", "ssd_w_in", "ssd_conv_w", "ssd_conv_b", "ssd_dt_bias", "ssd_a_log", "ssd_d",
           "ssd_norm_g", "ssd_w_out", "fox_w_in", "fox_b_f", "fox_q_norm_g", "fox_k_norm_g", "fox_w_out", "ffn_w_up",
           "ffn_conv_w", "ffn_conv_b", "ffn_w_down", "final_norm_g"]
BIG = ["ssd_w_in", "ssd_w_out", "fox_w_in", "fox_w_out", "ffn_w_up", "ffn_w_down"]
COLUMN_SHARDED = ["ssd_w_in", "fox_w_in", "ffn_w_up"]
CONV = ["ssd_conv_w", "ffn_conv_w"]
REPLICATED = [n for n in WEIGHTS if n not in BIG + CONV]
DEPTH = 4
LAYER_SHARDED = (["ssd_w_in", "ssd_conv_w", "ssd_w_out", "ffn_w_up", "ffn_conv_w", "ffn_w_down"],
                 ["fox_w_in", "fox_w_out", "ffn_w_up", "ffn_conv_w", "ffn_w_down"])


def _to_shards(full, on_columns):
    nl, r, c = full.shape
    if on_columns:
        return full.reshape(nl, r, N_DEV, c // N_DEV).transpose(2, 0, 1, 3)
    return full.reshape(nl, N_DEV, r // N_DEV, c).transpose(1, 0, 2, 3)


def _pad_cols(w):
    return jnp.pad(w, ((0, 0), (0, LANES - w.shape[1])))


def kernel(x, mix_norm_g, ffn_norm_g, ssd_w_in, ssd_conv_w, ssd_conv_b, ssd_dt_bias, ssd_a_log, ssd_d, ssd_norm_g, ssd_w_out, fox_w_in, fox_b_f, fox_q_norm_g, fox_k_norm_g, fox_w_out, ffn_w_up, ffn_conv_w, ffn_conv_b, ffn_w_down, final_norm_g, loss_target, m_mix_norm_g, m_ffn_norm_g, m_ssd_w_in, m_ssd_conv_w, m_ssd_conv_b, m_ssd_dt_bias, m_ssd_a_log, m_ssd_d, m_ssd_norm_g, m_ssd_w_out, m_fox_w_in, m_fox_b_f, m_fox_q_norm_g, m_fox_k_norm_g, m_fox_w_out, m_ffn_w_up, m_ffn_conv_w, m_ffn_conv_b, m_ffn_w_down, m_final_norm_g, v_mix_norm_g, v_ffn_norm_g, v_ssd_w_in, v_ssd_conv_w, v_ssd_conv_b, v_ssd_dt_bias, v_ssd_a_log, v_ssd_d, v_ssd_norm_g, v_ssd_w_out, v_fox_w_in, v_fox_b_f, v_fox_q_norm_g, v_fox_k_norm_g, v_fox_w_out, v_ffn_w_up, v_ffn_conv_w, v_ffn_conv_b, v_ffn_w_down, v_final_norm_g):
    given = dict(locals())
    w = {n: given[n] for n in WEIGHTS}
    mom = {n: given["m_" + n] for n in WEIGHTS}
    var = {n: given["v_" + n] for n in WEIGHTS}
    me = _my_index()

    sharded = BIG + CONV
    shards = [w[n].astype(MXU_DT) if n in BIG else w[n] for n in sharded]
    zones, groups = [], []
    for i in range(DEPTH):
        group = []
        for n in LAYER_SHARDED[i % 2]:
            layer = i if n.startswith("ffn") else i // 2
            group.append((sharded.index(n), functools.partial(lambda me, k, layer: (layer,), layer=layer),
                          len(zones), lambda me, k: (me,), lambda me: (pl.ds(0, N_DEV - 1),)))
            zones.append((n, layer))
        groups.append(group)
    empty = [lax.empty((N_DEV,) + shards[sharded.index(n)].shape[1:], shards[sharded.index(n)].dtype) for n, _ in zones]
    gather_sems, shards_kept, landing, _ = _copies_start("gather_start", shards, empty, groups)

    def layer_weights(i, after):
        mine = [item[2] for item in groups[i]]
        local_group = [item[:2] + (pos,) + item[3:] for pos, item in enumerate(groups[i])]
        landed = _copies_wait(f"gather_wait_{i}", shards_kept if i == DEPTH - 1 else [], [landing[z] for z in mine],
                              gather_sems[i], local_group, after)
        out = {}
        for z, stack in zip(mine, landed):
            n, layer = zones[z]
            stack = lax.dynamic_update_index_in_dim(stack, shards[sharded.index(n)][layer], me, 0)
            _, r, c = stack.shape
            on_columns = n in COLUMN_SHARDED or n in CONV
            out[n] = stack.transpose(1, 0, 2).reshape(r, N_DEV * c) if on_columns else stack.reshape(N_DEV * r, c)
        return out

    def ssd_params(i, j, full):
        w_in = full["ssd_w_in"]
        return dict(norm_g=w["mix_norm_g"][i], w_zx=w_in[:, :SSD_ZX], w_dt=_pad_cols(w_in[:, SSD_ZX:]),
                    conv_w=full["ssd_conv_w"], conv_b=w["ssd_conv_b"][j], dt_bias=w["ssd_dt_bias"][j],
                    a_log=w["ssd_a_log"][j], d=w["ssd_d"][j], gnorm_g=w["ssd_norm_g"][j], w_out=full["ssd_w_out"])

    def fox_params(i, j, full):
        w_in = full["fox_w_in"]
        return dict(norm_g=w["mix_norm_g"][i], w_qkvg=w_in[:, :4 * FOX_D], w_f=_pad_cols(w_in[:, 4 * FOX_D:]),
                    b_f=w["fox_b_f"][j], q_norm_g=w["fox_q_norm_g"][j], k_norm_g=w["fox_k_norm_g"][j],
                    w_out=full["fox_w_out"])

    def ffn_params(i, full):
        return (w["ffn_norm_g"][i], full["ffn_w_up"], full["ffn_conv_w"], w["ffn_conv_b"][i], full["ffn_w_down"])

    h = x[0]
    tape = []
    for i in range(DEPTH):
        j = i // 2
        full = layer_weights(i, h)
        if i % 2 == 0:
            mp = ssd_params(i, j, full)
            h_mid, mix_saved = _ssd_fwd(h, mp)
        else:
            mp = fox_params(i, j, full)
            h_mid, mix_saved = _fox_fwd(h, mp)
        fp = ffn_params(i, full)
        h_out, ffn_saved = _ffn_fwd(h_mid, *fp, "ffn")
        tape.append((h, mp, mix_saved, h_mid, fp, ffn_saved))
        h = h_out
    *dh, dfinal_g, loss_part = _loss_head(h, w["final_norm_g"], loss_target[0], "loss_head")

    grads = {n: [None] * w[n].shape[0] for n in WEIGHTS if n not in BIG + ["final_norm_g"]}
    partials = [lax.empty((N_DEV,) + w[n].shape, MXU_DT) for n in BIG]
    in_flight, token = [], None

    def send_partials(i, layer_grads):
        nonlocal partials, token
        srcs, group = [], []
        for n, (layer, grad) in layer_grads.items():
            slots = _to_shards(grad[None], n in COLUMN_SHARDED)[:, 0]
            mine = lax.dynamic_index_in_dim(slots, me, 0, keepdims=False)
            b = BIG.index(n)
            partials[b] = lax.dynamic_update_slice(partials[b], mine[None, None], (me, layer, 0, 0))
            group.append((len(srcs), lambda me, k: (jnp.bitwise_xor(me, k),), b,
                          functools.partial(lambda me, k, layer: (me, layer), layer=layer),
                          functools.partial(lambda me, layer: (pl.ds(0, N_DEV - 1), layer), layer=layer)))
            srcs.append(slots)
        sems, kept, partials, token = _copies_start(f"scatter_start_{i}", srcs, partials, [group])
        in_flight.append((i, sems[0], kept, group))

    for i in reversed(range(DEPTH)):
        j = i // 2
        h_in, mp, mix_saved, h_mid, fp, ffn_saved = tape[i]
        if token is not None:
            fp = (fp[0] + token[0, 0],) + fp[1:]
        dh, g = _ffn_bwd(h_mid, dh, ffn_saved, *fp)
        grads["ffn_norm_g"][i], grads["ffn_conv_w"][i], grads["ffn_conv_b"][i] = g["norm_g"][0], g["conv_w"], g["conv_b"][0]
        big = {"ffn_w_up": (i, g["w_up"]), "ffn_w_down": (i, g["w_down"])}
        if i % 2 == 0:
            dh, g = _ssd_bwd(h_in, dh, mix_saved, mp)
            big.update(ssd_w_in=(j, g["w_in"]), ssd_w_out=(j, g["w_out"]))
            grads["ssd_conv_w"][j] = g["conv_w"]
            for key, name in (("conv_b", "ssd_conv_b"), ("dt_bias", "ssd_dt_bias"), ("a_log", "ssd_a_log"),
                              ("d", "ssd_d"), ("gnorm_g", "ssd_norm_g")):
                grads[name][j] = g[key][0]
        else:
            dh, g = _fox_bwd(h_in, dh, mix_saved, mp)
            big.update(fox_w_in=(j, g["w_in"]), fox_w_out=(j, g["w_out"]))
            for key, name in (("b_f", "fox_b_f"), ("q_norm_g", "fox_q_norm_g"), ("k_norm_g", "fox_k_norm_g")):
                grads[name][j] = g[key][0]
        grads["mix_norm_g"][i] = g["norm_g"][0]
        send_partials(i, big)
    grads = {n: jnp.stack(v) for n, v in grads.items()}
    grads["final_norm_g"] = dfinal_g[0]

    small_names = REPLICATED + CONV
    summed = _unpack(_allreduce_small(_pack([grads[n] for n in small_names] + [loss_part]), "allreduce_small"),
                     [grads[n].shape for n in small_names] + [(1, 1)])
    loss = summed[-1][0, 0]
    g_small = dict(zip(small_names, summed[:-1]))
    for n in CONV:
        width = w[n].shape[-1]
        g_small[n] = lax.dynamic_slice_in_dim(g_small[n], me * width, width, axis=2)
    pk = lambda d: _pack([d[n] for n in small_names])
    d_small, m_small, v_small = _adamw_small(pk(w), pk(g_small), pk(mom), pk(var), "adamw_small")
    shapes = [w[n].shape for n in small_names]
    out_g, out_d, out_m, out_v = dict(g_small), {}, {}, {}
    for dst, buf in ((out_d, d_small), (out_m, m_small), (out_v, v_small)):
        dst.update(zip(small_names, _unpack(buf, shapes)))

    for i, sems, kept, group in in_flight:
        partials = _copies_wait(f"scatter_wait_{i}", kept, partials, sems, group, d_small)
    for n, part in zip(BIG, partials):
        shp = w[n].shape
        flat = lambda a: a.reshape(shp[0] * shp[1], shp[2])
        res = _adamw_sharded(flat(w[n]), flat(mom[n]), flat(var[n]), part.reshape(N_DEV, shp[0] * shp[1], shp[2]),
                             "adamw_" + n)
        out_g[n], out_d[n], out_m[n], out_v[n] = [r.reshape(shp) for r in res]

    return (loss, dh[0][None], *[out_g[n] for n in WEIGHTS], *[out_d[n] for n in WEIGHTS],
            *[out_m[n] for n in WEIGHTS], *[out_v[n] for n in WEIGHTS])
```

```python
import functools
import math

import jax
import jax.numpy as jnp
from jax import lax
from jax.experimental import pallas as pl
from jax.experimental.pallas import tpu as pltpu

F32 = jnp.float32
MXU_DT = jnp.bfloat16
VMEM_LIMIT_BYTES = 56 * 1024 * 1024
LANES = 128
N_DEV = 8
MESH_IDS = pl.DeviceIdType.MESH

EPS = 1e-6
D_MODEL = 1024
SSD_D_INNER = 2048
SSD_HEAD_DIM = 64
SSD_HEADS = 32
SSD_GROUPS = 4
SSD_HPG = 8
SSD_STATE = 128
SSD_CONV = 4
SSD_CHUNK = 128
SSD_CONV_DIM = 3072
SSD_ZX = SSD_D_INNER + SSD_CONV_DIM
FOX_HEAD_DIM = 64
FOX_HEADS = 16
FOX_D = 1024
D_FF = 2816
FFN_CONV = 3
ADAM_LR, ADAM_B1, ADAM_B2, ADAM_EPS, ADAM_WD, ADAM_STEP = 0.001, 0.9, 0.999, 1e-08, 0.01, 10


def _params(*sem):
    return pltpu.CompilerParams(dimension_semantics=sem or None, vmem_limit_bytes=VMEM_LIMIT_BYTES)


def _sds(shape, dtype=F32):
    return jax.ShapeDtypeStruct(tuple(shape), dtype)


def _col_tile(n, cap=1536):
    best = None
    for t in range(LANES, min(n, cap) + 1, LANES):
        if n % t == 0:
            best = t
    assert best is not None, n
    return best


def _sigmoid(x):
    return 1.0 / (1.0 + jnp.exp(-x))


def _matmul(a, b, *, ta=False, tb=False, res=None, out_dtype=F32, tm=512, tn=None, name):
    (kdim, m) = a.shape if ta else a.shape[::-1]
    (n, k2) = b.shape if tb else b.shape[::-1]
    assert kdim == k2, (a.shape, b.shape, ta, tb)
    tm = min(tm, m)
    if m % tm:
        tm = _col_tile(m, tm)
    tn = tn or _col_tile(n)
    assert m % tm == 0 and n % tn == 0, (m, tm, n, tn)
    dims = (((0 if ta else 1,), (1 if tb else 0,)), ((), ()))

    def body(*refs):
        a_ref, b_ref = refs[0], refs[1]
        o_ref = refs[-1]
        acc = lax.dot_general(a_ref[...].astype(MXU_DT), b_ref[...].astype(MXU_DT), dims,
                              preferred_element_type=F32)
        if res is not None:
            acc = acc + refs[2][...].astype(F32)
        o_ref[...] = acc.astype(o_ref.dtype)

    a_spec = pl.BlockSpec((kdim, tm), lambda i, j: (0, i)) if ta else pl.BlockSpec((tm, kdim), lambda i, j: (i, 0))
    b_spec = pl.BlockSpec((tn, kdim), lambda i, j: (j, 0)) if tb else pl.BlockSpec((kdim, tn), lambda i, j: (0, j))
    o_spec = pl.BlockSpec((tm, tn), lambda i, j: (i, j))
    ins, specs = [a, b], [a_spec, b_spec]
    if res is not None:
        ins.append(res)
        specs.append(o_spec)
    return pl.pallas_call(
        body, name=name, grid=(m // tm, n // tn), in_specs=specs, out_specs=o_spec,
        out_shape=_sds((m, n), out_dtype), compiler_params=_params("parallel", "parallel"),
    )(*ins)


def _rowwise(fn, rows, consts, out_rows, out_sums, *, tr=256, name):
    rows = [r if isinstance(r, tuple) else (r, 0, r.shape[1]) for r in rows]
    s = rows[0][0].shape[0]
    tr = min(tr, s)
    assert s % tr == 0
    n_in, n_c, n_or = len(rows), len(consts), len(out_rows)

    def body(*refs):
        ins = [r[...] for r in refs[:n_in + n_c]]
        outs = fn(*ins)
        o_refs = refs[n_in + n_c:]
        for o_ref, val in zip(o_refs[:n_or], outs[:n_or]):
            o_ref[...] = val.astype(o_ref.dtype)
        if out_sums:
            first = pl.program_id(0) == 0

            @pl.when(first)
            def _():
                for o_ref, val in zip(o_refs[n_or:], outs[n_or:]):
                    o_ref[...] = val.astype(o_ref.dtype)

            @pl.when(jnp.logical_not(first))
            def _():
                for o_ref, val in zip(o_refs[n_or:], outs[n_or:]):
                    o_ref[...] += val.astype(o_ref.dtype)

    in_specs = [pl.BlockSpec((tr, width), functools.partial(lambda i, cb: (i, cb), cb=cb)) for _, cb, width in rows]
    in_specs += [pl.BlockSpec(c.shape, lambda i: (0, 0)) for c in consts]
    out_specs = [pl.BlockSpec((tr, o.shape[1]), lambda i: (i, 0)) for o in out_rows]
    out_specs += [pl.BlockSpec(o.shape, lambda i: (0, 0)) for o in out_sums]
    return pl.pallas_call(
        body, name=name, grid=(s // tr,), in_specs=in_specs, out_specs=out_specs,
        out_shape=list(out_rows) + list(out_sums),
        compiler_params=_params("arbitrary" if out_sums else "parallel"),
    )(*[r[0] for r in rows], *consts)


def _rms_fwd(h, g, name):
    def fn(x, gv):
        r = lax.rsqrt(jnp.mean(x * x, axis=-1, keepdims=True) + EPS)
        return (x * r * gv,)
    return _rowwise(fn, [h], [g.reshape(1, -1)], [_sds(h.shape, MXU_DT)], [], name=name)[0]


def _rms_bwd(h, g, dy, dres, name):
    def fn(x, dyv, dr, gv):
        r = lax.rsqrt(jnp.mean(x * x, axis=-1, keepdims=True) + EPS)
        dyg = dyv * gv
        dx = dr + r * dyg - x * (r * r * r) * jnp.mean(x * dyg, axis=-1, keepdims=True)
        return dx, dx, jnp.sum(dyv * x * r, axis=0, keepdims=True)
    return _rowwise(fn, [h, dy, dres], [g.reshape(1, -1)], [_sds(h.shape), _sds(h.shape, MXU_DT)],
                    [_sds((1, h.shape[1]))], name=name)


def _loss_head(h, g, target, name):
    c = h.shape[1]

    def fn(x, t, gv):
        r = lax.rsqrt(jnp.mean(x * x, axis=-1, keepdims=True) + EPS)
        y = x * r * gv
        err = y - t
        dyv = err * (1.0 / c)
        dyg = dyv * gv
        dx = r * dyg - x * (r * r * r) * jnp.mean(x * dyg, axis=-1, keepdims=True)
        loss = 0.5 * jnp.sum(jnp.mean(err * err, axis=-1, keepdims=True), axis=0, keepdims=True)
        return dx, dx, jnp.sum(dyv * x * r, axis=0, keepdims=True), loss
    return _rowwise(fn, [h, target], [g.reshape(1, -1)], [_sds(h.shape), _sds(h.shape, MXU_DT)],
                    [_sds((1, c)), _sds((1, 1))], name=name)


PAD_ROWS = 8
ROW_TILE = 512


def _shifted_conv(xp_ref, w, r0, tr, kw):
    acc = None
    for k in range(kw):
        xk = xp_ref[pl.ds(PAD_ROWS + r0 - (kw - 1) + k, tr), :]
        term = xk * w[k:k + 1, :]
        acc = term if acc is None else acc + term
    return acc


def _convglu_fwd(u, conv_w, conv_b, name):
    s = u.shape[0]
    nt = D_FF // LANES
    tr = min(ROW_TILE, s)

    def body(ug_ref, uv_ref, w_ref, b_ref, act_ref, xp_ref):
        xp_ref[pl.ds(0, PAD_ROWS), :] = jnp.zeros((PAD_ROWS, LANES), F32)
        xp_ref[pl.ds(PAD_ROWS, s), :] = ug_ref[...]
        w = w_ref[...]
        b = b_ref[...]
        for r0 in range(0, s, tr):
            gate = _shifted_conv(xp_ref, w, r0, tr, FFN_CONV) + b
            act = gate * _sigmoid(gate) * uv_ref[pl.ds(r0, tr), :]
            act_ref[pl.ds(r0, tr), :] = act.astype(act_ref.dtype)

    return pl.pallas_call(
        body, name=name, grid=(nt,),
        in_specs=[pl.BlockSpec((s, LANES), lambda j: (0, j)), pl.BlockSpec((s, LANES), lambda j: (0, nt + j)),
                  pl.BlockSpec((FFN_CONV, LANES), lambda j: (0, j)), pl.BlockSpec((1, LANES), lambda j: (0, j))],
        out_specs=pl.BlockSpec((s, LANES), lambda j: (0, j)),
        out_shape=_sds((s, D_FF), MXU_DT),
        scratch_shapes=[pltpu.VMEM((s + PAD_ROWS, LANES), F32)],
        compiler_params=_params("parallel"),
    )(u, u, conv_w, conv_b.reshape(1, -1))


def _convglu_bwd(u, dact, conv_w, conv_b, name):
    s = u.shape[0]
    nt = D_FF // LANES
    tr = min(ROW_TILE, s)
    kw = FFN_CONV

    def body(ug_ref, uv_ref, da_ref, w_ref, b_ref, dug_ref, duv_ref, dw_ref, db_ref, xp_ref, dgp_ref):
        xp_ref[pl.ds(0, PAD_ROWS), :] = jnp.zeros((PAD_ROWS, LANES), F32)
        xp_ref[pl.ds(PAD_ROWS, s), :] = ug_ref[...]
        dgp_ref[pl.ds(s, PAD_ROWS), :] = jnp.zeros((PAD_ROWS, LANES), F32)
        w = w_ref[...]
        b = b_ref[...]
        dw = [jnp.zeros((1, LANES), F32) for _ in range(kw)]
        db = jnp.zeros((1, LANES), F32)
        for r0 in range(0, s, tr):
            gate = _shifted_conv(xp_ref, w, r0, tr, kw) + b
            sg = _sigmoid(gate)
            da = da_ref[pl.ds(r0, tr), :].astype(F32)
            duv_ref[pl.ds(r0, tr), :] = (da * gate * sg).astype(duv_ref.dtype)
            dgate = da * uv_ref[pl.ds(r0, tr), :] * (sg * (1.0 + gate * (1.0 - sg)))
            dgp_ref[pl.ds(r0, tr), :] = dgate
            db = db + jnp.sum(dgate, axis=0, keepdims=True)
            for k in range(kw):
                xk = xp_ref[pl.ds(PAD_ROWS + r0 - (kw - 1) + k, tr), :]
                dw[k] = dw[k] + jnp.sum(dgate * xk, axis=0, keepdims=True)
        for r0 in range(0, s, tr):
            acc = None
            for k in range(kw):
                term = dgp_ref[pl.ds(r0 + (kw - 1) - k, tr), :] * w[k:k + 1, :]
                acc = term if acc is None else acc + term
            dug_ref[pl.ds(r0, tr), :] = acc.astype(dug_ref.dtype)
        for k in range(kw):
            dw_ref[pl.ds(k, 1), :] = dw[k]
        db_ref[...] = db

    col = lambda j: (0, j)
    return pl.pallas_call(
        body, name=name, grid=(nt,),
        in_specs=[pl.BlockSpec((s, LANES), col), pl.BlockSpec((s, LANES), lambda j: (0, nt + j)),
                  pl.BlockSpec((s, LANES), col), pl.BlockSpec((kw, LANES), col), pl.BlockSpec((1, LANES), col)],
        out_specs=[pl.BlockSpec((s, LANES), col), pl.BlockSpec((s, LANES), col),
                   pl.BlockSpec((kw, LANES), col), pl.BlockSpec((1, LANES), col)],
        out_shape=[_sds((s, D_FF), MXU_DT), _sds((s, D_FF), MXU_DT), _sds((kw, D_FF)), _sds((1, D_FF))],
        scratch_shapes=[pltpu.VMEM((s + PAD_ROWS, LANES), F32), pltpu.VMEM((s + PAD_ROWS, LANES), F32)],
        compiler_params=_params("parallel"),
    )(u, u, dact, conv_w, conv_b.reshape(1, -1))


class _Weights:
    def __init__(self, fetch):
        self._fetch, self._got = fetch, {}

    def whole(self, name, after):
        if name not in self._got:
            self._got[name] = self._fetch(name, after)
        return self._got[name]


class _Gradients:
    def __init__(self, start):
        self._start, self._tokens = start, []

    def send(self, name, grad):
        token = self._start(name, grad)
        if token is not None:
            self._tokens.append(token)

    def tie(self, x):
        for token in self._tokens:
            x = x + token[0, 0]
        self._tokens = []
        return x


def _ffn_fwd(h, p, big):
    hf = _rms_fwd(h, p["norm_g"], "ffn_norm_fwd")
    w_up = big.whole("ffn_w_up", hf)
    u = _matmul(hf, w_up, name="ffn_up")
    conv_w = big.whole("ffn_conv_w", u)
    act = _convglu_fwd(u, conv_w, p["conv_b"], "convglu_fwd")
    w_down = big.whole("ffn_w_down", act)
    h_out = _matmul(act, w_down, res=h, name="ffn_down")
    return h_out, (hf, u, act, w_up, conv_w, w_down)


def _ffn_bwd(h, dh, saved, p, out):
    hf, u, act, w_up, conv_w, w_down = saved
    dh, dh_b = dh
    dact = _matmul(dh_b, w_down, tb=True, name="ffn_down_dx")
    out.send("ffn_w_down", _matmul(act, dh_b, ta=True, out_dtype=MXU_DT, name="ffn_down_dw"))
    dug, duv, dconv_w, dconv_b = _convglu_bwd(u, dact, conv_w, out.tie(p["conv_b"]), "convglu_bwd")
    dhf = _matmul(dug, w_up[:, :D_FF], tb=True, name="ffn_up_dx_gate")
    dhf = _matmul(duv, w_up[:, D_FF:], tb=True, res=dhf, name="ffn_up_dx_val")
    out.send("ffn_w_up", jnp.concatenate([_matmul(hf, dug, ta=True, out_dtype=MXU_DT, name="ffn_up_dw_gate"),
                                          _matmul(hf, duv, ta=True, out_dtype=MXU_DT, name="ffn_up_dw_val")], axis=1))
    *dh_in, dnorm_g = _rms_bwd(h, out.tie(p["norm_g"]), dhf, dh, "ffn_norm_bwd")
    return dh_in, dict(norm_g=dnorm_g, conv_w=dconv_w, conv_b=dconv_b)


def _dwconv_silu_fwd(proj, col0, n_ch, conv_w, conv_b, name):
    s = proj.shape[0]
    nt, t0, kw = n_ch // LANES, col0 // LANES, conv_w.shape[0]
    tr = min(ROW_TILE, s)

    def body(x_ref, w_ref, b_ref, o_ref, xp_ref):
        xp_ref[pl.ds(0, PAD_ROWS), :] = jnp.zeros((PAD_ROWS, LANES), F32)
        xp_ref[pl.ds(PAD_ROWS, s), :] = x_ref[...]
        w = w_ref[...]
        b = b_ref[...]
        for r0 in range(0, s, tr):
            pre = _shifted_conv(xp_ref, w, r0, tr, kw) + b
            o_ref[pl.ds(r0, tr), :] = pre * _sigmoid(pre)

    col = lambda j: (0, j)
    return pl.pallas_call(
        body, name=name, grid=(nt,),
        in_specs=[pl.BlockSpec((s, LANES), lambda j: (0, t0 + j)), pl.BlockSpec((kw, LANES), col),
                  pl.BlockSpec((1, LANES), col)],
        out_specs=pl.BlockSpec((s, LANES), col), out_shape=_sds((s, n_ch)),
        scratch_shapes=[pltpu.VMEM((s + PAD_ROWS, LANES), F32)],
        compiler_params=_params("parallel"),
    )(proj, conv_w, conv_b.reshape(1, -1))


def _dwconv_silu_bwd(proj, col0, n_ch, dout, conv_w, conv_b, name):
    s = proj.shape[0]
    nt, t0, kw = n_ch // LANES, col0 // LANES, conv_w.shape[0]
    tr = min(ROW_TILE, s)

    def body(x_ref, do_ref, w_ref, b_ref, dx_ref, dw_ref, db_ref, xp_ref, dgp_ref):
        xp_ref[pl.ds(0, PAD_ROWS), :] = jnp.zeros((PAD_ROWS, LANES), F32)
        xp_ref[pl.ds(PAD_ROWS, s), :] = x_ref[...]
        dgp_ref[pl.ds(s, PAD_ROWS), :] = jnp.zeros((PAD_ROWS, LANES), F32)
        w = w_ref[...]
        b = b_ref[...]
        dw = [jnp.zeros((1, LANES), F32) for _ in range(kw)]
        db = jnp.zeros((1, LANES), F32)
        for r0 in range(0, s, tr):
            pre = _shifted_conv(xp_ref, w, r0, tr, kw) + b
            sg = _sigmoid(pre)
            dpre = do_ref[pl.ds(r0, tr), :] * (sg * (1.0 + pre * (1.0 - sg)))
            dgp_ref[pl.ds(r0, tr), :] = dpre
            db = db + jnp.sum(dpre, axis=0, keepdims=True)
            for k in range(kw):
                xk = xp_ref[pl.ds(PAD_ROWS + r0 - (kw - 1) + k, tr), :]
                dw[k] = dw[k] + jnp.sum(dpre * xk, axis=0, keepdims=True)
        for r0 in range(0, s, tr):
            acc = None
            for k in range(kw):
                term = dgp_ref[pl.ds(r0 + (kw - 1) - k, tr), :] * w[k:k + 1, :]
                acc = term if acc is None else acc + term
            dx_ref[pl.ds(r0, tr), :] = acc.astype(dx_ref.dtype)
        for k in range(kw):
            dw_ref[pl.ds(k, 1), :] = dw[k]
        db_ref[...] = db

    col = lambda j: (0, j)
    return pl.pallas_call(
        body, name=name, grid=(nt,),
        in_specs=[pl.BlockSpec((s, LANES), lambda j: (0, t0 + j)), pl.BlockSpec((s, LANES), col),
                  pl.BlockSpec((kw, LANES), col), pl.BlockSpec((1, LANES), col)],
        out_specs=[pl.BlockSpec((s, LANES), col), pl.BlockSpec((kw, LANES), col), pl.BlockSpec((1, LANES), col)],
        out_shape=[_sds((s, n_ch), MXU_DT), _sds((kw, n_ch)), _sds((1, n_ch))],
        scratch_shapes=[pltpu.VMEM((s + PAD_ROWS, LANES), F32), pltpu.VMEM((s + PAD_ROWS, LANES), F32)],
        compiler_params=_params("parallel"),
    )(proj, dout, conv_w, conv_b.reshape(1, -1))


HIGHEST = lax.Precision.HIGHEST
PAIRS = SSD_HPG // 2
PAIR_W = 2 * SSD_HEAD_DIM
GROUP_W = SSD_HPG * SSD_HEAD_DIM


def _iota2(shape, axis):
    return lax.broadcasted_iota(jnp.int32, shape, axis)


def _lane_pad(v):
    return jnp.pad(v.reshape(1, -1), ((0, 0), (0, LANES - v.shape[0])))


def _heads_to_groups(a):
    s = a.shape[0]
    return a[:, :SSD_HEADS].reshape(s, SSD_GROUPS, SSD_HPG).transpose(1, 0, 2)


def _groups_to_heads(a):
    s = a.shape[1]
    return jnp.pad(a.transpose(1, 0, 2).reshape(s, SSD_HEADS), ((0, 0), (0, LANES - SSD_HEADS)))


def _ssd_prep(dt_raw, dt_bias, a_log, name):
    s = dt_raw.shape[0]
    lc = SSD_CHUNK

    def body(x_ref, b_ref, al_ref, dt_ref, acs_ref, acst_ref):
        x = x_ref[...] + b_ref[...]
        dt = jnp.maximum(x, 0.0) + jnp.log1p(jnp.exp(-jnp.abs(x)))
        da = dt * (-jnp.exp(al_ref[...]))
        lower = (_iota2((lc, lc), 0) >= _iota2((lc, lc), 1)).astype(F32)
        upper = (_iota2((lc, lc), 0) <= _iota2((lc, lc), 1)).astype(F32)
        dt_ref[...] = dt
        acs_ref[...] = jnp.dot(lower, da, precision=HIGHEST, preferred_element_type=F32)
        acst_ref[...] = lax.dot_general(da, upper, (((0,), (0,)), ((), ())), precision=HIGHEST,
                                        preferred_element_type=F32)

    row = pl.BlockSpec((lc, LANES), lambda c: (c, 0))
    one = pl.BlockSpec((1, LANES), lambda c: (0, 0))
    return pl.pallas_call(
        body, name=name, grid=(s // lc,), in_specs=[row, one, one],
        out_specs=[row, row, pl.BlockSpec((LANES, lc), lambda c: (0, c))],
        out_shape=[_sds((s, LANES)), _sds((s, LANES)), _sds((LANES, s))],
        compiler_params=_params("parallel"),
    )(dt_raw, dt_bias, a_log)


def _pair_cols(v, p, lo):
    return jnp.where(lo, v[:, 2 * p:2 * p + 1], v[:, 2 * p + 1:2 * p + 2])


def _decay_matrix(acs, acst, h, tri):
    return jnp.exp(jnp.where(tri, acs[:, h:h + 1] - acst[h:h + 1, :], -jnp.inf))


def _dot(a, b, ca, cb):
    return lax.dot_general(a, b, (((ca,), (cb,)), ((), ())), preferred_element_type=F32)


def _ssd_scan_fwd(xbc, dtg, acsg, acstg, d_skip, name):
    s = xbc.shape[0]
    lc, nc = SSD_CHUNK, s // SSD_CHUNK
    xt, bt = GROUP_W // LANES, SSD_D_INNER // LANES

    def body(x_ref, b_ref, c_ref, dt_ref, acs_ref, acst_ref, dsk_ref, y_ref, hp_ref, st_ref):
        @pl.when(pl.program_id(1) == 0)
        def _():
            st_ref[...] = jnp.zeros(st_ref.shape, F32)

        bm, cmb = b_ref[...], c_ref[...].astype(MXU_DT)
        dt, acs, acst = dt_ref[...], acs_ref[...], acst_ref[...]
        cb = _dot(cmb, bm.astype(MXU_DT), 1, 1)
        tri = _iota2((lc, lc), 0) >= _iota2((lc, lc), 1)
        lo = _iota2((lc, PAIR_W), 1) < SSD_HEAD_DIM
        a_last = acs[lc - 1:lc, :]
        e_acs, e_ds, e_cd = jnp.exp(acs), jnp.exp(a_last - acs), jnp.exp(a_last)
        for p in range(PAIRS):
            sl = pl.ds(p * PAIR_W, PAIR_W)
            xp = x_ref[:, sl]
            ub = (xp * _pair_cols(dt, p, lo)).astype(MXU_DT)
            m0 = (cb * _decay_matrix(acs, acst, 2 * p, tri)).astype(MXU_DT)
            m1 = (cb * _decay_matrix(acs, acst, 2 * p + 1, tri)).astype(MXU_DT)
            ht = st_ref[p]
            hp_ref[p] = ht
            y = jnp.where(lo, _dot(m0, ub, 1, 0), _dot(m1, ub, 1, 0))
            y = y + _dot(cmb, ht.astype(MXU_DT), 1, 0) * _pair_cols(e_acs, p, lo)
            y_ref[:, sl] = y + xp * dsk_ref[:, sl]
            bd0 = (bm * e_ds[:, 2 * p:2 * p + 1]).astype(MXU_DT)
            bd1 = (bm * e_ds[:, 2 * p + 1:2 * p + 2]).astype(MXU_DT)
            st_ref[p] = ht * _pair_cols(e_cd, p, lo[:1]) + jnp.where(lo, _dot(bd0, ub, 0, 0), _dot(bd1, ub, 0, 0))

    small = pl.BlockSpec((None, lc, SSD_HPG), lambda g, c: (g, c, 0))
    return pl.pallas_call(
        body, name=name, grid=(SSD_GROUPS, nc),
        in_specs=[pl.BlockSpec((lc, GROUP_W), lambda g, c: (c, g)),
                  pl.BlockSpec((lc, LANES), lambda g, c: (c, bt + g)),
                  pl.BlockSpec((lc, LANES), lambda g, c: (c, bt + SSD_GROUPS + g)),
                  small, small, pl.BlockSpec((None, SSD_HPG, lc), lambda g, c: (g, 0, c)),
                  pl.BlockSpec((1, GROUP_W), lambda g, c: (0, g))],
        out_specs=[pl.BlockSpec((lc, GROUP_W), lambda g, c: (c, g)),
                   pl.BlockSpec((None, PAIRS, SSD_STATE, PAIR_W), lambda g, c: (c, g, 0, 0))],
        out_shape=[_sds((s, SSD_D_INNER)), _sds((nc, SSD_GROUPS * PAIRS, SSD_STATE, PAIR_W))],
        scratch_shapes=[pltpu.VMEM((PAIRS, SSD_STATE, PAIR_W), F32)],
        compiler_params=_params("parallel", "arbitrary"),
    )(xbc, xbc, xbc, dtg, acsg, acstg, d_skip)


def _ssd_scan_bwd(xbc, dtg, acsg, acstg, d_skip, dy, hprev, name):
    s = xbc.shape[0]
    lc, nc = SSD_CHUNK, s // SSD_CHUNK
    bt = SSD_D_INNER // LANES

    def body(x_ref, b_ref, c_ref, dt_ref, acs_ref, acst_ref, dsk_ref, dy_ref, hp_ref, hn_ref,
             dx_ref, db_ref, dc_ref, daq_ref, dar_ref, ddtx_ref, dd_ref, dst_ref, ta_ref, tx_ref):
        @pl.when(pl.program_id(1) == 0)
        def _():
            dst_ref[...] = jnp.zeros(dst_ref.shape, F32)
            dd_ref[...] = jnp.zeros(dd_ref.shape, F32)

        bm, cmb = b_ref[...], c_ref[...].astype(MXU_DT)
        bmb = bm.astype(MXU_DT)
        dt, acs, acst = dt_ref[...], acs_ref[...], acst_ref[...]
        cb = _dot(cmb, bmb, 1, 1)
        tri = _iota2((lc, lc), 0) >= _iota2((lc, lc), 1)
        lo = _iota2((lc, PAIR_W), 1) < SSD_HEAD_DIM
        a_last = acs[lc - 1:lc, :]
        e_acs, e_ds, e_cd = jnp.exp(acs), jnp.exp(a_last - acs), jnp.exp(a_last)
        dcb = jnp.zeros((lc, lc), F32)
        dc_x = jnp.zeros((lc, SSD_STATE), F32)
        db_x = jnp.zeros((lc, SSD_STATE), F32)
        da_in = jnp.zeros((lc, LANES), F32)
        da_out = jnp.zeros((SSD_HPG, lc), F32)
        head_col = _iota2((lc, LANES), 1)
        head_row = _iota2((SSD_HPG, lc), 0)
        last = _iota2((SSD_HPG, lc), 1) == lc - 1
        for p in range(PAIRS):
            sl = pl.ds(p * PAIR_W, PAIR_W)
            xp, dyp, dsk = x_ref[:, sl], dy_ref[:, sl], dsk_ref[:, sl]
            dtp = _pair_cols(dt, p, lo)
            u = xp * dtp
            ub, dyb = u.astype(MXU_DT), dyp.astype(MXU_DT)
            lmat = (_decay_matrix(acs, acst, 2 * p, tri), _decay_matrix(acs, acst, 2 * p + 1, tri))
            m0, m1 = (cb * lmat[0]).astype(MXU_DT), (cb * lmat[1]).astype(MXU_DT)
            ea, dsl = _pair_cols(e_acs, p, lo), _pair_cols(e_ds, p, lo)
            dht, ht = dst_ref[p], hp_ref[p]
            dhtb, htb = dht.astype(MXU_DT), ht.astype(MXU_DT)
            bd0 = (bm * e_ds[:, 2 * p:2 * p + 1]).astype(MXU_DT)
            bd1 = (bm * e_ds[:, 2 * p + 1:2 * p + 2]).astype(MXU_DT)
            du_state = jnp.where(lo, _dot(bd0, dhtb, 1, 0), _dot(bd1, dhtb, 1, 0))
            du = jnp.where(lo, _dot(m0, dyb, 0, 0), _dot(m1, dyb, 0, 0)) + du_state
            y_off = _dot(cmb, htb, 1, 0) * ea
            ta_ref[:, sl] = dyp * y_off - u * du_state
            tx_ref[:, sl] = du * xp
            dx_ref[:, sl] = dtp * du + dsk * dyp
            dd_ref[:, sl] += jnp.sum(dyp * xp, axis=0, keepdims=True)
            dy_h = (jnp.where(lo, dyp, 0.0).astype(MXU_DT), jnp.where(lo, 0.0, dyp).astype(MXU_DT))
            carry = jnp.sum(dht * hn_ref[p], axis=0, keepdims=True)
            for hh in range(2):
                h = 2 * p + hh
                dml = _dot(dy_h[hh], ub, 1, 1) * lmat[hh]
                dcb = dcb + dml
                flow = cb * dml
                da_in = da_in + jnp.where(head_col == h, jnp.sum(flow, axis=1, keepdims=True), 0.0)
                through = jnp.sum(jnp.where(lo[:1] == (hh == 0), carry, 0.0), axis=1, keepdims=True)
                da_out = da_out + jnp.where(head_row == h, jnp.sum(flow, axis=0, keepdims=True)
                                            - jnp.where(last, through, 0.0), 0.0)
            dye = (dyp * ea).astype(MXU_DT)
            dc_x = dc_x + _dot(dye, htb, 1, 1)
            db_x = db_x + _dot((u * dsl).astype(MXU_DT), dhtb, 1, 1)
            dst_ref[p] = dht * _pair_cols(e_cd, p, lo[:1]) + _dot(cmb, dye, 0, 0)
        dcbb = dcb.astype(MXU_DT)
        dc_ref[...] = _dot(dcbb, bmb, 1, 0) + dc_x
        db_ref[...] = _dot(dcbb, cmb, 0, 0) + db_x
        seg_lo = _iota2((GROUP_W, LANES), 1) * SSD_HEAD_DIM
        chan = _iota2((GROUP_W, LANES), 0)
        seg = jnp.logical_and(chan >= seg_lo, chan < seg_lo + SSD_HEAD_DIM).astype(F32)
        da_in = da_in + jnp.dot(ta_ref[...], seg, precision=HIGHEST, preferred_element_type=F32)
        daq_ref[...] = da_in[:, :SSD_HPG]
        dar_ref[...] = da_out
        ddtx_ref[...] = jnp.dot(tx_ref[...], seg, precision=HIGHEST, preferred_element_type=F32)[:, :SSD_HPG]

    rev = lambda c: nc - 1 - c
    small = pl.BlockSpec((None, lc, SSD_HPG), lambda g, c: (g, rev(c), 0))
    small_t = pl.BlockSpec((None, SSD_HPG, lc), lambda g, c: (g, 0, rev(c)))
    wide = pl.BlockSpec((lc, GROUP_W), lambda g, c: (rev(c), g))
    state = lambda at: pl.BlockSpec((None, PAIRS, SSD_STATE, PAIR_W), lambda g, c: (at(c), g, 0, 0))
    return pl.pallas_call(
        body, name=name, grid=(SSD_GROUPS, nc),
        in_specs=[pl.BlockSpec((lc, GROUP_W), lambda g, c: (rev(c), g)),
                  pl.BlockSpec((lc, LANES), lambda g, c: (rev(c), bt + g)),
                  pl.BlockSpec((lc, LANES), lambda g, c: (rev(c), bt + SSD_GROUPS + g)),
                  small, small, small_t, pl.BlockSpec((1, GROUP_W), lambda g, c: (0, g)), wide,
                  state(rev), state(lambda c: jnp.minimum(rev(c) + 1, nc - 1))],
        out_specs=[wide, pl.BlockSpec((lc, LANES), lambda g, c: (rev(c), g)),
                   pl.BlockSpec((lc, LANES), lambda g, c: (rev(c), g)), small, small_t, small,
                   pl.BlockSpec((1, GROUP_W), lambda g, c: (0, g))],
        out_shape=[_sds((s, SSD_D_INNER)), _sds((s, SSD_GROUPS * SSD_STATE)), _sds((s, SSD_GROUPS * SSD_STATE)),
                   _sds((SSD_GROUPS, s, SSD_HPG)), _sds((SSD_GROUPS, SSD_HPG, s)), _sds((SSD_GROUPS, s, SSD_HPG)),
                   _sds((1, SSD_D_INNER))],
        scratch_shapes=[pltpu.VMEM((PAIRS, SSD_STATE, PAIR_W), F32), pltpu.VMEM((lc, GROUP_W), F32),
                        pltpu.VMEM((lc, GROUP_W), F32)],
        compiler_params=_params("parallel", "arbitrary"),
    )(xbc, xbc, xbc, dtg, acsg, acstg, d_skip, dy, hprev, hprev)


def _ssd_post(da_in, da_out, ddtx, dt, dt_raw, dt_bias, a_log, name):
    s = da_in.shape[0]
    lc = SSD_CHUNK

    def body(dain_ref, daout_ref, ddtx_ref, dt_ref, x_ref, b_ref, al_ref, ddr_ref, dal_ref, dbias_ref):
        @pl.when(pl.program_id(0) == 0)
        def _():
            dal_ref[...] = jnp.zeros(dal_ref.shape, F32)
            dbias_ref[...] = jnp.zeros(dbias_ref.shape, F32)

        upper = (_iota2((lc, lc), 0) <= _iota2((lc, lc), 1)).astype(F32)
        dda = jnp.dot(upper, dain_ref[...] - daout_ref[...], precision=HIGHEST, preferred_element_type=F32)
        a = -jnp.exp(al_ref[...])
        ddt = dda * a + ddtx_ref[...]
        dal_ref[...] += jnp.sum(dda * dt_ref[...], axis=0, keepdims=True) * a
        ddr = ddt * _sigmoid(x_ref[...] + b_ref[...])
        ddr_ref[...] = ddr.astype(ddr_ref.dtype)
        dbias_ref[...] += jnp.sum(ddr, axis=0, keepdims=True)

    row = pl.BlockSpec((lc, LANES), lambda i: (i, 0))
    one = pl.BlockSpec((1, LANES), lambda i: (0, 0))
    return pl.pallas_call(
        body, name=name, grid=(s // lc,), in_specs=[row, row, row, row, row, one, one], out_specs=[row, one, one],
        out_shape=[_sds((s, LANES), MXU_DT), _sds((1, LANES)), _sds((1, LANES))],
        compiler_params=_params("arbitrary"),
    )(da_in, da_out, ddtx, dt, dt_raw, dt_bias, a_log)


NORM_GROUP_W = SSD_D_INNER // SSD_GROUPS


def _group_rstd(yz):
    return [lax.rsqrt(jnp.mean(jnp.square(yz[:, g * NORM_GROUP_W:(g + 1) * NORM_GROUP_W]), axis=-1, keepdims=True) + EPS)
            for g in range(SSD_GROUPS)]


def _gated_norm_fwd(y, proj, norm_g, name):
    def fn(yv, z, gv):
        yz = yv * (z * _sigmoid(z))
        parts = [yz[:, g * NORM_GROUP_W:(g + 1) * NORM_GROUP_W] * r for g, r in enumerate(_group_rstd(yz))]
        return (jnp.concatenate(parts, axis=1) * gv,)
    return _rowwise(fn, [y, (proj, 0, SSD_D_INNER)], [norm_g.reshape(1, -1)], [_sds(y.shape, MXU_DT)], [],
                    name=name)[0]


def _gated_norm_bwd(y, proj, norm_g, dout, name):
    def fn(yv, z, do, gv):
        sg = _sigmoid(z)
        sz = z * sg
        yz = yv * sz
        dog = do * gv
        dyz, dg = [], []
        for g, r in enumerate(_group_rstd(yz)):
            cols = slice(g * NORM_GROUP_W, (g + 1) * NORM_GROUP_W)
            yzg, dogg = yz[:, cols], dog[:, cols]
            dyz.append(r * dogg - yzg * (r * r * r) * jnp.mean(yzg * dogg, axis=-1, keepdims=True))
            dg.append(jnp.sum(do[:, cols] * yzg * r, axis=0, keepdims=True))
        dyz = jnp.concatenate(dyz, axis=1)
        return dyz * sz, dyz * yv * (sg * (1.0 + z * (1.0 - sg))), jnp.concatenate(dg, axis=1)
    return _rowwise(fn, [y, (proj, 0, SSD_D_INNER), dout], [norm_g.reshape(1, -1)],
                    [_sds(y.shape), _sds(y.shape, MXU_DT)], [_sds((1, y.shape[1]))], name=name)


def _ssd_fwd(h, p, big):
    hn = _rms_fwd(h, p["norm_g"], "mix_norm_fwd")
    w_in = big.whole("ssd_w_in", hn)
    w_zx, w_dt = w_in[:, :SSD_ZX], _pad_cols(w_in[:, SSD_ZX:])
    proj = _matmul(hn, w_zx, name="ssd_in_zx")
    dt_raw = _matmul(hn, w_dt, name="ssd_in_dt")
    conv_w = big.whole("ssd_conv_w", proj)
    xbc = _dwconv_silu_fwd(proj, SSD_D_INNER, SSD_CONV_DIM, conv_w, p["conv_b"], "ssd_conv_fwd")
    dt, acs, acst = _ssd_prep(dt_raw, _lane_pad(p["dt_bias"]), _lane_pad(p["a_log"]), "ssd_prep")
    dtg, acsg = _heads_to_groups(dt), _heads_to_groups(acs)
    acstg = acst[:SSD_HEADS].reshape(SSD_GROUPS, SSD_HPG, -1)
    d_skip = jnp.repeat(p["d"], SSD_HEAD_DIM).reshape(1, -1)
    y, hprev = _ssd_scan_fwd(xbc, dtg, acsg, acstg, d_skip, "ssd_scan_fwd")
    yn = _gated_norm_fwd(y, proj, p["gnorm_g"], "ssd_gnorm_fwd")
    w_out = big.whole("ssd_w_out", yn)
    h_out = _matmul(yn, w_out, res=h, name="ssd_out")
    return h_out, (hn, proj, dt_raw, xbc, dt, dtg, acsg, acstg, d_skip, y, hprev, yn, w_zx, w_dt, conv_w, w_out)


def _ssd_bwd(h, dh, saved, p, out):
    hn, proj, dt_raw, xbc, dt, dtg, acsg, acstg, d_skip, y, hprev, yn, w_zx, w_dt, conv_w, w_out = saved
    dh, dh_b = dh
    dyn = _matmul(dh_b, w_out, tb=True, name="ssd_out_dx")
    out.send("ssd_w_out", _matmul(yn, dh_b, ta=True, out_dtype=MXU_DT, name="ssd_out_dw"))
    dy, dz, dgnorm = _gated_norm_bwd(y, proj, out.tie(p["gnorm_g"]), dyn, "ssd_gnorm_bwd")
    dx, dbm, dcm, daq, dar, ddtx, dd = _ssd_scan_bwd(xbc, dtg, acsg, acstg, d_skip, dy, hprev, "ssd_scan_bwd")
    dxbc = jnp.concatenate([dx, dbm, dcm], axis=1)
    dpre, dconv_w, dconv_b = _dwconv_silu_bwd(proj, SSD_D_INNER, SSD_CONV_DIM, dxbc, conv_w, p["conv_b"],
                                              "ssd_conv_bwd")
    ddr, dalog, dbias = _ssd_post(_groups_to_heads(daq), _groups_to_heads(dar.transpose(0, 2, 1)),
                                  _groups_to_heads(ddtx), dt, dt_raw,
                                  _lane_pad(p["dt_bias"]), _lane_pad(p["a_log"]), "ssd_post")
    w_z, w_x = w_zx[:, :SSD_D_INNER], w_zx[:, SSD_D_INNER:]
    dhn = _matmul(dz, w_z, tb=True, name="ssd_in_dx_z")
    dhn = _matmul(dpre, w_x, tb=True, res=dhn, name="ssd_in_dx_x")
    dhn = _matmul(ddr, w_dt, tb=True, res=dhn, name="ssd_in_dx_dt")
    out.send("ssd_w_in", jnp.concatenate(
        [_matmul(hn, dz, ta=True, out_dtype=MXU_DT, name="ssd_in_dw_z"),
         _matmul(hn, dpre, ta=True, out_dtype=MXU_DT, name="ssd_in_dw_x"),
         _matmul(hn, ddr, ta=True, out_dtype=MXU_DT, name="ssd_in_dw_dt")[:, :SSD_HEADS]], axis=1))
    *dh_in, dnorm_g = _rms_bwd(h, out.tie(p["norm_g"]), dhn, dh, "mix_norm_bwd")
    grads = dict(norm_g=dnorm_g, conv_w=dconv_w, conv_b=dconv_b, dt_bias=dbias[:, :SSD_HEADS],
                 a_log=dalog[:, :SSD_HEADS], d=dd.reshape(SSD_HEADS, SSD_HEAD_DIM).sum(axis=1).reshape(1, -1),
                 gnorm_g=dgnorm)
    return dh_in, grads


FOX_PAIRS = FOX_HEADS // 2
ATT_TQ = 256
ATT_TK = 512
NEG_BIG = -1e30
FOX_SCALE = FOX_HEAD_DIM ** -0.5
FOX_AUG_D = FOX_HEADS * LANES
QSIDE = FOX_HEAD_DIM
KSIDE = FOX_HEAD_DIM + 3


def _split3(x):
    a = x.astype(MXU_DT).astype(F32)
    b = (x - a).astype(MXU_DT).astype(F32)
    return a, b, (x - a - b).astype(MXU_DT).astype(F32)


def _head_tiles(pair_tile):
    return pair_tile, pltpu.roll(pair_tile, FOX_HEAD_DIM, 1)


def _fill_lanes(base, lane, first, values):
    for i, v in enumerate(values):
        base = jnp.where(lane == first + i, v, base)
    return base


def _pair_tile(lane, tile0, tile1):
    return jnp.where(lane < FOX_HEAD_DIM, tile0, pltpu.roll(tile1, FOX_HEAD_DIM, 1))


def _compact_heads(a, lane):
    return jnp.concatenate([_pair_tile(lane, a[:, 2 * j * LANES:(2 * j + 1) * LANES],
                                       a[:, (2 * j + 1) * LANES:(2 * j + 2) * LANES]) for j in range(FOX_PAIRS)], axis=1)


def _head_sum_matrix():
    return (_iota2((LANES, LANES), 0) < FOX_HEAD_DIM) == (_iota2((LANES, LANES), 1) < FOX_HEAD_DIM)


def _head_sums(x):
    bd = _head_sum_matrix().astype(F32)
    parts = [jnp.dot(x[:, j * LANES:(j + 1) * LANES], bd, precision=HIGHEST, preferred_element_type=F32)
             for j in range(x.shape[1] // LANES)]
    return parts[0] if len(parts) == 1 else jnp.concatenate(parts, axis=1)


def _fox_prep_fwd(proj, f_raw, qg, kg, b_f, name):
    s = proj.shape[0]
    tr = min(ATT_TQ, s)

    def body(q_ref, k_ref, v_ref, f_ref, qg_ref, kg_ref, b_ref, qa_ref, ka_ref, va_ref, carry_ref):
        @pl.when(pl.program_id(0) == 0)
        def _():
            carry_ref[...] = jnp.zeros(carry_ref.shape, F32)

        normed = []
        for x_ref, g_ref in ((q_ref, qg_ref), (k_ref, kg_ref)):
            x = x_ref[...]
            r = lax.rsqrt(_head_sums(x * x) * (1.0 / FOX_HEAD_DIM) + EPS)
            normed.append(x * r * g_ref[...])
        qn, kn, v = normed[0] * FOX_SCALE, normed[1], v_ref[...]
        x = f_ref[...] + b_ref[...]
        lf = jnp.minimum(x, 0.0) - jnp.log1p(jnp.exp(-jnp.abs(x)))
        lower = (_iota2((tr, tr), 0) >= _iota2((tr, tr), 1)).astype(F32)
        cum = jnp.dot(lower, lf, precision=HIGHEST, preferred_element_type=F32) + carry_ref[...]
        carry_ref[...] += jnp.sum(lf, axis=0, keepdims=True)
        first = _iota2((LANES, FOX_D), 0) * FOX_HEAD_DIM
        chan = _iota2((LANES, FOX_D), 1)
        spread = jnp.logical_and(chan >= first, chan < first + FOX_HEAD_DIM).astype(F32)
        cum = jnp.dot(cum, spread, precision=HIGHEST, preferred_element_type=F32)
        lane = _iota2((tr, LANES), 1)
        ones = jnp.where(jnp.logical_and(lane >= QSIDE, lane < KSIDE + 3), 1.0, 0.0)
        for j in range(FOX_PAIRS):
            cols = slice(j * LANES, (j + 1) * LANES)
            tiles = zip(_head_tiles(qn[:, cols]), _head_tiles(kn[:, cols]), _head_tiles(v[:, cols]),
                        reversed(_head_tiles(cum[:, cols])))
            for hh, (qt, kt, vt, ct) in enumerate(tiles):
                out = slice((2 * j + hh) * LANES, (2 * j + hh + 1) * LANES)
                c3 = _split3(ct)
                head = lane < FOX_HEAD_DIM
                qa_ref[:, out] = _fill_lanes(jnp.where(head, qt, ones), lane, QSIDE, c3).astype(qa_ref.dtype)
                ka_ref[:, out] = _fill_lanes(jnp.where(head, kt, ones), lane, KSIDE, [-c for c in c3]).astype(ka_ref.dtype)
                va_ref[:, out] = jnp.where(head, vt, jnp.where(lane < KSIDE, 1.0, 0.0)).astype(va_ref.dtype)

    wide = lambda cb: pl.BlockSpec((tr, FOX_D), lambda i: (i, cb))
    aug = pl.BlockSpec((tr, FOX_AUG_D), lambda i: (i, 0))
    one = lambda n: pl.BlockSpec((1, n), lambda i: (0, 0))
    return pl.pallas_call(
        body, name=name, grid=(s // tr,),
        in_specs=[wide(0), wide(1), wide(2), pl.BlockSpec((tr, LANES), lambda i: (i, 0)), one(FOX_D), one(FOX_D),
                  one(LANES)],
        out_specs=[aug, aug, aug], out_shape=[_sds((s, FOX_AUG_D), MXU_DT)] * 3,
        scratch_shapes=[pltpu.VMEM((1, LANES), F32)],
        compiler_params=_params("arbitrary"),
    )(proj, proj, proj, f_raw, qg, kg, b_f)


def _fox_prep_bwd(proj, f_raw, qg, kg, b_f, dqa, dka, row_sums, col_sums, name):
    s = proj.shape[0]
    tr = min(ATT_TQ, s)
    nb = s // tr

    def body(q_ref, k_ref, f_ref, qg_ref, kg_ref, b_ref, dqa_ref, dka_ref, dcq_ref, dck_ref,
             dq_ref, dk_ref, df_ref, dqg_ref, dkg_ref, db_ref, carry_ref):
        @pl.when(pl.program_id(0) == 0)
        def _():
            carry_ref[...] = jnp.zeros(carry_ref.shape, F32)
            dqg_ref[...] = jnp.zeros(dqg_ref.shape, F32)
            dkg_ref[...] = jnp.zeros(dkg_ref.shape, F32)
            db_ref[...] = jnp.zeros(db_ref.shape, F32)

        lane = _iota2((tr, LANES), 1)
        for x_ref, g_ref, dt_ref, scale, dx_ref, dg_ref in ((q_ref, qg_ref, dqa_ref, FOX_SCALE, dq_ref, dqg_ref),
                                                            (k_ref, kg_ref, dka_ref, 1.0, dk_ref, dkg_ref)):
            x, dy = x_ref[...], _compact_heads(dt_ref[...], lane) * scale
            r = lax.rsqrt(_head_sums(x * x) * (1.0 / FOX_HEAD_DIM) + EPS)
            dyg = dy * g_ref[...]
            dx = r * dyg - x * (r * r * r) * (_head_sums(x * dyg) * (1.0 / FOX_HEAD_DIM))
            dx_ref[...] = dx.astype(dx_ref.dtype)
            dg_ref[...] += jnp.sum(dy * x * r, axis=0, keepdims=True)
        dc = dcq_ref[...] - dck_ref[...]
        upper = (_iota2((tr, tr), 0) <= _iota2((tr, tr), 1)).astype(F32)
        dlf = jnp.dot(upper, dc, precision=HIGHEST, preferred_element_type=F32) + carry_ref[...]
        carry_ref[...] += jnp.sum(dc, axis=0, keepdims=True)
        df = dlf * _sigmoid(-(f_ref[...] + b_ref[...]))
        df_ref[...] = df.astype(df_ref.dtype)
        db_ref[...] += jnp.sum(df, axis=0, keepdims=True)

    wide = lambda cb: pl.BlockSpec((tr, FOX_D), lambda i: (nb - 1 - i, cb))
    aug = pl.BlockSpec((tr, FOX_AUG_D), lambda i: (nb - 1 - i, 0))
    row = pl.BlockSpec((tr, LANES), lambda i: (nb - 1 - i, 0))
    one = lambda n: pl.BlockSpec((1, n), lambda i: (0, 0))
    return pl.pallas_call(
        body, name=name, grid=(nb,),
        in_specs=[wide(0), wide(1), row, one(FOX_D), one(FOX_D), one(LANES), aug, aug, row, row],
        out_specs=[wide(0), wide(0), row, one(FOX_D), one(FOX_D), one(LANES)],
        out_shape=[_sds((s, FOX_D), MXU_DT), _sds((s, FOX_D), MXU_DT), _sds((s, LANES), MXU_DT),
                   _sds((1, FOX_D)), _sds((1, FOX_D)), _sds((1, LANES))],
        scratch_shapes=[pltpu.VMEM((1, LANES), F32)],
        compiler_params=_params("arbitrary"),
    )(proj, proj, f_raw, qg, kg, b_f, dqa, dka, row_sums, col_sums)


def _fox_attn_fwd(qa, ka, va, proj, name):
    s = qa.shape[0]
    tq, tk = min(ATT_TQ, s), min(ATT_TK, s)
    assert s % tq == 0 and s % tk == 0
    gt = 3 * FOX_D // LANES
    head_lanes = [slice(hh * LANES, (hh + 1) * LANES) for hh in range(2)]

    def body(qa_ref, ka_ref, va_ref, g_ref, o_ref, og_ref, qb_ref):
        qi = pl.program_id(1)
        lane = _iota2((tq, LANES), 1)
        ahead = _iota2((tq, tk), 0) - _iota2((tq, tk), 1)
        q = [qa_ref[:, hs] for hs in head_lanes]

        def kv_step(j, carry, masked):
            rows = pl.ds(pl.multiple_of(j * tk, tk), tk)
            out = []
            for hh, hs in enumerate(head_lanes):
                m, acc = carry[2 * hh:2 * hh + 2]
                sc = _dot(q[hh], ka_ref[rows, hs], 1, 1)
                if masked:
                    sc = jnp.where(ahead >= j * tk - qi * tq, sc, NEG_BIG)
                m_new = jnp.maximum(m, jnp.max(sc, axis=1, keepdims=True))
                pr = jnp.exp(sc - m_new).astype(MXU_DT)
                out += [m_new, jnp.exp(m - m_new) * acc + _dot(pr, va_ref[rows, hs], 1, 0)]
            return tuple(out)

        n_clear = lax.div(qi * tq, tk)
        n_all = lax.div((qi + 1) * tq + tk - 1, tk)
        init = (jnp.full((tq, 1), NEG_BIG, F32), jnp.zeros((tq, LANES), F32)) * 2
        carry = lax.fori_loop(0, n_clear, functools.partial(kv_step, masked=False), init)
        carry = lax.fori_loop(n_clear, n_all, functools.partial(kv_step, masked=True), carry)
        heads = []
        for hh, hs in enumerate(head_lanes):
            m, acc = carry[2 * hh:2 * hh + 2]
            l = acc[:, QSIDE:QSIDE + 1]
            heads.append(acc / l)
            qf = q[hh].astype(F32)
            bias = qf[:, QSIDE:QSIDE + 1] + qf[:, QSIDE + 1:QSIDE + 2] + qf[:, QSIDE + 2:QSIDE + 3]
            qb_ref[:, hs] = _fill_lanes(qf, lane, QSIDE, _split3(bias - (m + jnp.log(l)))).astype(qb_ref.dtype)
        o = _pair_tile(lane, heads[0], heads[1])
        o_ref[...] = o
        og_ref[...] = (o * _sigmoid(g_ref[...])).astype(og_ref.dtype)

    blk2 = pl.BlockSpec((tq, 2 * LANES), lambda p, i: (i, p))
    seq2 = pl.BlockSpec((s, 2 * LANES), lambda p, i: (0, p))
    blk = pl.BlockSpec((tq, LANES), lambda p, i: (i, p))
    return pl.pallas_call(
        body, name=name, grid=(FOX_PAIRS, s // tq),
        in_specs=[blk2, seq2, seq2, pl.BlockSpec((tq, LANES), lambda p, i: (i, gt + p))],
        out_specs=[blk, blk, blk2],
        out_shape=[_sds((s, FOX_D)), _sds((s, FOX_D), MXU_DT), _sds((s, FOX_AUG_D), MXU_DT)],
        compiler_params=_params("parallel", "parallel"),
    )(qa, ka, va, proj)


def _fox_gate_bwd(dog, o, proj, name):
    def fn(dogv, ov, gate):
        sg = _sigmoid(gate)
        do = dogv * sg
        delta = _head_sums(do * ov)
        lane = _iota2((do.shape[0], LANES), 1)
        tiles = []
        for j in range(FOX_PAIRS):
            cols = slice(j * LANES, (j + 1) * LANES)
            for dt, dl in zip(_head_tiles(do[:, cols]), reversed(_head_tiles(delta[:, cols]))):
                tiles.append(_fill_lanes(jnp.where(lane < FOX_HEAD_DIM, dt, 0.0), lane, QSIDE,
                                         [-d for d in _split3(dl)]))
        return dogv * ov * sg * (1.0 - sg), jnp.concatenate(tiles, axis=1)
    return _rowwise(fn, [dog, o, (proj, 3, FOX_D)], [], [_sds(o.shape, MXU_DT), _sds((o.shape[0], FOX_AUG_D), MXU_DT)],
                    [], name=name)


def _fox_attn_bwd(qb, ka, va, doa, name):
    s = qb.shape[0]
    tq, tk = min(ATT_TQ, s), min(ATT_TK, s)
    nq, nk = s // tq, s // tk
    head_lanes = [slice(hh * LANES, (hh + 1) * LANES) for hh in range(2)]

    def body(qb_ref, doa_ref, ka_ref, va_ref, dqa_ref, dka_ref, dv_ref):
        kj = pl.program_id(1)

        @pl.when(kj == 0)
        def _():
            dqa_ref[...] = jnp.zeros(dqa_ref.shape, F32)

        ahead = _iota2((tq, tk), 0) - _iota2((tq, tk), 1)
        kb = [ka_ref[:, hs] for hs in head_lanes]
        vb = [va_ref[:, hs] for hs in head_lanes]

        def q_step(i, carry, masked):
            rows = pl.ds(pl.multiple_of(i * tq, tq), tq)
            out = []
            for hh, hs in enumerate(head_lanes):
                dk, dv = carry[2 * hh:2 * hh + 2]
                q, do = qb_ref[rows, hs], doa_ref[rows, hs]
                pr = jnp.exp(_dot(q, kb[hh], 1, 1))
                if masked:
                    pr = jnp.where(ahead >= kj * tk - i * tq, pr, 0.0)
                dv = dv + _dot(pr.astype(MXU_DT), do, 0, 0)
                ds = (pr * _dot(do, vb[hh], 1, 1)).astype(MXU_DT)
                dqa_ref[rows, hs] += _dot(ds, kb[hh], 1, 0)
                out += [dk + _dot(ds, q, 0, 0), dv]
            return tuple(out)

        first = lax.div(kj * tk, tq)
        n_masked = lax.div((kj + 1) * tk + tq - 1, tq)
        carry = lax.fori_loop(first, n_masked, functools.partial(q_step, masked=True),
                              (jnp.zeros((tk, LANES), F32),) * 4)
        dk0, dv0, dk1, dv1 = lax.fori_loop(n_masked, nq, functools.partial(q_step, masked=False), carry)
        dka_ref[:, head_lanes[0]] = dk0
        dka_ref[:, head_lanes[1]] = dk1
        dv_ref[...] = _pair_tile(_iota2((tk, LANES), 1), dv0, dv1).astype(dv_ref.dtype)

    seq2 = pl.BlockSpec((s, 2 * LANES), lambda p, j: (0, p))
    blk2 = pl.BlockSpec((tk, 2 * LANES), lambda p, j: (j, p))
    return pl.pallas_call(
        body, name=name, grid=(FOX_PAIRS, nk), in_specs=[seq2, seq2, blk2, blk2],
        out_specs=[seq2, blk2, pl.BlockSpec((tk, LANES), lambda p, j: (j, p))],
        out_shape=[_sds((s, FOX_AUG_D)), _sds((s, FOX_AUG_D)), _sds((s, FOX_D), MXU_DT)],
        compiler_params=_params("parallel", "arbitrary"),
    )(qb, doa, ka, va)


def _fox_fwd(h, p, big):
    hn = _rms_fwd(h, p["norm_g"], "mix_norm_fwd")
    w_in = big.whole("fox_w_in", hn)
    w_qkvg, w_f = w_in[:, :4 * FOX_D], _pad_cols(w_in[:, 4 * FOX_D:])
    proj = _matmul(hn, w_qkvg, name="fox_in_qkvg")
    f_raw = _matmul(hn, w_f, name="fox_in_f")
    qg = jnp.tile(p["q_norm_g"], FOX_HEADS).reshape(1, -1)
    kg = jnp.tile(p["k_norm_g"], FOX_HEADS).reshape(1, -1)
    qa, ka, va = _fox_prep_fwd(proj, f_raw, qg, kg, _lane_pad(p["b_f"]), "fox_prep_fwd")
    o, og, qb = _fox_attn_fwd(qa, ka, va, proj, "fox_attn_fwd")
    w_out = big.whole("fox_w_out", og)
    h_out = _matmul(og, w_out, res=h, name="fox_out")
    return h_out, (hn, proj, f_raw, qg, kg, ka, va, qb, o, og, w_qkvg, w_f, w_out)


def _fox_bwd(h, dh, saved, p, out):
    hn, proj, f_raw, qg, kg, ka, va, qb, o, og, w_qkvg, w_f, w_out = saved
    s = h.shape[0]
    dh, dh_b = dh
    dog = _matmul(dh_b, w_out, tb=True, name="fox_out_dx")
    out.send("fox_w_out", _matmul(og, dh_b, ta=True, out_dtype=MXU_DT, name="fox_out_dw"))
    dgate, doa = _fox_gate_bwd(dog, o, proj, "fox_gate_bwd")
    dqa, dka, dv = _fox_attn_bwd(qb, ka, va, doa, "fox_attn_bwd")
    head_cols = lambda a, lane: jnp.pad(a.reshape(s, FOX_HEADS, LANES)[:, :, lane], ((0, 0), (0, LANES - FOX_HEADS)))
    dq, dk, df, dqg, dkg, dbf = _fox_prep_bwd(proj, f_raw, qg, kg, out.tie(_lane_pad(p["b_f"])), dqa, dka,
                                              head_cols(dqa, QSIDE), head_cols(dka, KSIDE), "fox_prep_bwd")
    dproj = jnp.concatenate([dq, dk, dv, dgate], axis=1)
    dhn = _matmul(dproj, w_qkvg, tb=True, name="fox_in_dx_qkvg")
    dhn = _matmul(df, w_f, tb=True, res=dhn, name="fox_in_dx_f")
    out.send("fox_w_in", jnp.concatenate(
        [_matmul(hn, dproj, ta=True, out_dtype=MXU_DT, name="fox_in_dw_qkvg"),
         _matmul(hn, df, ta=True, out_dtype=MXU_DT, name="fox_in_dw_f")[:, :FOX_HEADS]], axis=1))
    *dh_in, dnorm_g = _rms_bwd(h, out.tie(p["norm_g"]), dhn, dh, "mix_norm_bwd")
    fold = lambda g: g.reshape(FOX_HEADS, FOX_HEAD_DIM).sum(axis=0).reshape(1, -1)
    grads = dict(norm_g=dnorm_g, b_f=dbf[:, :FOX_HEADS], q_norm_g=fold(dqg), k_norm_g=fold(dkg))
    return dh_in, grads


def _my_index():
    return 4 * lax.axis_index("x") + 2 * lax.axis_index("y") + lax.axis_index("c")


def _peer(k):
    x, y, c = lax.axis_index("x"), lax.axis_index("y"), lax.axis_index("c")
    flip = lambda v, bit: 1 - v if bit else v
    return (flip(x, k & 4), flip(y, k & 2), flip(c, k & 1))


SEM_SPEC = pl.BlockSpec(memory_space=pltpu.SEMAPHORE)
DATAFLOW = pltpu.SideEffectType.DATAFLOW_SIDE_EFFECTING


def _at(ref, idx):
    return ref.at[tuple(idx)] if idx else ref


def _copies_start(name, srcs, lands, groups):
    ns, nl = len(srcs), len(lands)
    items = [item for group in groups for item in group]

    def body(*refs):
        src_refs, land_refs = refs[:ns], refs[ns:ns + nl]
        sems = refs[ns + nl:ns + nl + 2 * len(items)]
        me = _my_index()
        for t, (i, src_slot, j, dst_slot, _) in enumerate(items):
            for k in range(1, N_DEV):
                pltpu.make_async_remote_copy(
                    src_ref=_at(src_refs[i], src_slot(me, k)), dst_ref=_at(land_refs[j], dst_slot(me, k)),
                    send_sem=sems[2 * t], recv_sem=sems[2 * t + 1], device_id=_peer(k),
                    device_id_type=MESH_IDS).start()
        refs[-1][...] = jnp.zeros(refs[-1].shape, F32)

    hbm = pl.BlockSpec(memory_space=pltpu.HBM)
    bufs = [pltpu.with_memory_space_constraint(a, pltpu.HBM) for a in list(srcs) + list(lands)]
    n_sems = 2 * len(items)
    out = pl.pallas_call(
        body, name=name, in_specs=[hbm] * (ns + nl),
        out_specs=(*[SEM_SPEC] * n_sems, *[hbm] * (ns + nl), pl.BlockSpec(memory_space=pltpu.VMEM)),
        out_shape=(*[pltpu.SemaphoreType.DMA(())] * n_sems, *[pltpu.HBM(a.shape, a.dtype) for a in bufs],
                   _sds((8, LANES))),
        input_output_aliases={i: n_sems + i for i in range(ns + nl)},
        compiler_params=pltpu.CompilerParams(has_side_effects=DATAFLOW),
    )(*bufs)
    sems, t = [], 0
    for group in groups:
        sems.append([(out[2 * (t + u)], out[2 * (t + u) + 1]) for u in range(len(group))])
        t += len(group)
    return sems, list(out[n_sems:n_sems + ns]), list(out[n_sems + ns:n_sems + ns + nl]), out[-1]


def _copies_wait(name, keep, lands, sems, group, after):
    nk, nl, n = len(keep), len(lands), len(group)

    def body(*refs):
        land_refs = refs[nk:nk + nl]
        sem_refs = refs[nk + nl:nk + nl + 2 * n]
        me = _my_index()
        copies = []
        for t, (_, _, j, _, seven) in enumerate(group):
            blocks = _at(land_refs[j], seven(me))
            copies.append(pltpu.make_async_remote_copy(src_ref=blocks, dst_ref=blocks, send_sem=sem_refs[2 * t],
                                                       recv_sem=sem_refs[2 * t + 1], device_id=_peer(1),
                                                       device_id_type=MESH_IDS))
        for cp in copies:
            cp.wait_recv()
        for cp in copies:
            cp.wait_send()

    hbm = pl.BlockSpec(memory_space=pltpu.HBM)
    bufs = list(keep) + list(lands)
    out = pl.pallas_call(
        body, name=name, in_specs=[hbm] * (nk + nl) + [SEM_SPEC] * (2 * n) + [pl.BlockSpec(memory_space=pl.ANY)],
        out_specs=[hbm] * (nk + nl), out_shape=[pltpu.HBM(a.shape, a.dtype) for a in bufs],
        input_output_aliases={i: i for i in range(nk + nl)},
        compiler_params=pltpu.CompilerParams(has_side_effects=DATAFLOW),
    )(*bufs, *[s for pair in sems for s in pair], after)
    return list(out[nk:])


def _allreduce_small(buf, name):
    def body(in_ref, all_ref, sum_ref, send_sems, recv_sems):
        me = _my_index()
        all_ref[me] = in_ref[...]

        def copy(k, slot):
            return pltpu.make_async_remote_copy(
                src_ref=in_ref, dst_ref=all_ref.at[slot], send_sem=send_sems.at[k - 1], recv_sem=recv_sems.at[k - 1],
                device_id=_peer(k), device_id_type=MESH_IDS)

        for k in range(1, N_DEV):
            copy(k, me).start()
        for k in range(1, N_DEV):
            copy(k, jnp.bitwise_xor(me, k)).wait_recv()
        for k in range(1, N_DEV):
            copy(k, me).wait_send()
        acc = all_ref[0]
        for j in range(1, N_DEV):
            acc = acc + all_ref[j]
        sum_ref[...] = acc

    vmem = pl.BlockSpec(memory_space=pltpu.VMEM)
    return pl.pallas_call(
        body, name=name, in_specs=[vmem], out_specs=[vmem, vmem],
        out_shape=[_sds((N_DEV,) + buf.shape), _sds(buf.shape)],
        scratch_shapes=[pltpu.SemaphoreType.DMA((N_DEV - 1,)), pltpu.SemaphoreType.DMA((N_DEV - 1,))],
        compiler_params=pltpu.CompilerParams(vmem_limit_bytes=VMEM_LIMIT_BYTES),
    )(buf)[1]


def _adamw_math(w, g, m, v):
    m = ADAM_B1 * m + (1.0 - ADAM_B1) * g
    v = ADAM_B2 * v + (1.0 - ADAM_B2) * (g * g)
    m_hat = m / (1.0 - ADAM_B1 ** ADAM_STEP)
    v_hat = v / (1.0 - ADAM_B2 ** ADAM_STEP)
    return -ADAM_LR * (m_hat / (jnp.sqrt(v_hat) + ADAM_EPS) + ADAM_WD * w), m, v


def _row_tile(rows, cap=256, mult=16):
    best = None
    for t in range(mult, min(rows, cap) + 1, mult):
        if rows % t == 0:
            best = t
    assert best is not None, rows
    return best


def _adamw_sharded(w, m, v, partials, name):
    layers, rows, cols = w.shape
    tr = _row_tile(rows)

    def body(w_ref, m_ref, v_ref, p_ref, g_ref, d_ref, nm_ref, nv_ref):
        g = p_ref[0].astype(F32)
        for j in range(1, N_DEV):
            g = g + p_ref[j].astype(F32)
        delta, m_new, v_new = _adamw_math(w_ref[...], g, m_ref[...], v_ref[...])
        g_ref[...], d_ref[...], nm_ref[...], nv_ref[...] = g, delta, m_new, v_new

    blk = pl.BlockSpec((None, tr, cols), lambda l, i: (l, i, 0))
    return pl.pallas_call(
        body, name=name, grid=(layers, rows // tr),
        in_specs=[blk, blk, blk, pl.BlockSpec((N_DEV, None, tr, cols), lambda l, i: (0, l, i, 0))],
        out_specs=[blk] * 4, out_shape=[_sds(w.shape)] * 4, compiler_params=_params("parallel", "parallel"),
    )(w, m, v, partials)


def _adamw_small(w, g, m, v, name):
    def body(w_ref, g_ref, m_ref, v_ref, d_ref, nm_ref, nv_ref):
        d_ref[...], nm_ref[...], nv_ref[...] = _adamw_math(w_ref[...], g_ref[...], m_ref[...], v_ref[...])

    return pl.pallas_call(body, name=name, out_shape=[_sds(w.shape)] * 3,
                          compiler_params=_params())(w, g, m, v)


def _pack(arrays):
    flat = jnp.concatenate([a.reshape(-1).astype(F32) for a in arrays])
    pad = -flat.shape[0] % (8 * LANES)
    return jnp.pad(flat, (0, pad)).reshape(-1, LANES)


def _unpack(buf, shapes):
    flat, out, off = buf.reshape(-1), [], 0
    for shp in shapes:
        size = math.prod(shp)
        out.append(flat[off:off + size].reshape(shp))
        off += size
    return out


WEIGHTS = ["mix_norm_g", "ffn_norm_g", "ssd_w_in", "ssd_conv_w", "ssd_conv_b", "ssd_dt_bias", "ssd_a_log", "ssd_d",
           "ssd_norm_g", "ssd_w_out", "fox_w_in", "fox_b_f", "fox_q_norm_g", "fox_k_norm_g", "fox_w_out", "ffn_w_up",
           "ffn_conv_w", "ffn_conv_b", "ffn_w_down", "final_norm_g"]
BIG = ["ssd_w_in", "ssd_w_out", "fox_w_in", "fox_w_out", "ffn_w_up", "ffn_w_down"]
COLUMN_SHARDED = ["ssd_w_in", "fox_w_in", "ffn_w_up"]
CONV = ["ssd_conv_w", "ffn_conv_w"]
REPLICATED = [n for n in WEIGHTS if n not in BIG + CONV]
DEPTH = 4
ADAMW_ORDER = ["fox_w_out", "fox_w_in", "ffn_w_down", "ffn_w_up", "ssd_w_out", "ssd_w_in"]
LAYER_SHARDED = (["ssd_w_in", "ssd_conv_w", "ssd_w_out", "ffn_w_up", "ffn_conv_w", "ffn_w_down"],
                 ["fox_w_in", "fox_w_out", "ffn_w_up", "ffn_conv_w", "ffn_w_down"])


def _to_shards(full, on_columns):
    nl, r, c = full.shape
    if on_columns:
        return full.reshape(nl, r, N_DEV, c // N_DEV).transpose(2, 0, 1, 3)
    return full.reshape(nl, N_DEV, r // N_DEV, c).transpose(1, 0, 2, 3)


def _pad_cols(w):
    return jnp.pad(w, ((0, 0), (0, LANES - w.shape[1])))


def kernel(x, mix_norm_g, ffn_norm_g, ssd_w_in, ssd_conv_w, ssd_conv_b, ssd_dt_bias, ssd_a_log, ssd_d, ssd_norm_g, ssd_w_out, fox_w_in, fox_b_f, fox_q_norm_g, fox_k_norm_g, fox_w_out, ffn_w_up, ffn_conv_w, ffn_conv_b, ffn_w_down, final_norm_g, loss_target, m_mix_norm_g, m_ffn_norm_g, m_ssd_w_in, m_ssd_conv_w, m_ssd_conv_b, m_ssd_dt_bias, m_ssd_a_log, m_ssd_d, m_ssd_norm_g, m_ssd_w_out, m_fox_w_in, m_fox_b_f, m_fox_q_norm_g, m_fox_k_norm_g, m_fox_w_out, m_ffn_w_up, m_ffn_conv_w, m_ffn_conv_b, m_ffn_w_down, m_final_norm_g, v_mix_norm_g, v_ffn_norm_g, v_ssd_w_in, v_ssd_conv_w, v_ssd_conv_b, v_ssd_dt_bias, v_ssd_a_log, v_ssd_d, v_ssd_norm_g, v_ssd_w_out, v_fox_w_in, v_fox_b_f, v_fox_q_norm_g, v_fox_k_norm_g, v_fox_w_out, v_ffn_w_up, v_ffn_conv_w, v_ffn_conv_b, v_ffn_w_down, v_final_norm_g):
    given = dict(locals())
    w = {n: given[n] for n in WEIGHTS}
    mom = {n: given["m_" + n] for n in WEIGHTS}
    var = {n: given["v_" + n] for n in WEIGHTS}
    me = _my_index()

    sharded = BIG + CONV
    shards = [w[n].astype(MXU_DT) if n in BIG else w[n] for n in sharded]
    zones = [(n, i if n.startswith("ffn") else i // 2) for i in range(DEPTH) for n in LAYER_SHARDED[i % 2]]
    items = [[(sharded.index(n), functools.partial(lambda me, k, layer: (layer,), layer=layer), z,
               lambda me, k: (me,), lambda me: (pl.ds(0, N_DEV - 1),))] for z, (n, layer) in enumerate(zones)]
    empty = [lax.empty((N_DEV,) + shards[sharded.index(n)].shape[1:], shards[sharded.index(n)].dtype) for n, _ in zones]
    gather_sems, shards_kept, landing, _ = _copies_start("gather_start", shards, empty, items)

    def arrived(n, layer, after):
        z = zones.index((n, layer))
        item = items[z][0]
        stack, = _copies_wait(f"gather_wait_{n}_{layer}", shards_kept if z == len(zones) - 1 else [], [landing[z]],
                              gather_sems[z], [item[:2] + (0,) + item[3:]], after)
        stack = lax.dynamic_update_index_in_dim(stack, shards[sharded.index(n)][layer], me, 0)
        _, r, c = stack.shape
        on_columns = n in COLUMN_SHARDED or n in CONV
        return stack.transpose(1, 0, 2).reshape(r, N_DEV * c) if on_columns else stack.reshape(N_DEV * r, c)

    def mixer_params(i):
        j = i // 2
        if i % 2 == 0:
            return dict(norm_g=w["mix_norm_g"][i], conv_b=w["ssd_conv_b"][j], dt_bias=w["ssd_dt_bias"][j],
                        a_log=w["ssd_a_log"][j], d=w["ssd_d"][j], gnorm_g=w["ssd_norm_g"][j])
        return dict(norm_g=w["mix_norm_g"][i], b_f=w["fox_b_f"][j], q_norm_g=w["fox_q_norm_g"][j],
                    k_norm_g=w["fox_k_norm_g"][j])

    h = x[0]
    tape = []
    for i in range(DEPTH):
        big = _Weights(functools.partial(lambda n, after, i: arrived(n, i if n.startswith("ffn") else i // 2, after), i=i))
        mp, fp = mixer_params(i), dict(norm_g=w["ffn_norm_g"][i], conv_b=w["ffn_conv_b"][i])
        h_mid, mix_saved = (_ssd_fwd if i % 2 == 0 else _fox_fwd)(h, mp, big)
        h_out, ffn_saved = _ffn_fwd(h_mid, fp, big)
        tape.append((h, mp, mix_saved, h_mid, fp, ffn_saved))
        h = h_out
    *dh, dfinal_g, loss_part = _loss_head(h, w["final_norm_g"], loss_target[0], "loss_head")

    grads = {n: [None] * w[n].shape[0] for n in WEIGHTS if n not in BIG + ["final_norm_g"]}
    partials = {n: lax.empty((N_DEV,) + w[n].shape, MXU_DT) for n in BIG}
    in_flight = {n: [] for n in BIG}

    def send_partial(n, layer, grad):
        slots = _to_shards(grad[None], n in COLUMN_SHARDED)[:, 0]
        mine = lax.dynamic_index_in_dim(slots, me, 0, keepdims=False)
        zone = lax.dynamic_update_slice(partials[n], mine[None, None], (me, layer, 0, 0))
        item = (0, lambda me, k: (jnp.bitwise_xor(me, k),), 0,
                functools.partial(lambda me, k, layer: (me, layer), layer=layer),
                functools.partial(lambda me, layer: (pl.ds(0, N_DEV - 1), layer), layer=layer))
        sems, kept, (partials[n],), token = _copies_start(f"scatter_start_{n}_{layer}", [slots], [zone], [[item]])
        in_flight[n].append((layer, sems[0], kept, item))
        return token

    for i in reversed(range(DEPTH)):
        j = i // 2
        h_in, mp, mix_saved, h_mid, fp, ffn_saved = tape[i]
        out = _Gradients(functools.partial(lambda n, grad, i: send_partial(n, i if n.startswith("ffn") else i // 2, grad),
                                           i=i))
        dh, g = _ffn_bwd(h_mid, dh, ffn_saved, fp, out)
        grads["ffn_norm_g"][i], grads["ffn_conv_w"][i], grads["ffn_conv_b"][i] = g["norm_g"][0], g["conv_w"], g["conv_b"][0]
        if i % 2 == 0:
            dh, g = _ssd_bwd(h_in, dh, mix_saved, mp, out)
            grads["ssd_conv_w"][j] = g["conv_w"]
            for key, name in (("conv_b", "ssd_conv_b"), ("dt_bias", "ssd_dt_bias"), ("a_log", "ssd_a_log"),
                              ("d", "ssd_d"), ("gnorm_g", "ssd_norm_g")):
                grads[name][j] = g[key][0]
        else:
            dh, g = _fox_bwd(h_in, dh, mix_saved, mp, out)
            for key, name in (("b_f", "fox_b_f"), ("q_norm_g", "fox_q_norm_g"), ("k_norm_g", "fox_k_norm_g")):
                grads[name][j] = g[key][0]
        grads["mix_norm_g"][i] = g["norm_g"][0]
    grads = {n: jnp.stack(v) for n, v in grads.items()}
    grads["final_norm_g"] = dfinal_g[0]

    small_names = REPLICATED + CONV
    summed = _unpack(_allreduce_small(_pack([grads[n] for n in small_names] + [loss_part]), "allreduce_small"),
                     [grads[n].shape for n in small_names] + [(1, 1)])
    loss = summed[-1][0, 0]
    g_small = dict(zip(small_names, summed[:-1]))
    for n in CONV:
        width = w[n].shape[-1]
        g_small[n] = lax.dynamic_slice_in_dim(g_small[n], me * width, width, axis=2)
    pk = lambda d: _pack([d[n] for n in small_names])
    d_small, m_small, v_small = _adamw_small(pk(w), pk(g_small), pk(mom), pk(var), "adamw_small")
    shapes = [w[n].shape for n in small_names]
    out_g, out_d, out_m, out_v = dict(g_small), {}, {}, {}
    for dst, buf in ((out_d, d_small), (out_m, m_small), (out_v, v_small)):
        dst.update(zip(small_names, _unpack(buf, shapes)))

    after = d_small
    for n in ADAMW_ORDER:
        for layer, sems, kept, item in in_flight[n]:
            partials[n], = _copies_wait(f"scatter_wait_{n}_{layer}", kept, [partials[n]], sems, [item], after)
        out_g[n], out_d[n], out_m[n], out_v[n] = _adamw_sharded(w[n], mom[n], var[n], partials[n], "adamw_" + n)
        after = out_v[n]

    return (loss, dh[0][None], *[out_g[n] for n in WEIGHTS], *[out_d[n] for n in WEIGHTS],
            *[out_m[n] for n in WEIGHTS], *[out_v[n] for n in WEIGHTS])
```

```python
import functools
import math

import jax
import jax.numpy as jnp
from jax import lax
from jax.experimental import pallas as pl
from jax.experimental.pallas import tpu as pltpu

F32 = jnp.float32
MXU_DT = jnp.bfloat16
VMEM_LIMIT_BYTES = 56 * 1024 * 1024
LANES = 128
N_DEV = 8
MESH_IDS = pl.DeviceIdType.MESH

EPS = 1e-6
D_MODEL = 1024
SSD_D_INNER = 2048
SSD_HEAD_DIM = 64
SSD_HEADS = 32
SSD_GROUPS = 4
SSD_HPG = 8
SSD_STATE = 128
SSD_CONV = 4
SSD_CHUNK = 128
SSD_CONV_DIM = 3072
SSD_ZX = SSD_D_INNER + SSD_CONV_DIM
FOX_HEAD_DIM = 64
FOX_HEADS = 16
FOX_D = 1024
D_FF = 2816
FFN_CONV = 3
ADAM_LR, ADAM_B1, ADAM_B2, ADAM_EPS, ADAM_WD, ADAM_STEP = 0.001, 0.9, 0.999, 1e-08, 0.01, 10


def _params(*sem):
    return pltpu.CompilerParams(dimension_semantics=sem or None, vmem_limit_bytes=VMEM_LIMIT_BYTES)


def _sds(shape, dtype=F32):
    return jax.ShapeDtypeStruct(tuple(shape), dtype)


def _col_tile(n, cap=1536):
    best = None
    for t in range(LANES, min(n, cap) + 1, LANES):
        if n % t == 0:
            best = t
    assert best is not None, n
    return best


def _sigmoid(x):
    return 0.5 * jnp.tanh(0.5 * x) + 0.5


def _matmul(a, b, *, ta=False, tb=False, res=None, out_dtype=F32, tm=512, tn=None, b_kblock=0, name):
    (kdim, m) = a.shape if ta else a.shape[::-1]
    (n, k2) = b.shape if tb else b.shape[::-1]
    assert kdim == k2 or (tb and k2 % kdim == 0), (a.shape, b.shape, ta, tb)
    tm = min(tm, m)
    if m % tm:
        tm = _col_tile(m, tm)
    tn = tn or _col_tile(n)
    assert m % tm == 0 and n % tn == 0, (m, tm, n, tn)
    dims = (((0 if ta else 1,), (1 if tb else 0,)), ((), ()))

    def body(*refs):
        a_ref, b_ref = refs[0], refs[1]
        o_ref = refs[-1]
        acc = lax.dot_general(a_ref[...].astype(MXU_DT), b_ref[...].astype(MXU_DT), dims,
                              preferred_element_type=F32)
        if res is not None:
            acc = acc + refs[2][...].astype(F32)
        o_ref[...] = acc.astype(o_ref.dtype)

    a_spec = pl.BlockSpec((kdim, tm), lambda i, j: (0, i)) if ta else pl.BlockSpec((tm, kdim), lambda i, j: (i, 0))
    b_spec = pl.BlockSpec((tn, kdim), lambda i, j: (j, b_kblock)) if tb else pl.BlockSpec((kdim, tn), lambda i, j: (0, j))
    o_spec = pl.BlockSpec((tm, tn), lambda i, j: (i, j))
    ins, specs = [a, b], [a_spec, b_spec]
    if res is not None:
        ins.append(res)
        specs.append(o_spec)
    return pl.pallas_call(
        body, name=name, grid=(m // tm, n // tn), in_specs=specs, out_specs=o_spec,
        out_shape=_sds((m, n), out_dtype), compiler_params=_params("parallel", "parallel"),
    )(*ins)


def _rowwise(fn, rows, consts, out_rows, out_sums, *, tr=256, name):
    rows = [r if isinstance(r, tuple) else (r, 0, r.shape[1]) for r in rows]
    s = rows[0][0].shape[0]
    tr = min(tr, s)
    assert s % tr == 0
    n_in, n_c, n_or = len(rows), len(consts), len(out_rows)

    def body(*refs):
        ins = [r[...] for r in refs[:n_in + n_c]]
        outs = fn(*ins)
        o_refs = refs[n_in + n_c:]
        for o_ref, val in zip(o_refs[:n_or], outs[:n_or]):
            o_ref[...] = val.astype(o_ref.dtype)
        if out_sums:
            first = pl.program_id(0) == 0

            @pl.when(first)
            def _():
                for o_ref, val in zip(o_refs[n_or:], outs[n_or:]):
                    o_ref[...] = val.astype(o_ref.dtype)

            @pl.when(jnp.logical_not(first))
            def _():
                for o_ref, val in zip(o_refs[n_or:], outs[n_or:]):
                    o_ref[...] += val.astype(o_ref.dtype)

    in_specs = [pl.BlockSpec((tr, width), functools.partial(lambda i, cb: (i, cb), cb=cb)) for _, cb, width in rows]
    in_specs += [pl.BlockSpec(c.shape, lambda i: (0, 0)) for c in consts]
    out_specs = [pl.BlockSpec((tr, o.shape[1]), lambda i: (i, 0)) for o in out_rows]
    out_specs += [pl.BlockSpec(o.shape, lambda i: (0, 0)) for o in out_sums]
    return pl.pallas_call(
        body, name=name, grid=(s // tr,), in_specs=in_specs, out_specs=out_specs,
        out_shape=list(out_rows) + list(out_sums),
        compiler_params=_params("arbitrary" if out_sums else "parallel"),
    )(*[r[0] for r in rows], *consts)


def _rms_fwd(h, g, name):
    def fn(x, gv):
        r = lax.rsqrt(jnp.mean(x * x, axis=-1, keepdims=True) + EPS)
        return (x * r * gv,)
    return _rowwise(fn, [h], [g.reshape(1, -1)], [_sds(h.shape, MXU_DT)], [], name=name)[0]


def _rms_bwd(h, g, dy, dres, name):
    def fn(x, dyv, dr, gv):
        r = lax.rsqrt(jnp.mean(x * x, axis=-1, keepdims=True) + EPS)
        dyg = dyv * gv
        dx = dr + r * dyg - x * (r * r * r) * jnp.mean(x * dyg, axis=-1, keepdims=True)
        return dx, dx, jnp.sum(dyv * x * r, axis=0, keepdims=True)
    return _rowwise(fn, [h, dy, dres], [g.reshape(1, -1)], [_sds(h.shape), _sds(h.shape, MXU_DT)],
                    [_sds((1, h.shape[1]))], name=name)


def _loss_head(h, g, target, name):
    c = h.shape[1]

    def fn(x, t, gv):
        r = lax.rsqrt(jnp.mean(x * x, axis=-1, keepdims=True) + EPS)
        y = x * r * gv
        err = y - t
        dyv = err * (1.0 / c)
        dyg = dyv * gv
        dx = r * dyg - x * (r * r * r) * jnp.mean(x * dyg, axis=-1, keepdims=True)
        loss = 0.5 * jnp.sum(jnp.mean(err * err, axis=-1, keepdims=True), axis=0, keepdims=True)
        return dx, dx, jnp.sum(dyv * x * r, axis=0, keepdims=True), loss
    return _rowwise(fn, [h, target], [g.reshape(1, -1)], [_sds(h.shape), _sds(h.shape, MXU_DT)],
                    [_sds((1, c)), _sds((1, 1))], name=name)


PAD_ROWS = 8
ROW_TILE = 128


def _shifted_conv(xp_ref, w, r0, tr, kw):
    acc = None
    for k in range(kw):
        xk = xp_ref[pl.ds(PAD_ROWS + r0 - (kw - 1) + k, tr), :]
        term = xk * w[k:k + 1, :]
        acc = term if acc is None else acc + term
    return acc


def _convglu_fwd(u, conv_w, conv_b, name):
    s = u.shape[0]
    nt = D_FF // LANES
    tr = min(ROW_TILE, s)

    def body(ug_ref, uv_ref, w_ref, b_ref, act_ref, xp_ref):
        xp_ref[pl.ds(0, PAD_ROWS), :] = jnp.zeros((PAD_ROWS, LANES), F32)
        xp_ref[pl.ds(PAD_ROWS, s), :] = ug_ref[...]
        w = w_ref[...]
        b = b_ref[...]
        for r0 in range(0, s, tr):
            gate = _shifted_conv(xp_ref, w, r0, tr, FFN_CONV) + b
            act = gate * _sigmoid(gate) * uv_ref[pl.ds(r0, tr), :]
            act_ref[pl.ds(r0, tr), :] = act.astype(act_ref.dtype)

    return pl.pallas_call(
        body, name=name, grid=(nt,),
        in_specs=[pl.BlockSpec((s, LANES), lambda j: (0, j)), pl.BlockSpec((s, LANES), lambda j: (0, nt + j)),
                  pl.BlockSpec((FFN_CONV, LANES), lambda j: (0, j)), pl.BlockSpec((1, LANES), lambda j: (0, j))],
        out_specs=pl.BlockSpec((s, LANES), lambda j: (0, j)),
        out_shape=_sds((s, D_FF), MXU_DT),
        scratch_shapes=[pltpu.VMEM((s + PAD_ROWS, LANES), F32)],
        compiler_params=_params("parallel"),
    )(u, u, conv_w, conv_b.reshape(1, -1))


def _convglu_bwd(u, dact, conv_w, conv_b, name):
    s = u.shape[0]
    nt = D_FF // LANES
    tr = min(ROW_TILE, s)
    kw = FFN_CONV

    def body(ug_ref, uv_ref, da_ref, w_ref, b_ref, dug_ref, duv_ref, dw_ref, db_ref, xp_ref, dgp_ref):
        xp_ref[pl.ds(0, PAD_ROWS), :] = jnp.zeros((PAD_ROWS, LANES), F32)
        xp_ref[pl.ds(PAD_ROWS, s), :] = ug_ref[...]
        dgp_ref[pl.ds(s, PAD_ROWS), :] = jnp.zeros((PAD_ROWS, LANES), F32)
        w = w_ref[...]
        b = b_ref[...]
        dw = [jnp.zeros((1, LANES), F32) for _ in range(kw)]
        db = jnp.zeros((1, LANES), F32)
        for r0 in range(0, s, tr):
            gate = _shifted_conv(xp_ref, w, r0, tr, kw) + b
            sg = _sigmoid(gate)
            da = da_ref[pl.ds(r0, tr), :].astype(F32)
            duv_ref[pl.ds(r0, tr), :] = (da * gate * sg).astype(duv_ref.dtype)
            dgate = da * uv_ref[pl.ds(r0, tr), :] * (sg * (1.0 + gate * (1.0 - sg)))
            dgp_ref[pl.ds(r0, tr), :] = dgate
            db = db + jnp.sum(dgate, axis=0, keepdims=True)
            for k in range(kw):
                xk = xp_ref[pl.ds(PAD_ROWS + r0 - (kw - 1) + k, tr), :]
                dw[k] = dw[k] + jnp.sum(dgate * xk, axis=0, keepdims=True)
        for r0 in range(0, s, tr):
            acc = None
            for k in range(kw):
                term = dgp_ref[pl.ds(r0 + (kw - 1) - k, tr), :] * w[k:k + 1, :]
                acc = term if acc is None else acc + term
            dug_ref[pl.ds(r0, tr), :] = acc.astype(dug_ref.dtype)
        for k in range(kw):
            dw_ref[pl.ds(k, 1), :] = dw[k]
        db_ref[...] = db

    col = lambda j: (0, j)
    return pl.pallas_call(
        body, name=name, grid=(nt,),
        in_specs=[pl.BlockSpec((s, LANES), col), pl.BlockSpec((s, LANES), lambda j: (0, nt + j)),
                  pl.BlockSpec((s, LANES), col), pl.BlockSpec((kw, LANES), col), pl.BlockSpec((1, LANES), col)],
        out_specs=[pl.BlockSpec((s, LANES), col), pl.BlockSpec((s, LANES), col),
                   pl.BlockSpec((kw, LANES), col), pl.BlockSpec((1, LANES), col)],
        out_shape=[_sds((s, D_FF), MXU_DT), _sds((s, D_FF), MXU_DT), _sds((kw, D_FF)), _sds((1, D_FF))],
        scratch_shapes=[pltpu.VMEM((s + PAD_ROWS, LANES), F32), pltpu.VMEM((s + PAD_ROWS, LANES), F32)],
        compiler_params=_params("parallel"),
    )(u, u, dact, conv_w, conv_b.reshape(1, -1))


class _Weights:
    def __init__(self, fetch):
        self._fetch, self._got = fetch, {}

    def whole(self, name, after):
        if name not in self._got:
            self._got[name] = self._fetch(name, after)
        return self._got[name]


class _Gradients:
    def __init__(self, start):
        self._start, self._tokens = start, []

    def send(self, name, grad):
        token = self._start(name, grad)
        if token is not None:
            self._tokens.append(token)

    def tie(self, x):
        for token in self._tokens:
            x = x + token[0, 0]
        self._tokens = []
        return x


def _ffn_fwd(h, p, big):
    hf = _rms_fwd(h, p["norm_g"], "ffn_norm_fwd")
    w_up = big.whole("ffn_w_up", hf)
    u = _matmul(hf, w_up, tm=1024, name="ffn_up")
    conv_w = big.whole("ffn_conv_w", u)
    act = _convglu_fwd(u, conv_w, p["conv_b"], "convglu_fwd")
    w_down = big.whole("ffn_w_down", act)
    h_out = _matmul(act, w_down, res=h, name="ffn_down")
    return h_out, (hf, u, act, w_up, conv_w, w_down)


def _ffn_bwd(h, dh, saved, p, out):
    hf, u, act, w_up, conv_w, w_down = saved
    dh, dh_b = dh
    dact = _matmul(dh_b, w_down, tb=True, name="ffn_down_dx")
    out.send("ffn_w_down", _matmul(act, dh_b, ta=True, out_dtype=MXU_DT, name="ffn_down_dw"))
    dug, duv, dconv_w, dconv_b = _convglu_bwd(u, dact, conv_w, out.tie(p["conv_b"]), "convglu_bwd")
    dhf = _matmul(dug, w_up, tb=True, b_kblock=0, name="ffn_up_dx_gate")
    dhf = _matmul(duv, w_up, tb=True, b_kblock=1, res=dhf, name="ffn_up_dx_val")
    out.send("ffn_w_up", jnp.concatenate([_matmul(hf, dug, ta=True, out_dtype=MXU_DT, name="ffn_up_dw_gate"),
                                          _matmul(hf, duv, ta=True, out_dtype=MXU_DT, name="ffn_up_dw_val")], axis=1))
    *dh_in, dnorm_g = _rms_bwd(h, out.tie(p["norm_g"]), dhf, dh, "ffn_norm_bwd")
    return dh_in, dict(norm_g=dnorm_g, conv_w=dconv_w, conv_b=dconv_b)


def _dwconv_silu_fwd(proj, col0, n_ch, conv_w, conv_b, name):
    s = proj.shape[0]
    nt, t0, kw = n_ch // LANES, col0 // LANES, conv_w.shape[0]
    tr = min(ROW_TILE, s)

    def body(x_ref, w_ref, b_ref, o_ref, xp_ref):
        xp_ref[pl.ds(0, PAD_ROWS), :] = jnp.zeros((PAD_ROWS, LANES), F32)
        xp_ref[pl.ds(PAD_ROWS, s), :] = x_ref[...]
        w = w_ref[...]
        b = b_ref[...]
        for r0 in range(0, s, tr):
            pre = _shifted_conv(xp_ref, w, r0, tr, kw) + b
            o_ref[pl.ds(r0, tr), :] = pre * _sigmoid(pre)

    col = lambda j: (0, j)
    return pl.pallas_call(
        body, name=name, grid=(nt,),
        in_specs=[pl.BlockSpec((s, LANES), lambda j: (0, t0 + j)), pl.BlockSpec((kw, LANES), col),
                  pl.BlockSpec((1, LANES), col)],
        out_specs=pl.BlockSpec((s, LANES), col), out_shape=_sds((s, n_ch)),
        scratch_shapes=[pltpu.VMEM((s + PAD_ROWS, LANES), F32)],
        compiler_params=_params("parallel"),
    )(proj, conv_w, conv_b.reshape(1, -1))


def _dwconv_silu_bwd(proj, col0, n_ch, dout, conv_w, conv_b, name):
    s = proj.shape[0]
    nt, t0, kw = n_ch // LANES, col0 // LANES, conv_w.shape[0]
    tr = min(ROW_TILE, s)

    def body(x_ref, do_ref, w_ref, b_ref, dx_ref, dw_ref, db_ref, xp_ref, dgp_ref):
        xp_ref[pl.ds(0, PAD_ROWS), :] = jnp.zeros((PAD_ROWS, LANES), F32)
        xp_ref[pl.ds(PAD_ROWS, s), :] = x_ref[...]
        dgp_ref[pl.ds(s, PAD_ROWS), :] = jnp.zeros((PAD_ROWS, LANES), F32)
        w = w_ref[...]
        b = b_ref[...]
        dw = [jnp.zeros((1, LANES), F32) for _ in range(kw)]
        db = jnp.zeros((1, LANES), F32)
        for r0 in range(0, s, tr):
            pre = _shifted_conv(xp_ref, w, r0, tr, kw) + b
            sg = _sigmoid(pre)
            dpre = do_ref[pl.ds(r0, tr), :] * (sg * (1.0 + pre * (1.0 - sg)))
            dgp_ref[pl.ds(r0, tr), :] = dpre
            db = db + jnp.sum(dpre, axis=0, keepdims=True)
            for k in range(kw):
                xk = xp_ref[pl.ds(PAD_ROWS + r0 - (kw - 1) + k, tr), :]
                dw[k] = dw[k] + jnp.sum(dpre * xk, axis=0, keepdims=True)
        for r0 in range(0, s, tr):
            acc = None
            for k in range(kw):
                term = dgp_ref[pl.ds(r0 + (kw - 1) - k, tr), :] * w[k:k + 1, :]
                acc = term if acc is None else acc + term
            dx_ref[pl.ds(r0, tr), :] = acc.astype(dx_ref.dtype)
        for k in range(kw):
            dw_ref[pl.ds(k, 1), :] = dw[k]
        db_ref[...] = db

    col = lambda j: (0, j)
    return pl.pallas_call(
        body, name=name, grid=(nt,),
        in_specs=[pl.BlockSpec((s, LANES), lambda j: (0, t0 + j)), pl.BlockSpec((s, LANES), col),
                  pl.BlockSpec((kw, LANES), col), pl.BlockSpec((1, LANES), col)],
        out_specs=[pl.BlockSpec((s, LANES), col), pl.BlockSpec((kw, LANES), col), pl.BlockSpec((1, LANES), col)],
        out_shape=[_sds((s, n_ch), MXU_DT), _sds((kw, n_ch)), _sds((1, n_ch))],
        scratch_shapes=[pltpu.VMEM((s + PAD_ROWS, LANES), F32), pltpu.VMEM((s + PAD_ROWS, LANES), F32)],
        compiler_params=_params("parallel"),
    )(proj, dout, conv_w, conv_b.reshape(1, -1))


HIGHEST = lax.Precision.HIGHEST
PAIRS = SSD_HPG // 2
PAIR_W = 2 * SSD_HEAD_DIM
GROUP_W = SSD_HPG * SSD_HEAD_DIM


def _iota2(shape, axis):
    return lax.broadcasted_iota(jnp.int32, shape, axis)


def _lane_pad(v):
    return jnp.pad(v.reshape(1, -1), ((0, 0), (0, LANES - v.shape[0])))


def _heads_to_groups(a):
    s = a.shape[0]
    return a[:, :SSD_HEADS].reshape(s, SSD_GROUPS, SSD_HPG).transpose(1, 0, 2)


def _groups_to_heads(a):
    s = a.shape[1]
    return jnp.pad(a.transpose(1, 0, 2).reshape(s, SSD_HEADS), ((0, 0), (0, LANES - SSD_HEADS)))


def _ssd_prep(dt_raw, dt_bias, a_log, name):
    s = dt_raw.shape[0]
    lc = SSD_CHUNK

    def body(x_ref, b_ref, al_ref, dt_ref, acs_ref, acst_ref):
        x = x_ref[...] + b_ref[...]
        dt = jnp.maximum(x, 0.0) + jnp.log1p(jnp.exp(-jnp.abs(x)))
        da = dt * (-jnp.exp(al_ref[...]))
        lower = (_iota2((lc, lc), 0) >= _iota2((lc, lc), 1)).astype(F32)
        upper = (_iota2((lc, lc), 0) <= _iota2((lc, lc), 1)).astype(F32)
        dt_ref[...] = dt
        acs_ref[...] = jnp.dot(lower, da, precision=HIGHEST, preferred_element_type=F32)
        acst_ref[...] = lax.dot_general(da, upper, (((0,), (0,)), ((), ())), precision=HIGHEST,
                                        preferred_element_type=F32)

    row = pl.BlockSpec((lc, LANES), lambda c: (c, 0))
    one = pl.BlockSpec((1, LANES), lambda c: (0, 0))
    return pl.pallas_call(
        body, name=name, grid=(s // lc,), in_specs=[row, one, one],
        out_specs=[row, row, pl.BlockSpec((LANES, lc), lambda c: (0, c))],
        out_shape=[_sds((s, LANES)), _sds((s, LANES)), _sds((LANES, s))],
        compiler_params=_params("parallel"),
    )(dt_raw, dt_bias, a_log)


def _pair_cols(v, p, lo):
    return jnp.where(lo, v[:, 2 * p:2 * p + 1], v[:, 2 * p + 1:2 * p + 2])


def _decay_matrix(acs, acst, h, tri):
    return jnp.exp(jnp.where(tri, acs[:, h:h + 1] - acst[h:h + 1, :], -jnp.inf))


def _dot(a, b, ca, cb):
    return lax.dot_general(a, b, (((ca,), (cb,)), ((), ())), preferred_element_type=F32)


def _ssd_scan_fwd(xbc, dtg, acsg, acstg, d_skip, name):
    s = xbc.shape[0]
    lc, nc = SSD_CHUNK, s // SSD_CHUNK
    xt, bt = GROUP_W // LANES, SSD_D_INNER // LANES

    def body(x_ref, b_ref, c_ref, dt_ref, acs_ref, acst_ref, dsk_ref, y_ref, hp_ref, st_ref):
        @pl.when(pl.program_id(1) == 0)
        def _():
            st_ref[...] = jnp.zeros(st_ref.shape, F32)

        bm, cmb = b_ref[...], c_ref[...].astype(MXU_DT)
        dt, acs, acst = dt_ref[...], acs_ref[...], acst_ref[...]
        cb = _dot(cmb, bm.astype(MXU_DT), 1, 1)
        tri = _iota2((lc, lc), 0) >= _iota2((lc, lc), 1)
        lo = _iota2((lc, PAIR_W), 1) < SSD_HEAD_DIM
        a_last = acs[lc - 1:lc, :]
        e_acs, e_ds, e_cd = jnp.exp(acs), jnp.exp(a_last - acs), jnp.exp(a_last)
        for p in range(PAIRS):
            sl = pl.ds(p * PAIR_W, PAIR_W)
            xp = x_ref[:, sl]
            ub = (xp * _pair_cols(dt, p, lo)).astype(MXU_DT)
            m0 = (cb * _decay_matrix(acs, acst, 2 * p, tri)).astype(MXU_DT)
            m1 = (cb * _decay_matrix(acs, acst, 2 * p + 1, tri)).astype(MXU_DT)
            ht = st_ref[p]
            hp_ref[p] = ht
            y = jnp.where(lo, _dot(m0, ub, 1, 0), _dot(m1, ub, 1, 0))
            y = y + _dot(cmb, ht.astype(MXU_DT), 1, 0) * _pair_cols(e_acs, p, lo)
            y_ref[:, sl] = y + xp * dsk_ref[:, sl]
            bd0 = (bm * e_ds[:, 2 * p:2 * p + 1]).astype(MXU_DT)
            bd1 = (bm * e_ds[:, 2 * p + 1:2 * p + 2]).astype(MXU_DT)
            st_ref[p] = ht * _pair_cols(e_cd, p, lo[:1]) + jnp.where(lo, _dot(bd0, ub, 0, 0), _dot(bd1, ub, 0, 0))

    small = pl.BlockSpec((None, lc, SSD_HPG), lambda g, c: (g, c, 0))
    return pl.pallas_call(
        body, name=name, grid=(SSD_GROUPS, nc),
        in_specs=[pl.BlockSpec((lc, GROUP_W), lambda g, c: (c, g)),
                  pl.BlockSpec((lc, LANES), lambda g, c: (c, bt + g)),
                  pl.BlockSpec((lc, LANES), lambda g, c: (c, bt + SSD_GROUPS + g)),
                  small, small, pl.BlockSpec((None, SSD_HPG, lc), lambda g, c: (g, 0, c)),
                  pl.BlockSpec((1, GROUP_W), lambda g, c: (0, g))],
        out_specs=[pl.BlockSpec((lc, GROUP_W), lambda g, c: (c, g)),
                   pl.BlockSpec((None, PAIRS, SSD_STATE, PAIR_W), lambda g, c: (c, g, 0, 0))],
        out_shape=[_sds((s, SSD_D_INNER)), _sds((nc, SSD_GROUPS * PAIRS, SSD_STATE, PAIR_W))],
        scratch_shapes=[pltpu.VMEM((PAIRS, SSD_STATE, PAIR_W), F32)],
        compiler_params=_params("parallel", "arbitrary"),
    )(xbc, xbc, xbc, dtg, acsg, acstg, d_skip)


def _ssd_scan_bwd(xbc, dtg, acsg, acstg, d_skip, dy, hprev, name):
    s = xbc.shape[0]
    lc, nc = SSD_CHUNK, s // SSD_CHUNK
    bt = SSD_D_INNER // LANES

    def body(x_ref, b_ref, c_ref, dt_ref, acs_ref, acst_ref, dsk_ref, dy_ref, hp_ref, hn_ref,
             dx_ref, db_ref, dc_ref, daq_ref, dar_ref, ddtx_ref, dd_ref, dst_ref, ta_ref, tx_ref):
        @pl.when(pl.program_id(1) == 0)
        def _():
            dst_ref[...] = jnp.zeros(dst_ref.shape, F32)
            dd_ref[...] = jnp.zeros(dd_ref.shape, F32)

        bm, cmb = b_ref[...], c_ref[...].astype(MXU_DT)
        bmb = bm.astype(MXU_DT)
        dt, acs, acst = dt_ref[...], acs_ref[...], acst_ref[...]
        cb = _dot(cmb, bmb, 1, 1)
        tri = _iota2((lc, lc), 0) >= _iota2((lc, lc), 1)
        lo = _iota2((lc, PAIR_W), 1) < SSD_HEAD_DIM
        a_last = acs[lc - 1:lc, :]
        e_acs, e_ds, e_cd = jnp.exp(acs), jnp.exp(a_last - acs), jnp.exp(a_last)
        dcb = jnp.zeros((lc, lc), F32)
        dc_x = jnp.zeros((lc, SSD_STATE), F32)
        db_x = jnp.zeros((lc, SSD_STATE), F32)
        da_in = jnp.zeros((lc, LANES), F32)
        da_out = jnp.zeros((SSD_HPG, lc), F32)
        head_col = _iota2((lc, LANES), 1)
        head_row = _iota2((SSD_HPG, lc), 0)
        last = _iota2((SSD_HPG, lc), 1) == lc - 1
        for p in range(PAIRS):
            sl = pl.ds(p * PAIR_W, PAIR_W)
            xp, dyp, dsk = x_ref[:, sl], dy_ref[:, sl], dsk_ref[:, sl]
            dtp = _pair_cols(dt, p, lo)
            u = xp * dtp
            ub, dyb = u.astype(MXU_DT), dyp.astype(MXU_DT)
            lmat = (_decay_matrix(acs, acst, 2 * p, tri), _decay_matrix(acs, acst, 2 * p + 1, tri))
            m0, m1 = (cb * lmat[0]).astype(MXU_DT), (cb * lmat[1]).astype(MXU_DT)
            ea, dsl = _pair_cols(e_acs, p, lo), _pair_cols(e_ds, p, lo)
            dht, ht = dst_ref[p], hp_ref[p]
            dhtb, htb = dht.astype(MXU_DT), ht.astype(MXU_DT)
            bd0 = (bm * e_ds[:, 2 * p:2 * p + 1]).astype(MXU_DT)
            bd1 = (bm * e_ds[:, 2 * p + 1:2 * p + 2]).astype(MXU_DT)
            du_state = jnp.where(lo, _dot(bd0, dhtb, 1, 0), _dot(bd1, dhtb, 1, 0))
            du = jnp.where(lo, _dot(m0, dyb, 0, 0), _dot(m1, dyb, 0, 0)) + du_state
            y_off = _dot(cmb, htb, 1, 0) * ea
            ta_ref[:, sl] = dyp * y_off - u * du_state
            tx_ref[:, sl] = du * xp
            dx_ref[:, sl] = dtp * du + dsk * dyp
            dd_ref[:, sl] += jnp.sum(dyp * xp, axis=0, keepdims=True)
            dy_h = (jnp.where(lo, dyp, 0.0).astype(MXU_DT), jnp.where(lo, 0.0, dyp).astype(MXU_DT))
            carry = jnp.sum(dht * hn_ref[p], axis=0, keepdims=True)
            for hh in range(2):
                h = 2 * p + hh
                dml = _dot(dy_h[hh], ub, 1, 1) * lmat[hh]
                dcb = dcb + dml
                flow = cb * dml
                da_in = da_in + jnp.where(head_col == h, jnp.sum(flow, axis=1, keepdims=True), 0.0)
                through = jnp.sum(jnp.where(lo[:1] == (hh == 0), carry, 0.0), axis=1, keepdims=True)
                da_out = da_out + jnp.where(head_row == h, jnp.sum(flow, axis=0, keepdims=True)
                                            - jnp.where(last, through, 0.0), 0.0)
            dye = (dyp * ea).astype(MXU_DT)
            dc_x = dc_x + _dot(dye, htb, 1, 1)
            db_x = db_x + _dot((u * dsl).astype(MXU_DT), dhtb, 1, 1)
            dst_ref[p] = dht * _pair_cols(e_cd, p, lo[:1]) + _dot(cmb, dye, 0, 0)
        dcbb = dcb.astype(MXU_DT)
        dc_ref[...] = _dot(dcbb, bmb, 1, 0) + dc_x
        db_ref[...] = _dot(dcbb, cmb, 0, 0) + db_x
        seg_lo = _iota2((GROUP_W, LANES), 1) * SSD_HEAD_DIM
        chan = _iota2((GROUP_W, LANES), 0)
        seg = jnp.logical_and(chan >= seg_lo, chan < seg_lo + SSD_HEAD_DIM).astype(F32)
        da_in = da_in + jnp.dot(ta_ref[...], seg, precision=HIGHEST, preferred_element_type=F32)
        daq_ref[...] = da_in[:, :SSD_HPG]
        dar_ref[...] = da_out
        ddtx_ref[...] = jnp.dot(tx_ref[...], seg, precision=HIGHEST, preferred_element_type=F32)[:, :SSD_HPG]

    rev = lambda c: nc - 1 - c
    small = pl.BlockSpec((None, lc, SSD_HPG), lambda g, c: (g, rev(c), 0))
    small_t = pl.BlockSpec((None, SSD_HPG, lc), lambda g, c: (g, 0, rev(c)))
    wide = pl.BlockSpec((lc, GROUP_W), lambda g, c: (rev(c), g))
    state = lambda at: pl.BlockSpec((None, PAIRS, SSD_STATE, PAIR_W), lambda g, c: (at(c), g, 0, 0))
    return pl.pallas_call(
        body, name=name, grid=(SSD_GROUPS, nc),
        in_specs=[pl.BlockSpec((lc, GROUP_W), lambda g, c: (rev(c), g)),
                  pl.BlockSpec((lc, LANES), lambda g, c: (rev(c), bt + g)),
                  pl.BlockSpec((lc, LANES), lambda g, c: (rev(c), bt + SSD_GROUPS + g)),
                  small, small, small_t, pl.BlockSpec((1, GROUP_W), lambda g, c: (0, g)), wide,
                  state(rev), state(lambda c: jnp.minimum(rev(c) + 1, nc - 1))],
        out_specs=[wide, pl.BlockSpec((lc, LANES), lambda g, c: (rev(c), g)),
                   pl.BlockSpec((lc, LANES), lambda g, c: (rev(c), g)), small, small_t, small,
                   pl.BlockSpec((1, GROUP_W), lambda g, c: (0, g))],
        out_shape=[_sds((s, SSD_D_INNER)), _sds((s, SSD_GROUPS * SSD_STATE)), _sds((s, SSD_GROUPS * SSD_STATE)),
                   _sds((SSD_GROUPS, s, SSD_HPG)), _sds((SSD_GROUPS, SSD_HPG, s)), _sds((SSD_GROUPS, s, SSD_HPG)),
                   _sds((1, SSD_D_INNER))],
        scratch_shapes=[pltpu.VMEM((PAIRS, SSD_STATE, PAIR_W), F32), pltpu.VMEM((lc, GROUP_W), F32),
                        pltpu.VMEM((lc, GROUP_W), F32)],
        compiler_params=_params("parallel", "arbitrary"),
    )(xbc, xbc, xbc, dtg, acsg, acstg, d_skip, dy, hprev, hprev)


def _ssd_post(da_in, da_out, ddtx, dt, dt_raw, dt_bias, a_log, name):
    s = da_in.shape[0]
    lc = SSD_CHUNK

    def body(dain_ref, daout_ref, ddtx_ref, dt_ref, x_ref, b_ref, al_ref, ddr_ref, dal_ref, dbias_ref):
        @pl.when(pl.program_id(0) == 0)
        def _():
            dal_ref[...] = jnp.zeros(dal_ref.shape, F32)
            dbias_ref[...] = jnp.zeros(dbias_ref.shape, F32)

        upper = (_iota2((lc, lc), 0) <= _iota2((lc, lc), 1)).astype(F32)
        dda = jnp.dot(upper, dain_ref[...] - daout_ref[...], precision=HIGHEST, preferred_element_type=F32)
        a = -jnp.exp(al_ref[...])
        ddt = dda * a + ddtx_ref[...]
        dal_ref[...] += jnp.sum(dda * dt_ref[...], axis=0, keepdims=True) * a
        ddr = ddt * _sigmoid(x_ref[...] + b_ref[...])
        ddr_ref[...] = ddr.astype(ddr_ref.dtype)
        dbias_ref[...] += jnp.sum(ddr, axis=0, keepdims=True)

    row = pl.BlockSpec((lc, LANES), lambda i: (i, 0))
    one = pl.BlockSpec((1, LANES), lambda i: (0, 0))
    return pl.pallas_call(
        body, name=name, grid=(s // lc,), in_specs=[row, row, row, row, row, one, one], out_specs=[row, one, one],
        out_shape=[_sds((s, LANES), MXU_DT), _sds((1, LANES)), _sds((1, LANES))],
        compiler_params=_params("arbitrary"),
    )(da_in, da_out, ddtx, dt, dt_raw, dt_bias, a_log)


NORM_GROUP_W = SSD_D_INNER // SSD_GROUPS


def _group_rstd(yz):
    return [lax.rsqrt(jnp.mean(jnp.square(yz[:, g * NORM_GROUP_W:(g + 1) * NORM_GROUP_W]), axis=-1, keepdims=True) + EPS)
            for g in range(SSD_GROUPS)]


def _gated_norm_fwd(y, proj, norm_g, name):
    def fn(yv, z, gv):
        yz = yv * (z * _sigmoid(z))
        parts = [yz[:, g * NORM_GROUP_W:(g + 1) * NORM_GROUP_W] * r for g, r in enumerate(_group_rstd(yz))]
        return (jnp.concatenate(parts, axis=1) * gv,)
    return _rowwise(fn, [y, (proj, 0, SSD_D_INNER)], [norm_g.reshape(1, -1)], [_sds(y.shape, MXU_DT)], [],
                    name=name)[0]


def _gated_norm_bwd(y, proj, norm_g, dout, name):
    def fn(yv, z, do, gv):
        sg = _sigmoid(z)
        sz = z * sg
        yz = yv * sz
        dog = do * gv
        dyz, dg = [], []
        for g, r in enumerate(_group_rstd(yz)):
            cols = slice(g * NORM_GROUP_W, (g + 1) * NORM_GROUP_W)
            yzg, dogg = yz[:, cols], dog[:, cols]
            dyz.append(r * dogg - yzg * (r * r * r) * jnp.mean(yzg * dogg, axis=-1, keepdims=True))
            dg.append(jnp.sum(do[:, cols] * yzg * r, axis=0, keepdims=True))
        dyz = jnp.concatenate(dyz, axis=1)
        return dyz * sz, dyz * yv * (sg * (1.0 + z * (1.0 - sg))), jnp.concatenate(dg, axis=1)
    return _rowwise(fn, [y, (proj, 0, SSD_D_INNER), dout], [norm_g.reshape(1, -1)],
                    [_sds(y.shape), _sds(y.shape, MXU_DT)], [_sds((1, y.shape[1]))], name=name)


def _ssd_fwd(h, p, big):
    hn = _rms_fwd(h, p["norm_g"], "mix_norm_fwd")
    w_in = big.whole("ssd_w_in", hn)
    w_zx, w_dt = w_in[:, :SSD_ZX], _pad_cols(w_in[:, SSD_ZX:])
    proj = _matmul(hn, w_zx, tm=1024, name="ssd_in_zx")
    dt_raw = _matmul(hn, w_dt, name="ssd_in_dt")
    conv_w = big.whole("ssd_conv_w", proj)
    xbc = _dwconv_silu_fwd(proj, SSD_D_INNER, SSD_CONV_DIM, conv_w, p["conv_b"], "ssd_conv_fwd")
    dt, acs, acst = _ssd_prep(dt_raw, _lane_pad(p["dt_bias"]), _lane_pad(p["a_log"]), "ssd_prep")
    dtg, acsg = _heads_to_groups(dt), _heads_to_groups(acs)
    acstg = acst[:SSD_HEADS].reshape(SSD_GROUPS, SSD_HPG, -1)
    d_skip = jnp.repeat(p["d"], SSD_HEAD_DIM).reshape(1, -1)
    y, hprev = _ssd_scan_fwd(xbc, dtg, acsg, acstg, d_skip, "ssd_scan_fwd")
    yn = _gated_norm_fwd(y, proj, p["gnorm_g"], "ssd_gnorm_fwd")
    w_out = big.whole("ssd_w_out", yn)
    h_out = _matmul(yn, w_out, res=h, name="ssd_out")
    return h_out, (hn, proj, dt_raw, xbc, dt, dtg, acsg, acstg, d_skip, y, hprev, yn, w_zx, w_dt, conv_w, w_out)


def _ssd_bwd(h, dh, saved, p, out):
    hn, proj, dt_raw, xbc, dt, dtg, acsg, acstg, d_skip, y, hprev, yn, w_zx, w_dt, conv_w, w_out = saved
    dh, dh_b = dh
    dyn = _matmul(dh_b, w_out, tb=True, name="ssd_out_dx")
    out.send("ssd_w_out", _matmul(yn, dh_b, ta=True, out_dtype=MXU_DT, name="ssd_out_dw"))
    dy, dz, dgnorm = _gated_norm_bwd(y, proj, out.tie(p["gnorm_g"]), dyn, "ssd_gnorm_bwd")
    dx, dbm, dcm, daq, dar, ddtx, dd = _ssd_scan_bwd(xbc, dtg, acsg, acstg, d_skip, dy, hprev, "ssd_scan_bwd")
    dxbc = jnp.concatenate([dx, dbm, dcm], axis=1)
    dpre, dconv_w, dconv_b = _dwconv_silu_bwd(proj, SSD_D_INNER, SSD_CONV_DIM, dxbc, conv_w, p["conv_b"],
                                              "ssd_conv_bwd")
    ddr, dalog, dbias = _ssd_post(_groups_to_heads(daq), _groups_to_heads(dar.transpose(0, 2, 1)),
                                  _groups_to_heads(ddtx), dt, dt_raw,
                                  _lane_pad(p["dt_bias"]), _lane_pad(p["a_log"]), "ssd_post")
    w_z, w_x = w_zx[:, :SSD_D_INNER], w_zx[:, SSD_D_INNER:]
    dhn = _matmul(dz, w_z, tb=True, name="ssd_in_dx_z")
    dhn = _matmul(dpre, w_x, tb=True, res=dhn, name="ssd_in_dx_x")
    dhn = _matmul(ddr, w_dt, tb=True, res=dhn, name="ssd_in_dx_dt")
    out.send("ssd_w_in", jnp.concatenate(
        [_matmul(hn, dz, ta=True, out_dtype=MXU_DT, name="ssd_in_dw_z"),
         _matmul(hn, dpre, ta=True, out_dtype=MXU_DT, name="ssd_in_dw_x"),
         _matmul(hn, ddr, ta=True, out_dtype=MXU_DT, name="ssd_in_dw_dt")[:, :SSD_HEADS]], axis=1))
    *dh_in, dnorm_g = _rms_bwd(h, out.tie(p["norm_g"]), dhn, dh, "mix_norm_bwd")
    grads = dict(norm_g=dnorm_g, conv_w=dconv_w, conv_b=dconv_b, dt_bias=dbias[:, :SSD_HEADS],
                 a_log=dalog[:, :SSD_HEADS], d=dd.reshape(SSD_HEADS, SSD_HEAD_DIM).sum(axis=1).reshape(1, -1),
                 gnorm_g=dgnorm)
    return dh_in, grads


FOX_PAIRS = FOX_HEADS // 2
ATT_TQ = 256
ATT_TK = 512
FOX_PREP_ROWS = 256
NEG_BIG = -1e30
FOX_SCALE = FOX_HEAD_DIM ** -0.5
FOX_AUG_D = FOX_HEADS * LANES
QSIDE = FOX_HEAD_DIM
KSIDE = FOX_HEAD_DIM + 3


def _split3(x):
    a = x.astype(MXU_DT).astype(F32)
    b = (x - a).astype(MXU_DT).astype(F32)
    return a, b, (x - a - b).astype(MXU_DT).astype(F32)


def _head_tiles(pair_tile):
    return pair_tile, pltpu.roll(pair_tile, FOX_HEAD_DIM, 1)


def _fill_lanes(base, lane, first, values):
    for i, v in enumerate(values):
        base = jnp.where(lane == first + i, v, base)
    return base


def _pair_tile(lane, tile0, tile1):
    return jnp.where(lane < FOX_HEAD_DIM, tile0, pltpu.roll(tile1, FOX_HEAD_DIM, 1))


def _compact_heads(a, lane):
    return jnp.concatenate([_pair_tile(lane, a[:, 2 * j * LANES:(2 * j + 1) * LANES],
                                       a[:, (2 * j + 1) * LANES:(2 * j + 2) * LANES]) for j in range(FOX_PAIRS)], axis=1)


def _head_sum_matrix():
    return (_iota2((LANES, LANES), 0) < FOX_HEAD_DIM) == (_iota2((LANES, LANES), 1) < FOX_HEAD_DIM)


def _head_sums(x):
    bd = _head_sum_matrix().astype(F32)
    parts = [jnp.dot(x[:, j * LANES:(j + 1) * LANES], bd, precision=HIGHEST, preferred_element_type=F32)
             for j in range(x.shape[1] // LANES)]
    return parts[0] if len(parts) == 1 else jnp.concatenate(parts, axis=1)


def _fox_prep_fwd(proj, f_raw, qg, kg, b_f, name):
    s = proj.shape[0]
    tr = min(FOX_PREP_ROWS, s)

    def body(q_ref, k_ref, v_ref, f_ref, qg_ref, kg_ref, b_ref, qa_ref, ka_ref, va_ref, carry_ref):
        @pl.when(pl.program_id(0) == 0)
        def _():
            carry_ref[...] = jnp.zeros(carry_ref.shape, F32)

        normed = []
        for x_ref, g_ref in ((q_ref, qg_ref), (k_ref, kg_ref)):
            x = x_ref[...]
            r = lax.rsqrt(_head_sums(x * x) * (1.0 / FOX_HEAD_DIM) + EPS)
            normed.append(x * r * g_ref[...])
        qn, kn, v = normed[0] * FOX_SCALE, normed[1], v_ref[...]
        x = f_ref[...] + b_ref[...]
        lf = jnp.minimum(x, 0.0) - jnp.log1p(jnp.exp(-jnp.abs(x)))
        lower = (_iota2((tr, tr), 0) >= _iota2((tr, tr), 1)).astype(F32)
        cum = jnp.dot(lower, lf, precision=HIGHEST, preferred_element_type=F32) + carry_ref[...]
        carry_ref[...] += jnp.sum(lf, axis=0, keepdims=True)
        first = _iota2((LANES, FOX_D), 0) * FOX_HEAD_DIM
        chan = _iota2((LANES, FOX_D), 1)
        spread = jnp.logical_and(chan >= first, chan < first + FOX_HEAD_DIM).astype(F32)
        cum = jnp.dot(cum, spread, precision=HIGHEST, preferred_element_type=F32)
        lane = _iota2((tr, LANES), 1)
        ones = jnp.where(jnp.logical_and(lane >= QSIDE, lane < KSIDE + 3), 1.0, 0.0)
        for j in range(FOX_PAIRS):
            cols = slice(j * LANES, (j + 1) * LANES)
            tiles = zip(_head_tiles(qn[:, cols]), _head_tiles(kn[:, cols]), _head_tiles(v[:, cols]),
                        reversed(_head_tiles(cum[:, cols])))
            for hh, (qt, kt, vt, ct) in enumerate(tiles):
                out = slice((2 * j + hh) * LANES, (2 * j + hh + 1) * LANES)
                c3 = _split3(ct)
                head = lane < FOX_HEAD_DIM
                qa_ref[:, out] = _fill_lanes(jnp.where(head, qt, ones), lane, QSIDE, c3).astype(qa_ref.dtype)
                ka_ref[:, out] = _fill_lanes(jnp.where(head, kt, ones), lane, KSIDE, [-c for c in c3]).astype(ka_ref.dtype)
                va_ref[:, out] = jnp.where(head, vt, jnp.where(lane < KSIDE, 1.0, 0.0)).astype(va_ref.dtype)

    wide = lambda cb: pl.BlockSpec((tr, FOX_D), lambda i: (i, cb))
    aug = pl.BlockSpec((tr, FOX_AUG_D), lambda i: (i, 0))
    one = lambda n: pl.BlockSpec((1, n), lambda i: (0, 0))
    return pl.pallas_call(
        body, name=name, grid=(s // tr,),
        in_specs=[wide(0), wide(1), wide(2), pl.BlockSpec((tr, LANES), lambda i: (i, 0)), one(FOX_D), one(FOX_D),
                  one(LANES)],
        out_specs=[aug, aug, aug], out_shape=[_sds((s, FOX_AUG_D), MXU_DT)] * 3,
        scratch_shapes=[pltpu.VMEM((1, LANES), F32)],
        compiler_params=_params("arbitrary"),
    )(proj, proj, proj, f_raw, qg, kg, b_f)


def _fox_prep_bwd(proj, f_raw, qg, kg, b_f, dqa, dka, row_sums, col_sums, name):
    s = proj.shape[0]
    tr = min(FOX_PREP_ROWS, s)
    nb = s // tr

    def body(q_ref, k_ref, f_ref, qg_ref, kg_ref, b_ref, dqa_ref, dka_ref, dcq_ref, dck_ref,
             dq_ref, dk_ref, df_ref, dqg_ref, dkg_ref, db_ref, carry_ref):
        @pl.when(pl.program_id(0) == 0)
        def _():
            carry_ref[...] = jnp.zeros(carry_ref.shape, F32)
            dqg_ref[...] = jnp.zeros(dqg_ref.shape, F32)
            dkg_ref[...] = jnp.zeros(dkg_ref.shape, F32)
            db_ref[...] = jnp.zeros(db_ref.shape, F32)

        lane = _iota2((tr, LANES), 1)
        for x_ref, g_ref, dt_ref, scale, dx_ref, dg_ref in ((q_ref, qg_ref, dqa_ref, FOX_SCALE, dq_ref, dqg_ref),
                                                            (k_ref, kg_ref, dka_ref, 1.0, dk_ref, dkg_ref)):
            x, dy = x_ref[...], _compact_heads(dt_ref[...], lane) * scale
            r = lax.rsqrt(_head_sums(x * x) * (1.0 / FOX_HEAD_DIM) + EPS)
            dyg = dy * g_ref[...]
            dx = r * dyg - x * (r * r * r) * (_head_sums(x * dyg) * (1.0 / FOX_HEAD_DIM))
            dx_ref[...] = dx.astype(dx_ref.dtype)
            dg_ref[...] += jnp.sum(dy * x * r, axis=0, keepdims=True)
        dc = dcq_ref[...] - dck_ref[...]
        upper = (_iota2((tr, tr), 0) <= _iota2((tr, tr), 1)).astype(F32)
        dlf = jnp.dot(upper, dc, precision=HIGHEST, preferred_element_type=F32) + carry_ref[...]
        carry_ref[...] += jnp.sum(dc, axis=0, keepdims=True)
        df = dlf * _sigmoid(-(f_ref[...] + b_ref[...]))
        df_ref[...] = df.astype(df_ref.dtype)
        db_ref[...] += jnp.sum(df, axis=0, keepdims=True)

    wide = lambda cb: pl.BlockSpec((tr, FOX_D), lambda i: (nb - 1 - i, cb))
    aug = pl.BlockSpec((tr, FOX_AUG_D), lambda i: (nb - 1 - i, 0))
    row = pl.BlockSpec((tr, LANES), lambda i: (nb - 1 - i, 0))
    one = lambda n: pl.BlockSpec((1, n), lambda i: (0, 0))
    return pl.pallas_call(
        body, name=name, grid=(nb,),
        in_specs=[wide(0), wide(1), row, one(FOX_D), one(FOX_D), one(LANES), aug, aug, row, row],
        out_specs=[wide(0), wide(0), row, one(FOX_D), one(FOX_D), one(LANES)],
        out_shape=[_sds((s, FOX_D), MXU_DT), _sds((s, FOX_D), MXU_DT), _sds((s, LANES), MXU_DT),
                   _sds((1, FOX_D)), _sds((1, FOX_D)), _sds((1, LANES))],
        scratch_shapes=[pltpu.VMEM((1, LANES), F32)],
        compiler_params=_params("arbitrary"),
    )(proj, proj, f_raw, qg, kg, b_f, dqa, dka, row_sums, col_sums)


def _fox_attn_fwd(qa, ka, va, proj, name):
    s = qa.shape[0]
    tq, tk = min(ATT_TQ, s), min(ATT_TK, s)
    assert s % tq == 0 and s % tk == 0
    gt = 3 * FOX_D // LANES
    head_lanes = [slice(hh * LANES, (hh + 1) * LANES) for hh in range(2)]

    def body(qa_ref, ka_ref, va_ref, g_ref, o_ref, og_ref, qb_ref):
        qi = pl.program_id(1)
        lane = _iota2((tq, LANES), 1)
        ahead = _iota2((tq, tk), 0) - _iota2((tq, tk), 1)
        q = [qa_ref[:, hs] for hs in head_lanes]

        def kv_step(j, carry, masked):
            rows = pl.ds(pl.multiple_of(j * tk, tk), tk)
            out = []
            for hh, hs in enumerate(head_lanes):
                m, acc = carry[2 * hh:2 * hh + 2]
                sc = _dot(q[hh], ka_ref[rows, hs], 1, 1)
                if masked:
                    sc = jnp.where(ahead >= j * tk - qi * tq, sc, NEG_BIG)
                m_new = jnp.maximum(m, jnp.max(sc, axis=1, keepdims=True))
                pr = jnp.exp(sc - m_new).astype(MXU_DT)
                out += [m_new, jnp.exp(m - m_new) * acc + _dot(pr, va_ref[rows, hs], 1, 0)]
            return tuple(out)

        n_clear = lax.div(qi * tq, tk)
        n_all = lax.div((qi + 1) * tq + tk - 1, tk)
        init = (jnp.full((tq, 1), NEG_BIG, F32), jnp.zeros((tq, LANES), F32)) * 2
        carry = lax.fori_loop(0, n_clear, functools.partial(kv_step, masked=False), init)
        carry = lax.fori_loop(n_clear, n_all, functools.partial(kv_step, masked=True), carry)
        heads = []
        for hh, hs in enumerate(head_lanes):
            m, acc = carry[2 * hh:2 * hh + 2]
            l = acc[:, QSIDE:QSIDE + 1]
            heads.append(acc / l)
            qf = q[hh].astype(F32)
            bias = qf[:, QSIDE:QSIDE + 1] + qf[:, QSIDE + 1:QSIDE + 2] + qf[:, QSIDE + 2:QSIDE + 3]
            qb_ref[:, hs] = _fill_lanes(qf, lane, QSIDE, _split3(bias - (m + jnp.log(l)))).astype(qb_ref.dtype)
        o = _pair_tile(lane, heads[0], heads[1])
        o_ref[...] = o
        og_ref[...] = (o * _sigmoid(g_ref[...])).astype(og_ref.dtype)

    blk2 = pl.BlockSpec((tq, 2 * LANES), lambda p, i: (i, p))
    seq2 = pl.BlockSpec((s, 2 * LANES), lambda p, i: (0, p))
    blk = pl.BlockSpec((tq, LANES), lambda p, i: (i, p))
    return pl.pallas_call(
        body, name=name, grid=(FOX_PAIRS, s // tq),
        in_specs=[blk2, seq2, seq2, pl.BlockSpec((tq, LANES), lambda p, i: (i, gt + p))],
        out_specs=[blk, blk, blk2],
        out_shape=[_sds((s, FOX_D)), _sds((s, FOX_D), MXU_DT), _sds((s, FOX_AUG_D), MXU_DT)],
        compiler_params=_params("parallel", "parallel"),
    )(qa, ka, va, proj)


def _fox_gate_bwd(dog, o, proj, name):
    def fn(dogv, ov, gate):
        sg = _sigmoid(gate)
        do = dogv * sg
        delta = _head_sums(do * ov)
        lane = _iota2((do.shape[0], LANES), 1)
        tiles = []
        for j in range(FOX_PAIRS):
            cols = slice(j * LANES, (j + 1) * LANES)
            for dt, dl in zip(_head_tiles(do[:, cols]), reversed(_head_tiles(delta[:, cols]))):
                tiles.append(_fill_lanes(jnp.where(lane < FOX_HEAD_DIM, dt, 0.0), lane, QSIDE,
                                         [-d for d in _split3(dl)]))
        return dogv * ov * sg * (1.0 - sg), jnp.concatenate(tiles, axis=1)
    return _rowwise(fn, [dog, o, (proj, 3, FOX_D)], [], [_sds(o.shape, MXU_DT), _sds((o.shape[0], FOX_AUG_D), MXU_DT)],
                    [], name=name)


def _fox_attn_bwd(qb, ka, va, doa, name):
    s = qb.shape[0]
    tq, tk = min(ATT_TQ, s), min(ATT_TK, s)
    nq, nk = s // tq, s // tk
    head_lanes = [slice(hh * LANES, (hh + 1) * LANES) for hh in range(2)]

    def body(qb_ref, doa_ref, ka_ref, va_ref, dqa_ref, dka_ref, dv_ref):
        kj = pl.program_id(1)

        @pl.when(kj == 0)
        def _():
            dqa_ref[...] = jnp.zeros(dqa_ref.shape, F32)

        ahead = _iota2((tq, tk), 0) - _iota2((tq, tk), 1)
        kb = [ka_ref[:, hs] for hs in head_lanes]
        vb = [va_ref[:, hs] for hs in head_lanes]

        def q_step(i, carry, masked):
            rows = pl.ds(pl.multiple_of(i * tq, tq), tq)
            out = []
            for hh, hs in enumerate(head_lanes):
                dk, dv = carry[2 * hh:2 * hh + 2]
                q, do = qb_ref[rows, hs], doa_ref[rows, hs]
                pr = jnp.exp(_dot(q, kb[hh], 1, 1))
                if masked:
                    pr = jnp.where(ahead >= kj * tk - i * tq, pr, 0.0)
                dv = dv + _dot(pr.astype(MXU_DT), do, 0, 0)
                ds = (pr * _dot(do, vb[hh], 1, 1)).astype(MXU_DT)
                dqa_ref[rows, hs] += _dot(ds, kb[hh], 1, 0)
                out += [dk + _dot(ds, q, 0, 0), dv]
            return tuple(out)

        first = lax.div(kj * tk, tq)
        n_masked = lax.div((kj + 1) * tk + tq - 1, tq)
        carry = lax.fori_loop(first, n_masked, functools.partial(q_step, masked=True),
                              (jnp.zeros((tk, LANES), F32),) * 4)
        dk0, dv0, dk1, dv1 = lax.fori_loop(n_masked, nq, functools.partial(q_step, masked=False), carry)
        dka_ref[:, head_lanes[0]] = dk0
        dka_ref[:, head_lanes[1]] = dk1
        dv_ref[...] = _pair_tile(_iota2((tk, LANES), 1), dv0, dv1).astype(dv_ref.dtype)

    seq2 = pl.BlockSpec((s, 2 * LANES), lambda p, j: (0, p))
    blk2 = pl.BlockSpec((tk, 2 * LANES), lambda p, j: (j, p))
    return pl.pallas_call(
        body, name=name, grid=(FOX_PAIRS, nk), in_specs=[seq2, seq2, blk2, blk2],
        out_specs=[seq2, blk2, pl.BlockSpec((tk, LANES), lambda p, j: (j, p))],
        out_shape=[_sds((s, FOX_AUG_D)), _sds((s, FOX_AUG_D)), _sds((s, FOX_D), MXU_DT)],
        compiler_params=_params("parallel", "arbitrary"),
    )(qb, doa, ka, va)


def _fox_fwd(h, p, big):
    hn = _rms_fwd(h, p["norm_g"], "mix_norm_fwd")
    w_in = big.whole("fox_w_in", hn)
    w_qkvg, w_f = w_in[:, :4 * FOX_D], _pad_cols(w_in[:, 4 * FOX_D:])
    proj = _matmul(hn, w_qkvg, tm=1024, name="fox_in_qkvg")
    f_raw = _matmul(hn, w_f, name="fox_in_f")
    qg = jnp.tile(p["q_norm_g"], FOX_HEADS).reshape(1, -1)
    kg = jnp.tile(p["k_norm_g"], FOX_HEADS).reshape(1, -1)
    qa, ka, va = _fox_prep_fwd(proj, f_raw, qg, kg, _lane_pad(p["b_f"]), "fox_prep_fwd")
    o, og, qb = _fox_attn_fwd(qa, ka, va, proj, "fox_attn_fwd")
    w_out = big.whole("fox_w_out", og)
    h_out = _matmul(og, w_out, res=h, name="fox_out")
    return h_out, (hn, proj, f_raw, qg, kg, ka, va, qb, o, og, w_qkvg, w_f, w_out)


def _fox_bwd(h, dh, saved, p, out):
    hn, proj, f_raw, qg, kg, ka, va, qb, o, og, w_qkvg, w_f, w_out = saved
    s = h.shape[0]
    dh, dh_b = dh
    dog = _matmul(dh_b, w_out, tb=True, name="fox_out_dx")
    out.send("fox_w_out", _matmul(og, dh_b, ta=True, out_dtype=MXU_DT, name="fox_out_dw"))
    dgate, doa = _fox_gate_bwd(dog, o, proj, "fox_gate_bwd")
    dqa, dka, dv = _fox_attn_bwd(qb, ka, va, doa, "fox_attn_bwd")
    head_cols = lambda a, lane: jnp.pad(a.reshape(s, FOX_HEADS, LANES)[:, :, lane], ((0, 0), (0, LANES - FOX_HEADS)))
    dq, dk, df, dqg, dkg, dbf = _fox_prep_bwd(proj, f_raw, qg, kg, out.tie(_lane_pad(p["b_f"])), dqa, dka,
                                              head_cols(dqa, QSIDE), head_cols(dka, KSIDE), "fox_prep_bwd")
    dproj = jnp.concatenate([dq, dk, dv, dgate], axis=1)
    dhn = _matmul(dproj, w_qkvg, tb=True, name="fox_in_dx_qkvg")
    dhn = _matmul(df, w_f, tb=True, res=dhn, name="fox_in_dx_f")
    out.send("fox_w_in", jnp.concatenate(
        [_matmul(hn, dproj, ta=True, out_dtype=MXU_DT, name="fox_in_dw_qkvg"),
         _matmul(hn, df, ta=True, out_dtype=MXU_DT, name="fox_in_dw_f")[:, :FOX_HEADS]], axis=1))
    *dh_in, dnorm_g = _rms_bwd(h, out.tie(p["norm_g"]), dhn, dh, "mix_norm_bwd")
    fold = lambda g: g.reshape(FOX_HEADS, FOX_HEAD_DIM).sum(axis=0).reshape(1, -1)
    grads = dict(norm_g=dnorm_g, b_f=dbf[:, :FOX_HEADS], q_norm_g=fold(dqg), k_norm_g=fold(dkg))
    return dh_in, grads


def _my_index():
    return 4 * lax.axis_index("x") + 2 * lax.axis_index("y") + lax.axis_index("c")


def _peer(k):
    x, y, c = lax.axis_index("x"), lax.axis_index("y"), lax.axis_index("c")
    flip = lambda v, bit: 1 - v if bit else v
    return (flip(x, k & 4), flip(y, k & 2), flip(c, k & 1))


SEM_SPEC = pl.BlockSpec(memory_space=pltpu.SEMAPHORE)
DATAFLOW = pltpu.SideEffectType.DATAFLOW_SIDE_EFFECTING


def _at(ref, idx):
    return ref.at[tuple(idx)] if idx else ref


def _copies_start(name, srcs, lands, groups):
    ns, nl = len(srcs), len(lands)
    items = [item for group in groups for item in group]

    def body(*refs):
        src_refs, land_refs = refs[:ns], refs[ns:ns + nl]
        sems = refs[ns + nl:ns + nl + 2 * len(items)]
        me = _my_index()
        for t, (i, src_slot, j, dst_slot, _) in enumerate(items):
            for k in range(1, N_DEV):
                pltpu.make_async_remote_copy(
                    src_ref=_at(src_refs[i], src_slot(me, k)), dst_ref=_at(land_refs[j], dst_slot(me, k)),
                    send_sem=sems[2 * t], recv_sem=sems[2 * t + 1], device_id=_peer(k),
                    device_id_type=MESH_IDS).start()
        refs[-1][...] = jnp.zeros(refs[-1].shape, F32)

    hbm = pl.BlockSpec(memory_space=pltpu.HBM)
    bufs = [pltpu.with_memory_space_constraint(a, pltpu.HBM) for a in list(srcs) + list(lands)]
    n_sems = 2 * len(items)
    out = pl.pallas_call(
        body, name=name, in_specs=[hbm] * (ns + nl),
        out_specs=(*[SEM_SPEC] * n_sems, *[hbm] * (ns + nl), pl.BlockSpec(memory_space=pltpu.VMEM)),
        out_shape=(*[pltpu.SemaphoreType.DMA(())] * n_sems, *[pltpu.HBM(a.shape, a.dtype) for a in bufs],
                   _sds((8, LANES))),
        input_output_aliases={i: n_sems + i for i in range(ns + nl)},
        compiler_params=pltpu.CompilerParams(has_side_effects=DATAFLOW),
    )(*bufs)
    sems, t = [], 0
    for group in groups:
        sems.append([(out[2 * (t + u)], out[2 * (t + u) + 1]) for u in range(len(group))])
        t += len(group)
    return sems, list(out[n_sems:n_sems + ns]), list(out[n_sems + ns:n_sems + ns + nl]), out[-1]


def _copies_wait(name, keep, lands, sems, group, after):
    nk, nl, n = len(keep), len(lands), len(group)

    def body(*refs):
        land_refs = refs[nk:nk + nl]
        sem_refs = refs[nk + nl:nk + nl + 2 * n]
        me = _my_index()
        copies = []
        for t, (_, _, j, _, seven) in enumerate(group):
            blocks = _at(land_refs[j], seven(me))
            copies.append(pltpu.make_async_remote_copy(src_ref=blocks, dst_ref=blocks, send_sem=sem_refs[2 * t],
                                                       recv_sem=sem_refs[2 * t + 1], device_id=_peer(1),
                                                       device_id_type=MESH_IDS))
        for cp in copies:
            cp.wait_recv()
        for cp in copies:
            cp.wait_send()

    hbm = pl.BlockSpec(memory_space=pltpu.HBM)
    bufs = list(keep) + list(lands)
    out = pl.pallas_call(
        body, name=name, in_specs=[hbm] * (nk + nl) + [SEM_SPEC] * (2 * n) + [pl.BlockSpec(memory_space=pl.ANY)],
        out_specs=[hbm] * (nk + nl), out_shape=[pltpu.HBM(a.shape, a.dtype) for a in bufs],
        input_output_aliases={i: i for i in range(nk + nl)},
        compiler_params=pltpu.CompilerParams(has_side_effects=DATAFLOW),
    )(*bufs, *[s for pair in sems for s in pair], after)
    return list(out[nk:])


def _allreduce_small(buf, name):
    def body(in_ref, all_ref, sum_ref, send_sems, recv_sems):
        me = _my_index()
        all_ref[me] = in_ref[...]

        def copy(k, slot):
            return pltpu.make_async_remote_copy(
                src_ref=in_ref, dst_ref=all_ref.at[slot], send_sem=send_sems.at[k - 1], recv_sem=recv_sems.at[k - 1],
                device_id=_peer(k), device_id_type=MESH_IDS)

        for k in range(1, N_DEV):
            copy(k, me).start()
        for k in range(1, N_DEV):
            copy(k, jnp.bitwise_xor(me, k)).wait_recv()
        for k in range(1, N_DEV):
            copy(k, me).wait_send()
        acc = all_ref[0]
        for j in range(1, N_DEV):
            acc = acc + all_ref[j]
        sum_ref[...] = acc

    vmem = pl.BlockSpec(memory_space=pltpu.VMEM)
    return pl.pallas_call(
        body, name=name, in_specs=[vmem], out_specs=[vmem, vmem],
        out_shape=[_sds((N_DEV,) + buf.shape), _sds(buf.shape)],
        scratch_shapes=[pltpu.SemaphoreType.DMA((N_DEV - 1,)), pltpu.SemaphoreType.DMA((N_DEV - 1,))],
        compiler_params=pltpu.CompilerParams(vmem_limit_bytes=VMEM_LIMIT_BYTES),
    )(buf)[1]


def _adamw_math(w, g, m, v):
    m = ADAM_B1 * m + (1.0 - ADAM_B1) * g
    v = ADAM_B2 * v + (1.0 - ADAM_B2) * (g * g)
    m_hat = m / (1.0 - ADAM_B1 ** ADAM_STEP)
    v_hat = v / (1.0 - ADAM_B2 ** ADAM_STEP)
    return -ADAM_LR * (m_hat / (jnp.sqrt(v_hat) + ADAM_EPS) + ADAM_WD * w), m, v


def _row_tile(rows, cap=256, mult=16):
    best = None
    for t in range(mult, min(rows, cap) + 1, mult):
        if rows % t == 0:
            best = t
    assert best is not None, rows
    return best


def _adamw_sharded(w, m, v, partials, name):
    layers, rows, cols = w.shape
    tr = _row_tile(rows)

    def body(w_ref, m_ref, v_ref, p_ref, g_ref, d_ref, nm_ref, nv_ref):
        g = p_ref[0].astype(F32)
        for j in range(1, N_DEV):
            g = g + p_ref[j].astype(F32)
        delta, m_new, v_new = _adamw_math(w_ref[...], g, m_ref[...], v_ref[...])
        g_ref[...], d_ref[...], nm_ref[...], nv_ref[...] = g, delta, m_new, v_new

    blk = pl.BlockSpec((None, tr, cols), lambda l, i: (l, i, 0))
    return pl.pallas_call(
        body, name=name, grid=(layers, rows // tr),
        in_specs=[blk, blk, blk, pl.BlockSpec((N_DEV, None, tr, cols), lambda l, i: (0, l, i, 0))],
        out_specs=[blk] * 4, out_shape=[_sds(w.shape)] * 4, compiler_params=_params("parallel", "parallel"),
    )(w, m, v, partials)


def _adamw_small(w, g, m, v, name):
    def body(w_ref, g_ref, m_ref, v_ref, d_ref, nm_ref, nv_ref):
        d_ref[...], nm_ref[...], nv_ref[...] = _adamw_math(w_ref[...], g_ref[...], m_ref[...], v_ref[...])

    return pl.pallas_call(body, name=name, out_shape=[_sds(w.shape)] * 3,
                          compiler_params=_params())(w, g, m, v)


def _pack(arrays):
    flat = jnp.concatenate([a.reshape(-1).astype(F32) for a in arrays])
    pad = -flat.shape[0] % (8 * LANES)
    return jnp.pad(flat, (0, pad)).reshape(-1, LANES)


def _unpack(buf, shapes):
    flat, out, off = buf.reshape(-1), [], 0
    for shp in shapes:
        size = math.prod(shp)
        out.append(flat[off:off + size].reshape(shp))
        off += size
    return out


WEIGHTS = ["mix_norm_g", "ffn_norm_g", "ssd_w_in", "ssd_conv_w", "ssd_conv_b", "ssd_dt_bias", "ssd_a_log", "ssd_d",
           "ssd_norm_g", "ssd_w_out", "fox_w_in", "fox_b_f", "fox_q_norm_g", "fox_k_norm_g", "fox_w_out", "ffn_w_up",
           "ffn_conv_w", "ffn_conv_b", "ffn_w_down", "final_norm_g"]
BIG = ["ssd_w_in", "ssd_w_out", "fox_w_in", "fox_w_out", "ffn_w_up", "ffn_w_down"]
COLUMN_SHARDED = ["ssd_w_in", "fox_w_in", "ffn_w_up"]
CONV = ["ssd_conv_w", "ffn_conv_w"]
REPLICATED = [n for n in WEIGHTS if n not in BIG + CONV]
DEPTH = 4
ADAMW_ORDER = ["fox_w_out", "fox_w_in", "ffn_w_down", "ffn_w_up", "ssd_w_out", "ssd_w_in"]
LAYER_SHARDED = (["ssd_w_in", "ssd_conv_w", "ssd_w_out", "ffn_w_up", "ffn_conv_w", "ffn_w_down"],
                 ["fox_w_in", "fox_w_out", "ffn_w_up", "ffn_conv_w", "ffn_w_down"])


def _to_shards(full, on_columns):
    nl, r, c = full.shape
    if on_columns:
        return full.reshape(nl, r, N_DEV, c // N_DEV).transpose(2, 0, 1, 3)
    return full.reshape(nl, N_DEV, r // N_DEV, c).transpose(1, 0, 2, 3)


def _pad_cols(w):
    return jnp.pad(w, ((0, 0), (0, LANES - w.shape[1])))


def kernel(x, mix_norm_g, ffn_norm_g, ssd_w_in, ssd_conv_w, ssd_conv_b, ssd_dt_bias, ssd_a_log, ssd_d, ssd_norm_g, ssd_w_out, fox_w_in, fox_b_f, fox_q_norm_g, fox_k_norm_g, fox_w_out, ffn_w_up, ffn_conv_w, ffn_conv_b, ffn_w_down, final_norm_g, loss_target, m_mix_norm_g, m_ffn_norm_g, m_ssd_w_in, m_ssd_conv_w, m_ssd_conv_b, m_ssd_dt_bias, m_ssd_a_log, m_ssd_d, m_ssd_norm_g, m_ssd_w_out, m_fox_w_in, m_fox_b_f, m_fox_q_norm_g, m_fox_k_norm_g, m_fox_w_out, m_ffn_w_up, m_ffn_conv_w, m_ffn_conv_b, m_ffn_w_down, m_final_norm_g, v_mix_norm_g, v_ffn_norm_g, v_ssd_w_in, v_ssd_conv_w, v_ssd_conv_b, v_ssd_dt_bias, v_ssd_a_log, v_ssd_d, v_ssd_norm_g, v_ssd_w_out, v_fox_w_in, v_fox_b_f, v_fox_q_norm_g, v_fox_k_norm_g, v_fox_w_out, v_ffn_w_up, v_ffn_conv_w, v_ffn_conv_b, v_ffn_w_down, v_final_norm_g):
    given = dict(locals())
    w = {n: given[n] for n in WEIGHTS}
    mom = {n: given["m_" + n] for n in WEIGHTS}
    var = {n: given["v_" + n] for n in WEIGHTS}
    me = _my_index()

    sharded = BIG + CONV
    shards = [w[n].astype(MXU_DT) if n in BIG else w[n] for n in sharded]
    zones = [(n, i if n.startswith("ffn") else i // 2) for i in range(DEPTH) for n in LAYER_SHARDED[i % 2]]
    items = [[(sharded.index(n), functools.partial(lambda me, k, layer: (layer,), layer=layer), z,
               lambda me, k: (me,), lambda me: (pl.ds(0, N_DEV - 1),))] for z, (n, layer) in enumerate(zones)]
    empty = [lax.empty((N_DEV,) + shards[sharded.index(n)].shape[1:], shards[sharded.index(n)].dtype) for n, _ in zones]
    gather_sems, shards_kept, landing, _ = _copies_start("gather_start", shards, empty, items)

    def arrived(n, layer, after):
        z = zones.index((n, layer))
        item = items[z][0]
        stack, = _copies_wait(f"gather_wait_{n}_{layer}", shards_kept if z == len(zones) - 1 else [], [landing[z]],
                              gather_sems[z], [item[:2] + (0,) + item[3:]], after)
        stack = lax.dynamic_update_index_in_dim(stack, shards[sharded.index(n)][layer], me, 0)
        _, r, c = stack.shape
        on_columns = n in COLUMN_SHARDED or n in CONV
        return stack.transpose(1, 0, 2).reshape(r, N_DEV * c) if on_columns else stack.reshape(N_DEV * r, c)

    def mixer_params(i):
        j = i // 2
        if i % 2 == 0:
            return dict(norm_g=w["mix_norm_g"][i], conv_b=w["ssd_conv_b"][j], dt_bias=w["ssd_dt_bias"][j],
                        a_log=w["ssd_a_log"][j], d=w["ssd_d"][j], gnorm_g=w["ssd_norm_g"][j])
        return dict(norm_g=w["mix_norm_g"][i], b_f=w["fox_b_f"][j], q_norm_g=w["fox_q_norm_g"][j],
                    k_norm_g=w["fox_k_norm_g"][j])

    h = x[0]
    tape = []
    for i in range(DEPTH):
        big = _Weights(functools.partial(lambda n, after, i: arrived(n, i if n.startswith("ffn") else i // 2, after), i=i))
        mp, fp = mixer_params(i), dict(norm_g=w["ffn_norm_g"][i], conv_b=w["ffn_conv_b"][i])
        h_mid, mix_saved = (_ssd_fwd if i % 2 == 0 else _fox_fwd)(h, mp, big)
        h_out, ffn_saved = _ffn_fwd(h_mid, fp, big)
        tape.append((h, mp, mix_saved, h_mid, fp, ffn_saved))
        h = h_out
    *dh, dfinal_g, loss_part = _loss_head(h, w["final_norm_g"], loss_target[0], "loss_head")

    grads = {n: [None] * w[n].shape[0] for n in WEIGHTS if n not in BIG + ["final_norm_g"]}
    partials = {n: lax.empty((N_DEV,) + w[n].shape, MXU_DT) for n in BIG}
    in_flight = {n: [] for n in BIG}

    def send_partial(n, layer, grad):
        slots = _to_shards(grad[None], n in COLUMN_SHARDED)[:, 0]
        mine = lax.dynamic_index_in_dim(slots, me, 0, keepdims=False)
        zone = lax.dynamic_update_slice(partials[n], mine[None, None], (me, layer, 0, 0))
        item = (0, lambda me, k: (jnp.bitwise_xor(me, k),), 0,
                functools.partial(lambda me, k, layer: (me, layer), layer=layer),
                functools.partial(lambda me, layer: (pl.ds(0, N_DEV - 1), layer), layer=layer))
        sems, kept, (partials[n],), token = _copies_start(f"scatter_start_{n}_{layer}", [slots], [zone], [[item]])
        in_flight[n].append((layer, sems[0], kept, item))
        return token

    for i in reversed(range(DEPTH)):
        j = i // 2
        h_in, mp, mix_saved, h_mid, fp, ffn_saved = tape[i]
        out = _Gradients(functools.partial(lambda n, grad, i: send_partial(n, i if n.startswith("ffn") else i // 2, grad),
                                           i=i))
        dh, g = _ffn_bwd(h_mid, dh, ffn_saved, fp, out)
        grads["ffn_norm_g"][i], grads["ffn_conv_w"][i], grads["ffn_conv_b"][i] = g["norm_g"][0], g["conv_w"], g["conv_b"][0]
        if i % 2 == 0:
            dh, g = _ssd_bwd(h_in, dh, mix_saved, mp, out)
            grads["ssd_conv_w"][j] = g["conv_w"]
            for key, name in (("conv_b", "ssd_conv_b"), ("dt_bias", "ssd_dt_bias"), ("a_log", "ssd_a_log"),
                              ("d", "ssd_d"), ("gnorm_g", "ssd_norm_g")):
                grads[name][j] = g[key][0]
        else:
            dh, g = _fox_bwd(h_in, dh, mix_saved, mp, out)
            for key, name in (("b_f", "fox_b_f"), ("q_norm_g", "fox_q_norm_g"), ("k_norm_g", "fox_k_norm_g")):
                grads[name][j] = g[key][0]
        grads["mix_norm_g"][i] = g["norm_g"][0]
    grads = {n: jnp.stack(v) for n, v in grads.items()}
    grads["final_norm_g"] = dfinal_g[0]

    small_names = REPLICATED + CONV
    summed = _unpack(_allreduce_small(_pack([grads[n] for n in small_names] + [loss_part]), "allreduce_small"),
                     [grads[n].shape for n in small_names] + [(1, 1)])
    loss = summed[-1][0, 0]
    g_small = dict(zip(small_names, summed[:-1]))
    for n in CONV:
        width = w[n].shape[-1]
        g_small[n] = lax.dynamic_slice_in_dim(g_small[n], me * width, width, axis=2)
    pk = lambda d: _pack([d[n] for n in small_names])
    d_small, m_small, v_small = _adamw_small(pk(w), pk(g_small), pk(mom), pk(var), "adamw_small")
    shapes = [w[n].shape for n in small_names]
    out_g, out_d, out_m, out_v = dict(g_small), {}, {}, {}
    for dst, buf in ((out_d, d_small), (out_m, m_small), (out_v, v_small)):
        dst.update(zip(small_names, _unpack(buf, shapes)))

    after = d_small
    for n in ADAMW_ORDER:
        for layer, sems, kept, item in in_flight[n]:
            partials[n], = _copies_wait(f"scatter_wait_{n}_{layer}", kept, [partials[n]], sems, [item], after)
        out_g[n], out_d[n], out_m[n], out_v[n] = _adamw_sharded(w[n], mom[n], var[n], partials[n], "adamw_" + n)
        after = out_v[n]

    return (loss, dh[0][None], *[out_g[n] for n in WEIGHTS], *[out_d[n] for n in WEIGHTS],
            *[out_m[n] for n in WEIGHTS], *[out_v[n] for n in WEIGHTS])
```

```python
import functools
import math

import jax
import jax.numpy as jnp
from jax import lax
from jax.experimental import pallas as pl
from jax.experimental.pallas import tpu as pltpu

F32 = jnp.float32
MXU_DT = jnp.bfloat16
VMEM_LIMIT_BYTES = 56 * 1024 * 1024
LANES = 128
N_DEV = 8
MESH_IDS = pl.DeviceIdType.MESH

EPS = 1e-6
D_MODEL = 1024
SSD_D_INNER = 2048
SSD_HEAD_DIM = 64
SSD_HEADS = 32
SSD_GROUPS = 4
SSD_HPG = 8
SSD_STATE = 128
SSD_CONV = 4
SSD_CHUNK = 128
SSD_CONV_DIM = 3072
SSD_ZX = SSD_D_INNER + SSD_CONV_DIM
FOX_HEAD_DIM = 64
FOX_HEADS = 16
FOX_D = 1024
D_FF = 2816
FFN_CONV = 3
ADAM_LR, ADAM_B1, ADAM_B2, ADAM_EPS, ADAM_WD, ADAM_STEP = 0.001, 0.9, 0.999, 1e-08, 0.01, 10


def _params(*sem):
    return pltpu.CompilerParams(dimension_semantics=sem or None, vmem_limit_bytes=VMEM_LIMIT_BYTES)


def _sds(shape, dtype=F32):
    return jax.ShapeDtypeStruct(tuple(shape), dtype)


def _col_tile(n, cap=1536):
    best = None
    for t in range(LANES, min(n, cap) + 1, LANES):
        if n % t == 0:
            best = t
    assert best is not None, n
    return best


def _sigmoid(x):
    return 0.5 * jnp.tanh(0.5 * x) + 0.5


def _matmul(a, b, *, ta=False, tb=False, res=None, out_dtype=F32, tm=512, tn=None, b_kblock=0, name):
    (kdim, m) = a.shape if ta else a.shape[::-1]
    (n, k2) = b.shape if tb else b.shape[::-1]
    assert kdim == k2 or (tb and k2 % kdim == 0), (a.shape, b.shape, ta, tb)
    tm = min(tm, m)
    if m % tm:
        tm = _col_tile(m, tm)
    tn = tn or _col_tile(n)
    assert m % tm == 0 and n % tn == 0, (m, tm, n, tn)
    dims = (((0 if ta else 1,), (1 if tb else 0,)), ((), ()))

    def body(*refs):
        a_ref, b_ref = refs[0], refs[1]
        o_ref = refs[-1]
        acc = lax.dot_general(a_ref[...].astype(MXU_DT), b_ref[...].astype(MXU_DT), dims,
                              preferred_element_type=F32)
        if res is not None:
            acc = acc + refs[2][...].astype(F32)
        o_ref[...] = acc.astype(o_ref.dtype)

    a_spec = pl.BlockSpec((kdim, tm), lambda i, j: (0, i)) if ta else pl.BlockSpec((tm, kdim), lambda i, j: (i, 0))
    b_spec = pl.BlockSpec((tn, kdim), lambda i, j: (j, b_kblock)) if tb else pl.BlockSpec((kdim, tn), lambda i, j: (0, j))
    o_spec = pl.BlockSpec((tm, tn), lambda i, j: (i, j))
    ins, specs = [a, b], [a_spec, b_spec]
    if res is not None:
        ins.append(res)
        specs.append(o_spec)
    return pl.pallas_call(
        body, name=name, grid=(m // tm, n // tn), in_specs=specs, out_specs=o_spec,
        out_shape=_sds((m, n), out_dtype), compiler_params=_params("parallel", "parallel"),
    )(*ins)


def _rowwise(fn, rows, consts, out_rows, out_sums, *, tr=256, name):
    rows = [r if isinstance(r, tuple) else (r, 0, r.shape[1]) for r in rows]
    s = rows[0][0].shape[0]
    tr = min(tr, s)
    assert s % tr == 0
    n_in, n_c, n_or = len(rows), len(consts), len(out_rows)

    def body(*refs):
        ins = [r[...] for r in refs[:n_in + n_c]]
        outs = fn(*ins)
        o_refs = refs[n_in + n_c:]
        for o_ref, val in zip(o_refs[:n_or], outs[:n_or]):
            o_ref[...] = val.astype(o_ref.dtype)
        if out_sums:
            first = pl.program_id(0) == 0

            @pl.when(first)
            def _():
                for o_ref, val in zip(o_refs[n_or:], outs[n_or:]):
                    o_ref[...] = val.astype(o_ref.dtype)

            @pl.when(jnp.logical_not(first))
            def _():
                for o_ref, val in zip(o_refs[n_or:], outs[n_or:]):
                    o_ref[...] += val.astype(o_ref.dtype)

    in_specs = [pl.BlockSpec((tr, width), functools.partial(lambda i, cb: (i, cb), cb=cb)) for _, cb, width in rows]
    in_specs += [pl.BlockSpec(c.shape, lambda i: (0, 0)) for c in consts]
    out_specs = [pl.BlockSpec((tr, o.shape[1]), lambda i: (i, 0)) for o in out_rows]
    out_specs += [pl.BlockSpec(o.shape, lambda i: (0, 0)) for o in out_sums]
    return pl.pallas_call(
        body, name=name, grid=(s // tr,), in_specs=in_specs, out_specs=out_specs,
        out_shape=list(out_rows) + list(out_sums),
        compiler_params=_params("arbitrary" if out_sums else "parallel"),
    )(*[r[0] for r in rows], *consts)


def _rms_fwd(h, g, name):
    def fn(x, gv):
        r = lax.rsqrt(jnp.mean(x * x, axis=-1, keepdims=True) + EPS)
        return (x * r * gv,)
    return _rowwise(fn, [h], [g.reshape(1, -1)], [_sds(h.shape, MXU_DT)], [], name=name)[0]


def _rms_bwd(h, g, dy, dres, name):
    def fn(x, dyv, dr, gv):
        r = lax.rsqrt(jnp.mean(x * x, axis=-1, keepdims=True) + EPS)
        dyg = dyv * gv
        dx = dr + r * dyg - x * (r * r * r) * jnp.mean(x * dyg, axis=-1, keepdims=True)
        return dx, dx, jnp.sum(dyv * x * r, axis=0, keepdims=True)
    return _rowwise(fn, [h, dy, dres], [g.reshape(1, -1)], [_sds(h.shape), _sds(h.shape, MXU_DT)],
                    [_sds((1, h.shape[1]))], name=name)


def _loss_head(h, g, target, name):
    c = h.shape[1]

    def fn(x, t, gv):
        r = lax.rsqrt(jnp.mean(x * x, axis=-1, keepdims=True) + EPS)
        y = x * r * gv
        err = y - t
        dyv = err * (1.0 / c)
        dyg = dyv * gv
        dx = r * dyg - x * (r * r * r) * jnp.mean(x * dyg, axis=-1, keepdims=True)
        loss = 0.5 * jnp.sum(jnp.mean(err * err, axis=-1, keepdims=True), axis=0, keepdims=True)
        return dx, dx, jnp.sum(dyv * x * r, axis=0, keepdims=True), loss
    return _rowwise(fn, [h, target], [g.reshape(1, -1)], [_sds(h.shape), _sds(h.shape, MXU_DT)],
                    [_sds((1, c)), _sds((1, 1))], name=name)


PAD_ROWS = 8
ROW_TILE = 128


def _shifted_conv(xp_ref, w, r0, tr, kw):
    acc = None
    for k in range(kw):
        xk = xp_ref[pl.ds(PAD_ROWS + r0 - (kw - 1) + k, tr), :]
        term = xk * w[k:k + 1, :]
        acc = term if acc is None else acc + term
    return acc


def _convglu_fwd(u, conv_w, conv_b, name):
    s = u.shape[0]
    nt = D_FF // LANES
    tr = min(ROW_TILE, s)

    def body(ug_ref, uv_ref, w_ref, b_ref, act_ref, xp_ref):
        xp_ref[pl.ds(0, PAD_ROWS), :] = jnp.zeros((PAD_ROWS, LANES), F32)
        xp_ref[pl.ds(PAD_ROWS, s), :] = ug_ref[...]
        w = w_ref[...]
        b = b_ref[...]
        for r0 in range(0, s, tr):
            gate = _shifted_conv(xp_ref, w, r0, tr, FFN_CONV) + b
            act = gate * _sigmoid(gate) * uv_ref[pl.ds(r0, tr), :]
            act_ref[pl.ds(r0, tr), :] = act.astype(act_ref.dtype)

    return pl.pallas_call(
        body, name=name, grid=(nt,),
        in_specs=[pl.BlockSpec((s, LANES), lambda j: (0, j)), pl.BlockSpec((s, LANES), lambda j: (0, nt + j)),
                  pl.BlockSpec((FFN_CONV, LANES), lambda j: (0, j)), pl.BlockSpec((1, LANES), lambda j: (0, j))],
        out_specs=pl.BlockSpec((s, LANES), lambda j: (0, j)),
        out_shape=_sds((s, D_FF), MXU_DT),
        scratch_shapes=[pltpu.VMEM((s + PAD_ROWS, LANES), F32)],
        compiler_params=_params("parallel"),
    )(u, u, conv_w, conv_b.reshape(1, -1))


def _convglu_bwd(u, dact, conv_w, conv_b, name):
    s = u.shape[0]
    nt = D_FF // LANES
    tr = min(ROW_TILE, s)
    kw = FFN_CONV

    def body(ug_ref, uv_ref, da_ref, w_ref, b_ref, dug_ref, duv_ref, dw_ref, db_ref, xp_ref, dgp_ref):
        xp_ref[pl.ds(0, PAD_ROWS), :] = jnp.zeros((PAD_ROWS, LANES), F32)
        xp_ref[pl.ds(PAD_ROWS, s), :] = ug_ref[...]
        dgp_ref[pl.ds(s, PAD_ROWS), :] = jnp.zeros((PAD_ROWS, LANES), F32)
        w = w_ref[...]
        b = b_ref[...]
        dw = [jnp.zeros((1, LANES), F32) for _ in range(kw)]
        db = jnp.zeros((1, LANES), F32)
        for r0 in range(0, s, tr):
            gate = _shifted_conv(xp_ref, w, r0, tr, kw) + b
            sg = _sigmoid(gate)
            da = da_ref[pl.ds(r0, tr), :].astype(F32)
            duv_ref[pl.ds(r0, tr), :] = (da * gate * sg).astype(duv_ref.dtype)
            dgate = da * uv_ref[pl.ds(r0, tr), :] * (sg * (1.0 + gate * (1.0 - sg)))
            dgp_ref[pl.ds(r0, tr), :] = dgate
            db = db + jnp.sum(dgate, axis=0, keepdims=True)
            for k in range(kw):
                xk = xp_ref[pl.ds(PAD_ROWS + r0 - (kw - 1) + k, tr), :]
                dw[k] = dw[k] + jnp.sum(dgate * xk, axis=0, keepdims=True)
        for r0 in range(0, s, tr):
            acc = None
            for k in range(kw):
                term = dgp_ref[pl.ds(r0 + (kw - 1) - k, tr), :] * w[k:k + 1, :]
                acc = term if acc is None else acc + term
            dug_ref[pl.ds(r0, tr), :] = acc.astype(dug_ref.dtype)
        for k in range(kw):
            dw_ref[pl.ds(k, 1), :] = dw[k]
        db_ref[...] = db

    col = lambda j: (0, j)
    return pl.pallas_call(
        body, name=name, grid=(nt,),
        in_specs=[pl.BlockSpec((s, LANES), col), pl.BlockSpec((s, LANES), lambda j: (0, nt + j)),
                  pl.BlockSpec((s, LANES), col), pl.BlockSpec((kw, LANES), col), pl.BlockSpec((1, LANES), col)],
        out_specs=[pl.BlockSpec((s, LANES), col), pl.BlockSpec((s, LANES), col),
                   pl.BlockSpec((kw, LANES), col), pl.BlockSpec((1, LANES), col)],
        out_shape=[_sds((s, D_FF), MXU_DT), _sds((s, D_FF), MXU_DT), _sds((kw, D_FF)), _sds((1, D_FF))],
        scratch_shapes=[pltpu.VMEM((s + PAD_ROWS, LANES), F32), pltpu.VMEM((s + PAD_ROWS, LANES), F32)],
        compiler_params=_params("parallel"),
    )(u, u, dact, conv_w, conv_b.reshape(1, -1))


class _Weights:
    def __init__(self, fetch):
        self._fetch, self._got = fetch, {}

    def whole(self, name, after):
        if name not in self._got:
            self._got[name] = self._fetch(name, after)
        return self._got[name]


class _Gradients:
    def __init__(self, start):
        self._start, self._tokens = start, []

    def send(self, name, grad):
        token = self._start(name, grad)
        if token is not None:
            self._tokens.append(token)

    def tie(self, x):
        for token in self._tokens:
            x = x + token[0, 0]
        self._tokens = []
        return x


def _ffn_fwd(h, p, big):
    hf = _rms_fwd(h, p["norm_g"], "ffn_norm_fwd")
    w_up = big.whole("ffn_w_up", hf)
    u = _matmul(hf, w_up, tm=1024, name="ffn_up")
    conv_w = big.whole("ffn_conv_w", u)
    act = _convglu_fwd(u, conv_w, p["conv_b"], "convglu_fwd")
    w_down = big.whole("ffn_w_down", act)
    h_out = _matmul(act, w_down, res=h, name="ffn_down")
    return h_out, (hf, u, act, w_up, conv_w, w_down)


def _ffn_bwd(h, dh, saved, p, out):
    hf, u, act, w_up, conv_w, w_down = saved
    dh, dh_b = dh
    dact = _matmul(dh_b, w_down, tb=True, name="ffn_down_dx")
    out.send("ffn_w_down", _matmul(act, dh_b, ta=True, out_dtype=MXU_DT, name="ffn_down_dw"))
    dug, duv, dconv_w, dconv_b = _convglu_bwd(u, dact, conv_w, out.tie(p["conv_b"]), "convglu_bwd")
    dhf = _matmul(dug, w_up, tb=True, b_kblock=0, name="ffn_up_dx_gate")
    dhf = _matmul(duv, w_up, tb=True, b_kblock=1, res=dhf, name="ffn_up_dx_val")
    out.send("ffn_w_up", jnp.concatenate([_matmul(hf, dug, ta=True, out_dtype=MXU_DT, name="ffn_up_dw_gate"),
                                          _matmul(hf, duv, ta=True, out_dtype=MXU_DT, name="ffn_up_dw_val")], axis=1))
    *dh_in, dnorm_g = _rms_bwd(h, out.tie(p["norm_g"]), dhf, dh, "ffn_norm_bwd")
    return dh_in, dict(norm_g=dnorm_g, conv_w=dconv_w, conv_b=dconv_b)


def _dwconv_silu_fwd(proj, col0, n_ch, conv_w, conv_b, name):
    s = proj.shape[0]
    nt, t0, kw = n_ch // LANES, col0 // LANES, conv_w.shape[0]
    tr = min(ROW_TILE, s)

    def body(x_ref, w_ref, b_ref, o_ref, xp_ref):
        xp_ref[pl.ds(0, PAD_ROWS), :] = jnp.zeros((PAD_ROWS, LANES), F32)
        xp_ref[pl.ds(PAD_ROWS, s), :] = x_ref[...]
        w = w_ref[...]
        b = b_ref[...]
        for r0 in range(0, s, tr):
            pre = _shifted_conv(xp_ref, w, r0, tr, kw) + b
            o_ref[pl.ds(r0, tr), :] = pre * _sigmoid(pre)

    col = lambda j: (0, j)
    return pl.pallas_call(
        body, name=name, grid=(nt,),
        in_specs=[pl.BlockSpec((s, LANES), lambda j: (0, t0 + j)), pl.BlockSpec((kw, LANES), col),
                  pl.BlockSpec((1, LANES), col)],
        out_specs=pl.BlockSpec((s, LANES), col), out_shape=_sds((s, n_ch)),
        scratch_shapes=[pltpu.VMEM((s + PAD_ROWS, LANES), F32)],
        compiler_params=_params("parallel"),
    )(proj, conv_w, conv_b.reshape(1, -1))


def _dwconv_silu_bwd(proj, col0, n_ch, dout, conv_w, conv_b, name):
    s = proj.shape[0]
    nt, t0, kw = n_ch // LANES, col0 // LANES, conv_w.shape[0]
    tr = min(ROW_TILE, s)

    def body(x_ref, do_ref, w_ref, b_ref, dx_ref, dw_ref, db_ref, xp_ref, dgp_ref):
        xp_ref[pl.ds(0, PAD_ROWS), :] = jnp.zeros((PAD_ROWS, LANES), F32)
        xp_ref[pl.ds(PAD_ROWS, s), :] = x_ref[...]
        dgp_ref[pl.ds(s, PAD_ROWS), :] = jnp.zeros((PAD_ROWS, LANES), F32)
        w = w_ref[...]
        b = b_ref[...]
        dw = [jnp.zeros((1, LANES), F32) for _ in range(kw)]
        db = jnp.zeros((1, LANES), F32)
        for r0 in range(0, s, tr):
            pre = _shifted_conv(xp_ref, w, r0, tr, kw) + b
            sg = _sigmoid(pre)
            dpre = do_ref[pl.ds(r0, tr), :] * (sg * (1.0 + pre * (1.0 - sg)))
            dgp_ref[pl.ds(r0, tr), :] = dpre
            db = db + jnp.sum(dpre, axis=0, keepdims=True)
            for k in range(kw):
                xk = xp_ref[pl.ds(PAD_ROWS + r0 - (kw - 1) + k, tr), :]
                dw[k] = dw[k] + jnp.sum(dpre * xk, axis=0, keepdims=True)
        for r0 in range(0, s, tr):
            acc = None
            for k in range(kw):
                term = dgp_ref[pl.ds(r0 + (kw - 1) - k, tr), :] * w[k:k + 1, :]
                acc = term if acc is None else acc + term
            dx_ref[pl.ds(r0, tr), :] = acc.astype(dx_ref.dtype)
        for k in range(kw):
            dw_ref[pl.ds(k, 1), :] = dw[k]
        db_ref[...] = db

    col = lambda j: (0, j)
    return pl.pallas_call(
        body, name=name, grid=(nt,),
        in_specs=[pl.BlockSpec((s, LANES), lambda j: (0, t0 + j)), pl.BlockSpec((s, LANES), col),
                  pl.BlockSpec((kw, LANES), col), pl.BlockSpec((1, LANES), col)],
        out_specs=[pl.BlockSpec((s, LANES), col), pl.BlockSpec((kw, LANES), col), pl.BlockSpec((1, LANES), col)],
        out_shape=[_sds((s, n_ch), MXU_DT), _sds((kw, n_ch)), _sds((1, n_ch))],
        scratch_shapes=[pltpu.VMEM((s + PAD_ROWS, LANES), F32), pltpu.VMEM((s + PAD_ROWS, LANES), F32)],
        compiler_params=_params("parallel"),
    )(proj, dout, conv_w, conv_b.reshape(1, -1))


HIGHEST = lax.Precision.HIGHEST
PAIRS = SSD_HPG // 2
PAIR_W = 2 * SSD_HEAD_DIM
GROUP_W = SSD_HPG * SSD_HEAD_DIM


def _iota2(shape, axis):
    return lax.broadcasted_iota(jnp.int32, shape, axis)


def _lane_pad(v):
    return jnp.pad(v.reshape(1, -1), ((0, 0), (0, LANES - v.shape[0])))


def _heads_to_groups(a):
    s = a.shape[0]
    return a[:, :SSD_HEADS].reshape(s, SSD_GROUPS, SSD_HPG).transpose(1, 0, 2)


def _groups_to_heads(a):
    s = a.shape[1]
    return jnp.pad(a.transpose(1, 0, 2).reshape(s, SSD_HEADS), ((0, 0), (0, LANES - SSD_HEADS)))


def _ssd_prep(dt_raw, dt_bias, a_log, name):
    s = dt_raw.shape[0]
    lc = SSD_CHUNK

    def body(x_ref, b_ref, al_ref, dt_ref, acs_ref, acst_ref):
        x = x_ref[...] + b_ref[...]
        dt = jnp.maximum(x, 0.0) + jnp.log1p(jnp.exp(-jnp.abs(x)))
        da = dt * (-jnp.exp(al_ref[...]))
        lower = (_iota2((lc, lc), 0) >= _iota2((lc, lc), 1)).astype(F32)
        upper = (_iota2((lc, lc), 0) <= _iota2((lc, lc), 1)).astype(F32)
        dt_ref[...] = dt
        acs_ref[...] = jnp.dot(lower, da, precision=HIGHEST, preferred_element_type=F32)
        acst_ref[...] = lax.dot_general(da, upper, (((0,), (0,)), ((), ())), precision=HIGHEST,
                                        preferred_element_type=F32)

    row = pl.BlockSpec((lc, LANES), lambda c: (c, 0))
    one = pl.BlockSpec((1, LANES), lambda c: (0, 0))
    return pl.pallas_call(
        body, name=name, grid=(s // lc,), in_specs=[row, one, one],
        out_specs=[row, row, pl.BlockSpec((LANES, lc), lambda c: (0, c))],
        out_shape=[_sds((s, LANES)), _sds((s, LANES)), _sds((LANES, s))],
        compiler_params=_params("parallel"),
    )(dt_raw, dt_bias, a_log)


def _pair_cols(v, p, lo):
    return jnp.where(lo, v[:, 2 * p:2 * p + 1], v[:, 2 * p + 1:2 * p + 2])


def _decay_matrix(acs, acst, h, tri):
    return jnp.exp(jnp.where(tri, acs[:, h:h + 1] - acst[h:h + 1, :], -jnp.inf))


def _dot(a, b, ca, cb):
    return lax.dot_general(a, b, (((ca,), (cb,)), ((), ())), preferred_element_type=F32)


def _ssd_scan_fwd(xbc, dtg, acsg, acstg, d_skip, name):
    s = xbc.shape[0]
    lc, nc = SSD_CHUNK, s // SSD_CHUNK
    xt, bt = GROUP_W // LANES, SSD_D_INNER // LANES

    def body(x_ref, b_ref, c_ref, dt_ref, acs_ref, acst_ref, dsk_ref, y_ref, hp_ref, st_ref):
        @pl.when(pl.program_id(1) == 0)
        def _():
            st_ref[...] = jnp.zeros(st_ref.shape, F32)

        bm, cmb = b_ref[...], c_ref[...].astype(MXU_DT)
        dt, acs, acst = dt_ref[...], acs_ref[...], acst_ref[...]
        cb = _dot(cmb, bm.astype(MXU_DT), 1, 1)
        tri = _iota2((lc, lc), 0) >= _iota2((lc, lc), 1)
        lo = _iota2((lc, PAIR_W), 1) < SSD_HEAD_DIM
        a_last = acs[lc - 1:lc, :]
        e_acs, e_ds, e_cd = jnp.exp(acs), jnp.exp(a_last - acs), jnp.exp(a_last)
        for p in range(PAIRS):
            sl = pl.ds(p * PAIR_W, PAIR_W)
            xp = x_ref[:, sl]
            ub = (xp * _pair_cols(dt, p, lo)).astype(MXU_DT)
            m0 = (cb * _decay_matrix(acs, acst, 2 * p, tri)).astype(MXU_DT)
            m1 = (cb * _decay_matrix(acs, acst, 2 * p + 1, tri)).astype(MXU_DT)
            ht = st_ref[p]
            hp_ref[p] = ht
            y = jnp.where(lo, _dot(m0, ub, 1, 0), _dot(m1, ub, 1, 0))
            y = y + _dot(cmb, ht.astype(MXU_DT), 1, 0) * _pair_cols(e_acs, p, lo)
            y_ref[:, sl] = y + xp * dsk_ref[:, sl]
            bd0 = (bm * e_ds[:, 2 * p:2 * p + 1]).astype(MXU_DT)
            bd1 = (bm * e_ds[:, 2 * p + 1:2 * p + 2]).astype(MXU_DT)
            st_ref[p] = ht * _pair_cols(e_cd, p, lo[:1]) + jnp.where(lo, _dot(bd0, ub, 0, 0), _dot(bd1, ub, 0, 0))

    small = pl.BlockSpec((None, lc, SSD_HPG), lambda g, c: (g, c, 0))
    return pl.pallas_call(
        body, name=name, grid=(SSD_GROUPS, nc),
        in_specs=[pl.BlockSpec((lc, GROUP_W), lambda g, c: (c, g)),
                  pl.BlockSpec((lc, LANES), lambda g, c: (c, bt + g)),
                  pl.BlockSpec((lc, LANES), lambda g, c: (c, bt + SSD_GROUPS + g)),
                  small, small, pl.BlockSpec((None, SSD_HPG, lc), lambda g, c: (g, 0, c)),
                  pl.BlockSpec((1, GROUP_W), lambda g, c: (0, g))],
        out_specs=[pl.BlockSpec((lc, GROUP_W), lambda g, c: (c, g)),
                   pl.BlockSpec((None, PAIRS, SSD_STATE, PAIR_W), lambda g, c: (c, g, 0, 0))],
        out_shape=[_sds((s, SSD_D_INNER)), _sds((nc, SSD_GROUPS * PAIRS, SSD_STATE, PAIR_W))],
        scratch_shapes=[pltpu.VMEM((PAIRS, SSD_STATE, PAIR_W), F32)],
        compiler_params=_params("parallel", "arbitrary"),
    )(xbc, xbc, xbc, dtg, acsg, acstg, d_skip)


def _ssd_scan_bwd(xbc, dtg, acsg, acstg, d_skip, dy, hprev, name):
    s = xbc.shape[0]
    lc, nc = SSD_CHUNK, s // SSD_CHUNK
    bt = SSD_D_INNER // LANES

    def body(x_ref, b_ref, c_ref, dt_ref, acs_ref, acst_ref, dsk_ref, dy_ref, hp_ref, hn_ref,
             dx_ref, db_ref, dc_ref, daq_ref, dar_ref, ddtx_ref, dd_ref, dst_ref, ta_ref, tx_ref):
        @pl.when(pl.program_id(1) == 0)
        def _():
            dst_ref[...] = jnp.zeros(dst_ref.shape, F32)
            dd_ref[...] = jnp.zeros(dd_ref.shape, F32)

        bm, cmb = b_ref[...], c_ref[...].astype(MXU_DT)
        bmb = bm.astype(MXU_DT)
        dt, acs, acst = dt_ref[...], acs_ref[...], acst_ref[...]
        cb = _dot(cmb, bmb, 1, 1)
        tri = _iota2((lc, lc), 0) >= _iota2((lc, lc), 1)
        lo = _iota2((lc, PAIR_W), 1) < SSD_HEAD_DIM
        a_last = acs[lc - 1:lc, :]
        e_acs, e_ds, e_cd = jnp.exp(acs), jnp.exp(a_last - acs), jnp.exp(a_last)
        dcb = jnp.zeros((lc, lc), F32)
        dc_x = jnp.zeros((lc, SSD_STATE), F32)
        db_x = jnp.zeros((lc, SSD_STATE), F32)
        da_in = jnp.zeros((lc, LANES), F32)
        da_out = jnp.zeros((SSD_HPG, lc), F32)
        head_col = _iota2((lc, LANES), 1)
        head_row = _iota2((SSD_HPG, lc), 0)
        last = _iota2((SSD_HPG, lc), 1) == lc - 1
        for p in range(PAIRS):
            sl = pl.ds(p * PAIR_W, PAIR_W)
            xp, dyp, dsk = x_ref[:, sl], dy_ref[:, sl], dsk_ref[:, sl]
            dtp = _pair_cols(dt, p, lo)
            u = xp * dtp
            ub, dyb = u.astype(MXU_DT), dyp.astype(MXU_DT)
            lmat = (_decay_matrix(acs, acst, 2 * p, tri), _decay_matrix(acs, acst, 2 * p + 1, tri))
            m0, m1 = (cb * lmat[0]).astype(MXU_DT), (cb * lmat[1]).astype(MXU_DT)
            ea, dsl = _pair_cols(e_acs, p, lo), _pair_cols(e_ds, p, lo)
            dht, ht = dst_ref[p], hp_ref[p]
            dhtb, htb = dht.astype(MXU_DT), ht.astype(MXU_DT)
            bd0 = (bm * e_ds[:, 2 * p:2 * p + 1]).astype(MXU_DT)
            bd1 = (bm * e_ds[:, 2 * p + 1:2 * p + 2]).astype(MXU_DT)
            du_state = jnp.where(lo, _dot(bd0, dhtb, 1, 0), _dot(bd1, dhtb, 1, 0))
            du = jnp.where(lo, _dot(m0, dyb, 0, 0), _dot(m1, dyb, 0, 0)) + du_state
            y_off = _dot(cmb, htb, 1, 0) * ea
            ta_ref[:, sl] = dyp * y_off - u * du_state
            tx_ref[:, sl] = du * xp
            dx_ref[:, sl] = dtp * du + dsk * dyp
            dd_ref[:, sl] += jnp.sum(dyp * xp, axis=0, keepdims=True)
            dy_h = (jnp.where(lo, dyp, 0.0).astype(MXU_DT), jnp.where(lo, 0.0, dyp).astype(MXU_DT))
            carry = jnp.sum(dht * hn_ref[p], axis=0, keepdims=True)
            for hh in range(2):
                h = 2 * p + hh
                dml = _dot(dy_h[hh], ub, 1, 1) * lmat[hh]
                dcb = dcb + dml
                flow = cb * dml
                da_in = da_in + jnp.where(head_col == h, jnp.sum(flow, axis=1, keepdims=True), 0.0)
                through = jnp.sum(jnp.where(lo[:1] == (hh == 0), carry, 0.0), axis=1, keepdims=True)
                da_out = da_out + jnp.where(head_row == h, jnp.sum(flow, axis=0, keepdims=True)
                                            - jnp.where(last, through, 0.0), 0.0)
            dye = (dyp * ea).astype(MXU_DT)
            dc_x = dc_x + _dot(dye, htb, 1, 1)
            db_x = db_x + _dot((u * dsl).astype(MXU_DT), dhtb, 1, 1)
            dst_ref[p] = dht * _pair_cols(e_cd, p, lo[:1]) + _dot(cmb, dye, 0, 0)
        dcbb = dcb.astype(MXU_DT)
        dc_ref[...] = _dot(dcbb, bmb, 1, 0) + dc_x
        db_ref[...] = _dot(dcbb, cmb, 0, 0) + db_x
        seg_lo = _iota2((GROUP_W, LANES), 1) * SSD_HEAD_DIM
        chan = _iota2((GROUP_W, LANES), 0)
        seg = jnp.logical_and(chan >= seg_lo, chan < seg_lo + SSD_HEAD_DIM).astype(F32)
        da_in = da_in + jnp.dot(ta_ref[...], seg, precision=HIGHEST, preferred_element_type=F32)
        daq_ref[...] = da_in[:, :SSD_HPG]
        dar_ref[...] = da_out
        ddtx_ref[...] = jnp.dot(tx_ref[...], seg, precision=HIGHEST, preferred_element_type=F32)[:, :SSD_HPG]

    rev = lambda c: nc - 1 - c
    small = pl.BlockSpec((None, lc, SSD_HPG), lambda g, c: (g, rev(c), 0))
    small_t = pl.BlockSpec((None, SSD_HPG, lc), lambda g, c: (g, 0, rev(c)))
    wide = pl.BlockSpec((lc, GROUP_W), lambda g, c: (rev(c), g))
    state = lambda at: pl.BlockSpec((None, PAIRS, SSD_STATE, PAIR_W), lambda g, c: (at(c), g, 0, 0))
    return pl.pallas_call(
        body, name=name, grid=(SSD_GROUPS, nc),
        in_specs=[pl.BlockSpec((lc, GROUP_W), lambda g, c: (rev(c), g)),
                  pl.BlockSpec((lc, LANES), lambda g, c: (rev(c), bt + g)),
                  pl.BlockSpec((lc, LANES), lambda g, c: (rev(c), bt + SSD_GROUPS + g)),
                  small, small, small_t, pl.BlockSpec((1, GROUP_W), lambda g, c: (0, g)), wide,
                  state(rev), state(lambda c: jnp.minimum(rev(c) + 1, nc - 1))],
        out_specs=[wide, pl.BlockSpec((lc, LANES), lambda g, c: (rev(c), g)),
                   pl.BlockSpec((lc, LANES), lambda g, c: (rev(c), g)), small, small_t, small,
                   pl.BlockSpec((1, GROUP_W), lambda g, c: (0, g))],
        out_shape=[_sds((s, SSD_D_INNER)), _sds((s, SSD_GROUPS * SSD_STATE)), _sds((s, SSD_GROUPS * SSD_STATE)),
                   _sds((SSD_GROUPS, s, SSD_HPG)), _sds((SSD_GROUPS, SSD_HPG, s)), _sds((SSD_GROUPS, s, SSD_HPG)),
                   _sds((1, SSD_D_INNER))],
        scratch_shapes=[pltpu.VMEM((PAIRS, SSD_STATE, PAIR_W), F32), pltpu.VMEM((lc, GROUP_W), F32),
                        pltpu.VMEM((lc, GROUP_W), F32)],
        compiler_params=_params("parallel", "arbitrary"),
    )(xbc, xbc, xbc, dtg, acsg, acstg, d_skip, dy, hprev, hprev)


def _ssd_post(da_in, da_out, ddtx, dt, dt_raw, dt_bias, a_log, name):
    s = da_in.shape[0]
    lc = SSD_CHUNK

    def body(dain_ref, daout_ref, ddtx_ref, dt_ref, x_ref, b_ref, al_ref, ddr_ref, dal_ref, dbias_ref):
        @pl.when(pl.program_id(0) == 0)
        def _():
            dal_ref[...] = jnp.zeros(dal_ref.shape, F32)
            dbias_ref[...] = jnp.zeros(dbias_ref.shape, F32)

        upper = (_iota2((lc, lc), 0) <= _iota2((lc, lc), 1)).astype(F32)
        dda = jnp.dot(upper, dain_ref[...] - daout_ref[...], precision=HIGHEST, preferred_element_type=F32)
        a = -jnp.exp(al_ref[...])
        ddt = dda * a + ddtx_ref[...]
        dal_ref[...] += jnp.sum(dda * dt_ref[...], axis=0, keepdims=True) * a
        ddr = ddt * _sigmoid(x_ref[...] + b_ref[...])
        ddr_ref[...] = ddr.astype(ddr_ref.dtype)
        dbias_ref[...] += jnp.sum(ddr, axis=0, keepdims=True)

    row = pl.BlockSpec((lc, LANES), lambda i: (i, 0))
    one = pl.BlockSpec((1, LANES), lambda i: (0, 0))
    return pl.pallas_call(
        body, name=name, grid=(s // lc,), in_specs=[row, row, row, row, row, one, one], out_specs=[row, one, one],
        out_shape=[_sds((s, LANES), MXU_DT), _sds((1, LANES)), _sds((1, LANES))],
        compiler_params=_params("arbitrary"),
    )(da_in, da_out, ddtx, dt, dt_raw, dt_bias, a_log)


NORM_GROUP_W = SSD_D_INNER // SSD_GROUPS


def _group_rstd(yz):
    return [lax.rsqrt(jnp.mean(jnp.square(yz[:, g * NORM_GROUP_W:(g + 1) * NORM_GROUP_W]), axis=-1, keepdims=True) + EPS)
            for g in range(SSD_GROUPS)]


def _gated_norm_fwd(y, proj, norm_g, name):
    def fn(yv, z, gv):
        yz = yv * (z * _sigmoid(z))
        parts = [yz[:, g * NORM_GROUP_W:(g + 1) * NORM_GROUP_W] * r for g, r in enumerate(_group_rstd(yz))]
        return (jnp.concatenate(parts, axis=1) * gv,)
    return _rowwise(fn, [y, (proj, 0, SSD_D_INNER)], [norm_g.reshape(1, -1)], [_sds(y.shape, MXU_DT)], [],
                    name=name)[0]


def _gated_norm_bwd(y, proj, norm_g, dout, name):
    def fn(yv, z, do, gv):
        sg = _sigmoid(z)
        sz = z * sg
        yz = yv * sz
        dog = do * gv
        dyz, dg = [], []
        for g, r in enumerate(_group_rstd(yz)):
            cols = slice(g * NORM_GROUP_W, (g + 1) * NORM_GROUP_W)
            yzg, dogg = yz[:, cols], dog[:, cols]
            dyz.append(r * dogg - yzg * (r * r * r) * jnp.mean(yzg * dogg, axis=-1, keepdims=True))
            dg.append(jnp.sum(do[:, cols] * yzg * r, axis=0, keepdims=True))
        dyz = jnp.concatenate(dyz, axis=1)
        return dyz * sz, dyz * yv * (sg * (1.0 + z * (1.0 - sg))), jnp.concatenate(dg, axis=1)
    return _rowwise(fn, [y, (proj, 0, SSD_D_INNER), dout], [norm_g.reshape(1, -1)],
                    [_sds(y.shape), _sds(y.shape, MXU_DT)], [_sds((1, y.shape[1]))], name=name)


def _ssd_fwd(h, p, big):
    hn = _rms_fwd(h, p["norm_g"], "mix_norm_fwd")
    w_in = big.whole("ssd_w_in", hn)
    w_zx, w_dt = w_in[:, :SSD_ZX], _pad_cols(w_in[:, SSD_ZX:])
    proj = _matmul(hn, w_zx, tm=1024, name="ssd_in_zx")
    dt_raw = _matmul(hn, w_dt, name="ssd_in_dt")
    conv_w = big.whole("ssd_conv_w", proj)
    xbc = _dwconv_silu_fwd(proj, SSD_D_INNER, SSD_CONV_DIM, conv_w, p["conv_b"], "ssd_conv_fwd")
    dt, acs, acst = _ssd_prep(dt_raw, _lane_pad(p["dt_bias"]), _lane_pad(p["a_log"]), "ssd_prep")
    dtg, acsg = _heads_to_groups(dt), _heads_to_groups(acs)
    acstg = acst[:SSD_HEADS].reshape(SSD_GROUPS, SSD_HPG, -1)
    d_skip = jnp.repeat(p["d"], SSD_HEAD_DIM).reshape(1, -1)
    y, hprev = _ssd_scan_fwd(xbc, dtg, acsg, acstg, d_skip, "ssd_scan_fwd")
    yn = _gated_norm_fwd(y, proj, p["gnorm_g"], "ssd_gnorm_fwd")
    w_out = big.whole("ssd_w_out", yn)
    h_out = _matmul(yn, w_out, res=h, name="ssd_out")
    return h_out, (hn, proj, dt_raw, xbc, dt, dtg, acsg, acstg, d_skip, y, hprev, yn, w_zx, w_dt, conv_w, w_out)


def _ssd_bwd(h, dh, saved, p, out):
    hn, proj, dt_raw, xbc, dt, dtg, acsg, acstg, d_skip, y, hprev, yn, w_zx, w_dt, conv_w, w_out = saved
    dh, dh_b = dh
    dyn = _matmul(dh_b, w_out, tb=True, name="ssd_out_dx")
    out.send("ssd_w_out", _matmul(yn, dh_b, ta=True, out_dtype=MXU_DT, name="ssd_out_dw"))
    dy, dz, dgnorm = _gated_norm_bwd(y, proj, out.tie(p["gnorm_g"]), dyn, "ssd_gnorm_bwd")
    dx, dbm, dcm, daq, dar, ddtx, dd = _ssd_scan_bwd(xbc, dtg, acsg, acstg, d_skip, dy, hprev, "ssd_scan_bwd")
    dxbc = jnp.concatenate([dx, dbm, dcm], axis=1)
    dpre, dconv_w, dconv_b = _dwconv_silu_bwd(proj, SSD_D_INNER, SSD_CONV_DIM, dxbc, conv_w, p["conv_b"],
                                              "ssd_conv_bwd")
    ddr, dalog, dbias = _ssd_post(_groups_to_heads(daq), _groups_to_heads(dar.transpose(0, 2, 1)),
                                  _groups_to_heads(ddtx), dt, dt_raw,
                                  _lane_pad(p["dt_bias"]), _lane_pad(p["a_log"]), "ssd_post")
    w_z, w_x = w_zx[:, :SSD_D_INNER], w_zx[:, SSD_D_INNER:]
    dhn = _matmul(dz, w_z, tb=True, name="ssd_in_dx_z")
    dhn = _matmul(dpre, w_x, tb=True, res=dhn, name="ssd_in_dx_x")
    dhn = _matmul(ddr, w_dt, tb=True, res=dhn, name="ssd_in_dx_dt")
    out.send("ssd_w_in", jnp.concatenate(
        [_matmul(hn, dz, ta=True, out_dtype=MXU_DT, name="ssd_in_dw_z"),
         _matmul(hn, dpre, ta=True, out_dtype=MXU_DT, name="ssd_in_dw_x"),
         _matmul(hn, ddr, ta=True, out_dtype=MXU_DT, name="ssd_in_dw_dt")[:, :SSD_HEADS]], axis=1))
    *dh_in, dnorm_g = _rms_bwd(h, out.tie(p["norm_g"]), dhn, dh, "mix_norm_bwd")
    grads = dict(norm_g=dnorm_g, conv_w=dconv_w, conv_b=dconv_b, dt_bias=dbias[:, :SSD_HEADS],
                 a_log=dalog[:, :SSD_HEADS], d=dd.reshape(SSD_HEADS, SSD_HEAD_DIM).sum(axis=1).reshape(1, -1),
                 gnorm_g=dgnorm)
    return dh_in, grads


FOX_PAIRS = FOX_HEADS // 2
ATT_TQ = 512
ATT_TK = 512
ATT_SUB = 512
FOX_PREP_ROWS = 256
NEG_BIG = -1e30
FOX_SCALE = FOX_HEAD_DIM ** -0.5
FOX_AUG_D = FOX_HEADS * LANES
QSIDE = FOX_HEAD_DIM
KSIDE = FOX_HEAD_DIM + 3


def _split3(x):
    a = x.astype(MXU_DT).astype(F32)
    b = (x - a).astype(MXU_DT).astype(F32)
    return a, b, (x - a - b).astype(MXU_DT).astype(F32)


def _head_tiles(pair_tile):
    return pair_tile, pltpu.roll(pair_tile, FOX_HEAD_DIM, 1)


def _fill_lanes(base, lane, first, values):
    for i, v in enumerate(values):
        base = jnp.where(lane == first + i, v, base)
    return base


def _pair_tile(lane, tile0, tile1):
    return jnp.where(lane < FOX_HEAD_DIM, tile0, pltpu.roll(tile1, FOX_HEAD_DIM, 1))


def _compact_heads(a, lane):
    return jnp.concatenate([_pair_tile(lane, a[:, 2 * j * LANES:(2 * j + 1) * LANES],
                                       a[:, (2 * j + 1) * LANES:(2 * j + 2) * LANES]) for j in range(FOX_PAIRS)], axis=1)


def _head_sum_matrix():
    return (_iota2((LANES, LANES), 0) < FOX_HEAD_DIM) == (_iota2((LANES, LANES), 1) < FOX_HEAD_DIM)


def _head_sums(x):
    bd = _head_sum_matrix().astype(F32)
    parts = [jnp.dot(x[:, j * LANES:(j + 1) * LANES], bd, precision=HIGHEST, preferred_element_type=F32)
             for j in range(x.shape[1] // LANES)]
    return parts[0] if len(parts) == 1 else jnp.concatenate(parts, axis=1)


def _fox_prep_fwd(proj, f_raw, qg, kg, b_f, name):
    s = proj.shape[0]
    tr = min(FOX_PREP_ROWS, s)

    def body(q_ref, k_ref, v_ref, f_ref, qg_ref, kg_ref, b_ref, qa_ref, ka_ref, va_ref, carry_ref):
        @pl.when(pl.program_id(0) == 0)
        def _():
            carry_ref[...] = jnp.zeros(carry_ref.shape, F32)

        normed = []
        for x_ref, g_ref in ((q_ref, qg_ref), (k_ref, kg_ref)):
            x = x_ref[...]
            r = lax.rsqrt(_head_sums(x * x) * (1.0 / FOX_HEAD_DIM) + EPS)
            normed.append(x * r * g_ref[...])
        qn, kn, v = normed[0] * FOX_SCALE, normed[1], v_ref[...]
        x = f_ref[...] + b_ref[...]
        lf = jnp.minimum(x, 0.0) - jnp.log1p(jnp.exp(-jnp.abs(x)))
        lower = (_iota2((tr, tr), 0) >= _iota2((tr, tr), 1)).astype(F32)
        cum = jnp.dot(lower, lf, precision=HIGHEST, preferred_element_type=F32) + carry_ref[...]
        carry_ref[...] += jnp.sum(lf, axis=0, keepdims=True)
        first = _iota2((LANES, FOX_D), 0) * FOX_HEAD_DIM
        chan = _iota2((LANES, FOX_D), 1)
        spread = jnp.logical_and(chan >= first, chan < first + FOX_HEAD_DIM).astype(F32)
        cum = jnp.dot(cum, spread, precision=HIGHEST, preferred_element_type=F32)
        lane = _iota2((tr, LANES), 1)
        ones = jnp.where(jnp.logical_and(lane >= QSIDE, lane < KSIDE + 3), 1.0, 0.0)
        for j in range(FOX_PAIRS):
            cols = slice(j * LANES, (j + 1) * LANES)
            tiles = zip(_head_tiles(qn[:, cols]), _head_tiles(kn[:, cols]), _head_tiles(v[:, cols]),
                        reversed(_head_tiles(cum[:, cols])))
            for hh, (qt, kt, vt, ct) in enumerate(tiles):
                out = slice((2 * j + hh) * LANES, (2 * j + hh + 1) * LANES)
                c3 = _split3(ct)
                head = lane < FOX_HEAD_DIM
                qa_ref[:, out] = _fill_lanes(jnp.where(head, qt, ones), lane, QSIDE, c3).astype(qa_ref.dtype)
                ka_ref[:, out] = _fill_lanes(jnp.where(head, kt, ones), lane, KSIDE, [-c for c in c3]).astype(ka_ref.dtype)
                va_ref[:, out] = jnp.where(head, vt, jnp.where(lane < KSIDE, 1.0, 0.0)).astype(va_ref.dtype)

    wide = lambda cb: pl.BlockSpec((tr, FOX_D), lambda i: (i, cb))
    aug = pl.BlockSpec((tr, FOX_AUG_D), lambda i: (i, 0))
    one = lambda n: pl.BlockSpec((1, n), lambda i: (0, 0))
    return pl.pallas_call(
        body, name=name, grid=(s // tr,),
        in_specs=[wide(0), wide(1), wide(2), pl.BlockSpec((tr, LANES), lambda i: (i, 0)), one(FOX_D), one(FOX_D),
                  one(LANES)],
        out_specs=[aug, aug, aug], out_shape=[_sds((s, FOX_AUG_D), MXU_DT)] * 3,
        scratch_shapes=[pltpu.VMEM((1, LANES), F32)],
        compiler_params=_params("arbitrary"),
    )(proj, proj, proj, f_raw, qg, kg, b_f)


def _fox_prep_bwd(proj, f_raw, qg, kg, b_f, dqa, dka, name):
    s = proj.shape[0]
    tr = min(FOX_PREP_ROWS, s)
    nb = s // tr

    def body(q_ref, k_ref, f_ref, qg_ref, kg_ref, b_ref, dqa_ref, dka_ref,
             dq_ref, dk_ref, df_ref, dqg_ref, dkg_ref, db_ref, carry_ref):
        @pl.when(pl.program_id(0) == 0)
        def _():
            carry_ref[...] = jnp.zeros(carry_ref.shape, F32)
            dqg_ref[...] = jnp.zeros(dqg_ref.shape, F32)
            dkg_ref[...] = jnp.zeros(dkg_ref.shape, F32)
            db_ref[...] = jnp.zeros(db_ref.shape, F32)

        lane = _iota2((tr, LANES), 1)
        for x_ref, g_ref, dt_ref, scale, dx_ref, dg_ref in ((q_ref, qg_ref, dqa_ref, FOX_SCALE, dq_ref, dqg_ref),
                                                            (k_ref, kg_ref, dka_ref, 1.0, dk_ref, dkg_ref)):
            x, dy = x_ref[...], _compact_heads(dt_ref[...], lane) * scale
            r = lax.rsqrt(_head_sums(x * x) * (1.0 / FOX_HEAD_DIM) + EPS)
            dyg = dy * g_ref[...]
            dx = r * dyg - x * (r * r * r) * (_head_sums(x * dyg) * (1.0 / FOX_HEAD_DIM))
            dx_ref[...] = dx.astype(dx_ref.dtype)
            dg_ref[...] += jnp.sum(dy * x * r, axis=0, keepdims=True)
        dc = jnp.zeros((tr, LANES), F32)
        for h in range(FOX_HEADS):
            sums = dqa_ref[:, pl.ds(h * LANES + QSIDE, 1)] - dka_ref[:, pl.ds(h * LANES + KSIDE, 1)]
            dc = jnp.where(lane == h, sums, dc)
        upper = (_iota2((tr, tr), 0) <= _iota2((tr, tr), 1)).astype(F32)
        dlf = jnp.dot(upper, dc, precision=HIGHEST, preferred_element_type=F32) + carry_ref[...]
        carry_ref[...] += jnp.sum(dc, axis=0, keepdims=True)
        df = dlf * _sigmoid(-(f_ref[...] + b_ref[...]))
        df_ref[...] = df.astype(df_ref.dtype)
        db_ref[...] += jnp.sum(df, axis=0, keepdims=True)

    wide = lambda cb: pl.BlockSpec((tr, FOX_D), lambda i: (nb - 1 - i, cb))
    aug = pl.BlockSpec((tr, FOX_AUG_D), lambda i: (nb - 1 - i, 0))
    row = pl.BlockSpec((tr, LANES), lambda i: (nb - 1 - i, 0))
    one = lambda n: pl.BlockSpec((1, n), lambda i: (0, 0))
    return pl.pallas_call(
        body, name=name, grid=(nb,),
        in_specs=[wide(0), wide(1), row, one(FOX_D), one(FOX_D), one(LANES), aug, aug],
        out_specs=[wide(0), wide(0), row, one(FOX_D), one(FOX_D), one(LANES)],
        out_shape=[_sds((s, FOX_D), MXU_DT), _sds((s, FOX_D), MXU_DT), _sds((s, LANES), MXU_DT),
                   _sds((1, FOX_D)), _sds((1, FOX_D)), _sds((1, LANES))],
        scratch_shapes=[pltpu.VMEM((1, LANES), F32)],
        compiler_params=_params("arbitrary"),
    )(proj, proj, f_raw, qg, kg, b_f, dqa, dka)


def _fox_attn_fwd(qa, ka, va, proj, name):
    s = qa.shape[0]
    tq, tk = min(ATT_TQ, s), min(ATT_TK, s)
    assert s % tq == 0 and s % tk == 0
    gt = 3 * FOX_D // LANES
    head_lanes = [slice(hh * LANES, (hh + 1) * LANES) for hh in range(2)]

    def body(qa_ref, ka_ref, va_ref, g_ref, o_ref, og_ref, qb_ref):
        qi = pl.program_id(1)
        sub = min(ATT_SUB, tq)
        lane = _iota2((sub, LANES), 1)
        ahead = _iota2((sub, tk), 0) - _iota2((sub, tk), 1)
        chains = [(hh, r0) for hh in range(2) for r0 in range(0, tq, sub)]
        q = [qa_ref[pl.ds(r0, sub), head_lanes[hh]] for hh, r0 in chains]

        def kv_step(j, carry, masked):
            rows = pl.ds(pl.multiple_of(j * tk, tk), tk)
            out = []
            for c, (hh, r0) in enumerate(chains):
                m, acc = carry[2 * c:2 * c + 2]
                sc = _dot(q[c], ka_ref[rows, head_lanes[hh]], 1, 1)
                if masked:
                    sc = jnp.where(ahead >= j * tk - qi * tq - r0, sc, NEG_BIG)
                m_new = jnp.maximum(m, jnp.max(sc, axis=1, keepdims=True))
                pr = jnp.exp(sc - m_new).astype(MXU_DT)
                out += [m_new, jnp.exp(m - m_new) * acc + _dot(pr, va_ref[rows, head_lanes[hh]], 1, 0)]
            return tuple(out)

        n_clear = lax.div(qi * tq, tk)
        n_all = lax.div((qi + 1) * tq + tk - 1, tk)
        init = (jnp.full((sub, 1), NEG_BIG, F32), jnp.zeros((sub, LANES), F32)) * len(chains)
        carry = lax.fori_loop(0, n_clear, functools.partial(kv_step, masked=False), init)
        carry = lax.fori_loop(n_clear, n_all, functools.partial(kv_step, masked=True), carry)
        heads = [[], []]
        for c, (hh, r0) in enumerate(chains):
            m, acc = carry[2 * c:2 * c + 2]
            l = acc[:, QSIDE:QSIDE + 1]
            heads[hh].append(acc / l)
            qf = q[c].astype(F32)
            bias = qf[:, QSIDE:QSIDE + 1] + qf[:, QSIDE + 1:QSIDE + 2] + qf[:, QSIDE + 2:QSIDE + 3]
            qb_ref[pl.ds(r0, sub), head_lanes[hh]] = _fill_lanes(
                qf, lane, QSIDE, _split3(bias - (m + jnp.log(l)))).astype(qb_ref.dtype)
        heads = [jnp.concatenate(h, axis=0) for h in heads]
        o = _pair_tile(_iota2((tq, LANES), 1), heads[0], heads[1])
        o_ref[...] = o
        og_ref[...] = (o * _sigmoid(g_ref[...])).astype(og_ref.dtype)

    blk2 = pl.BlockSpec((tq, 2 * LANES), lambda p, i: (i, p))
    seq2 = pl.BlockSpec((s, 2 * LANES), lambda p, i: (0, p))
    blk = pl.BlockSpec((tq, LANES), lambda p, i: (i, p))
    return pl.pallas_call(
        body, name=name, grid=(FOX_PAIRS, s // tq),
        in_specs=[blk2, seq2, seq2, pl.BlockSpec((tq, LANES), lambda p, i: (i, gt + p))],
        out_specs=[blk, blk, blk2],
        out_shape=[_sds((s, FOX_D)), _sds((s, FOX_D), MXU_DT), _sds((s, FOX_AUG_D), MXU_DT)],
        compiler_params=_params("parallel", "parallel"),
    )(qa, ka, va, proj)


def _fox_gate_bwd(dog, o, proj, name):
    def fn(dogv, ov, gate):
        sg = _sigmoid(gate)
        do = dogv * sg
        delta = _head_sums(do * ov)
        lane = _iota2((do.shape[0], LANES), 1)
        tiles = []
        for j in range(FOX_PAIRS):
            cols = slice(j * LANES, (j + 1) * LANES)
            for dt, dl in zip(_head_tiles(do[:, cols]), reversed(_head_tiles(delta[:, cols]))):
                tiles.append(_fill_lanes(jnp.where(lane < FOX_HEAD_DIM, dt, 0.0), lane, QSIDE,
                                         [-d for d in _split3(dl)]))
        return dogv * ov * sg * (1.0 - sg), jnp.concatenate(tiles, axis=1)
    return _rowwise(fn, [dog, o, (proj, 3, FOX_D)], [], [_sds(o.shape, MXU_DT), _sds((o.shape[0], FOX_AUG_D), MXU_DT)],
                    [], name=name)


def _fox_attn_bwd(qb, ka, va, doa, name):
    s = qb.shape[0]
    tq, tk = min(ATT_TQ, s), min(ATT_TK, s)
    nq, nk = s // tq, s // tk
    head_lanes = [slice(hh * LANES, (hh + 1) * LANES) for hh in range(2)]

    def body(qb_ref, doa_ref, ka_ref, va_ref, dqa_ref, dka_ref, dv_ref):
        kj = pl.program_id(1)

        @pl.when(kj == 0)
        def _():
            dqa_ref[...] = jnp.zeros(dqa_ref.shape, F32)

        ahead = _iota2((tq, tk), 0) - _iota2((tq, tk), 1)
        kb = [ka_ref[:, hs] for hs in head_lanes]
        vb = [va_ref[:, hs] for hs in head_lanes]

        def q_step(i, carry, masked):
            rows = pl.ds(pl.multiple_of(i * tq, tq), tq)
            out = []
            for hh, hs in enumerate(head_lanes):
                dk, dv = carry[2 * hh:2 * hh + 2]
                q, do = qb_ref[rows, hs], doa_ref[rows, hs]
                pr = jnp.exp(_dot(q, kb[hh], 1, 1))
                if masked:
                    pr = jnp.where(ahead >= kj * tk - i * tq, pr, 0.0)
                dv = dv + _dot(pr.astype(MXU_DT), do, 0, 0)
                ds = (pr * _dot(do, vb[hh], 1, 1)).astype(MXU_DT)
                dqa_ref[rows, hs] += _dot(ds, kb[hh], 1, 0)
                out += [dk + _dot(ds, q, 0, 0), dv]
            return tuple(out)

        first = lax.div(kj * tk, tq)
        n_masked = lax.div((kj + 1) * tk + tq - 1, tq)
        carry = lax.fori_loop(first, n_masked, functools.partial(q_step, masked=True),
                              (jnp.zeros((tk, LANES), F32),) * 4)
        dk0, dv0, dk1, dv1 = lax.fori_loop(n_masked, nq, functools.partial(q_step, masked=False), carry)
        dka_ref[:, head_lanes[0]] = dk0
        dka_ref[:, head_lanes[1]] = dk1
        dv_ref[...] = _pair_tile(_iota2((tk, LANES), 1), dv0, dv1).astype(dv_ref.dtype)

    seq2 = pl.BlockSpec((s, 2 * LANES), lambda p, j: (0, p))
    blk2 = pl.BlockSpec((tk, 2 * LANES), lambda p, j: (j, p))
    return pl.pallas_call(
        body, name=name, grid=(FOX_PAIRS, nk), in_specs=[seq2, seq2, blk2, blk2],
        out_specs=[seq2, blk2, pl.BlockSpec((tk, LANES), lambda p, j: (j, p))],
        out_shape=[_sds((s, FOX_AUG_D)), _sds((s, FOX_AUG_D)), _sds((s, FOX_D), MXU_DT)],
        compiler_params=_params("parallel", "arbitrary"),
    )(qb, doa, ka, va)


def _fox_fwd(h, p, big):
    hn = _rms_fwd(h, p["norm_g"], "mix_norm_fwd")
    w_in = big.whole("fox_w_in", hn)
    w_qkvg, w_f = w_in[:, :4 * FOX_D], _pad_cols(w_in[:, 4 * FOX_D:])
    proj = _matmul(hn, w_qkvg, tm=1024, name="fox_in_qkvg")
    f_raw = _matmul(hn, w_f, name="fox_in_f")
    qg = jnp.tile(p["q_norm_g"], FOX_HEADS).reshape(1, -1)
    kg = jnp.tile(p["k_norm_g"], FOX_HEADS).reshape(1, -1)
    qa, ka, va = _fox_prep_fwd(proj, f_raw, qg, kg, _lane_pad(p["b_f"]), "fox_prep_fwd")
    o, og, qb = _fox_attn_fwd(qa, ka, va, proj, "fox_attn_fwd")
    w_out = big.whole("fox_w_out", og)
    h_out = _matmul(og, w_out, res=h, name="fox_out")
    return h_out, (hn, proj, f_raw, qg, kg, ka, va, qb, o, og, w_qkvg, w_f, w_out)


def _fox_bwd(h, dh, saved, p, out):
    hn, proj, f_raw, qg, kg, ka, va, qb, o, og, w_qkvg, w_f, w_out = saved
    s = h.shape[0]
    dh, dh_b = dh
    dog = _matmul(dh_b, w_out, tb=True, name="fox_out_dx")
    out.send("fox_w_out", _matmul(og, dh_b, ta=True, out_dtype=MXU_DT, name="fox_out_dw"))
    dgate, doa = _fox_gate_bwd(dog, o, proj, "fox_gate_bwd")
    dqa, dka, dv = _fox_attn_bwd(qb, ka, va, doa, "fox_attn_bwd")
    dq, dk, df, dqg, dkg, dbf = _fox_prep_bwd(proj, f_raw, qg, kg, out.tie(_lane_pad(p["b_f"])), dqa, dka,
                                              "fox_prep_bwd")
    dproj = jnp.concatenate([dq, dk, dv, dgate], axis=1)
    dhn = _matmul(dproj, w_qkvg, tb=True, name="fox_in_dx_qkvg")
    dhn = _matmul(df, w_f, tb=True, res=dhn, name="fox_in_dx_f")
    out.send("fox_w_in", jnp.concatenate(
        [_matmul(hn, dproj, ta=True, out_dtype=MXU_DT, name="fox_in_dw_qkvg"),
         _matmul(hn, df, ta=True, out_dtype=MXU_DT, name="fox_in_dw_f")[:, :FOX_HEADS]], axis=1))
    *dh_in, dnorm_g = _rms_bwd(h, out.tie(p["norm_g"]), dhn, dh, "mix_norm_bwd")
    fold = lambda g: g.reshape(FOX_HEADS, FOX_HEAD_DIM).sum(axis=0).reshape(1, -1)
    grads = dict(norm_g=dnorm_g, b_f=dbf[:, :FOX_HEADS], q_norm_g=fold(dqg), k_norm_g=fold(dkg))
    return dh_in, grads


def _my_index():
    return 4 * lax.axis_index("x") + 2 * lax.axis_index("y") + lax.axis_index("c")


def _peer(k):
    x, y, c = lax.axis_index("x"), lax.axis_index("y"), lax.axis_index("c")
    flip = lambda v, bit: 1 - v if bit else v
    return (flip(x, k & 4), flip(y, k & 2), flip(c, k & 1))


SEM_SPEC = pl.BlockSpec(memory_space=pltpu.SEMAPHORE)
DATAFLOW = pltpu.SideEffectType.DATAFLOW_SIDE_EFFECTING


def _at(ref, idx):
    return ref.at[tuple(idx)] if idx else ref


def _copies_start(name, srcs, lands, groups):
    ns, nl = len(srcs), len(lands)
    items = [item for group in groups for item in group]

    def body(*refs):
        src_refs, land_refs = refs[:ns], refs[ns:ns + nl]
        sems = refs[ns + nl:ns + nl + 2 * len(items)]
        me = _my_index()
        for t, (i, src_slot, j, dst_slot, _) in enumerate(items):
            for k in range(1, N_DEV):
                pltpu.make_async_remote_copy(
                    src_ref=_at(src_refs[i], src_slot(me, k)), dst_ref=_at(land_refs[j], dst_slot(me, k)),
                    send_sem=sems[2 * t], recv_sem=sems[2 * t + 1], device_id=_peer(k),
                    device_id_type=MESH_IDS).start()
        refs[-1][...] = jnp.zeros(refs[-1].shape, F32)

    hbm = pl.BlockSpec(memory_space=pltpu.HBM)
    bufs = [pltpu.with_memory_space_constraint(a, pltpu.HBM) for a in list(srcs) + list(lands)]
    n_sems = 2 * len(items)
    out = pl.pallas_call(
        body, name=name, in_specs=[hbm] * (ns + nl),
        out_specs=(*[SEM_SPEC] * n_sems, *[hbm] * (ns + nl), pl.BlockSpec(memory_space=pltpu.VMEM)),
        out_shape=(*[pltpu.SemaphoreType.DMA(())] * n_sems, *[pltpu.HBM(a.shape, a.dtype) for a in bufs],
                   _sds((8, LANES))),
        input_output_aliases={i: n_sems + i for i in range(ns + nl)},
        compiler_params=pltpu.CompilerParams(has_side_effects=DATAFLOW),
    )(*bufs)
    sems, t = [], 0
    for group in groups:
        sems.append([(out[2 * (t + u)], out[2 * (t + u) + 1]) for u in range(len(group))])
        t += len(group)
    return sems, list(out[n_sems:n_sems + ns]), list(out[n_sems + ns:n_sems + ns + nl]), out[-1]


def _copies_wait(name, keep, lands, sems, group, after):
    nk, nl, n = len(keep), len(lands), len(group)

    def body(*refs):
        land_refs = refs[nk:nk + nl]
        sem_refs = refs[nk + nl:nk + nl + 2 * n]
        me = _my_index()
        copies = []
        for t, (_, _, j, _, seven) in enumerate(group):
            blocks = _at(land_refs[j], seven(me))
            copies.append(pltpu.make_async_remote_copy(src_ref=blocks, dst_ref=blocks, send_sem=sem_refs[2 * t],
                                                       recv_sem=sem_refs[2 * t + 1], device_id=_peer(1),
                                                       device_id_type=MESH_IDS))
        for cp in copies:
            cp.wait_recv()
        for cp in copies:
            cp.wait_send()

    hbm = pl.BlockSpec(memory_space=pltpu.HBM)
    bufs = list(keep) + list(lands)
    out = pl.pallas_call(
        body, name=name, in_specs=[hbm] * (nk + nl) + [SEM_SPEC] * (2 * n) + [pl.BlockSpec(memory_space=pl.ANY)],
        out_specs=[hbm] * (nk + nl), out_shape=[pltpu.HBM(a.shape, a.dtype) for a in bufs],
        input_output_aliases={i: i for i in range(nk + nl)},
        compiler_params=pltpu.CompilerParams(has_side_effects=DATAFLOW),
    )(*bufs, *[s for pair in sems for s in pair], after)
    return list(out[nk:])


def _allreduce_small(buf, name):
    def body(in_ref, all_ref, sum_ref, send_sems, recv_sems):
        me = _my_index()
        all_ref[me] = in_ref[...]

        def copy(k, slot):
            return pltpu.make_async_remote_copy(
                src_ref=in_ref, dst_ref=all_ref.at[slot], send_sem=send_sems.at[k - 1], recv_sem=recv_sems.at[k - 1],
                device_id=_peer(k), device_id_type=MESH_IDS)

        for k in range(1, N_DEV):
            copy(k, me).start()
        for k in range(1, N_DEV):
            copy(k, jnp.bitwise_xor(me, k)).wait_recv()
        for k in range(1, N_DEV):
            copy(k, me).wait_send()
        acc = all_ref[0]
        for j in range(1, N_DEV):
            acc = acc + all_ref[j]
        sum_ref[...] = acc

    vmem = pl.BlockSpec(memory_space=pltpu.VMEM)
    return pl.pallas_call(
        body, name=name, in_specs=[vmem], out_specs=[vmem, vmem],
        out_shape=[_sds((N_DEV,) + buf.shape), _sds(buf.shape)],
        scratch_shapes=[pltpu.SemaphoreType.DMA((N_DEV - 1,)), pltpu.SemaphoreType.DMA((N_DEV - 1,))],
        compiler_params=pltpu.CompilerParams(vmem_limit_bytes=VMEM_LIMIT_BYTES),
    )(buf)[1]


def _adamw_math(w, g, m, v):
    m = ADAM_B1 * m + (1.0 - ADAM_B1) * g
    v = ADAM_B2 * v + (1.0 - ADAM_B2) * (g * g)
    m_hat = m / (1.0 - ADAM_B1 ** ADAM_STEP)
    v_hat = v / (1.0 - ADAM_B2 ** ADAM_STEP)
    return -ADAM_LR * (m_hat / (jnp.sqrt(v_hat) + ADAM_EPS) + ADAM_WD * w), m, v


def _row_tile(rows, cap=256, mult=16):
    best = None
    for t in range(mult, min(rows, cap) + 1, mult):
        if rows % t == 0:
            best = t
    assert best is not None, rows
    return best


def _adamw_sharded(w, m, v, partials, name):
    layers, rows, cols = w.shape
    tr = _row_tile(rows)

    def body(w_ref, m_ref, v_ref, p_ref, g_ref, d_ref, nm_ref, nv_ref):
        g = p_ref[0].astype(F32)
        for j in range(1, N_DEV):
            g = g + p_ref[j].astype(F32)
        delta, m_new, v_new = _adamw_math(w_ref[...], g, m_ref[...], v_ref[...])
        g_ref[...], d_ref[...], nm_ref[...], nv_ref[...] = g, delta, m_new, v_new

    blk = pl.BlockSpec((None, tr, cols), lambda l, i: (l, i, 0))
    return pl.pallas_call(
        body, name=name, grid=(layers, rows // tr),
        in_specs=[blk, blk, blk, pl.BlockSpec((N_DEV, None, tr, cols), lambda l, i: (0, l, i, 0))],
        out_specs=[blk] * 4, out_shape=[_sds(w.shape)] * 4, compiler_params=_params("parallel", "parallel"),
    )(w, m, v, partials)


def _adamw_small(w, g, m, v, name):
    def body(w_ref, g_ref, m_ref, v_ref, d_ref, nm_ref, nv_ref):
        d_ref[...], nm_ref[...], nv_ref[...] = _adamw_math(w_ref[...], g_ref[...], m_ref[...], v_ref[...])

    return pl.pallas_call(body, name=name, out_shape=[_sds(w.shape)] * 3,
                          compiler_params=_params())(w, g, m, v)


def _pack(arrays):
    flat = jnp.concatenate([a.reshape(-1).astype(F32) for a in arrays])
    pad = -flat.shape[0] % (8 * LANES)
    return jnp.pad(flat, (0, pad)).reshape(-1, LANES)


def _unpack(buf, shapes):
    flat, out, off = buf.reshape(-1), [], 0
    for shp in shapes:
        size = math.prod(shp)
        out.append(flat[off:off + size].reshape(shp))
        off += size
    return out


WEIGHTS = ["mix_norm_g", "ffn_norm_g", "ssd_w_in", "ssd_conv_w", "ssd_conv_b", "ssd_dt_bias", "ssd_a_log", "ssd_d",
           "ssd_norm_g", "ssd_w_out", "fox_w_in", "fox_b_f", "fox_q_norm_g", "fox_k_norm_g", "fox_w_out", "ffn_w_up",
           "ffn_conv_w", "ffn_conv_b", "ffn_w_down", "final_norm_g"]
BIG = ["ssd_w_in", "ssd_w_out", "fox_w_in", "fox_w_out", "ffn_w_up", "ffn_w_down"]
COLUMN_SHARDED = ["ssd_w_in", "fox_w_in", "ffn_w_up"]
CONV = ["ssd_conv_w", "ffn_conv_w"]
REPLICATED = [n for n in WEIGHTS if n not in BIG + CONV]
DEPTH = 4
ADAMW_ORDER = ["fox_w_out", "fox_w_in", "ffn_w_down", "ffn_w_up", "ssd_w_out", "ssd_w_in"]
LAYER_SHARDED = (["ssd_w_in", "ssd_conv_w", "ssd_w_out", "ffn_w_up", "ffn_conv_w", "ffn_w_down"],
                 ["fox_w_in", "fox_w_out", "ffn_w_up", "ffn_conv_w", "ffn_w_down"])


def _to_shards(full, on_columns):
    nl, r, c = full.shape
    if on_columns:
        return full.reshape(nl, r, N_DEV, c // N_DEV).transpose(2, 0, 1, 3)
    return full.reshape(nl, N_DEV, r // N_DEV, c).transpose(1, 0, 2, 3)


def _pad_cols(w):
    return jnp.pad(w, ((0, 0), (0, LANES - w.shape[1])))


def kernel(x, mix_norm_g, ffn_norm_g, ssd_w_in, ssd_conv_w, ssd_conv_b, ssd_dt_bias, ssd_a_log, ssd_d, ssd_norm_g, ssd_w_out, fox_w_in, fox_b_f, fox_q_norm_g, fox_k_norm_g, fox_w_out, ffn_w_up, ffn_conv_w, ffn_conv_b, ffn_w_down, final_norm_g, loss_target, m_mix_norm_g, m_ffn_norm_g, m_ssd_w_in, m_ssd_conv_w, m_ssd_conv_b, m_ssd_dt_bias, m_ssd_a_log, m_ssd_d, m_ssd_norm_g, m_ssd_w_out, m_fox_w_in, m_fox_b_f, m_fox_q_norm_g, m_fox_k_norm_g, m_fox_w_out, m_ffn_w_up, m_ffn_conv_w, m_ffn_conv_b, m_ffn_w_down, m_final_norm_g, v_mix_norm_g, v_ffn_norm_g, v_ssd_w_in, v_ssd_conv_w, v_ssd_conv_b, v_ssd_dt_bias, v_ssd_a_log, v_ssd_d, v_ssd_norm_g, v_ssd_w_out, v_fox_w_in, v_fox_b_f, v_fox_q_norm_g, v_fox_k_norm_g, v_fox_w_out, v_ffn_w_up, v_ffn_conv_w, v_ffn_conv_b, v_ffn_w_down, v_final_norm_g):
    given = dict(locals())
    w = {n: given[n] for n in WEIGHTS}
    mom = {n: given["m_" + n] for n in WEIGHTS}
    var = {n: given["v_" + n] for n in WEIGHTS}
    me = _my_index()

    sharded = BIG + CONV
    shards = [w[n].astype(MXU_DT) if n in BIG else w[n] for n in sharded]
    zones = [(n, i if n.startswith("ffn") else i // 2) for i in range(DEPTH) for n in LAYER_SHARDED[i % 2]]
    items = [[(sharded.index(n), functools.partial(lambda me, k, layer: (layer,), layer=layer), z,
               lambda me, k: (me,), lambda me: (pl.ds(0, N_DEV - 1),))] for z, (n, layer) in enumerate(zones)]
    empty = [lax.empty((N_DEV,) + shards[sharded.index(n)].shape[1:], shards[sharded.index(n)].dtype) for n, _ in zones]
    gather_sems, shards_kept, landing, _ = _copies_start("gather_start", shards, empty, items)

    def arrived(n, layer, after):
        z = zones.index((n, layer))
        item = items[z][0]
        stack, = _copies_wait(f"gather_wait_{n}_{layer}", shards_kept if z == len(zones) - 1 else [], [landing[z]],
                              gather_sems[z], [item[:2] + (0,) + item[3:]], after)
        stack = lax.dynamic_update_index_in_dim(stack, shards[sharded.index(n)][layer], me, 0)
        _, r, c = stack.shape
        on_columns = n in COLUMN_SHARDED or n in CONV
        return stack.transpose(1, 0, 2).reshape(r, N_DEV * c) if on_columns else stack.reshape(N_DEV * r, c)

    def mixer_params(i):
        j = i // 2
        if i % 2 == 0:
            return dict(norm_g=w["mix_norm_g"][i], conv_b=w["ssd_conv_b"][j], dt_bias=w["ssd_dt_bias"][j],
                        a_log=w["ssd_a_log"][j], d=w["ssd_d"][j], gnorm_g=w["ssd_norm_g"][j])
        return dict(norm_g=w["mix_norm_g"][i], b_f=w["fox_b_f"][j], q_norm_g=w["fox_q_norm_g"][j],
                    k_norm_g=w["fox_k_norm_g"][j])

    h = x[0]
    tape = []
    for i in range(DEPTH):
        big = _Weights(functools.partial(lambda n, after, i: arrived(n, i if n.startswith("ffn") else i // 2, after), i=i))
        mp, fp = mixer_params(i), dict(norm_g=w["ffn_norm_g"][i], conv_b=w["ffn_conv_b"][i])
        h_mid, mix_saved = (_ssd_fwd if i % 2 == 0 else _fox_fwd)(h, mp, big)
        h_out, ffn_saved = _ffn_fwd(h_mid, fp, big)
        tape.append((h, mp, mix_saved, h_mid, fp, ffn_saved))
        h = h_out
    *dh, dfinal_g, loss_part = _loss_head(h, w["final_norm_g"], loss_target[0], "loss_head")

    grads = {n: [None] * w[n].shape[0] for n in WEIGHTS if n not in BIG + ["final_norm_g"]}
    partials = {n: lax.empty((N_DEV,) + w[n].shape, MXU_DT) for n in BIG}
    in_flight = {n: [] for n in BIG}

    def send_partial(n, layer, grad):
        slots = _to_shards(grad[None], n in COLUMN_SHARDED)[:, 0]
        mine = lax.dynamic_index_in_dim(slots, me, 0, keepdims=False)
        zone = lax.dynamic_update_slice(partials[n], mine[None, None], (me, layer, 0, 0))
        item = (0, lambda me, k: (jnp.bitwise_xor(me, k),), 0,
                functools.partial(lambda me, k, layer: (me, layer), layer=layer),
                functools.partial(lambda me, layer: (pl.ds(0, N_DEV - 1), layer), layer=layer))
        sems, kept, (partials[n],), token = _copies_start(f"scatter_start_{n}_{layer}", [slots], [zone], [[item]])
        in_flight[n].append((layer, sems[0], kept, item))
        return token

    for i in reversed(range(DEPTH)):
        j = i // 2
        h_in, mp, mix_saved, h_mid, fp, ffn_saved = tape[i]
        out = _Gradients(functools.partial(lambda n, grad, i: send_partial(n, i if n.startswith("ffn") else i // 2, grad),
                                           i=i))
        dh, g = _ffn_bwd(h_mid, dh, ffn_saved, fp, out)
        grads["ffn_norm_g"][i], grads["ffn_conv_w"][i], grads["ffn_conv_b"][i] = g["norm_g"][0], g["conv_w"], g["conv_b"][0]
        if i % 2 == 0:
            dh, g = _ssd_bwd(h_in, dh, mix_saved, mp, out)
            grads["ssd_conv_w"][j] = g["conv_w"]
            for key, name in (("conv_b", "ssd_conv_b"), ("dt_bias", "ssd_dt_bias"), ("a_log", "ssd_a_log"),
                              ("d", "ssd_d"), ("gnorm_g", "ssd_norm_g")):
                grads[name][j] = g[key][0]
        else:
            dh, g = _fox_bwd(h_in, dh, mix_saved, mp, out)
            for key, name in (("b_f", "fox_b_f"), ("q_norm_g", "fox_q_norm_g"), ("k_norm_g", "fox_k_norm_g")):
                grads[name][j] = g[key][0]
        grads["mix_norm_g"][i] = g["norm_g"][0]
    grads = {n: jnp.stack(v) for n, v in grads.items()}
    grads["final_norm_g"] = dfinal_g[0]

    small_names = REPLICATED + CONV
    summed = _unpack(_allreduce_small(_pack([grads[n] for n in small_names] + [loss_part]), "allreduce_small"),
                     [grads[n].shape for n in small_names] + [(1, 1)])
    loss = summed[-1][0, 0]
    g_small = dict(zip(small_names, summed[:-1]))
    for n in CONV:
        width = w[n].shape[-1]
        g_small[n] = lax.dynamic_slice_in_dim(g_small[n], me * width, width, axis=2)
    pk = lambda d: _pack([d[n] for n in small_names])
    d_small, m_small, v_small = _adamw_small(pk(w), pk(g_small), pk(mom), pk(var), "adamw_small")
    shapes = [w[n].shape for n in small_names]
    out_g, out_d, out_m, out_v = dict(g_small), {}, {}, {}
    for dst, buf in ((out_d, d_small), (out_m, m_small), (out_v, v_small)):
        dst.update(zip(small_names, _unpack(buf, shapes)))

    after = d_small
    for n in ADAMW_ORDER:
        for layer, sems, kept, item in in_flight[n]:
            partials[n], = _copies_wait(f"scatter_wait_{n}_{layer}", kept, [partials[n]], sems, [item], after)
        out_g[n], out_d[n], out_m[n], out_v[n] = _adamw_sharded(w[n], mom[n], var[n], partials[n], "adamw_" + n)
        after = out_v[n]

    return (loss, dh[0][None], *[out_g[n] for n in WEIGHTS], *[out_d[n] for n in WEIGHTS],
            *[out_m[n] for n in WEIGHTS], *[out_v[n] for n in WEIGHTS])
```

```python
import functools
import math

import jax
import jax.numpy as jnp
from jax import lax
from jax.experimental import pallas as pl
from jax.experimental.pallas import tpu as pltpu

F32 = jnp.float32
MXU_DT = jnp.bfloat16
VMEM_LIMIT_BYTES = 56 * 1024 * 1024
LANES = 128
N_DEV = 8
MESH_IDS = pl.DeviceIdType.MESH

EPS = 1e-6
D_MODEL = 1024
SSD_D_INNER = 2048
SSD_HEAD_DIM = 64
SSD_HEADS = 32
SSD_GROUPS = 4
SSD_HPG = 8
SSD_STATE = 128
SSD_CONV = 4
SSD_CHUNK = 128
SSD_CONV_DIM = 3072
SSD_ZX = SSD_D_INNER + SSD_CONV_DIM
FOX_HEAD_DIM = 64
FOX_HEADS = 16
FOX_D = 1024
D_FF = 2816
FFN_CONV = 3
ADAM_LR, ADAM_B1, ADAM_B2, ADAM_EPS, ADAM_WD, ADAM_STEP = 0.001, 0.9, 0.999, 1e-08, 0.01, 10


def _params(*sem):
    return pltpu.CompilerParams(dimension_semantics=sem or None, vmem_limit_bytes=VMEM_LIMIT_BYTES)


def _sds(shape, dtype=F32):
    return jax.ShapeDtypeStruct(tuple(shape), dtype)


def _col_tile(n, cap=1536):
    best = None
    for t in range(LANES, min(n, cap) + 1, LANES):
        if n % t == 0:
            best = t
    assert best is not None, n
    return best


def _sigmoid(x):
    return 0.5 * jnp.tanh(0.5 * x) + 0.5


def _matmul(a, b, *, ta=False, tb=False, res=None, out_dtype=F32, tm=512, tn=None, b_kblock=0, name):
    (kdim, m) = a.shape if ta else a.shape[::-1]
    (n, k2) = b.shape if tb else b.shape[::-1]
    assert kdim == k2 or (tb and k2 % kdim == 0), (a.shape, b.shape, ta, tb)
    tm = min(tm, m)
    if m % tm:
        tm = _col_tile(m, tm)
    tn = tn or _col_tile(n)
    assert m % tm == 0 and n % tn == 0, (m, tm, n, tn)
    dims = (((0 if ta else 1,), (1 if tb else 0,)), ((), ()))

    def body(*refs):
        a_ref, b_ref = refs[0], refs[1]
        o_ref = refs[-1]
        acc = lax.dot_general(a_ref[...].astype(MXU_DT), b_ref[...].astype(MXU_DT), dims,
                              preferred_element_type=F32)
        if res is not None:
            acc = acc + refs[2][...].astype(F32)
        o_ref[...] = acc.astype(o_ref.dtype)

    a_spec = pl.BlockSpec((kdim, tm), lambda i, j: (0, i)) if ta else pl.BlockSpec((tm, kdim), lambda i, j: (i, 0))
    b_spec = pl.BlockSpec((tn, kdim), lambda i, j: (j, b_kblock)) if tb else pl.BlockSpec((kdim, tn), lambda i, j: (0, j))
    o_spec = pl.BlockSpec((tm, tn), lambda i, j: (i, j))
    ins, specs = [a, b], [a_spec, b_spec]
    if res is not None:
        ins.append(res)
        specs.append(o_spec)
    return pl.pallas_call(
        body, name=name, grid=(m // tm, n // tn), in_specs=specs, out_specs=o_spec,
        out_shape=_sds((m, n), out_dtype), compiler_params=_params("parallel", "parallel"),
    )(*ins)


def _rowwise(fn, rows, consts, out_rows, out_sums, *, tr=256, name):
    rows = [r if isinstance(r, tuple) else (r, 0, r.shape[1]) for r in rows]
    s = rows[0][0].shape[0]
    tr = min(tr, s)
    assert s % tr == 0
    n_in, n_c, n_or = len(rows), len(consts), len(out_rows)

    def body(*refs):
        ins = [r[...] for r in refs[:n_in + n_c]]
        outs = fn(*ins)
        o_refs = refs[n_in + n_c:]
        for o_ref, val in zip(o_refs[:n_or], outs[:n_or]):
            o_ref[...] = val.astype(o_ref.dtype)
        if out_sums:
            first = pl.program_id(0) == 0

            @pl.when(first)
            def _():
                for o_ref, val in zip(o_refs[n_or:], outs[n_or:]):
                    o_ref[...] = val.astype(o_ref.dtype)

            @pl.when(jnp.logical_not(first))
            def _():
                for o_ref, val in zip(o_refs[n_or:], outs[n_or:]):
                    o_ref[...] += val.astype(o_ref.dtype)

    in_specs = [pl.BlockSpec((tr, width), functools.partial(lambda i, cb: (i, cb), cb=cb)) for _, cb, width in rows]
    in_specs += [pl.BlockSpec(c.shape, lambda i: (0, 0)) for c in consts]
    out_specs = [pl.BlockSpec((tr, o.shape[1]), lambda i: (i, 0)) for o in out_rows]
    out_specs += [pl.BlockSpec(o.shape, lambda i: (0, 0)) for o in out_sums]
    return pl.pallas_call(
        body, name=name, grid=(s // tr,), in_specs=in_specs, out_specs=out_specs,
        out_shape=list(out_rows) + list(out_sums),
        compiler_params=_params("arbitrary" if out_sums else "parallel"),
    )(*[r[0] for r in rows], *consts)


def _rms_fwd(h, g, name):
    def fn(x, gv):
        r = lax.rsqrt(jnp.mean(x * x, axis=-1, keepdims=True) + EPS)
        return (x * r * gv,)
    return _rowwise(fn, [h], [g.reshape(1, -1)], [_sds(h.shape, MXU_DT)], [], name=name)[0]


def _rms_bwd(h, g, dy, dres, name):
    def fn(x, dyv, dr, gv):
        r = lax.rsqrt(jnp.mean(x * x, axis=-1, keepdims=True) + EPS)
        dyg = dyv * gv
        dx = dr + r * dyg - x * (r * r * r) * jnp.mean(x * dyg, axis=-1, keepdims=True)
        return dx, dx, jnp.sum(dyv * x * r, axis=0, keepdims=True)
    return _rowwise(fn, [h, dy, dres], [g.reshape(1, -1)], [_sds(h.shape), _sds(h.shape, MXU_DT)],
                    [_sds((1, h.shape[1]))], name=name)


def _loss_head(h, g, target, name):
    c = h.shape[1]

    def fn(x, t, gv):
        r = lax.rsqrt(jnp.mean(x * x, axis=-1, keepdims=True) + EPS)
        y = x * r * gv
        err = y - t
        dyv = err * (1.0 / c)
        dyg = dyv * gv
        dx = r * dyg - x * (r * r * r) * jnp.mean(x * dyg, axis=-1, keepdims=True)
        loss = 0.5 * jnp.sum(jnp.mean(err * err, axis=-1, keepdims=True), axis=0, keepdims=True)
        return dx, dx, jnp.sum(dyv * x * r, axis=0, keepdims=True), loss
    return _rowwise(fn, [h, target], [g.reshape(1, -1)], [_sds(h.shape), _sds(h.shape, MXU_DT)],
                    [_sds((1, c)), _sds((1, 1))], name=name)


PAD_ROWS = 8
ROW_TILE = 128


def _shifted_conv(xp_ref, w, r0, tr, kw):
    acc = None
    for k in range(kw):
        xk = xp_ref[pl.ds(PAD_ROWS + r0 - (kw - 1) + k, tr), :]
        term = xk * w[k:k + 1, :]
        acc = term if acc is None else acc + term
    return acc


def _convglu_fwd(u, conv_w, conv_b, name):
    s = u.shape[0]
    nt = D_FF // LANES
    tr = min(ROW_TILE, s)

    def body(ug_ref, uv_ref, w_ref, b_ref, act_ref, xp_ref):
        xp_ref[pl.ds(0, PAD_ROWS), :] = jnp.zeros((PAD_ROWS, LANES), F32)
        xp_ref[pl.ds(PAD_ROWS, s), :] = ug_ref[...]
        w = w_ref[...]
        b = b_ref[...]
        for r0 in range(0, s, tr):
            gate = _shifted_conv(xp_ref, w, r0, tr, FFN_CONV) + b
            act = gate * _sigmoid(gate) * uv_ref[pl.ds(r0, tr), :]
            act_ref[pl.ds(r0, tr), :] = act.astype(act_ref.dtype)

    return pl.pallas_call(
        body, name=name, grid=(nt,),
        in_specs=[pl.BlockSpec((s, LANES), lambda j: (0, j)), pl.BlockSpec((s, LANES), lambda j: (0, nt + j)),
                  pl.BlockSpec((FFN_CONV, LANES), lambda j: (0, j)), pl.BlockSpec((1, LANES), lambda j: (0, j))],
        out_specs=pl.BlockSpec((s, LANES), lambda j: (0, j)),
        out_shape=_sds((s, D_FF), MXU_DT),
        scratch_shapes=[pltpu.VMEM((s + PAD_ROWS, LANES), F32)],
        compiler_params=_params("parallel"),
    )(u, u, conv_w, conv_b.reshape(1, -1))


def _convglu_bwd(u, dact, conv_w, conv_b, name):
    s = u.shape[0]
    nt = D_FF // LANES
    tr = min(ROW_TILE, s)
    kw = FFN_CONV

    def body(ug_ref, uv_ref, da_ref, w_ref, b_ref, dug_ref, duv_ref, dw_ref, db_ref, xp_ref, dgp_ref):
        xp_ref[pl.ds(0, PAD_ROWS), :] = jnp.zeros((PAD_ROWS, LANES), F32)
        xp_ref[pl.ds(PAD_ROWS, s), :] = ug_ref[...]
        dgp_ref[pl.ds(s, PAD_ROWS), :] = jnp.zeros((PAD_ROWS, LANES), F32)
        w = w_ref[...]
        b = b_ref[...]
        dw = [jnp.zeros((1, LANES), F32) for _ in range(kw)]
        db = jnp.zeros((1, LANES), F32)
        for r0 in range(0, s, tr):
            gate = _shifted_conv(xp_ref, w, r0, tr, kw) + b
            sg = _sigmoid(gate)
            da = da_ref[pl.ds(r0, tr), :].astype(F32)
            duv_ref[pl.ds(r0, tr), :] = (da * gate * sg).astype(duv_ref.dtype)
            dgate = da * uv_ref[pl.ds(r0, tr), :] * (sg * (1.0 + gate * (1.0 - sg)))
            dgp_ref[pl.ds(r0, tr), :] = dgate
            db = db + jnp.sum(dgate, axis=0, keepdims=True)
            for k in range(kw):
                xk = xp_ref[pl.ds(PAD_ROWS + r0 - (kw - 1) + k, tr), :]
                dw[k] = dw[k] + jnp.sum(dgate * xk, axis=0, keepdims=True)
        for r0 in range(0, s, tr):
            acc = None
            for k in range(kw):
                term = dgp_ref[pl.ds(r0 + (kw - 1) - k, tr), :] * w[k:k + 1, :]
                acc = term if acc is None else acc + term
            dug_ref[pl.ds(r0, tr), :] = acc.astype(dug_ref.dtype)
        for k in range(kw):
            dw_ref[pl.ds(k, 1), :] = dw[k]
        db_ref[...] = db

    col = lambda j: (0, j)
    return pl.pallas_call(
        body, name=name, grid=(nt,),
        in_specs=[pl.BlockSpec((s, LANES), col), pl.BlockSpec((s, LANES), lambda j: (0, nt + j)),
                  pl.BlockSpec((s, LANES), col), pl.BlockSpec((kw, LANES), col), pl.BlockSpec((1, LANES), col)],
        out_specs=[pl.BlockSpec((s, LANES), col), pl.BlockSpec((s, LANES), col),
                   pl.BlockSpec((kw, LANES), col), pl.BlockSpec((1, LANES), col)],
        out_shape=[_sds((s, D_FF), MXU_DT), _sds((s, D_FF), MXU_DT), _sds((kw, D_FF)), _sds((1, D_FF))],
        scratch_shapes=[pltpu.VMEM((s + PAD_ROWS, LANES), F32), pltpu.VMEM((s + PAD_ROWS, LANES), F32)],
        compiler_params=_params("parallel"),
    )(u, u, dact, conv_w, conv_b.reshape(1, -1))


class _Weights:
    def __init__(self, fetch):
        self._fetch, self._got = fetch, {}

    def whole(self, name, after):
        if name not in self._got:
            self._got[name] = self._fetch(name, after)
        return self._got[name]


class _Gradients:
    def __init__(self, start):
        self._start, self._tokens = start, []

    def send(self, name, grad):
        token = self._start(name, grad)
        if token is not None:
            self._tokens.append(token)

    def tie(self, x):
        for token in self._tokens:
            x = x + token[0, 0]
        self._tokens = []
        return x


def _ffn_fwd(h, p, big):
    hf = _rms_fwd(h, p["norm_g"], "ffn_norm_fwd")
    w_up = big.whole("ffn_w_up", hf)
    u = _matmul(hf, w_up, tm=1024, name="ffn_up")
    conv_w = big.whole("ffn_conv_w", u)
    act = _convglu_fwd(u, conv_w, p["conv_b"], "convglu_fwd")
    w_down = big.whole("ffn_w_down", act)
    h_out = _matmul(act, w_down, res=h, name="ffn_down")
    return h_out, (hf, u, act, w_up, conv_w, w_down)


def _ffn_bwd(h, dh, saved, p, out):
    hf, u, act, w_up, conv_w, w_down = saved
    dh, dh_b = dh
    dact = _matmul(dh_b, w_down, tb=True, tm=1024, name="ffn_down_dx")
    out.send("ffn_w_down", _matmul(act, dh_b, ta=True, out_dtype=MXU_DT, name="ffn_down_dw"))
    dug, duv, dconv_w, dconv_b = _convglu_bwd(u, dact, conv_w, out.tie(p["conv_b"]), "convglu_bwd")
    dhf = _matmul(dug, w_up, tb=True, b_kblock=0, name="ffn_up_dx_gate")
    dhf = _matmul(duv, w_up, tb=True, b_kblock=1, res=dhf, name="ffn_up_dx_val")
    out.send("ffn_w_up", jnp.concatenate([_matmul(hf, dug, ta=True, out_dtype=MXU_DT, name="ffn_up_dw_gate"),
                                          _matmul(hf, duv, ta=True, out_dtype=MXU_DT, name="ffn_up_dw_val")], axis=1))
    *dh_in, dnorm_g = _rms_bwd(h, out.tie(p["norm_g"]), dhf, dh, "ffn_norm_bwd")
    return dh_in, dict(norm_g=dnorm_g, conv_w=dconv_w, conv_b=dconv_b)


def _dwconv_silu_fwd(proj, col0, n_ch, conv_w, conv_b, name):
    s = proj.shape[0]
    nt, t0, kw = n_ch // LANES, col0 // LANES, conv_w.shape[0]
    tr = min(ROW_TILE, s)

    def body(x_ref, w_ref, b_ref, o_ref, xp_ref):
        xp_ref[pl.ds(0, PAD_ROWS), :] = jnp.zeros((PAD_ROWS, LANES), F32)
        xp_ref[pl.ds(PAD_ROWS, s), :] = x_ref[...]
        w = w_ref[...]
        b = b_ref[...]
        for r0 in range(0, s, tr):
            pre = _shifted_conv(xp_ref, w, r0, tr, kw) + b
            o_ref[pl.ds(r0, tr), :] = pre * _sigmoid(pre)

    col = lambda j: (0, j)
    return pl.pallas_call(
        body, name=name, grid=(nt,),
        in_specs=[pl.BlockSpec((s, LANES), lambda j: (0, t0 + j)), pl.BlockSpec((kw, LANES), col),
                  pl.BlockSpec((1, LANES), col)],
        out_specs=pl.BlockSpec((s, LANES), col), out_shape=_sds((s, n_ch)),
        scratch_shapes=[pltpu.VMEM((s + PAD_ROWS, LANES), F32)],
        compiler_params=_params("parallel"),
    )(proj, conv_w, conv_b.reshape(1, -1))


def _dwconv_silu_bwd(proj, col0, n_ch, dout, conv_w, conv_b, name):
    s = proj.shape[0]
    nt, t0, kw = n_ch // LANES, col0 // LANES, conv_w.shape[0]
    tr = min(ROW_TILE, s)

    def body(x_ref, do_ref, w_ref, b_ref, dx_ref, dw_ref, db_ref, xp_ref, dgp_ref):
        xp_ref[pl.ds(0, PAD_ROWS), :] = jnp.zeros((PAD_ROWS, LANES), F32)
        xp_ref[pl.ds(PAD_ROWS, s), :] = x_ref[...]
        dgp_ref[pl.ds(s, PAD_ROWS), :] = jnp.zeros((PAD_ROWS, LANES), F32)
        w = w_ref[...]
        b = b_ref[...]
        dw = [jnp.zeros((1, LANES), F32) for _ in range(kw)]
        db = jnp.zeros((1, LANES), F32)
        for r0 in range(0, s, tr):
            pre = _shifted_conv(xp_ref, w, r0, tr, kw) + b
            sg = _sigmoid(pre)
            dpre = do_ref[pl.ds(r0, tr), :] * (sg * (1.0 + pre * (1.0 - sg)))
            dgp_ref[pl.ds(r0, tr), :] = dpre
            db = db + jnp.sum(dpre, axis=0, keepdims=True)
            for k in range(kw):
                xk = xp_ref[pl.ds(PAD_ROWS + r0 - (kw - 1) + k, tr), :]
                dw[k] = dw[k] + jnp.sum(dpre * xk, axis=0, keepdims=True)
        for r0 in range(0, s, tr):
            acc = None
            for k in range(kw):
                term = dgp_ref[pl.ds(r0 + (kw - 1) - k, tr), :] * w[k:k + 1, :]
                acc = term if acc is None else acc + term
            dx_ref[pl.ds(r0, tr), :] = acc.astype(dx_ref.dtype)
        for k in range(kw):
            dw_ref[pl.ds(k, 1), :] = dw[k]
        db_ref[...] = db

    col = lambda j: (0, j)
    return pl.pallas_call(
        body, name=name, grid=(nt,),
        in_specs=[pl.BlockSpec((s, LANES), lambda j: (0, t0 + j)), pl.BlockSpec((s, LANES), col),
                  pl.BlockSpec((kw, LANES), col), pl.BlockSpec((1, LANES), col)],
        out_specs=[pl.BlockSpec((s, LANES), col), pl.BlockSpec((kw, LANES), col), pl.BlockSpec((1, LANES), col)],
        out_shape=[_sds((s, n_ch), MXU_DT), _sds((kw, n_ch)), _sds((1, n_ch))],
        scratch_shapes=[pltpu.VMEM((s + PAD_ROWS, LANES), F32), pltpu.VMEM((s + PAD_ROWS, LANES), F32)],
        compiler_params=_params("parallel"),
    )(proj, dout, conv_w, conv_b.reshape(1, -1))


HIGHEST = lax.Precision.HIGHEST
PAIRS = SSD_HPG // 2
PAIR_W = 2 * SSD_HEAD_DIM
GROUP_W = SSD_HPG * SSD_HEAD_DIM


def _iota2(shape, axis):
    return lax.broadcasted_iota(jnp.int32, shape, axis)


def _lane_pad(v):
    return jnp.pad(v.reshape(1, -1), ((0, 0), (0, LANES - v.shape[0])))


def _heads_to_groups(a):
    s = a.shape[0]
    return a[:, :SSD_HEADS].reshape(s, SSD_GROUPS, SSD_HPG).transpose(1, 0, 2)


def _groups_to_heads(a):
    s = a.shape[1]
    return jnp.pad(a.transpose(1, 0, 2).reshape(s, SSD_HEADS), ((0, 0), (0, LANES - SSD_HEADS)))


def _ssd_prep(dt_raw, dt_bias, a_log, name):
    s = dt_raw.shape[0]
    lc = SSD_CHUNK

    def body(x_ref, b_ref, al_ref, dt_ref, acs_ref, acst_ref):
        x = x_ref[...] + b_ref[...]
        dt = jnp.maximum(x, 0.0) + jnp.log1p(jnp.exp(-jnp.abs(x)))
        da = dt * (-jnp.exp(al_ref[...]))
        lower = (_iota2((lc, lc), 0) >= _iota2((lc, lc), 1)).astype(F32)
        upper = (_iota2((lc, lc), 0) <= _iota2((lc, lc), 1)).astype(F32)
        dt_ref[...] = dt
        acs_ref[...] = jnp.dot(lower, da, precision=HIGHEST, preferred_element_type=F32)
        acst_ref[...] = lax.dot_general(da, upper, (((0,), (0,)), ((), ())), precision=HIGHEST,
                                        preferred_element_type=F32)

    row = pl.BlockSpec((lc, LANES), lambda c: (c, 0))
    one = pl.BlockSpec((1, LANES), lambda c: (0, 0))
    return pl.pallas_call(
        body, name=name, grid=(s // lc,), in_specs=[row, one, one],
        out_specs=[row, row, pl.BlockSpec((LANES, lc), lambda c: (0, c))],
        out_shape=[_sds((s, LANES)), _sds((s, LANES)), _sds((LANES, s))],
        compiler_params=_params("parallel"),
    )(dt_raw, dt_bias, a_log)


def _pair_cols(v, p, lo):
    return jnp.where(lo, v[:, 2 * p:2 * p + 1], v[:, 2 * p + 1:2 * p + 2])


def _decay_matrix(acs, acst, h, tri):
    return jnp.exp(jnp.where(tri, acs[:, h:h + 1] - acst[h:h + 1, :], -jnp.inf))


def _dot(a, b, ca, cb):
    return lax.dot_general(a, b, (((ca,), (cb,)), ((), ())), preferred_element_type=F32)


def _ssd_scan_fwd(xbc, dtg, acsg, acstg, d_skip, name):
    s = xbc.shape[0]
    lc, nc = SSD_CHUNK, s // SSD_CHUNK
    xt, bt = GROUP_W // LANES, SSD_D_INNER // LANES

    def body(x_ref, b_ref, c_ref, dt_ref, acs_ref, acst_ref, dsk_ref, y_ref, hp_ref, st_ref):
        @pl.when(pl.program_id(1) == 0)
        def _():
            st_ref[...] = jnp.zeros(st_ref.shape, F32)

        bm, cmb = b_ref[...], c_ref[...].astype(MXU_DT)
        dt, acs, acst = dt_ref[...], acs_ref[...], acst_ref[...]
        cb = _dot(cmb, bm.astype(MXU_DT), 1, 1)
        tri = _iota2((lc, lc), 0) >= _iota2((lc, lc), 1)
        lo = _iota2((lc, PAIR_W), 1) < SSD_HEAD_DIM
        a_last = acs[lc - 1:lc, :]
        e_acs, e_ds, e_cd = jnp.exp(acs), jnp.exp(a_last - acs), jnp.exp(a_last)
        for p in range(PAIRS):
            sl = pl.ds(p * PAIR_W, PAIR_W)
            xp = x_ref[:, sl]
            ub = (xp * _pair_cols(dt, p, lo)).astype(MXU_DT)
            m0 = (cb * _decay_matrix(acs, acst, 2 * p, tri)).astype(MXU_DT)
            m1 = (cb * _decay_matrix(acs, acst, 2 * p + 1, tri)).astype(MXU_DT)
            ht = st_ref[p]
            hp_ref[p] = ht
            y = jnp.where(lo, _dot(m0, ub, 1, 0), _dot(m1, ub, 1, 0))
            y = y + _dot(cmb, ht.astype(MXU_DT), 1, 0) * _pair_cols(e_acs, p, lo)
            y_ref[:, sl] = y + xp * dsk_ref[:, sl]
            bd0 = (bm * e_ds[:, 2 * p:2 * p + 1]).astype(MXU_DT)
            bd1 = (bm * e_ds[:, 2 * p + 1:2 * p + 2]).astype(MXU_DT)
            st_ref[p] = ht * _pair_cols(e_cd, p, lo[:1]) + jnp.where(lo, _dot(bd0, ub, 0, 0), _dot(bd1, ub, 0, 0))

    small = pl.BlockSpec((None, lc, SSD_HPG), lambda g, c: (g, c, 0))
    return pl.pallas_call(
        body, name=name, grid=(SSD_GROUPS, nc),
        in_specs=[pl.BlockSpec((lc, GROUP_W), lambda g, c: (c, g)),
                  pl.BlockSpec((lc, LANES), lambda g, c: (c, bt + g)),
                  pl.BlockSpec((lc, LANES), lambda g, c: (c, bt + SSD_GROUPS + g)),
                  small, small, pl.BlockSpec((None, SSD_HPG, lc), lambda g, c: (g, 0, c)),
                  pl.BlockSpec((1, GROUP_W), lambda g, c: (0, g))],
        out_specs=[pl.BlockSpec((lc, GROUP_W), lambda g, c: (c, g)),
                   pl.BlockSpec((None, PAIRS, SSD_STATE, PAIR_W), lambda g, c: (c, g, 0, 0))],
        out_shape=[_sds((s, SSD_D_INNER)), _sds((nc, SSD_GROUPS * PAIRS, SSD_STATE, PAIR_W))],
        scratch_shapes=[pltpu.VMEM((PAIRS, SSD_STATE, PAIR_W), F32)],
        compiler_params=_params("parallel", "arbitrary"),
    )(xbc, xbc, xbc, dtg, acsg, acstg, d_skip)


def _ssd_scan_bwd(xbc, dtg, acsg, acstg, d_skip, dy, hprev, name):
    s = xbc.shape[0]
    lc, nc = SSD_CHUNK, s // SSD_CHUNK
    bt = SSD_D_INNER // LANES

    def body(x_ref, b_ref, c_ref, dt_ref, acs_ref, acst_ref, dsk_ref, dy_ref, hp_ref, hn_ref,
             dx_ref, db_ref, dc_ref, daq_ref, dar_ref, ddtx_ref, dd_ref, dst_ref, ta_ref, tx_ref):
        @pl.when(pl.program_id(1) == 0)
        def _():
            dst_ref[...] = jnp.zeros(dst_ref.shape, F32)
            dd_ref[...] = jnp.zeros(dd_ref.shape, F32)

        bm, cmb = b_ref[...], c_ref[...].astype(MXU_DT)
        bmb = bm.astype(MXU_DT)
        dt, acs, acst = dt_ref[...], acs_ref[...], acst_ref[...]
        cb = _dot(cmb, bmb, 1, 1)
        tri = _iota2((lc, lc), 0) >= _iota2((lc, lc), 1)
        lo = _iota2((lc, PAIR_W), 1) < SSD_HEAD_DIM
        a_last = acs[lc - 1:lc, :]
        e_acs, e_ds, e_cd = jnp.exp(acs), jnp.exp(a_last - acs), jnp.exp(a_last)
        dcb = jnp.zeros((lc, lc), F32)
        dc_x = jnp.zeros((lc, SSD_STATE), F32)
        db_x = jnp.zeros((lc, SSD_STATE), F32)
        da_in = jnp.zeros((lc, LANES), F32)
        da_out = jnp.zeros((SSD_HPG, lc), F32)
        head_col = _iota2((lc, LANES), 1)
        head_row = _iota2((SSD_HPG, lc), 0)
        last = _iota2((SSD_HPG, lc), 1) == lc - 1
        for p in range(PAIRS):
            sl = pl.ds(p * PAIR_W, PAIR_W)
            xp, dyp, dsk = x_ref[:, sl], dy_ref[:, sl], dsk_ref[:, sl]
            dtp = _pair_cols(dt, p, lo)
            u = xp * dtp
            ub, dyb = u.astype(MXU_DT), dyp.astype(MXU_DT)
            lmat = (_decay_matrix(acs, acst, 2 * p, tri), _decay_matrix(acs, acst, 2 * p + 1, tri))
            m0, m1 = (cb * lmat[0]).astype(MXU_DT), (cb * lmat[1]).astype(MXU_DT)
            ea, dsl = _pair_cols(e_acs, p, lo), _pair_cols(e_ds, p, lo)
            dht, ht = dst_ref[p], hp_ref[p]
            dhtb, htb = dht.astype(MXU_DT), ht.astype(MXU_DT)
            bd0 = (bm * e_ds[:, 2 * p:2 * p + 1]).astype(MXU_DT)
            bd1 = (bm * e_ds[:, 2 * p + 1:2 * p + 2]).astype(MXU_DT)
            du_state = jnp.where(lo, _dot(bd0, dhtb, 1, 0), _dot(bd1, dhtb, 1, 0))
            du = jnp.where(lo, _dot(m0, dyb, 0, 0), _dot(m1, dyb, 0, 0)) + du_state
            y_off = _dot(cmb, htb, 1, 0) * ea
            ta_ref[:, sl] = dyp * y_off - u * du_state
            tx_ref[:, sl] = du * xp
            dx_ref[:, sl] = dtp * du + dsk * dyp
            dd_ref[:, sl] += jnp.sum(dyp * xp, axis=0, keepdims=True)
            dy_h = (jnp.where(lo, dyp, 0.0).astype(MXU_DT), jnp.where(lo, 0.0, dyp).astype(MXU_DT))
            carry = jnp.sum(dht * hn_ref[p], axis=0, keepdims=True)
            for hh in range(2):
                h = 2 * p + hh
                dml = _dot(dy_h[hh], ub, 1, 1) * lmat[hh]
                dcb = dcb + dml
                flow = cb * dml
                da_in = da_in + jnp.where(head_col == h, jnp.sum(flow, axis=1, keepdims=True), 0.0)
                through = jnp.sum(jnp.where(lo[:1] == (hh == 0), carry, 0.0), axis=1, keepdims=True)
                da_out = da_out + jnp.where(head_row == h, jnp.sum(flow, axis=0, keepdims=True)
                                            - jnp.where(last, through, 0.0), 0.0)
            dye = (dyp * ea).astype(MXU_DT)
            dc_x = dc_x + _dot(dye, htb, 1, 1)
            db_x = db_x + _dot((u * dsl).astype(MXU_DT), dhtb, 1, 1)
            dst_ref[p] = dht * _pair_cols(e_cd, p, lo[:1]) + _dot(cmb, dye, 0, 0)
        dcbb = dcb.astype(MXU_DT)
        dc_ref[...] = _dot(dcbb, bmb, 1, 0) + dc_x
        db_ref[...] = _dot(dcbb, cmb, 0, 0) + db_x
        seg_lo = _iota2((GROUP_W, LANES), 1) * SSD_HEAD_DIM
        chan = _iota2((GROUP_W, LANES), 0)
        seg = jnp.logical_and(chan >= seg_lo, chan < seg_lo + SSD_HEAD_DIM).astype(F32)
        da_in = da_in + jnp.dot(ta_ref[...], seg, precision=HIGHEST, preferred_element_type=F32)
        daq_ref[...] = da_in[:, :SSD_HPG]
        dar_ref[...] = da_out
        ddtx_ref[...] = jnp.dot(tx_ref[...], seg, precision=HIGHEST, preferred_element_type=F32)[:, :SSD_HPG]

    rev = lambda c: nc - 1 - c
    small = pl.BlockSpec((None, lc, SSD_HPG), lambda g, c: (g, rev(c), 0))
    small_t = pl.BlockSpec((None, SSD_HPG, lc), lambda g, c: (g, 0, rev(c)))
    wide = pl.BlockSpec((lc, GROUP_W), lambda g, c: (rev(c), g))
    state = lambda at: pl.BlockSpec((None, PAIRS, SSD_STATE, PAIR_W), lambda g, c: (at(c), g, 0, 0))
    return pl.pallas_call(
        body, name=name, grid=(SSD_GROUPS, nc),
        in_specs=[pl.BlockSpec((lc, GROUP_W), lambda g, c: (rev(c), g)),
                  pl.BlockSpec((lc, LANES), lambda g, c: (rev(c), bt + g)),
                  pl.BlockSpec((lc, LANES), lambda g, c: (rev(c), bt + SSD_GROUPS + g)),
                  small, small, small_t, pl.BlockSpec((1, GROUP_W), lambda g, c: (0, g)), wide,
                  state(rev), state(lambda c: jnp.minimum(rev(c) + 1, nc - 1))],
        out_specs=[wide, pl.BlockSpec((lc, LANES), lambda g, c: (rev(c), g)),
                   pl.BlockSpec((lc, LANES), lambda g, c: (rev(c), g)), small, small_t, small,
                   pl.BlockSpec((1, GROUP_W), lambda g, c: (0, g))],
        out_shape=[_sds((s, SSD_D_INNER)), _sds((s, SSD_GROUPS * SSD_STATE)), _sds((s, SSD_GROUPS * SSD_STATE)),
                   _sds((SSD_GROUPS, s, SSD_HPG)), _sds((SSD_GROUPS, SSD_HPG, s)), _sds((SSD_GROUPS, s, SSD_HPG)),
                   _sds((1, SSD_D_INNER))],
        scratch_shapes=[pltpu.VMEM((PAIRS, SSD_STATE, PAIR_W), F32), pltpu.VMEM((lc, GROUP_W), F32),
                        pltpu.VMEM((lc, GROUP_W), F32)],
        compiler_params=_params("parallel", "arbitrary"),
    )(xbc, xbc, xbc, dtg, acsg, acstg, d_skip, dy, hprev, hprev)


def _ssd_post(da_in, da_out, ddtx, dt, dt_raw, dt_bias, a_log, name):
    s = da_in.shape[0]
    lc = SSD_CHUNK

    def body(dain_ref, daout_ref, ddtx_ref, dt_ref, x_ref, b_ref, al_ref, ddr_ref, dal_ref, dbias_ref):
        @pl.when(pl.program_id(0) == 0)
        def _():
            dal_ref[...] = jnp.zeros(dal_ref.shape, F32)
            dbias_ref[...] = jnp.zeros(dbias_ref.shape, F32)

        upper = (_iota2((lc, lc), 0) <= _iota2((lc, lc), 1)).astype(F32)
        dda = jnp.dot(upper, dain_ref[...] - daout_ref[...], precision=HIGHEST, preferred_element_type=F32)
        a = -jnp.exp(al_ref[...])
        ddt = dda * a + ddtx_ref[...]
        dal_ref[...] += jnp.sum(dda * dt_ref[...], axis=0, keepdims=True) * a
        ddr = ddt * _sigmoid(x_ref[...] + b_ref[...])
        ddr_ref[...] = ddr.astype(ddr_ref.dtype)
        dbias_ref[...] += jnp.sum(ddr, axis=0, keepdims=True)

    row = pl.BlockSpec((lc, LANES), lambda i: (i, 0))
    one = pl.BlockSpec((1, LANES), lambda i: (0, 0))
    return pl.pallas_call(
        body, name=name, grid=(s // lc,), in_specs=[row, row, row, row, row, one, one], out_specs=[row, one, one],
        out_shape=[_sds((s, LANES), MXU_DT), _sds((1, LANES)), _sds((1, LANES))],
        compiler_params=_params("arbitrary"),
    )(da_in, da_out, ddtx, dt, dt_raw, dt_bias, a_log)


NORM_GROUP_W = SSD_D_INNER // SSD_GROUPS


def _group_rstd(yz):
    return [lax.rsqrt(jnp.mean(jnp.square(yz[:, g * NORM_GROUP_W:(g + 1) * NORM_GROUP_W]), axis=-1, keepdims=True) + EPS)
            for g in range(SSD_GROUPS)]


def _gated_norm_fwd(y, proj, norm_g, name):
    def fn(yv, z, gv):
        yz = yv * (z * _sigmoid(z))
        parts = [yz[:, g * NORM_GROUP_W:(g + 1) * NORM_GROUP_W] * r for g, r in enumerate(_group_rstd(yz))]
        return (jnp.concatenate(parts, axis=1) * gv,)
    return _rowwise(fn, [y, (proj, 0, SSD_D_INNER)], [norm_g.reshape(1, -1)], [_sds(y.shape, MXU_DT)], [],
                    name=name)[0]


def _gated_norm_bwd(y, proj, norm_g, dout, name):
    def fn(yv, z, do, gv):
        sg = _sigmoid(z)
        sz = z * sg
        yz = yv * sz
        dog = do * gv
        dyz, dg = [], []
        for g, r in enumerate(_group_rstd(yz)):
            cols = slice(g * NORM_GROUP_W, (g + 1) * NORM_GROUP_W)
            yzg, dogg = yz[:, cols], dog[:, cols]
            dyz.append(r * dogg - yzg * (r * r * r) * jnp.mean(yzg * dogg, axis=-1, keepdims=True))
            dg.append(jnp.sum(do[:, cols] * yzg * r, axis=0, keepdims=True))
        dyz = jnp.concatenate(dyz, axis=1)
        return dyz * sz, dyz * yv * (sg * (1.0 + z * (1.0 - sg))), jnp.concatenate(dg, axis=1)
    return _rowwise(fn, [y, (proj, 0, SSD_D_INNER), dout], [norm_g.reshape(1, -1)],
                    [_sds(y.shape), _sds(y.shape, MXU_DT)], [_sds((1, y.shape[1]))], name=name)


def _ssd_fwd(h, p, big):
    hn = _rms_fwd(h, p["norm_g"], "mix_norm_fwd")
    w_in = big.whole("ssd_w_in", hn)
    w_zx, w_dt = w_in[:, :SSD_ZX], _pad_cols(w_in[:, SSD_ZX:])
    proj = _matmul(hn, w_zx, tm=1024, name="ssd_in_zx")
    dt_raw = _matmul(hn, w_dt, name="ssd_in_dt")
    conv_w = big.whole("ssd_conv_w", proj)
    xbc = _dwconv_silu_fwd(proj, SSD_D_INNER, SSD_CONV_DIM, conv_w, p["conv_b"], "ssd_conv_fwd")
    dt, acs, acst = _ssd_prep(dt_raw, _lane_pad(p["dt_bias"]), _lane_pad(p["a_log"]), "ssd_prep")
    dtg, acsg = _heads_to_groups(dt), _heads_to_groups(acs)
    acstg = acst[:SSD_HEADS].reshape(SSD_GROUPS, SSD_HPG, -1)
    d_skip = jnp.repeat(p["d"], SSD_HEAD_DIM).reshape(1, -1)
    y, hprev = _ssd_scan_fwd(xbc, dtg, acsg, acstg, d_skip, "ssd_scan_fwd")
    yn = _gated_norm_fwd(y, proj, p["gnorm_g"], "ssd_gnorm_fwd")
    w_out = big.whole("ssd_w_out", yn)
    h_out = _matmul(yn, w_out, res=h, name="ssd_out")
    return h_out, (hn, proj, dt_raw, xbc, dt, dtg, acsg, acstg, d_skip, y, hprev, yn, w_zx, w_dt, conv_w, w_out)


def _ssd_bwd(h, dh, saved, p, out):
    hn, proj, dt_raw, xbc, dt, dtg, acsg, acstg, d_skip, y, hprev, yn, w_zx, w_dt, conv_w, w_out = saved
    dh, dh_b = dh
    dyn = _matmul(dh_b, w_out, tb=True, name="ssd_out_dx")
    out.send("ssd_w_out", _matmul(yn, dh_b, ta=True, out_dtype=MXU_DT, name="ssd_out_dw"))
    dy, dz, dgnorm = _gated_norm_bwd(y, proj, out.tie(p["gnorm_g"]), dyn, "ssd_gnorm_bwd")
    dx, dbm, dcm, daq, dar, ddtx, dd = _ssd_scan_bwd(xbc, dtg, acsg, acstg, d_skip, dy, hprev, "ssd_scan_bwd")
    dxbc = jnp.concatenate([dx, dbm, dcm], axis=1)
    dpre, dconv_w, dconv_b = _dwconv_silu_bwd(proj, SSD_D_INNER, SSD_CONV_DIM, dxbc, conv_w, p["conv_b"],
                                              "ssd_conv_bwd")
    ddr, dalog, dbias = _ssd_post(_groups_to_heads(daq), _groups_to_heads(dar.transpose(0, 2, 1)),
                                  _groups_to_heads(ddtx), dt, dt_raw,
                                  _lane_pad(p["dt_bias"]), _lane_pad(p["a_log"]), "ssd_post")
    w_z, w_x = w_zx[:, :SSD_D_INNER], w_zx[:, SSD_D_INNER:]
    dhn = _matmul(dz, w_z, tb=True, name="ssd_in_dx_z")
    dhn = _matmul(dpre, w_x, tb=True, res=dhn, name="ssd_in_dx_x")
    dhn = _matmul(ddr, w_dt, tb=True, res=dhn, name="ssd_in_dx_dt")
    out.send("ssd_w_in", jnp.concatenate(
        [_matmul(hn, dz, ta=True, out_dtype=MXU_DT, name="ssd_in_dw_z"),
         _matmul(hn, dpre, ta=True, out_dtype=MXU_DT, name="ssd_in_dw_x"),
         _matmul(hn, ddr, ta=True, out_dtype=MXU_DT, name="ssd_in_dw_dt")[:, :SSD_HEADS]], axis=1))
    *dh_in, dnorm_g = _rms_bwd(h, out.tie(p["norm_g"]), dhn, dh, "mix_norm_bwd")
    grads = dict(norm_g=dnorm_g, conv_w=dconv_w, conv_b=dconv_b, dt_bias=dbias[:, :SSD_HEADS],
                 a_log=dalog[:, :SSD_HEADS], d=dd.reshape(SSD_HEADS, SSD_HEAD_DIM).sum(axis=1).reshape(1, -1),
                 gnorm_g=dgnorm)
    return dh_in, grads


FOX_PAIRS = FOX_HEADS // 2
ATT_TQ = 512
ATT_BWD_TQ = 1024
ATT_TK = 512
ATT_SUB = 512
FOX_PREP_ROWS = 256
NEG_BIG = -1e30
FOX_SCALE = FOX_HEAD_DIM ** -0.5
FOX_AUG_D = FOX_HEADS * LANES
QSIDE = FOX_HEAD_DIM
KSIDE = FOX_HEAD_DIM + 3


def _split3(x):
    a = x.astype(MXU_DT).astype(F32)
    b = (x - a).astype(MXU_DT).astype(F32)
    return a, b, (x - a - b).astype(MXU_DT).astype(F32)


def _head_tiles(pair_tile):
    return pair_tile, pltpu.roll(pair_tile, FOX_HEAD_DIM, 1)


def _fill_lanes(base, lane, first, values):
    for i, v in enumerate(values):
        base = jnp.where(lane == first + i, v, base)
    return base


def _pair_tile(lane, tile0, tile1):
    return jnp.where(lane < FOX_HEAD_DIM, tile0, pltpu.roll(tile1, FOX_HEAD_DIM, 1))


def _compact_heads(a, lane):
    return jnp.concatenate([_pair_tile(lane, a[:, 2 * j * LANES:(2 * j + 1) * LANES],
                                       a[:, (2 * j + 1) * LANES:(2 * j + 2) * LANES]) for j in range(FOX_PAIRS)], axis=1)


def _head_sum_matrix():
    return (_iota2((LANES, LANES), 0) < FOX_HEAD_DIM) == (_iota2((LANES, LANES), 1) < FOX_HEAD_DIM)


def _head_sums(x):
    bd = _head_sum_matrix().astype(F32)
    parts = [jnp.dot(x[:, j * LANES:(j + 1) * LANES], bd, precision=HIGHEST, preferred_element_type=F32)
             for j in range(x.shape[1] // LANES)]
    return parts[0] if len(parts) == 1 else jnp.concatenate(parts, axis=1)


def _fox_prep_fwd(proj, f_raw, qg, kg, b_f, name):
    s = proj.shape[0]
    tr = min(FOX_PREP_ROWS, s)

    def body(q_ref, k_ref, v_ref, f_ref, qg_ref, kg_ref, b_ref, qa_ref, ka_ref, va_ref, carry_ref):
        @pl.when(pl.program_id(0) == 0)
        def _():
            carry_ref[...] = jnp.zeros(carry_ref.shape, F32)

        normed = []
        for x_ref, g_ref in ((q_ref, qg_ref), (k_ref, kg_ref)):
            x = x_ref[...]
            r = lax.rsqrt(_head_sums(x * x) * (1.0 / FOX_HEAD_DIM) + EPS)
            normed.append(x * r * g_ref[...])
        qn, kn, v = normed[0] * FOX_SCALE, normed[1], v_ref[...]
        x = f_ref[...] + b_ref[...]
        lf = jnp.minimum(x, 0.0) - jnp.log1p(jnp.exp(-jnp.abs(x)))
        lower = (_iota2((tr, tr), 0) >= _iota2((tr, tr), 1)).astype(F32)
        cum = jnp.dot(lower, lf, precision=HIGHEST, preferred_element_type=F32) + carry_ref[...]
        carry_ref[...] += jnp.sum(lf, axis=0, keepdims=True)
        first = _iota2((LANES, FOX_D), 0) * FOX_HEAD_DIM
        chan = _iota2((LANES, FOX_D), 1)
        spread = jnp.logical_and(chan >= first, chan < first + FOX_HEAD_DIM).astype(F32)
        cum = jnp.dot(cum, spread, precision=HIGHEST, preferred_element_type=F32)
        lane = _iota2((tr, LANES), 1)
        ones = jnp.where(jnp.logical_and(lane >= QSIDE, lane < KSIDE + 3), 1.0, 0.0)
        for j in range(FOX_PAIRS):
            cols = slice(j * LANES, (j + 1) * LANES)
            tiles = zip(_head_tiles(qn[:, cols]), _head_tiles(kn[:, cols]), _head_tiles(v[:, cols]),
                        reversed(_head_tiles(cum[:, cols])))
            for hh, (qt, kt, vt, ct) in enumerate(tiles):
                out = slice((2 * j + hh) * LANES, (2 * j + hh + 1) * LANES)
                c3 = _split3(ct)
                head = lane < FOX_HEAD_DIM
                qa_ref[:, out] = _fill_lanes(jnp.where(head, qt, ones), lane, QSIDE, c3).astype(qa_ref.dtype)
                ka_ref[:, out] = _fill_lanes(jnp.where(head, kt, ones), lane, KSIDE, [-c for c in c3]).astype(ka_ref.dtype)
                va_ref[:, out] = jnp.where(head, vt, jnp.where(lane < KSIDE, 1.0, 0.0)).astype(va_ref.dtype)

    wide = lambda cb: pl.BlockSpec((tr, FOX_D), lambda i: (i, cb))
    aug = pl.BlockSpec((tr, FOX_AUG_D), lambda i: (i, 0))
    one = lambda n: pl.BlockSpec((1, n), lambda i: (0, 0))
    return pl.pallas_call(
        body, name=name, grid=(s // tr,),
        in_specs=[wide(0), wide(1), wide(2), pl.BlockSpec((tr, LANES), lambda i: (i, 0)), one(FOX_D), one(FOX_D),
                  one(LANES)],
        out_specs=[aug, aug, aug], out_shape=[_sds((s, FOX_AUG_D), MXU_DT)] * 3,
        scratch_shapes=[pltpu.VMEM((1, LANES), F32)],
        compiler_params=_params("arbitrary"),
    )(proj, proj, proj, f_raw, qg, kg, b_f)


def _fox_prep_bwd(proj, f_raw, qg, kg, b_f, dqa, dka, name):
    s = proj.shape[0]
    tr = min(FOX_PREP_ROWS, s)
    nb = s // tr

    def body(q_ref, k_ref, f_ref, qg_ref, kg_ref, b_ref, dqa_ref, dka_ref,
             dq_ref, dk_ref, df_ref, dqg_ref, dkg_ref, db_ref, carry_ref):
        @pl.when(pl.program_id(0) == 0)
        def _():
            carry_ref[...] = jnp.zeros(carry_ref.shape, F32)
            dqg_ref[...] = jnp.zeros(dqg_ref.shape, F32)
            dkg_ref[...] = jnp.zeros(dkg_ref.shape, F32)
            db_ref[...] = jnp.zeros(db_ref.shape, F32)

        lane = _iota2((tr, LANES), 1)
        for x_ref, g_ref, dt_ref, scale, dx_ref, dg_ref in ((q_ref, qg_ref, dqa_ref, FOX_SCALE, dq_ref, dqg_ref),
                                                            (k_ref, kg_ref, dka_ref, 1.0, dk_ref, dkg_ref)):
            x, dy = x_ref[...], _compact_heads(dt_ref[...], lane) * scale
            r = lax.rsqrt(_head_sums(x * x) * (1.0 / FOX_HEAD_DIM) + EPS)
            dyg = dy * g_ref[...]
            dx = r * dyg - x * (r * r * r) * (_head_sums(x * dyg) * (1.0 / FOX_HEAD_DIM))
            dx_ref[...] = dx.astype(dx_ref.dtype)
            dg_ref[...] += jnp.sum(dy * x * r, axis=0, keepdims=True)
        dc = jnp.zeros((tr, LANES), F32)
        for h in range(FOX_HEADS):
            sums = dqa_ref[:, pl.ds(h * LANES + QSIDE, 1)] - dka_ref[:, pl.ds(h * LANES + KSIDE, 1)]
            dc = jnp.where(lane == h, sums, dc)
        upper = (_iota2((tr, tr), 0) <= _iota2((tr, tr), 1)).astype(F32)
        dlf = jnp.dot(upper, dc, precision=HIGHEST, preferred_element_type=F32) + carry_ref[...]
        carry_ref[...] += jnp.sum(dc, axis=0, keepdims=True)
        df = dlf * _sigmoid(-(f_ref[...] + b_ref[...]))
        df_ref[...] = df.astype(df_ref.dtype)
        db_ref[...] += jnp.sum(df, axis=0, keepdims=True)

    wide = lambda cb: pl.BlockSpec((tr, FOX_D), lambda i: (nb - 1 - i, cb))
    aug = pl.BlockSpec((tr, FOX_AUG_D), lambda i: (nb - 1 - i, 0))
    row = pl.BlockSpec((tr, LANES), lambda i: (nb - 1 - i, 0))
    one = lambda n: pl.BlockSpec((1, n), lambda i: (0, 0))
    return pl.pallas_call(
        body, name=name, grid=(nb,),
        in_specs=[wide(0), wide(1), row, one(FOX_D), one(FOX_D), one(LANES), aug, aug],
        out_specs=[wide(0), wide(0), row, one(FOX_D), one(FOX_D), one(LANES)],
        out_shape=[_sds((s, FOX_D), MXU_DT), _sds((s, FOX_D), MXU_DT), _sds((s, LANES), MXU_DT),
                   _sds((1, FOX_D)), _sds((1, FOX_D)), _sds((1, LANES))],
        scratch_shapes=[pltpu.VMEM((1, LANES), F32)],
        compiler_params=_params("arbitrary"),
    )(proj, proj, f_raw, qg, kg, b_f, dqa, dka)


def _fox_attn_fwd(qa, ka, va, proj, name):
    s = qa.shape[0]
    tq, tk = min(ATT_TQ, s), min(ATT_TK, s)
    assert s % tq == 0 and s % tk == 0
    gt = 3 * FOX_D // LANES
    head_lanes = [slice(hh * LANES, (hh + 1) * LANES) for hh in range(2)]

    def body(qa_ref, ka_ref, va_ref, g_ref, o_ref, og_ref, qb_ref):
        qi = pl.program_id(1)
        sub = min(ATT_SUB, tq)
        lane = _iota2((sub, LANES), 1)
        ahead = _iota2((sub, tk), 0) - _iota2((sub, tk), 1)
        chains = [(hh, r0) for hh in range(2) for r0 in range(0, tq, sub)]
        q = [qa_ref[pl.ds(r0, sub), head_lanes[hh]] for hh, r0 in chains]

        def kv_step(j, carry, masked):
            rows = pl.ds(pl.multiple_of(j * tk, tk), tk)
            out = []
            for c, (hh, r0) in enumerate(chains):
                m, acc = carry[2 * c:2 * c + 2]
                sc = _dot(q[c], ka_ref[rows, head_lanes[hh]], 1, 1)
                if masked:
                    sc = jnp.where(ahead >= j * tk - qi * tq - r0, sc, NEG_BIG)
                m_new = jnp.maximum(m, jnp.max(sc, axis=1, keepdims=True))
                pr = jnp.exp(sc - m_new).astype(MXU_DT)
                out += [m_new, jnp.exp(m - m_new) * acc + _dot(pr, va_ref[rows, head_lanes[hh]], 1, 0)]
            return tuple(out)

        n_clear = lax.div(qi * tq, tk)
        n_all = lax.div((qi + 1) * tq + tk - 1, tk)
        init = (jnp.full((sub, 1), NEG_BIG, F32), jnp.zeros((sub, LANES), F32)) * len(chains)
        carry = lax.fori_loop(0, n_clear, functools.partial(kv_step, masked=False), init)
        carry = lax.fori_loop(n_clear, n_all, functools.partial(kv_step, masked=True), carry)
        heads = [[], []]
        for c, (hh, r0) in enumerate(chains):
            m, acc = carry[2 * c:2 * c + 2]
            l = acc[:, QSIDE:QSIDE + 1]
            heads[hh].append(acc / l)
            qf = q[c].astype(F32)
            bias = qf[:, QSIDE:QSIDE + 1] + qf[:, QSIDE + 1:QSIDE + 2] + qf[:, QSIDE + 2:QSIDE + 3]
            qb_ref[pl.ds(r0, sub), head_lanes[hh]] = _fill_lanes(
                qf, lane, QSIDE, _split3(bias - (m + jnp.log(l)))).astype(qb_ref.dtype)
        heads = [jnp.concatenate(h, axis=0) for h in heads]
        o = _pair_tile(_iota2((tq, LANES), 1), heads[0], heads[1])
        o_ref[...] = o
        og_ref[...] = (o * _sigmoid(g_ref[...])).astype(og_ref.dtype)

    blk2 = pl.BlockSpec((tq, 2 * LANES), lambda p, i: (i, p))
    seq2 = pl.BlockSpec((s, 2 * LANES), lambda p, i: (0, p))
    blk = pl.BlockSpec((tq, LANES), lambda p, i: (i, p))
    return pl.pallas_call(
        body, name=name, grid=(FOX_PAIRS, s // tq),
        in_specs=[blk2, seq2, seq2, pl.BlockSpec((tq, LANES), lambda p, i: (i, gt + p))],
        out_specs=[blk, blk, blk2],
        out_shape=[_sds((s, FOX_D)), _sds((s, FOX_D), MXU_DT), _sds((s, FOX_AUG_D), MXU_DT)],
        compiler_params=_params("parallel", "parallel"),
    )(qa, ka, va, proj)


def _fox_gate_bwd(dog, o, proj, name):
    def fn(dogv, ov, gate):
        sg = _sigmoid(gate)
        do = dogv * sg
        delta = _head_sums(do * ov)
        lane = _iota2((do.shape[0], LANES), 1)
        tiles = []
        for j in range(FOX_PAIRS):
            cols = slice(j * LANES, (j + 1) * LANES)
            for dt, dl in zip(_head_tiles(do[:, cols]), reversed(_head_tiles(delta[:, cols]))):
                tiles.append(_fill_lanes(jnp.where(lane < FOX_HEAD_DIM, dt, 0.0), lane, QSIDE,
                                         [-d for d in _split3(dl)]))
        return dogv * ov * sg * (1.0 - sg), jnp.concatenate(tiles, axis=1)
    return _rowwise(fn, [dog, o, (proj, 3, FOX_D)], [], [_sds(o.shape, MXU_DT), _sds((o.shape[0], FOX_AUG_D), MXU_DT)],
                    [], name=name)


def _fox_attn_bwd(qb, ka, va, doa, name):
    s = qb.shape[0]
    tq, tk = min(ATT_BWD_TQ, s), min(ATT_TK, s)
    nq, nk = s // tq, s // tk
    head_lanes = [slice(hh * LANES, (hh + 1) * LANES) for hh in range(2)]

    def body(qb_ref, doa_ref, ka_ref, va_ref, dqa_ref, dka_ref, dv_ref):
        kj = pl.program_id(1)

        @pl.when(kj == 0)
        def _():
            dqa_ref[...] = jnp.zeros(dqa_ref.shape, F32)

        ahead = _iota2((tq, tk), 0) - _iota2((tq, tk), 1)
        kb = [ka_ref[:, hs] for hs in head_lanes]
        vb = [va_ref[:, hs] for hs in head_lanes]

        def q_step(i, carry, masked):
            rows = pl.ds(pl.multiple_of(i * tq, tq), tq)
            out = []
            for hh, hs in enumerate(head_lanes):
                dk, dv = carry[2 * hh:2 * hh + 2]
                q, do = qb_ref[rows, hs], doa_ref[rows, hs]
                pr = jnp.exp(_dot(q, kb[hh], 1, 1))
                if masked:
                    pr = jnp.where(ahead >= kj * tk - i * tq, pr, 0.0)
                dv = dv + _dot(pr.astype(MXU_DT), do, 0, 0)
                ds = (pr * _dot(do, vb[hh], 1, 1)).astype(MXU_DT)
                dqa_ref[rows, hs] += _dot(ds, kb[hh], 1, 0)
                out += [dk + _dot(ds, q, 0, 0), dv]
            return tuple(out)

        first = lax.div(kj * tk, tq)
        n_masked = lax.div((kj + 1) * tk + tq - 1, tq)
        carry = lax.fori_loop(first, n_masked, functools.partial(q_step, masked=True),
                              (jnp.zeros((tk, LANES), F32),) * 4)
        dk0, dv0, dk1, dv1 = lax.fori_loop(n_masked, nq, functools.partial(q_step, masked=False), carry)
        dka_ref[:, head_lanes[0]] = dk0
        dka_ref[:, head_lanes[1]] = dk1
        dv_ref[...] = _pair_tile(_iota2((tk, LANES), 1), dv0, dv1).astype(dv_ref.dtype)

    seq2 = pl.BlockSpec((s, 2 * LANES), lambda p, j: (0, p))
    blk2 = pl.BlockSpec((tk, 2 * LANES), lambda p, j: (j, p))
    return pl.pallas_call(
        body, name=name, grid=(FOX_PAIRS, nk), in_specs=[seq2, seq2, blk2, blk2],
        out_specs=[seq2, blk2, pl.BlockSpec((tk, LANES), lambda p, j: (j, p))],
        out_shape=[_sds((s, FOX_AUG_D)), _sds((s, FOX_AUG_D)), _sds((s, FOX_D), MXU_DT)],
        compiler_params=_params("parallel", "arbitrary"),
    )(qb, doa, ka, va)


def _fox_fwd(h, p, big):
    hn = _rms_fwd(h, p["norm_g"], "mix_norm_fwd")
    w_in = big.whole("fox_w_in", hn)
    w_qkvg, w_f = w_in[:, :4 * FOX_D], _pad_cols(w_in[:, 4 * FOX_D:])
    proj = _matmul(hn, w_qkvg, tm=1024, name="fox_in_qkvg")
    f_raw = _matmul(hn, w_f, name="fox_in_f")
    qg = jnp.tile(p["q_norm_g"], FOX_HEADS).reshape(1, -1)
    kg = jnp.tile(p["k_norm_g"], FOX_HEADS).reshape(1, -1)
    qa, ka, va = _fox_prep_fwd(proj, f_raw, qg, kg, _lane_pad(p["b_f"]), "fox_prep_fwd")
    o, og, qb = _fox_attn_fwd(qa, ka, va, proj, "fox_attn_fwd")
    w_out = big.whole("fox_w_out", og)
    h_out = _matmul(og, w_out, res=h, name="fox_out")
    return h_out, (hn, proj, f_raw, qg, kg, ka, va, qb, o, og, w_qkvg, w_f, w_out)


def _fox_bwd(h, dh, saved, p, out):
    hn, proj, f_raw, qg, kg, ka, va, qb, o, og, w_qkvg, w_f, w_out = saved
    s = h.shape[0]
    dh, dh_b = dh
    dog = _matmul(dh_b, w_out, tb=True, name="fox_out_dx")
    out.send("fox_w_out", _matmul(og, dh_b, ta=True, out_dtype=MXU_DT, name="fox_out_dw"))
    dgate, doa = _fox_gate_bwd(dog, o, proj, "fox_gate_bwd")
    dqa, dka, dv = _fox_attn_bwd(qb, ka, va, doa, "fox_attn_bwd")
    dq, dk, df, dqg, dkg, dbf = _fox_prep_bwd(proj, f_raw, qg, kg, out.tie(_lane_pad(p["b_f"])), dqa, dka,
                                              "fox_prep_bwd")
    dproj = jnp.concatenate([dq, dk, dv, dgate], axis=1)
    dhn = _matmul(dproj, w_qkvg, tb=True, name="fox_in_dx_qkvg")
    dhn = _matmul(df, w_f, tb=True, res=dhn, name="fox_in_dx_f")
    out.send("fox_w_in", jnp.concatenate(
        [_matmul(hn, dproj, ta=True, out_dtype=MXU_DT, name="fox_in_dw_qkvg"),
         _matmul(hn, df, ta=True, out_dtype=MXU_DT, name="fox_in_dw_f")[:, :FOX_HEADS]], axis=1))
    *dh_in, dnorm_g = _rms_bwd(h, out.tie(p["norm_g"]), dhn, dh, "mix_norm_bwd")
    fold = lambda g: g.reshape(FOX_HEADS, FOX_HEAD_DIM).sum(axis=0).reshape(1, -1)
    grads = dict(norm_g=dnorm_g, b_f=dbf[:, :FOX_HEADS], q_norm_g=fold(dqg), k_norm_g=fold(dkg))
    return dh_in, grads


def _my_index():
    return 4 * lax.axis_index("x") + 2 * lax.axis_index("y") + lax.axis_index("c")


def _peer(k):
    x, y, c = lax.axis_index("x"), lax.axis_index("y"), lax.axis_index("c")
    flip = lambda v, bit: 1 - v if bit else v
    return (flip(x, k & 4), flip(y, k & 2), flip(c, k & 1))


SEM_SPEC = pl.BlockSpec(memory_space=pltpu.SEMAPHORE)
DATAFLOW = pltpu.SideEffectType.DATAFLOW_SIDE_EFFECTING


def _at(ref, idx):
    return ref.at[tuple(idx)] if idx else ref


def _copies_start(name, srcs, lands, groups):
    ns, nl = len(srcs), len(lands)
    items = [item for group in groups for item in group]

    def body(*refs):
        src_refs, land_refs = refs[:ns], refs[ns:ns + nl]
        sems = refs[ns + nl:ns + nl + 2 * len(items)]
        me = _my_index()
        for t, (i, src_slot, j, dst_slot, _) in enumerate(items):
            for k in range(1, N_DEV):
                pltpu.make_async_remote_copy(
                    src_ref=_at(src_refs[i], src_slot(me, k)), dst_ref=_at(land_refs[j], dst_slot(me, k)),
                    send_sem=sems[2 * t], recv_sem=sems[2 * t + 1], device_id=_peer(k),
                    device_id_type=MESH_IDS).start()
        refs[-1][...] = jnp.zeros(refs[-1].shape, F32)

    hbm = pl.BlockSpec(memory_space=pltpu.HBM)
    bufs = [pltpu.with_memory_space_constraint(a, pltpu.HBM) for a in list(srcs) + list(lands)]
    n_sems = 2 * len(items)
    out = pl.pallas_call(
        body, name=name, in_specs=[hbm] * (ns + nl),
        out_specs=(*[SEM_SPEC] * n_sems, *[hbm] * (ns + nl), pl.BlockSpec(memory_space=pltpu.VMEM)),
        out_shape=(*[pltpu.SemaphoreType.DMA(())] * n_sems, *[pltpu.HBM(a.shape, a.dtype) for a in bufs],
                   _sds((8, LANES))),
        input_output_aliases={i: n_sems + i for i in range(ns + nl)},
        compiler_params=pltpu.CompilerParams(has_side_effects=DATAFLOW),
    )(*bufs)
    sems, t = [], 0
    for group in groups:
        sems.append([(out[2 * (t + u)], out[2 * (t + u) + 1]) for u in range(len(group))])
        t += len(group)
    return sems, list(out[n_sems:n_sems + ns]), list(out[n_sems + ns:n_sems + ns + nl]), out[-1]


def _copies_wait(name, keep, lands, sems, group, after):
    nk, nl, n = len(keep), len(lands), len(group)

    def body(*refs):
        land_refs = refs[nk:nk + nl]
        sem_refs = refs[nk + nl:nk + nl + 2 * n]
        me = _my_index()
        copies = []
        for t, (_, _, j, _, seven) in enumerate(group):
            blocks = _at(land_refs[j], seven(me))
            copies.append(pltpu.make_async_remote_copy(src_ref=blocks, dst_ref=blocks, send_sem=sem_refs[2 * t],
                                                       recv_sem=sem_refs[2 * t + 1], device_id=_peer(1),
                                                       device_id_type=MESH_IDS))
        for cp in copies:
            cp.wait_recv()
        for cp in copies:
            cp.wait_send()

    hbm = pl.BlockSpec(memory_space=pltpu.HBM)
    bufs = list(keep) + list(lands)
    out = pl.pallas_call(
        body, name=name, in_specs=[hbm] * (nk + nl) + [SEM_SPEC] * (2 * n) + [pl.BlockSpec(memory_space=pl.ANY)],
        out_specs=[hbm] * (nk + nl), out_shape=[pltpu.HBM(a.shape, a.dtype) for a in bufs],
        input_output_aliases={i: i for i in range(nk + nl)},
        compiler_params=pltpu.CompilerParams(has_side_effects=DATAFLOW),
    )(*bufs, *[s for pair in sems for s in pair], after)
    return list(out[nk:])


def _allreduce_small(buf, name):
    def body(in_ref, all_ref, sum_ref, send_sems, recv_sems):
        me = _my_index()
        all_ref[me] = in_ref[...]

        def copy(k, slot):
            return pltpu.make_async_remote_copy(
                src_ref=in_ref, dst_ref=all_ref.at[slot], send_sem=send_sems.at[k - 1], recv_sem=recv_sems.at[k - 1],
                device_id=_peer(k), device_id_type=MESH_IDS)

        for k in range(1, N_DEV):
            copy(k, me).start()
        for k in range(1, N_DEV):
            copy(k, jnp.bitwise_xor(me, k)).wait_recv()
        for k in range(1, N_DEV):
            copy(k, me).wait_send()
        acc = all_ref[0]
        for j in range(1, N_DEV):
            acc = acc + all_ref[j]
        sum_ref[...] = acc

    vmem = pl.BlockSpec(memory_space=pltpu.VMEM)
    return pl.pallas_call(
        body, name=name, in_specs=[vmem], out_specs=[vmem, vmem],
        out_shape=[_sds((N_DEV,) + buf.shape), _sds(buf.shape)],
        scratch_shapes=[pltpu.SemaphoreType.DMA((N_DEV - 1,)), pltpu.SemaphoreType.DMA((N_DEV - 1,))],
        compiler_params=pltpu.CompilerParams(vmem_limit_bytes=VMEM_LIMIT_BYTES),
    )(buf)[1]


def _adamw_math(w, g, m, v):
    m = ADAM_B1 * m + (1.0 - ADAM_B1) * g
    v = ADAM_B2 * v + (1.0 - ADAM_B2) * (g * g)
    m_hat = m / (1.0 - ADAM_B1 ** ADAM_STEP)
    v_hat = v / (1.0 - ADAM_B2 ** ADAM_STEP)
    return -ADAM_LR * (m_hat / (jnp.sqrt(v_hat) + ADAM_EPS) + ADAM_WD * w), m, v


def _row_tile(rows, cap=256, mult=16):
    best = None
    for t in range(mult, min(rows, cap) + 1, mult):
        if rows % t == 0:
            best = t
    assert best is not None, rows
    return best


def _adamw_sharded(w, m, v, partials, name):
    layers, rows, cols = w.shape
    tr = _row_tile(rows)

    def body(w_ref, m_ref, v_ref, p_ref, g_ref, d_ref, nm_ref, nv_ref):
        g = p_ref[0].astype(F32)
        for j in range(1, N_DEV):
            g = g + p_ref[j].astype(F32)
        delta, m_new, v_new = _adamw_math(w_ref[...], g, m_ref[...], v_ref[...])
        g_ref[...], d_ref[...], nm_ref[...], nv_ref[...] = g, delta, m_new, v_new

    blk = pl.BlockSpec((None, tr, cols), lambda l, i: (l, i, 0))
    return pl.pallas_call(
        body, name=name, grid=(layers, rows // tr),
        in_specs=[blk, blk, blk, pl.BlockSpec((N_DEV, None, tr, cols), lambda l, i: (0, l, i, 0))],
        out_specs=[blk] * 4, out_shape=[_sds(w.shape)] * 4, compiler_params=_params("parallel", "parallel"),
    )(w, m, v, partials)


def _adamw_small(w, g, m, v, name):
    def body(w_ref, g_ref, m_ref, v_ref, d_ref, nm_ref, nv_ref):
        d_ref[...], nm_ref[...], nv_ref[...] = _adamw_math(w_ref[...], g_ref[...], m_ref[...], v_ref[...])

    return pl.pallas_call(body, name=name, out_shape=[_sds(w.shape)] * 3,
                          compiler_params=_params())(w, g, m, v)


def _pack(arrays):
    flat = jnp.concatenate([a.reshape(-1).astype(F32) for a in arrays])
    pad = -flat.shape[0] % (8 * LANES)
    return jnp.pad(flat, (0, pad)).reshape(-1, LANES)


def _unpack(buf, shapes):
    flat, out, off = buf.reshape(-1), [], 0
    for shp in shapes:
        size = math.prod(shp)
        out.append(flat[off:off + size].reshape(shp))
        off += size
    return out


WEIGHTS = ["mix_norm_g", "ffn_norm_g", "ssd_w_in", "ssd_conv_w", "ssd_conv_b", "ssd_dt_bias", "ssd_a_log", "ssd_d",
           "ssd_norm_g", "ssd_w_out", "fox_w_in", "fox_b_f", "fox_q_norm_g", "fox_k_norm_g", "fox_w_out", "ffn_w_up",
           "ffn_conv_w", "ffn_conv_b", "ffn_w_down", "final_norm_g"]
BIG = ["ssd_w_in", "ssd_w_out", "fox_w_in", "fox_w_out", "ffn_w_up", "ffn_w_down"]
COLUMN_SHARDED = ["ssd_w_in", "fox_w_in", "ffn_w_up"]
CONV = ["ssd_conv_w", "ffn_conv_w"]
REPLICATED = [n for n in WEIGHTS if n not in BIG + CONV]
DEPTH = 4
KEPT_TRANSPOSED = ["ffn_w_up"]
ADAMW_ORDER = ["fox_w_out", "fox_w_in", "ffn_w_down", "ffn_w_up", "ssd_w_out", "ssd_w_in"]
LAYER_SHARDED = (["ssd_w_in", "ssd_conv_w", "ssd_w_out", "ffn_w_up", "ffn_conv_w", "ffn_w_down"],
                 ["fox_w_in", "fox_w_out", "ffn_w_up", "ffn_conv_w", "ffn_w_down"])


def _to_shards(full, on_columns):
    nl, r, c = full.shape
    if on_columns:
        return full.reshape(nl, r, N_DEV, c // N_DEV).transpose(2, 0, 1, 3)
    return full.reshape(nl, N_DEV, r // N_DEV, c).transpose(1, 0, 2, 3)


def _pad_cols(w):
    return jnp.pad(w, ((0, 0), (0, LANES - w.shape[1])))


def kernel(x, mix_norm_g, ffn_norm_g, ssd_w_in, ssd_conv_w, ssd_conv_b, ssd_dt_bias, ssd_a_log, ssd_d, ssd_norm_g, ssd_w_out, fox_w_in, fox_b_f, fox_q_norm_g, fox_k_norm_g, fox_w_out, ffn_w_up, ffn_conv_w, ffn_conv_b, ffn_w_down, final_norm_g, loss_target, m_mix_norm_g, m_ffn_norm_g, m_ssd_w_in, m_ssd_conv_w, m_ssd_conv_b, m_ssd_dt_bias, m_ssd_a_log, m_ssd_d, m_ssd_norm_g, m_ssd_w_out, m_fox_w_in, m_fox_b_f, m_fox_q_norm_g, m_fox_k_norm_g, m_fox_w_out, m_ffn_w_up, m_ffn_conv_w, m_ffn_conv_b, m_ffn_w_down, m_final_norm_g, v_mix_norm_g, v_ffn_norm_g, v_ssd_w_in, v_ssd_conv_w, v_ssd_conv_b, v_ssd_dt_bias, v_ssd_a_log, v_ssd_d, v_ssd_norm_g, v_ssd_w_out, v_fox_w_in, v_fox_b_f, v_fox_q_norm_g, v_fox_k_norm_g, v_fox_w_out, v_ffn_w_up, v_ffn_conv_w, v_ffn_conv_b, v_ffn_w_down, v_final_norm_g):
    given = dict(locals())
    w = {n: given[n] for n in WEIGHTS}
    mom = {n: given["m_" + n] for n in WEIGHTS}
    var = {n: given["v_" + n] for n in WEIGHTS}
    me = _my_index()

    sharded = BIG + CONV
    shards = [w[n].astype(MXU_DT) if n in BIG else w[n] for n in sharded]
    zones = [(n, i if n.startswith("ffn") else i // 2) for i in range(DEPTH) for n in LAYER_SHARDED[i % 2]]
    items = [[(sharded.index(n), functools.partial(lambda me, k, layer: (layer,), layer=layer), z,
               lambda me, k: (me,), lambda me: (pl.ds(0, N_DEV - 1),))] for z, (n, layer) in enumerate(zones)]
    empty = [lax.empty((N_DEV,) + shards[sharded.index(n)].shape[1:], shards[sharded.index(n)].dtype) for n, _ in zones]
    gather_sems, shards_kept, landing, _ = _copies_start("gather_start", shards, empty, items)

    def arrived(n, layer, after):
        z = zones.index((n, layer))
        item = items[z][0]
        stack, = _copies_wait(f"gather_wait_{n}_{layer}", shards_kept if z == len(zones) - 1 else [], [landing[z]],
                              gather_sems[z], [item[:2] + (0,) + item[3:]], after)
        stack = lax.dynamic_update_index_in_dim(stack, shards[sharded.index(n)][layer], me, 0)
        _, r, c = stack.shape
        on_columns = n in COLUMN_SHARDED or n in CONV
        return stack.transpose(1, 0, 2).reshape(r, N_DEV * c) if on_columns else stack.reshape(N_DEV * r, c)

    def mixer_params(i):
        j = i // 2
        if i % 2 == 0:
            return dict(norm_g=w["mix_norm_g"][i], conv_b=w["ssd_conv_b"][j], dt_bias=w["ssd_dt_bias"][j],
                        a_log=w["ssd_a_log"][j], d=w["ssd_d"][j], gnorm_g=w["ssd_norm_g"][j])
        return dict(norm_g=w["mix_norm_g"][i], b_f=w["fox_b_f"][j], q_norm_g=w["fox_q_norm_g"][j],
                    k_norm_g=w["fox_k_norm_g"][j])

    h = x[0]
    tape = []
    for i in range(DEPTH):
        big = _Weights(functools.partial(lambda n, after, i: arrived(n, i if n.startswith("ffn") else i // 2, after), i=i))
        mp, fp = mixer_params(i), dict(norm_g=w["ffn_norm_g"][i], conv_b=w["ffn_conv_b"][i])
        h_mid, mix_saved = (_ssd_fwd if i % 2 == 0 else _fox_fwd)(h, mp, big)
        h_out, ffn_saved = _ffn_fwd(h_mid, fp, big)
        tape.append((h, mp, mix_saved, h_mid, fp, ffn_saved))
        h = h_out
    *dh, dfinal_g, loss_part = _loss_head(h, w["final_norm_g"], loss_target[0], "loss_head")

    grads = {n: [None] * w[n].shape[0] for n in WEIGHTS if n not in BIG + ["final_norm_g"]}
    shard_view = lambda n, a: a.swapaxes(-1, -2) if n in KEPT_TRANSPOSED else a
    partials = {n: lax.empty((N_DEV,) + shard_view(n, w[n]).shape, MXU_DT) for n in BIG}
    in_flight = {n: [] for n in BIG}

    def send_partial(n, layer, grad):
        slots = shard_view(n, _to_shards(grad[None], n in COLUMN_SHARDED)[:, 0])
        mine = lax.dynamic_index_in_dim(slots, me, 0, keepdims=False)
        zone = lax.dynamic_update_slice(partials[n], mine[None, None], (me, layer, 0, 0))
        item = (0, lambda me, k: (jnp.bitwise_xor(me, k),), 0,
                functools.partial(lambda me, k, layer: (me, layer), layer=layer),
                functools.partial(lambda me, layer: (pl.ds(0, N_DEV - 1), layer), layer=layer))
        sems, kept, (partials[n],), token = _copies_start(f"scatter_start_{n}_{layer}", [slots], [zone], [[item]])
        in_flight[n].append((layer, sems[0], kept, item))
        return token

    for i in reversed(range(DEPTH)):
        j = i // 2
        h_in, mp, mix_saved, h_mid, fp, ffn_saved = tape[i]
        out = _Gradients(functools.partial(lambda n, grad, i: send_partial(n, i if n.startswith("ffn") else i // 2, grad),
                                           i=i))
        dh, g = _ffn_bwd(h_mid, dh, ffn_saved, fp, out)
        grads["ffn_norm_g"][i], grads["ffn_conv_w"][i], grads["ffn_conv_b"][i] = g["norm_g"][0], g["conv_w"], g["conv_b"][0]
        if i % 2 == 0:
            dh, g = _ssd_bwd(h_in, dh, mix_saved, mp, out)
            grads["ssd_conv_w"][j] = g["conv_w"]
            for key, name in (("conv_b", "ssd_conv_b"), ("dt_bias", "ssd_dt_bias"), ("a_log", "ssd_a_log"),
                              ("d", "ssd_d"), ("gnorm_g", "ssd_norm_g")):
                grads[name][j] = g[key][0]
        else:
            dh, g = _fox_bwd(h_in, dh, mix_saved, mp, out)
            for key, name in (("b_f", "fox_b_f"), ("q_norm_g", "fox_q_norm_g"), ("k_norm_g", "fox_k_norm_g")):
                grads[name][j] = g[key][0]
        grads["mix_norm_g"][i] = g["norm_g"][0]
    grads = {n: jnp.stack(v) for n, v in grads.items()}
    grads["final_norm_g"] = dfinal_g[0]

    small_names = REPLICATED + CONV
    summed = _unpack(_allreduce_small(_pack([grads[n] for n in small_names] + [loss_part]), "allreduce_small"),
                     [grads[n].shape for n in small_names] + [(1, 1)])
    loss = summed[-1][0, 0]
    g_small = dict(zip(small_names, summed[:-1]))
    for n in CONV:
        width = w[n].shape[-1]
        g_small[n] = lax.dynamic_slice_in_dim(g_small[n], me * width, width, axis=2)
    pk = lambda d: _pack([d[n] for n in small_names])
    d_small, m_small, v_small = _adamw_small(pk(w), pk(g_small), pk(mom), pk(var), "adamw_small")
    shapes = [w[n].shape for n in small_names]
    out_g, out_d, out_m, out_v = dict(g_small), {}, {}, {}
    for dst, buf in ((out_d, d_small), (out_m, m_small), (out_v, v_small)):
        dst.update(zip(small_names, _unpack(buf, shapes)))

    after = d_small
    for n in ADAMW_ORDER:
        for layer, sems, kept, item in in_flight[n]:
            partials[n], = _copies_wait(f"scatter_wait_{n}_{layer}", kept, [partials[n]], sems, [item], after)
        res = _adamw_sharded(shard_view(n, w[n]), shard_view(n, mom[n]), shard_view(n, var[n]), partials[n], "adamw_" + n)
        out_g[n], out_d[n], out_m[n], out_v[n] = [shard_view(n, r) for r in res]
        after = out_v[n]

    return (loss, dh[0][None], *[out_g[n] for n in WEIGHTS], *[out_d[n] for n in WEIGHTS],
            *[out_m[n] for n in WEIGHTS], *[out_v[n] for n in WEIGHTS])
```

```python
import functools
import math

import jax
import jax.numpy as jnp
from jax import lax
from jax.experimental import pallas as pl
from jax.experimental.pallas import tpu as pltpu

F32 = jnp.float32
MXU_DT = jnp.bfloat16
VMEM_LIMIT_BYTES = 56 * 1024 * 1024
LANES = 128
N_DEV = 8
MESH_IDS = pl.DeviceIdType.MESH

EPS = 1e-6
D_MODEL = 1024
SSD_D_INNER = 2048
SSD_HEAD_DIM = 64
SSD_HEADS = 32
SSD_GROUPS = 4
SSD_HPG = 8
SSD_STATE = 128
SSD_CONV = 4
SSD_CHUNK = 128
SSD_CONV_DIM = 3072
SSD_ZX = SSD_D_INNER + SSD_CONV_DIM
FOX_HEAD_DIM = 64
FOX_HEADS = 16
FOX_D = 1024
D_FF = 2816
FFN_CONV = 3
ADAM_LR, ADAM_B1, ADAM_B2, ADAM_EPS, ADAM_WD, ADAM_STEP = 0.001, 0.9, 0.999, 1e-08, 0.01, 10


def _params(*sem):
    return pltpu.CompilerParams(dimension_semantics=sem or None, vmem_limit_bytes=VMEM_LIMIT_BYTES)


def _sds(shape, dtype=F32):
    return jax.ShapeDtypeStruct(tuple(shape), dtype)


def _col_tile(n, cap=1536):
    best = None
    for t in range(LANES, min(n, cap) + 1, LANES):
        if n % t == 0:
            best = t
    assert best is not None, n
    return best


def _sigmoid(x):
    return 0.5 * jnp.tanh(0.5 * x) + 0.5


def _matmul(a, b, *, ta=False, tb=False, res=None, out_dtype=F32, tm=512, tn=None, b_kblock=0, name):
    (kdim, m) = a.shape if ta else a.shape[::-1]
    (n, k2) = b.shape if tb else b.shape[::-1]
    assert kdim == k2 or (tb and k2 % kdim == 0), (a.shape, b.shape, ta, tb)
    tm = min(tm, m)
    if m % tm:
        tm = _col_tile(m, tm)
    tn = tn or _col_tile(n)
    assert m % tm == 0 and n % tn == 0, (m, tm, n, tn)
    dims = (((0 if ta else 1,), (1 if tb else 0,)), ((), ()))

    def body(*refs):
        a_ref, b_ref = refs[0], refs[1]
        o_ref = refs[-1]
        acc = lax.dot_general(a_ref[...].astype(MXU_DT), b_ref[...].astype(MXU_DT), dims,
                              preferred_element_type=F32)
        if res is not None:
            acc = acc + refs[2][...].astype(F32)
        o_ref[...] = acc.astype(o_ref.dtype)

    a_spec = pl.BlockSpec((kdim, tm), lambda i, j: (0, i)) if ta else pl.BlockSpec((tm, kdim), lambda i, j: (i, 0))
    b_spec = pl.BlockSpec((tn, kdim), lambda i, j: (j, b_kblock)) if tb else pl.BlockSpec((kdim, tn), lambda i, j: (0, j))
    o_spec = pl.BlockSpec((tm, tn), lambda i, j: (i, j))
    ins, specs = [a, b], [a_spec, b_spec]
    if res is not None:
        ins.append(res)
        specs.append(o_spec)
    return pl.pallas_call(
        body, name=name, grid=(m // tm, n // tn), in_specs=specs, out_specs=o_spec,
        out_shape=_sds((m, n), out_dtype), compiler_params=_params("parallel", "parallel"),
    )(*ins)


def _rowwise(fn, rows, consts, out_rows, out_sums, *, tr=256, name):
    rows = [r if isinstance(r, tuple) else (r, 0, r.shape[1]) for r in rows]
    s = rows[0][0].shape[0]
    tr = min(tr, s)
    assert s % tr == 0
    n_in, n_c, n_or = len(rows), len(consts), len(out_rows)

    def body(*refs):
        ins = [r[...] for r in refs[:n_in + n_c]]
        outs = fn(*ins)
        o_refs = refs[n_in + n_c:]
        for o_ref, val in zip(o_refs[:n_or], outs[:n_or]):
            o_ref[...] = val.astype(o_ref.dtype)
        if out_sums:
            first = pl.program_id(0) == 0

            @pl.when(first)
            def _():
                for o_ref, val in zip(o_refs[n_or:], outs[n_or:]):
                    o_ref[...] = val.astype(o_ref.dtype)

            @pl.when(jnp.logical_not(first))
            def _():
                for o_ref, val in zip(o_refs[n_or:], outs[n_or:]):
                    o_ref[...] += val.astype(o_ref.dtype)

    in_specs = [pl.BlockSpec((tr, width), functools.partial(lambda i, cb: (i, cb), cb=cb)) for _, cb, width in rows]
    in_specs += [pl.BlockSpec(c.shape, lambda i: (0, 0)) for c in consts]
    out_specs = [pl.BlockSpec((tr, o.shape[1]), lambda i: (i, 0)) for o in out_rows]
    out_specs += [pl.BlockSpec(o.shape, lambda i: (0, 0)) for o in out_sums]
    return pl.pallas_call(
        body, name=name, grid=(s // tr,), in_specs=in_specs, out_specs=out_specs,
        out_shape=list(out_rows) + list(out_sums),
        compiler_params=_params("arbitrary" if out_sums else "parallel"),
    )(*[r[0] for r in rows], *consts)


def _rms_fwd(h, g, name):
    def fn(x, gv):
        r = lax.rsqrt(jnp.mean(x * x, axis=-1, keepdims=True) + EPS)
        return (x * r * gv,)
    return _rowwise(fn, [h], [g.reshape(1, -1)], [_sds(h.shape, MXU_DT)], [], name=name)[0]


def _rms_bwd(h, g, dy, dres, name):
    def fn(x, dyv, dr, gv):
        r = lax.rsqrt(jnp.mean(x * x, axis=-1, keepdims=True) + EPS)
        dyg = dyv * gv
        dx = dr + r * dyg - x * (r * r * r) * jnp.mean(x * dyg, axis=-1, keepdims=True)
        return dx, dx, jnp.sum(dyv * x * r, axis=0, keepdims=True)
    return _rowwise(fn, [h, dy, dres], [g.reshape(1, -1)], [_sds(h.shape), _sds(h.shape, MXU_DT)],
                    [_sds((1, h.shape[1]))], name=name)


def _loss_head(h, g, target, name):
    c = h.shape[1]

    def fn(x, t, gv):
        r = lax.rsqrt(jnp.mean(x * x, axis=-1, keepdims=True) + EPS)
        y = x * r * gv
        err = y - t
        dyv = err * (1.0 / c)
        dyg = dyv * gv
        dx = r * dyg - x * (r * r * r) * jnp.mean(x * dyg, axis=-1, keepdims=True)
        loss = 0.5 * jnp.sum(jnp.mean(err * err, axis=-1, keepdims=True), axis=0, keepdims=True)
        return dx, dx, jnp.sum(dyv * x * r, axis=0, keepdims=True), loss
    return _rowwise(fn, [h, target], [g.reshape(1, -1)], [_sds(h.shape), _sds(h.shape, MXU_DT)],
                    [_sds((1, c)), _sds((1, 1))], name=name)


PAD_ROWS = 8
ROW_TILE = 128


def _shifted_conv(xp_ref, w, r0, tr, kw):
    acc = None
    for k in range(kw):
        xk = xp_ref[pl.ds(PAD_ROWS + r0 - (kw - 1) + k, tr), :]
        term = xk * w[k:k + 1, :]
        acc = term if acc is None else acc + term
    return acc


def _convglu_fwd(u, conv_w, conv_b, name):
    s = u.shape[0]
    nt = D_FF // LANES
    tr = min(ROW_TILE, s)

    def body(ug_ref, uv_ref, w_ref, b_ref, act_ref, xp_ref):
        xp_ref[pl.ds(0, PAD_ROWS), :] = jnp.zeros((PAD_ROWS, LANES), F32)
        xp_ref[pl.ds(PAD_ROWS, s), :] = ug_ref[...]
        w = w_ref[...]
        b = b_ref[...]
        for r0 in range(0, s, tr):
            gate = _shifted_conv(xp_ref, w, r0, tr, FFN_CONV) + b
            act = gate * _sigmoid(gate) * uv_ref[pl.ds(r0, tr), :]
            act_ref[pl.ds(r0, tr), :] = act.astype(act_ref.dtype)

    return pl.pallas_call(
        body, name=name, grid=(nt,),
        in_specs=[pl.BlockSpec((s, LANES), lambda j: (0, j)), pl.BlockSpec((s, LANES), lambda j: (0, nt + j)),
                  pl.BlockSpec((FFN_CONV, LANES), lambda j: (0, j)), pl.BlockSpec((1, LANES), lambda j: (0, j))],
        out_specs=pl.BlockSpec((s, LANES), lambda j: (0, j)),
        out_shape=_sds((s, D_FF), MXU_DT),
        scratch_shapes=[pltpu.VMEM((s + PAD_ROWS, LANES), F32)],
        compiler_params=_params("parallel"),
    )(u, u, conv_w, conv_b.reshape(1, -1))


def _convglu_bwd(u, dact, conv_w, conv_b, name):
    s = u.shape[0]
    nt = D_FF // LANES
    tr = min(ROW_TILE, s)
    kw = FFN_CONV

    def body(ug_ref, uv_ref, da_ref, w_ref, b_ref, dug_ref, duv_ref, dw_ref, db_ref, xp_ref, dgp_ref):
        xp_ref[pl.ds(0, PAD_ROWS), :] = jnp.zeros((PAD_ROWS, LANES), F32)
        xp_ref[pl.ds(PAD_ROWS, s), :] = ug_ref[...]
        dgp_ref[pl.ds(s, PAD_ROWS), :] = jnp.zeros((PAD_ROWS, LANES), F32)
        w = w_ref[...]
        b = b_ref[...]
        dw = [jnp.zeros((1, LANES), F32) for _ in range(kw)]
        db = jnp.zeros((1, LANES), F32)
        for r0 in range(0, s, tr):
            gate = _shifted_conv(xp_ref, w, r0, tr, kw) + b
            sg = _sigmoid(gate)
            da = da_ref[pl.ds(r0, tr), :].astype(F32)
            duv_ref[pl.ds(r0, tr), :] = (da * gate * sg).astype(duv_ref.dtype)
            dgate = da * uv_ref[pl.ds(r0, tr), :] * (sg * (1.0 + gate * (1.0 - sg)))
            dgp_ref[pl.ds(r0, tr), :] = dgate
            db = db + jnp.sum(dgate, axis=0, keepdims=True)
            for k in range(kw):
                xk = xp_ref[pl.ds(PAD_ROWS + r0 - (kw - 1) + k, tr), :]
                dw[k] = dw[k] + jnp.sum(dgate * xk, axis=0, keepdims=True)
        for r0 in range(0, s, tr):
            acc = None
            for k in range(kw):
                term = dgp_ref[pl.ds(r0 + (kw - 1) - k, tr), :] * w[k:k + 1, :]
                acc = term if acc is None else acc + term
            dug_ref[pl.ds(r0, tr), :] = acc.astype(dug_ref.dtype)
        for k in range(kw):
            dw_ref[pl.ds(k, 1), :] = dw[k]
        db_ref[...] = db

    col = lambda j: (0, j)
    return pl.pallas_call(
        body, name=name, grid=(nt,),
        in_specs=[pl.BlockSpec((s, LANES), col), pl.BlockSpec((s, LANES), lambda j: (0, nt + j)),
                  pl.BlockSpec((s, LANES), col), pl.BlockSpec((kw, LANES), col), pl.BlockSpec((1, LANES), col)],
        out_specs=[pl.BlockSpec((s, LANES), col), pl.BlockSpec((s, LANES), col),
                   pl.BlockSpec((kw, LANES), col), pl.BlockSpec((1, LANES), col)],
        out_shape=[_sds((s, D_FF), MXU_DT), _sds((s, D_FF), MXU_DT), _sds((kw, D_FF)), _sds((1, D_FF))],
        scratch_shapes=[pltpu.VMEM((s + PAD_ROWS, LANES), F32), pltpu.VMEM((s + PAD_ROWS, LANES), F32)],
        compiler_params=_params("parallel"),
    )(u, u, dact, conv_w, conv_b.reshape(1, -1))


class _Weights:
    def __init__(self, fetch):
        self._fetch, self._got = fetch, {}

    def whole(self, name, after):
        if name not in self._got:
            self._got[name] = self._fetch(name, after)
        return self._got[name]


class _Gradients:
    def __init__(self, start):
        self._start, self._tokens = start, []

    def send(self, name, grad):
        token = self._start(name, grad)
        if token is not None:
            self._tokens.append(token)

    def tie(self, x):
        for token in self._tokens:
            x = x + token[0, 0]
        self._tokens = []
        return x


def _ffn_fwd(h, p, big):
    hf = _rms_fwd(h, p["norm_g"], "ffn_norm_fwd")
    w_up = big.whole("ffn_w_up", hf)
    u = _matmul(hf, w_up, tm=1024, name="ffn_up")
    conv_w = big.whole("ffn_conv_w", u)
    act = _convglu_fwd(u, conv_w, p["conv_b"], "convglu_fwd")
    w_down = big.whole("ffn_w_down", act)
    h_out = _matmul(act, w_down, res=h, name="ffn_down")
    return h_out, (hf, u, act, w_up, conv_w, w_down)


def _ffn_bwd(h, dh, saved, p, out):
    hf, u, act, w_up, conv_w, w_down = saved
    dh, dh_b = dh
    dact = _matmul(dh_b, w_down, tb=True, tm=1024, name="ffn_down_dx")
    out.send("ffn_w_down", _matmul(act, dh_b, ta=True, out_dtype=MXU_DT, name="ffn_down_dw"))
    dug, duv, dconv_w, dconv_b = _convglu_bwd(u, dact, conv_w, out.tie(p["conv_b"]), "convglu_bwd")
    dhf = _matmul(dug, w_up, tb=True, b_kblock=0, name="ffn_up_dx_gate")
    dhf = _matmul(duv, w_up, tb=True, b_kblock=1, res=dhf, name="ffn_up_dx_val")
    out.send("ffn_w_up", jnp.concatenate([_matmul(hf, dug, ta=True, tm=1024, out_dtype=MXU_DT, name="ffn_up_dw_gate"),
                                          _matmul(hf, duv, ta=True, tm=1024, out_dtype=MXU_DT, name="ffn_up_dw_val")], axis=1))
    *dh_in, dnorm_g = _rms_bwd(h, out.tie(p["norm_g"]), dhf, dh, "ffn_norm_bwd")
    return dh_in, dict(norm_g=dnorm_g, conv_w=dconv_w, conv_b=dconv_b)


def _dwconv_silu_fwd(proj, col0, n_ch, conv_w, conv_b, name):
    s = proj.shape[0]
    nt, t0, kw = n_ch // LANES, col0 // LANES, conv_w.shape[0]
    tr = min(ROW_TILE, s)

    def body(x_ref, w_ref, b_ref, o_ref, xp_ref):
        xp_ref[pl.ds(0, PAD_ROWS), :] = jnp.zeros((PAD_ROWS, LANES), F32)
        xp_ref[pl.ds(PAD_ROWS, s), :] = x_ref[...]
        w = w_ref[...]
        b = b_ref[...]
        for r0 in range(0, s, tr):
            pre = _shifted_conv(xp_ref, w, r0, tr, kw) + b
            o_ref[pl.ds(r0, tr), :] = pre * _sigmoid(pre)

    col = lambda j: (0, j)
    return pl.pallas_call(
        body, name=name, grid=(nt,),
        in_specs=[pl.BlockSpec((s, LANES), lambda j: (0, t0 + j)), pl.BlockSpec((kw, LANES), col),
                  pl.BlockSpec((1, LANES), col)],
        out_specs=pl.BlockSpec((s, LANES), col), out_shape=_sds((s, n_ch)),
        scratch_shapes=[pltpu.VMEM((s + PAD_ROWS, LANES), F32)],
        compiler_params=_params("parallel"),
    )(proj, conv_w, conv_b.reshape(1, -1))


def _dwconv_silu_bwd(proj, col0, n_ch, dout, conv_w, conv_b, name):
    s = proj.shape[0]
    nt, t0, kw = n_ch // LANES, col0 // LANES, conv_w.shape[0]
    tr = min(ROW_TILE, s)

    def body(x_ref, do_ref, w_ref, b_ref, dx_ref, dw_ref, db_ref, xp_ref, dgp_ref):
        xp_ref[pl.ds(0, PAD_ROWS), :] = jnp.zeros((PAD_ROWS, LANES), F32)
        xp_ref[pl.ds(PAD_ROWS, s), :] = x_ref[...]
        dgp_ref[pl.ds(s, PAD_ROWS), :] = jnp.zeros((PAD_ROWS, LANES), F32)
        w = w_ref[...]
        b = b_ref[...]
        dw = [jnp.zeros((1, LANES), F32) for _ in range(kw)]
        db = jnp.zeros((1, LANES), F32)
        for r0 in range(0, s, tr):
            pre = _shifted_conv(xp_ref, w, r0, tr, kw) + b
            sg = _sigmoid(pre)
            dpre = do_ref[pl.ds(r0, tr), :] * (sg * (1.0 + pre * (1.0 - sg)))
            dgp_ref[pl.ds(r0, tr), :] = dpre
            db = db + jnp.sum(dpre, axis=0, keepdims=True)
            for k in range(kw):
                xk = xp_ref[pl.ds(PAD_ROWS + r0 - (kw - 1) + k, tr), :]
                dw[k] = dw[k] + jnp.sum(dpre * xk, axis=0, keepdims=True)
        for r0 in range(0, s, tr):
            acc = None
            for k in range(kw):
                term = dgp_ref[pl.ds(r0 + (kw - 1) - k, tr), :] * w[k:k + 1, :]
                acc = term if acc is None else acc + term
            dx_ref[pl.ds(r0, tr), :] = acc.astype(dx_ref.dtype)
        for k in range(kw):
            dw_ref[pl.ds(k, 1), :] = dw[k]
        db_ref[...] = db

    col = lambda j: (0, j)
    return pl.pallas_call(
        body, name=name, grid=(nt,),
        in_specs=[pl.BlockSpec((s, LANES), lambda j: (0, t0 + j)), pl.BlockSpec((s, LANES), col),
                  pl.BlockSpec((kw, LANES), col), pl.BlockSpec((1, LANES), col)],
        out_specs=[pl.BlockSpec((s, LANES), col), pl.BlockSpec((kw, LANES), col), pl.BlockSpec((1, LANES), col)],
        out_shape=[_sds((s, n_ch), MXU_DT), _sds((kw, n_ch)), _sds((1, n_ch))],
        scratch_shapes=[pltpu.VMEM((s + PAD_ROWS, LANES), F32), pltpu.VMEM((s + PAD_ROWS, LANES), F32)],
        compiler_params=_params("parallel"),
    )(proj, dout, conv_w, conv_b.reshape(1, -1))


HIGHEST = lax.Precision.HIGHEST
PAIRS = SSD_HPG // 2
PAIR_W = 2 * SSD_HEAD_DIM
GROUP_W = SSD_HPG * SSD_HEAD_DIM


def _iota2(shape, axis):
    return lax.broadcasted_iota(jnp.int32, shape, axis)


def _lane_pad(v):
    return jnp.pad(v.reshape(1, -1), ((0, 0), (0, LANES - v.shape[0])))


def _heads_to_groups(a):
    s = a.shape[0]
    return a[:, :SSD_HEADS].reshape(s, SSD_GROUPS, SSD_HPG).transpose(1, 0, 2)


def _groups_to_heads(a):
    s = a.shape[1]
    return jnp.pad(a.transpose(1, 0, 2).reshape(s, SSD_HEADS), ((0, 0), (0, LANES - SSD_HEADS)))


def _ssd_prep(dt_raw, dt_bias, a_log, name):
    s = dt_raw.shape[0]
    lc = SSD_CHUNK

    def body(x_ref, b_ref, al_ref, dt_ref, acs_ref, acst_ref):
        x = x_ref[...] + b_ref[...]
        dt = jnp.maximum(x, 0.0) + jnp.log1p(jnp.exp(-jnp.abs(x)))
        da = dt * (-jnp.exp(al_ref[...]))
        lower = (_iota2((lc, lc), 0) >= _iota2((lc, lc), 1)).astype(F32)
        upper = (_iota2((lc, lc), 0) <= _iota2((lc, lc), 1)).astype(F32)
        dt_ref[...] = dt
        acs_ref[...] = jnp.dot(lower, da, precision=HIGHEST, preferred_element_type=F32)
        acst_ref[...] = lax.dot_general(da, upper, (((0,), (0,)), ((), ())), precision=HIGHEST,
                                        preferred_element_type=F32)

    row = pl.BlockSpec((lc, LANES), lambda c: (c, 0))
    one = pl.BlockSpec((1, LANES), lambda c: (0, 0))
    return pl.pallas_call(
        body, name=name, grid=(s // lc,), in_specs=[row, one, one],
        out_specs=[row, row, pl.BlockSpec((LANES, lc), lambda c: (0, c))],
        out_shape=[_sds((s, LANES)), _sds((s, LANES)), _sds((LANES, s))],
        compiler_params=_params("parallel"),
    )(dt_raw, dt_bias, a_log)


def _pair_cols(v, p, lo):
    return jnp.where(lo, v[:, 2 * p:2 * p + 1], v[:, 2 * p + 1:2 * p + 2])


def _decay_matrix(acs, acst, h, tri):
    return jnp.exp(jnp.where(tri, acs[:, h:h + 1] - acst[h:h + 1, :], -jnp.inf))


def _dot(a, b, ca, cb):
    return lax.dot_general(a, b, (((ca,), (cb,)), ((), ())), preferred_element_type=F32)


def _ssd_scan_fwd(xbc, dtg, acsg, acstg, d_skip, name):
    s = xbc.shape[0]
    lc, nc = SSD_CHUNK, s // SSD_CHUNK
    xt, bt = GROUP_W // LANES, SSD_D_INNER // LANES

    def body(x_ref, b_ref, c_ref, dt_ref, acs_ref, acst_ref, dsk_ref, y_ref, hp_ref, st_ref):
        @pl.when(pl.program_id(1) == 0)
        def _():
            st_ref[...] = jnp.zeros(st_ref.shape, F32)

        bm, cmb = b_ref[...], c_ref[...].astype(MXU_DT)
        dt, acs, acst = dt_ref[...], acs_ref[...], acst_ref[...]
        cb = _dot(cmb, bm.astype(MXU_DT), 1, 1)
        tri = _iota2((lc, lc), 0) >= _iota2((lc, lc), 1)
        lo = _iota2((lc, PAIR_W), 1) < SSD_HEAD_DIM
        a_last = acs[lc - 1:lc, :]
        e_acs, e_ds, e_cd = jnp.exp(acs), jnp.exp(a_last - acs), jnp.exp(a_last)
        for p in range(PAIRS):
            sl = pl.ds(p * PAIR_W, PAIR_W)
            xp = x_ref[:, sl]
            ub = (xp * _pair_cols(dt, p, lo)).astype(MXU_DT)
            m0 = (cb * _decay_matrix(acs, acst, 2 * p, tri)).astype(MXU_DT)
            m1 = (cb * _decay_matrix(acs, acst, 2 * p + 1, tri)).astype(MXU_DT)
            ht = st_ref[p]
            hp_ref[p] = ht
            y = jnp.where(lo, _dot(m0, ub, 1, 0), _dot(m1, ub, 1, 0))
            y = y + _dot(cmb, ht.astype(MXU_DT), 1, 0) * _pair_cols(e_acs, p, lo)
            y_ref[:, sl] = y + xp * dsk_ref[:, sl]
            bd0 = (bm * e_ds[:, 2 * p:2 * p + 1]).astype(MXU_DT)
            bd1 = (bm * e_ds[:, 2 * p + 1:2 * p + 2]).astype(MXU_DT)
            st_ref[p] = ht * _pair_cols(e_cd, p, lo[:1]) + jnp.where(lo, _dot(bd0, ub, 0, 0), _dot(bd1, ub, 0, 0))

    small = pl.BlockSpec((None, lc, SSD_HPG), lambda g, c: (g, c, 0))
    return pl.pallas_call(
        body, name=name, grid=(SSD_GROUPS, nc),
        in_specs=[pl.BlockSpec((lc, GROUP_W), lambda g, c: (c, g)),
                  pl.BlockSpec((lc, LANES), lambda g, c: (c, bt + g)),
                  pl.BlockSpec((lc, LANES), lambda g, c: (c, bt + SSD_GROUPS + g)),
                  small, small, pl.BlockSpec((None, SSD_HPG, lc), lambda g, c: (g, 0, c)),
                  pl.BlockSpec((1, GROUP_W), lambda g, c: (0, g))],
        out_specs=[pl.BlockSpec((lc, GROUP_W), lambda g, c: (c, g)),
                   pl.BlockSpec((None, PAIRS, SSD_STATE, PAIR_W), lambda g, c: (c, g, 0, 0))],
        out_shape=[_sds((s, SSD_D_INNER)), _sds((nc, SSD_GROUPS * PAIRS, SSD_STATE, PAIR_W))],
        scratch_shapes=[pltpu.VMEM((PAIRS, SSD_STATE, PAIR_W), F32)],
        compiler_params=_params("parallel", "arbitrary"),
    )(xbc, xbc, xbc, dtg, acsg, acstg, d_skip)


def _ssd_scan_bwd(xbc, dtg, acsg, acstg, d_skip, dy, hprev, name):
    s = xbc.shape[0]
    lc, nc = SSD_CHUNK, s // SSD_CHUNK
    bt = SSD_D_INNER // LANES

    def body(x_ref, b_ref, c_ref, dt_ref, acs_ref, acst_ref, dsk_ref, dy_ref, hp_ref, hn_ref,
             dx_ref, db_ref, dc_ref, daq_ref, dar_ref, ddtx_ref, dd_ref, dst_ref, ta_ref, tx_ref):
        @pl.when(pl.program_id(1) == 0)
        def _():
            dst_ref[...] = jnp.zeros(dst_ref.shape, F32)
            dd_ref[...] = jnp.zeros(dd_ref.shape, F32)

        bm, cmb = b_ref[...], c_ref[...].astype(MXU_DT)
        bmb = bm.astype(MXU_DT)
        dt, acs, acst = dt_ref[...], acs_ref[...], acst_ref[...]
        cb = _dot(cmb, bmb, 1, 1)
        tri = _iota2((lc, lc), 0) >= _iota2((lc, lc), 1)
        lo = _iota2((lc, PAIR_W), 1) < SSD_HEAD_DIM
        a_last = acs[lc - 1:lc, :]
        e_acs, e_ds, e_cd = jnp.exp(acs), jnp.exp(a_last - acs), jnp.exp(a_last)
        dcb = jnp.zeros((lc, lc), F32)
        dc_x = jnp.zeros((lc, SSD_STATE), F32)
        db_x = jnp.zeros((lc, SSD_STATE), F32)
        da_in = jnp.zeros((lc, LANES), F32)
        da_out = jnp.zeros((SSD_HPG, lc), F32)
        head_col = _iota2((lc, LANES), 1)
        head_row = _iota2((SSD_HPG, lc), 0)
        last = _iota2((SSD_HPG, lc), 1) == lc - 1
        for p in range(PAIRS):
            sl = pl.ds(p * PAIR_W, PAIR_W)
            xp, dyp, dsk = x_ref[:, sl], dy_ref[:, sl], dsk_ref[:, sl]
            dtp = _pair_cols(dt, p, lo)
            u = xp * dtp
            ub, dyb = u.astype(MXU_DT), dyp.astype(MXU_DT)
            lmat = (_decay_matrix(acs, acst, 2 * p, tri), _decay_matrix(acs, acst, 2 * p + 1, tri))
            m0, m1 = (cb * lmat[0]).astype(MXU_DT), (cb * lmat[1]).astype(MXU_DT)
            ea, dsl = _pair_cols(e_acs, p, lo), _pair_cols(e_ds, p, lo)
            dht, ht = dst_ref[p], hp_ref[p]
            dhtb, htb = dht.astype(MXU_DT), ht.astype(MXU_DT)
            bd0 = (bm * e_ds[:, 2 * p:2 * p + 1]).astype(MXU_DT)
            bd1 = (bm * e_ds[:, 2 * p + 1:2 * p + 2]).astype(MXU_DT)
            du_state = jnp.where(lo, _dot(bd0, dhtb, 1, 0), _dot(bd1, dhtb, 1, 0))
            du = jnp.where(lo, _dot(m0, dyb, 0, 0), _dot(m1, dyb, 0, 0)) + du_state
            y_off = _dot(cmb, htb, 1, 0) * ea
            ta_ref[:, sl] = dyp * y_off - u * du_state
            tx_ref[:, sl] = du * xp
            dx_ref[:, sl] = dtp * du + dsk * dyp
            dd_ref[:, sl] += jnp.sum(dyp * xp, axis=0, keepdims=True)
            dy_h = (jnp.where(lo, dyp, 0.0).astype(MXU_DT), jnp.where(lo, 0.0, dyp).astype(MXU_DT))
            carry = jnp.sum(dht * hn_ref[p], axis=0, keepdims=True)
            for hh in range(2):
                h = 2 * p + hh
                dml = _dot(dy_h[hh], ub, 1, 1) * lmat[hh]
                dcb = dcb + dml
                flow = cb * dml
                da_in = da_in + jnp.where(head_col == h, jnp.sum(flow, axis=1, keepdims=True), 0.0)
                through = jnp.sum(jnp.where(lo[:1] == (hh == 0), carry, 0.0), axis=1, keepdims=True)
                da_out = da_out + jnp.where(head_row == h, jnp.sum(flow, axis=0, keepdims=True)
                                            - jnp.where(last, through, 0.0), 0.0)
            dye = (dyp * ea).astype(MXU_DT)
            dc_x = dc_x + _dot(dye, htb, 1, 1)
            db_x = db_x + _dot((u * dsl).astype(MXU_DT), dhtb, 1, 1)
            dst_ref[p] = dht * _pair_cols(e_cd, p, lo[:1]) + _dot(cmb, dye, 0, 0)
        dcbb = dcb.astype(MXU_DT)
        dc_ref[...] = _dot(dcbb, bmb, 1, 0) + dc_x
        db_ref[...] = _dot(dcbb, cmb, 0, 0) + db_x
        seg_lo = _iota2((GROUP_W, LANES), 1) * SSD_HEAD_DIM
        chan = _iota2((GROUP_W, LANES), 0)
        seg = jnp.logical_and(chan >= seg_lo, chan < seg_lo + SSD_HEAD_DIM).astype(F32)
        da_in = da_in + jnp.dot(ta_ref[...], seg, precision=HIGHEST, preferred_element_type=F32)
        daq_ref[...] = da_in[:, :SSD_HPG]
        dar_ref[...] = da_out
        ddtx_ref[...] = jnp.dot(tx_ref[...], seg, precision=HIGHEST, preferred_element_type=F32)[:, :SSD_HPG]

    rev = lambda c: nc - 1 - c
    small = pl.BlockSpec((None, lc, SSD_HPG), lambda g, c: (g, rev(c), 0))
    small_t = pl.BlockSpec((None, SSD_HPG, lc), lambda g, c: (g, 0, rev(c)))
    wide = pl.BlockSpec((lc, GROUP_W), lambda g, c: (rev(c), g))
    state = lambda at: pl.BlockSpec((None, PAIRS, SSD_STATE, PAIR_W), lambda g, c: (at(c), g, 0, 0))
    return pl.pallas_call(
        body, name=name, grid=(SSD_GROUPS, nc),
        in_specs=[pl.BlockSpec((lc, GROUP_W), lambda g, c: (rev(c), g)),
                  pl.BlockSpec((lc, LANES), lambda g, c: (rev(c), bt + g)),
                  pl.BlockSpec((lc, LANES), lambda g, c: (rev(c), bt + SSD_GROUPS + g)),
                  small, small, small_t, pl.BlockSpec((1, GROUP_W), lambda g, c: (0, g)), wide,
                  state(rev), state(lambda c: jnp.minimum(rev(c) + 1, nc - 1))],
        out_specs=[wide, pl.BlockSpec((lc, LANES), lambda g, c: (rev(c), g)),
                   pl.BlockSpec((lc, LANES), lambda g, c: (rev(c), g)), small, small_t, small,
                   pl.BlockSpec((1, GROUP_W), lambda g, c: (0, g))],
        out_shape=[_sds((s, SSD_D_INNER)), _sds((s, SSD_GROUPS * SSD_STATE)), _sds((s, SSD_GROUPS * SSD_STATE)),
                   _sds((SSD_GROUPS, s, SSD_HPG)), _sds((SSD_GROUPS, SSD_HPG, s)), _sds((SSD_GROUPS, s, SSD_HPG)),
                   _sds((1, SSD_D_INNER))],
        scratch_shapes=[pltpu.VMEM((PAIRS, SSD_STATE, PAIR_W), F32), pltpu.VMEM((lc, GROUP_W), F32),
                        pltpu.VMEM((lc, GROUP_W), F32)],
        compiler_params=_params("parallel", "arbitrary"),
    )(xbc, xbc, xbc, dtg, acsg, acstg, d_skip, dy, hprev, hprev)


def _ssd_post(da_in, da_out, ddtx, dt, dt_raw, dt_bias, a_log, name):
    s = da_in.shape[0]
    lc = SSD_CHUNK

    def body(dain_ref, daout_ref, ddtx_ref, dt_ref, x_ref, b_ref, al_ref, ddr_ref, dal_ref, dbias_ref):
        @pl.when(pl.program_id(0) == 0)
        def _():
            dal_ref[...] = jnp.zeros(dal_ref.shape, F32)
            dbias_ref[...] = jnp.zeros(dbias_ref.shape, F32)

        upper = (_iota2((lc, lc), 0) <= _iota2((lc, lc), 1)).astype(F32)
        dda = jnp.dot(upper, dain_ref[...] - daout_ref[...], precision=HIGHEST, preferred_element_type=F32)
        a = -jnp.exp(al_ref[...])
        ddt = dda * a + ddtx_ref[...]
        dal_ref[...] += jnp.sum(dda * dt_ref[...], axis=0, keepdims=True) * a
        ddr = ddt * _sigmoid(x_ref[...] + b_ref[...])
        ddr_ref[...] = ddr.astype(ddr_ref.dtype)
        dbias_ref[...] += jnp.sum(ddr, axis=0, keepdims=True)

    row = pl.BlockSpec((lc, LANES), lambda i: (i, 0))
    one = pl.BlockSpec((1, LANES), lambda i: (0, 0))
    return pl.pallas_call(
        body, name=name, grid=(s // lc,), in_specs=[row, row, row, row, row, one, one], out_specs=[row, one, one],
        out_shape=[_sds((s, LANES), MXU_DT), _sds((1, LANES)), _sds((1, LANES))],
        compiler_params=_params("arbitrary"),
    )(da_in, da_out, ddtx, dt, dt_raw, dt_bias, a_log)


NORM_GROUP_W = SSD_D_INNER // SSD_GROUPS


def _group_rstd(yz):
    return [lax.rsqrt(jnp.mean(jnp.square(yz[:, g * NORM_GROUP_W:(g + 1) * NORM_GROUP_W]), axis=-1, keepdims=True) + EPS)
            for g in range(SSD_GROUPS)]


def _gated_norm_fwd(y, proj, norm_g, name):
    def fn(yv, z, gv):
        yz = yv * (z * _sigmoid(z))
        parts = [yz[:, g * NORM_GROUP_W:(g + 1) * NORM_GROUP_W] * r for g, r in enumerate(_group_rstd(yz))]
        return (jnp.concatenate(parts, axis=1) * gv,)
    return _rowwise(fn, [y, (proj, 0, SSD_D_INNER)], [norm_g.reshape(1, -1)], [_sds(y.shape, MXU_DT)], [],
                    name=name)[0]


def _gated_norm_bwd(y, proj, norm_g, dout, name):
    def fn(yv, z, do, gv):
        sg = _sigmoid(z)
        sz = z * sg
        yz = yv * sz
        dog = do * gv
        dyz, dg = [], []
        for g, r in enumerate(_group_rstd(yz)):
            cols = slice(g * NORM_GROUP_W, (g + 1) * NORM_GROUP_W)
            yzg, dogg = yz[:, cols], dog[:, cols]
            dyz.append(r * dogg - yzg * (r * r * r) * jnp.mean(yzg * dogg, axis=-1, keepdims=True))
            dg.append(jnp.sum(do[:, cols] * yzg * r, axis=0, keepdims=True))
        dyz = jnp.concatenate(dyz, axis=1)
        return dyz * sz, dyz * yv * (sg * (1.0 + z * (1.0 - sg))), jnp.concatenate(dg, axis=1)
    return _rowwise(fn, [y, (proj, 0, SSD_D_INNER), dout], [norm_g.reshape(1, -1)],
                    [_sds(y.shape), _sds(y.shape, MXU_DT)], [_sds((1, y.shape[1]))], name=name)


def _ssd_fwd(h, p, big):
    hn = _rms_fwd(h, p["norm_g"], "mix_norm_fwd")
    w_in = big.whole("ssd_w_in", hn)
    w_zx, w_dt = w_in[:, :SSD_ZX], _pad_cols(w_in[:, SSD_ZX:])
    proj = _matmul(hn, w_zx, tm=1024, name="ssd_in_zx")
    dt_raw = _matmul(hn, w_dt, name="ssd_in_dt")
    conv_w = big.whole("ssd_conv_w", proj)
    xbc = _dwconv_silu_fwd(proj, SSD_D_INNER, SSD_CONV_DIM, conv_w, p["conv_b"], "ssd_conv_fwd")
    dt, acs, acst = _ssd_prep(dt_raw, _lane_pad(p["dt_bias"]), _lane_pad(p["a_log"]), "ssd_prep")
    dtg, acsg = _heads_to_groups(dt), _heads_to_groups(acs)
    acstg = acst[:SSD_HEADS].reshape(SSD_GROUPS, SSD_HPG, -1)
    d_skip = jnp.repeat(p["d"], SSD_HEAD_DIM).reshape(1, -1)
    y, hprev = _ssd_scan_fwd(xbc, dtg, acsg, acstg, d_skip, "ssd_scan_fwd")
    yn = _gated_norm_fwd(y, proj, p["gnorm_g"], "ssd_gnorm_fwd")
    w_out = big.whole("ssd_w_out", yn)
    h_out = _matmul(yn, w_out, res=h, name="ssd_out")
    return h_out, (hn, proj, dt_raw, xbc, dt, dtg, acsg, acstg, d_skip, y, hprev, yn, w_zx, w_dt, conv_w, w_out)


def _ssd_bwd(h, dh, saved, p, out):
    hn, proj, dt_raw, xbc, dt, dtg, acsg, acstg, d_skip, y, hprev, yn, w_zx, w_dt, conv_w, w_out = saved
    dh, dh_b = dh
    dyn = _matmul(dh_b, w_out, tb=True, name="ssd_out_dx")
    out.send("ssd_w_out", _matmul(yn, dh_b, ta=True, out_dtype=MXU_DT, name="ssd_out_dw"))
    dy, dz, dgnorm = _gated_norm_bwd(y, proj, out.tie(p["gnorm_g"]), dyn, "ssd_gnorm_bwd")
    dx, dbm, dcm, daq, dar, ddtx, dd = _ssd_scan_bwd(xbc, dtg, acsg, acstg, d_skip, dy, hprev, "ssd_scan_bwd")
    dxbc = jnp.concatenate([dx, dbm, dcm], axis=1)
    dpre, dconv_w, dconv_b = _dwconv_silu_bwd(proj, SSD_D_INNER, SSD_CONV_DIM, dxbc, conv_w, p["conv_b"],
                                              "ssd_conv_bwd")
    ddr, dalog, dbias = _ssd_post(_groups_to_heads(daq), _groups_to_heads(dar.transpose(0, 2, 1)),
                                  _groups_to_heads(ddtx), dt, dt_raw,
                                  _lane_pad(p["dt_bias"]), _lane_pad(p["a_log"]), "ssd_post")
    w_z, w_x = w_zx[:, :SSD_D_INNER], w_zx[:, SSD_D_INNER:]
    dhn = _matmul(dz, w_z, tb=True, name="ssd_in_dx_z")
    dhn = _matmul(dpre, w_x, tb=True, res=dhn, name="ssd_in_dx_x")
    dhn = _matmul(ddr, w_dt, tb=True, res=dhn, name="ssd_in_dx_dt")
    out.send("ssd_w_in", jnp.concatenate(
        [_matmul(hn, dz, ta=True, out_dtype=MXU_DT, name="ssd_in_dw_z"),
         _matmul(hn, dpre, ta=True, tm=1024, out_dtype=MXU_DT, name="ssd_in_dw_x"),
         _matmul(hn, ddr, ta=True, out_dtype=MXU_DT, name="ssd_in_dw_dt")[:, :SSD_HEADS]], axis=1))
    *dh_in, dnorm_g = _rms_bwd(h, out.tie(p["norm_g"]), dhn, dh, "mix_norm_bwd")
    grads = dict(norm_g=dnorm_g, conv_w=dconv_w, conv_b=dconv_b, dt_bias=dbias[:, :SSD_HEADS],
                 a_log=dalog[:, :SSD_HEADS], d=dd.reshape(SSD_HEADS, SSD_HEAD_DIM).sum(axis=1).reshape(1, -1),
                 gnorm_g=dgnorm)
    return dh_in, grads


FOX_PAIRS = FOX_HEADS // 2
ATT_TQ = 512
ATT_BWD_TQ = 1024
ATT_TK = 512
ATT_SUB = 512
FOX_PREP_ROWS = 256
NEG_BIG = -1e30
FOX_SCALE = FOX_HEAD_DIM ** -0.5
FOX_AUG_D = FOX_HEADS * LANES
QSIDE = FOX_HEAD_DIM
KSIDE = FOX_HEAD_DIM + 3


def _split3(x):
    a = x.astype(MXU_DT).astype(F32)
    b = (x - a).astype(MXU_DT).astype(F32)
    return a, b, (x - a - b).astype(MXU_DT).astype(F32)


def _head_tiles(pair_tile):
    return pair_tile, pltpu.roll(pair_tile, FOX_HEAD_DIM, 1)


def _fill_lanes(base, lane, first, values):
    for i, v in enumerate(values):
        base = jnp.where(lane == first + i, v, base)
    return base


def _pair_tile(lane, tile0, tile1):
    return jnp.where(lane < FOX_HEAD_DIM, tile0, pltpu.roll(tile1, FOX_HEAD_DIM, 1))


def _compact_heads(a, lane):
    return jnp.concatenate([_pair_tile(lane, a[:, 2 * j * LANES:(2 * j + 1) * LANES],
                                       a[:, (2 * j + 1) * LANES:(2 * j + 2) * LANES]) for j in range(FOX_PAIRS)], axis=1)


def _head_sum_matrix():
    return (_iota2((LANES, LANES), 0) < FOX_HEAD_DIM) == (_iota2((LANES, LANES), 1) < FOX_HEAD_DIM)


def _head_sums(x):
    bd = _head_sum_matrix().astype(F32)
    parts = [jnp.dot(x[:, j * LANES:(j + 1) * LANES], bd, precision=HIGHEST, preferred_element_type=F32)
             for j in range(x.shape[1] // LANES)]
    return parts[0] if len(parts) == 1 else jnp.concatenate(parts, axis=1)


def _fox_prep_fwd(proj, f_raw, qg, kg, b_f, name):
    s = proj.shape[0]
    tr = min(FOX_PREP_ROWS, s)

    def body(q_ref, k_ref, v_ref, f_ref, qg_ref, kg_ref, b_ref, qa_ref, ka_ref, va_ref, carry_ref):
        @pl.when(pl.program_id(0) == 0)
        def _():
            carry_ref[...] = jnp.zeros(carry_ref.shape, F32)

        normed = []
        for x_ref, g_ref in ((q_ref, qg_ref), (k_ref, kg_ref)):
            x = x_ref[...]
            r = lax.rsqrt(_head_sums(x * x) * (1.0 / FOX_HEAD_DIM) + EPS)
            normed.append(x * r * g_ref[...])
        qn, kn, v = normed[0] * FOX_SCALE, normed[1], v_ref[...]
        x = f_ref[...] + b_ref[...]
        lf = jnp.minimum(x, 0.0) - jnp.log1p(jnp.exp(-jnp.abs(x)))
        lower = (_iota2((tr, tr), 0) >= _iota2((tr, tr), 1)).astype(F32)
        cum = jnp.dot(lower, lf, precision=HIGHEST, preferred_element_type=F32) + carry_ref[...]
        carry_ref[...] += jnp.sum(lf, axis=0, keepdims=True)
        first = _iota2((LANES, FOX_D), 0) * FOX_HEAD_DIM
        chan = _iota2((LANES, FOX_D), 1)
        spread = jnp.logical_and(chan >= first, chan < first + FOX_HEAD_DIM).astype(F32)
        cum = jnp.dot(cum, spread, precision=HIGHEST, preferred_element_type=F32)
        lane = _iota2((tr, LANES), 1)
        ones = jnp.where(jnp.logical_and(lane >= QSIDE, lane < KSIDE + 3), 1.0, 0.0)
        for j in range(FOX_PAIRS):
            cols = slice(j * LANES, (j + 1) * LANES)
            tiles = zip(_head_tiles(qn[:, cols]), _head_tiles(kn[:, cols]), _head_tiles(v[:, cols]),
                        reversed(_head_tiles(cum[:, cols])))
            for hh, (qt, kt, vt, ct) in enumerate(tiles):
                out = slice((2 * j + hh) * LANES, (2 * j + hh + 1) * LANES)
                c3 = _split3(ct)
                head = lane < FOX_HEAD_DIM
                qa_ref[:, out] = _fill_lanes(jnp.where(head, qt, ones), lane, QSIDE, c3).astype(qa_ref.dtype)
                ka_ref[:, out] = _fill_lanes(jnp.where(head, kt, ones), lane, KSIDE, [-c for c in c3]).astype(ka_ref.dtype)
                va_ref[:, out] = jnp.where(head, vt, jnp.where(lane < KSIDE, 1.0, 0.0)).astype(va_ref.dtype)

    wide = lambda cb: pl.BlockSpec((tr, FOX_D), lambda i: (i, cb))
    aug = pl.BlockSpec((tr, FOX_AUG_D), lambda i: (i, 0))
    one = lambda n: pl.BlockSpec((1, n), lambda i: (0, 0))
    return pl.pallas_call(
        body, name=name, grid=(s // tr,),
        in_specs=[wide(0), wide(1), wide(2), pl.BlockSpec((tr, LANES), lambda i: (i, 0)), one(FOX_D), one(FOX_D),
                  one(LANES)],
        out_specs=[aug, aug, aug], out_shape=[_sds((s, FOX_AUG_D), MXU_DT)] * 3,
        scratch_shapes=[pltpu.VMEM((1, LANES), F32)],
        compiler_params=_params("arbitrary"),
    )(proj, proj, proj, f_raw, qg, kg, b_f)


def _fox_prep_bwd(proj, f_raw, qg, kg, b_f, dqa, dka, name):
    s = proj.shape[0]
    tr = min(FOX_PREP_ROWS, s)
    nb = s // tr

    def body(q_ref, k_ref, f_ref, qg_ref, kg_ref, b_ref, dqa_ref, dka_ref,
             dq_ref, dk_ref, df_ref, dqg_ref, dkg_ref, db_ref, carry_ref):
        @pl.when(pl.program_id(0) == 0)
        def _():
            carry_ref[...] = jnp.zeros(carry_ref.shape, F32)
            dqg_ref[...] = jnp.zeros(dqg_ref.shape, F32)
            dkg_ref[...] = jnp.zeros(dkg_ref.shape, F32)
            db_ref[...] = jnp.zeros(db_ref.shape, F32)

        lane = _iota2((tr, LANES), 1)
        for x_ref, g_ref, dt_ref, scale, dx_ref, dg_ref in ((q_ref, qg_ref, dqa_ref, FOX_SCALE, dq_ref, dqg_ref),
                                                            (k_ref, kg_ref, dka_ref, 1.0, dk_ref, dkg_ref)):
            x, dy = x_ref[...], _compact_heads(dt_ref[...], lane) * scale
            r = lax.rsqrt(_head_sums(x * x) * (1.0 / FOX_HEAD_DIM) + EPS)
            dyg = dy * g_ref[...]
            dx = r * dyg - x * (r * r * r) * (_head_sums(x * dyg) * (1.0 / FOX_HEAD_DIM))
            dx_ref[...] = dx.astype(dx_ref.dtype)
            dg_ref[...] += jnp.sum(dy * x * r, axis=0, keepdims=True)
        dc = jnp.zeros((tr, LANES), F32)
        for h in range(FOX_HEADS):
            sums = dqa_ref[:, pl.ds(h * LANES + QSIDE, 1)] - dka_ref[:, pl.ds(h * LANES + KSIDE, 1)]
            dc = jnp.where(lane == h, sums, dc)
        upper = (_iota2((tr, tr), 0) <= _iota2((tr, tr), 1)).astype(F32)
        dlf = jnp.dot(upper, dc, precision=HIGHEST, preferred_element_type=F32) + carry_ref[...]
        carry_ref[...] += jnp.sum(dc, axis=0, keepdims=True)
        df = dlf * _sigmoid(-(f_ref[...] + b_ref[...]))
        df_ref[...] = df.astype(df_ref.dtype)
        db_ref[...] += jnp.sum(df, axis=0, keepdims=True)

    wide = lambda cb: pl.BlockSpec((tr, FOX_D), lambda i: (nb - 1 - i, cb))
    aug = pl.BlockSpec((tr, FOX_AUG_D), lambda i: (nb - 1 - i, 0))
    row = pl.BlockSpec((tr, LANES), lambda i: (nb - 1 - i, 0))
    one = lambda n: pl.BlockSpec((1, n), lambda i: (0, 0))
    return pl.pallas_call(
        body, name=name, grid=(nb,),
        in_specs=[wide(0), wide(1), row, one(FOX_D), one(FOX_D), one(LANES), aug, aug],
        out_specs=[wide(0), wide(0), row, one(FOX_D), one(FOX_D), one(LANES)],
        out_shape=[_sds((s, FOX_D), MXU_DT), _sds((s, FOX_D), MXU_DT), _sds((s, LANES), MXU_DT),
                   _sds((1, FOX_D)), _sds((1, FOX_D)), _sds((1, LANES))],
        scratch_shapes=[pltpu.VMEM((1, LANES), F32)],
        compiler_params=_params("arbitrary"),
    )(proj, proj, f_raw, qg, kg, b_f, dqa, dka)


def _fox_attn_fwd(qa, ka, va, proj, name):
    s = qa.shape[0]
    tq, tk = min(ATT_TQ, s), min(ATT_TK, s)
    assert s % tq == 0 and s % tk == 0
    gt = 3 * FOX_D // LANES
    head_lanes = [slice(hh * LANES, (hh + 1) * LANES) for hh in range(2)]

    def body(qa_ref, ka_ref, va_ref, g_ref, o_ref, og_ref, qb_ref):
        qi = pl.program_id(1)
        sub = min(ATT_SUB, tq)
        lane = _iota2((sub, LANES), 1)
        ahead = _iota2((sub, tk), 0) - _iota2((sub, tk), 1)
        chains = [(hh, r0) for hh in range(2) for r0 in range(0, tq, sub)]
        q = [qa_ref[pl.ds(r0, sub), head_lanes[hh]] for hh, r0 in chains]

        def kv_step(j, carry, masked):
            rows = pl.ds(pl.multiple_of(j * tk, tk), tk)
            out = []
            for c, (hh, r0) in enumerate(chains):
                m, acc = carry[2 * c:2 * c + 2]
                sc = _dot(q[c], ka_ref[rows, head_lanes[hh]], 1, 1)
                if masked:
                    sc = jnp.where(ahead >= j * tk - qi * tq - r0, sc, NEG_BIG)
                m_new = jnp.maximum(m, jnp.max(sc, axis=1, keepdims=True))
                pr = jnp.exp(sc - m_new).astype(MXU_DT)
                out += [m_new, jnp.exp(m - m_new) * acc + _dot(pr, va_ref[rows, head_lanes[hh]], 1, 0)]
            return tuple(out)

        n_clear = lax.div(qi * tq, tk)
        n_all = lax.div((qi + 1) * tq + tk - 1, tk)
        init = (jnp.full((sub, 1), NEG_BIG, F32), jnp.zeros((sub, LANES), F32)) * len(chains)
        carry = lax.fori_loop(0, n_clear, functools.partial(kv_step, masked=False), init)
        carry = lax.fori_loop(n_clear, n_all, functools.partial(kv_step, masked=True), carry)
        heads = [[], []]
        for c, (hh, r0) in enumerate(chains):
            m, acc = carry[2 * c:2 * c + 2]
            l = acc[:, QSIDE:QSIDE + 1]
            heads[hh].append(acc / l)
            qf = q[c].astype(F32)
            bias = qf[:, QSIDE:QSIDE + 1] + qf[:, QSIDE + 1:QSIDE + 2] + qf[:, QSIDE + 2:QSIDE + 3]
            qb_ref[pl.ds(r0, sub), head_lanes[hh]] = _fill_lanes(
                qf, lane, QSIDE, _split3(bias - (m + jnp.log(l)))).astype(qb_ref.dtype)
        heads = [jnp.concatenate(h, axis=0) for h in heads]
        o = _pair_tile(_iota2((tq, LANES), 1), heads[0], heads[1])
        o_ref[...] = o
        og_ref[...] = (o * _sigmoid(g_ref[...])).astype(og_ref.dtype)

    blk2 = pl.BlockSpec((tq, 2 * LANES), lambda p, i: (i, p))
    seq2 = pl.BlockSpec((s, 2 * LANES), lambda p, i: (0, p))
    blk = pl.BlockSpec((tq, LANES), lambda p, i: (i, p))
    return pl.pallas_call(
        body, name=name, grid=(FOX_PAIRS, s // tq),
        in_specs=[blk2, seq2, seq2, pl.BlockSpec((tq, LANES), lambda p, i: (i, gt + p))],
        out_specs=[blk, blk, blk2],
        out_shape=[_sds((s, FOX_D)), _sds((s, FOX_D), MXU_DT), _sds((s, FOX_AUG_D), MXU_DT)],
        compiler_params=_params("parallel", "parallel"),
    )(qa, ka, va, proj)


def _fox_gate_bwd(dog, o, proj, name):
    def fn(dogv, ov, gate):
        sg = _sigmoid(gate)
        do = dogv * sg
        delta = _head_sums(do * ov)
        lane = _iota2((do.shape[0], LANES), 1)
        tiles = []
        for j in range(FOX_PAIRS):
            cols = slice(j * LANES, (j + 1) * LANES)
            for dt, dl in zip(_head_tiles(do[:, cols]), reversed(_head_tiles(delta[:, cols]))):
                tiles.append(_fill_lanes(jnp.where(lane < FOX_HEAD_DIM, dt, 0.0), lane, QSIDE,
                                         [-d for d in _split3(dl)]))
        return dogv * ov * sg * (1.0 - sg), jnp.concatenate(tiles, axis=1)
    return _rowwise(fn, [dog, o, (proj, 3, FOX_D)], [], [_sds(o.shape, MXU_DT), _sds((o.shape[0], FOX_AUG_D), MXU_DT)],
                    [], name=name)


def _fox_attn_bwd(qb, ka, va, doa, name):
    s = qb.shape[0]
    tq, tk = min(ATT_BWD_TQ, s), min(ATT_TK, s)
    nq, nk = s // tq, s // tk
    head_lanes = [slice(hh * LANES, (hh + 1) * LANES) for hh in range(2)]

    def body(qb_ref, doa_ref, ka_ref, va_ref, dqa_ref, dka_ref, dv_ref):
        kj = pl.program_id(1)

        @pl.when(kj == 0)
        def _():
            dqa_ref[...] = jnp.zeros(dqa_ref.shape, F32)

        ahead = _iota2((tq, tk), 0) - _iota2((tq, tk), 1)
        kb = [ka_ref[:, hs] for hs in head_lanes]
        vb = [va_ref[:, hs] for hs in head_lanes]

        def q_step(i, carry, masked):
            rows = pl.ds(pl.multiple_of(i * tq, tq), tq)
            out = []
            for hh, hs in enumerate(head_lanes):
                dk, dv = carry[2 * hh:2 * hh + 2]
                q, do = qb_ref[rows, hs], doa_ref[rows, hs]
                pr = jnp.exp(_dot(q, kb[hh], 1, 1))
                if masked:
                    pr = jnp.where(ahead >= kj * tk - i * tq, pr, 0.0)
                dv = dv + _dot(pr.astype(MXU_DT), do, 0, 0)
                ds = (pr * _dot(do, vb[hh], 1, 1)).astype(MXU_DT)
                dqa_ref[rows, hs] += _dot(ds, kb[hh], 1, 0)
                out += [dk + _dot(ds, q, 0, 0), dv]
            return tuple(out)

        first = lax.div(kj * tk, tq)
        n_masked = lax.div((kj + 1) * tk + tq - 1, tq)
        carry = lax.fori_loop(first, n_masked, functools.partial(q_step, masked=True),
                              (jnp.zeros((tk, LANES), F32),) * 4)
        dk0, dv0, dk1, dv1 = lax.fori_loop(n_masked, nq, functools.partial(q_step, masked=False), carry)
        dka_ref[:, head_lanes[0]] = dk0
        dka_ref[:, head_lanes[1]] = dk1
        dv_ref[...] = _pair_tile(_iota2((tk, LANES), 1), dv0, dv1).astype(dv_ref.dtype)

    seq2 = pl.BlockSpec((s, 2 * LANES), lambda p, j: (0, p))
    blk2 = pl.BlockSpec((tk, 2 * LANES), lambda p, j: (j, p))
    return pl.pallas_call(
        body, name=name, grid=(FOX_PAIRS, nk), in_specs=[seq2, seq2, blk2, blk2],
        out_specs=[seq2, blk2, pl.BlockSpec((tk, LANES), lambda p, j: (j, p))],
        out_shape=[_sds((s, FOX_AUG_D)), _sds((s, FOX_AUG_D)), _sds((s, FOX_D), MXU_DT)],
        compiler_params=_params("parallel", "arbitrary"),
    )(qb, doa, ka, va)


def _fox_fwd(h, p, big):
    hn = _rms_fwd(h, p["norm_g"], "mix_norm_fwd")
    w_in = big.whole("fox_w_in", hn)
    w_qkvg, w_f = w_in[:, :4 * FOX_D], _pad_cols(w_in[:, 4 * FOX_D:])
    proj = _matmul(hn, w_qkvg, tm=1024, name="fox_in_qkvg")
    f_raw = _matmul(hn, w_f, name="fox_in_f")
    qg = jnp.tile(p["q_norm_g"], FOX_HEADS).reshape(1, -1)
    kg = jnp.tile(p["k_norm_g"], FOX_HEADS).reshape(1, -1)
    qa, ka, va = _fox_prep_fwd(proj, f_raw, qg, kg, _lane_pad(p["b_f"]), "fox_prep_fwd")
    o, og, qb = _fox_attn_fwd(qa, ka, va, proj, "fox_attn_fwd")
    w_out = big.whole("fox_w_out", og)
    h_out = _matmul(og, w_out, res=h, name="fox_out")
    return h_out, (hn, proj, f_raw, qg, kg, ka, va, qb, o, og, w_qkvg, w_f, w_out)


def _fox_bwd(h, dh, saved, p, out):
    hn, proj, f_raw, qg, kg, ka, va, qb, o, og, w_qkvg, w_f, w_out = saved
    s = h.shape[0]
    dh, dh_b = dh
    dog = _matmul(dh_b, w_out, tb=True, name="fox_out_dx")
    out.send("fox_w_out", _matmul(og, dh_b, ta=True, out_dtype=MXU_DT, name="fox_out_dw"))
    dgate, doa = _fox_gate_bwd(dog, o, proj, "fox_gate_bwd")
    dqa, dka, dv = _fox_attn_bwd(qb, ka, va, doa, "fox_attn_bwd")
    dq, dk, df, dqg, dkg, dbf = _fox_prep_bwd(proj, f_raw, qg, kg, out.tie(_lane_pad(p["b_f"])), dqa, dka,
                                              "fox_prep_bwd")
    dproj = jnp.concatenate([dq, dk, dv, dgate], axis=1)
    dhn = _matmul(dproj, w_qkvg, tb=True, name="fox_in_dx_qkvg")
    dhn = _matmul(df, w_f, tb=True, res=dhn, name="fox_in_dx_f")
    out.send("fox_w_in", jnp.concatenate(
        [_matmul(hn, dproj, ta=True, tm=1024, out_dtype=MXU_DT, name="fox_in_dw_qkvg"),
         _matmul(hn, df, ta=True, out_dtype=MXU_DT, name="fox_in_dw_f")[:, :FOX_HEADS]], axis=1))
    *dh_in, dnorm_g = _rms_bwd(h, out.tie(p["norm_g"]), dhn, dh, "mix_norm_bwd")
    fold = lambda g: g.reshape(FOX_HEADS, FOX_HEAD_DIM).sum(axis=0).reshape(1, -1)
    grads = dict(norm_g=dnorm_g, b_f=dbf[:, :FOX_HEADS], q_norm_g=fold(dqg), k_norm_g=fold(dkg))
    return dh_in, grads


def _my_index():
    return 4 * lax.axis_index("x") + 2 * lax.axis_index("y") + lax.axis_index("c")


def _peer(k):
    x, y, c = lax.axis_index("x"), lax.axis_index("y"), lax.axis_index("c")
    flip = lambda v, bit: 1 - v if bit else v
    return (flip(x, k & 4), flip(y, k & 2), flip(c, k & 1))


SEM_SPEC = pl.BlockSpec(memory_space=pltpu.SEMAPHORE)
DATAFLOW = pltpu.SideEffectType.DATAFLOW_SIDE_EFFECTING


def _at(ref, idx):
    return ref.at[tuple(idx)] if idx else ref


def _copies_start(name, srcs, lands, groups):
    ns, nl = len(srcs), len(lands)
    items = [item for group in groups for item in group]

    def body(*refs):
        src_refs, land_refs = refs[:ns], refs[ns:ns + nl]
        sems = refs[ns + nl:ns + nl + 2 * len(items)]
        me = _my_index()
        for t, (i, src_slot, j, dst_slot, _) in enumerate(items):
            for k in range(1, N_DEV):
                pltpu.make_async_remote_copy(
                    src_ref=_at(src_refs[i], src_slot(me, k)), dst_ref=_at(land_refs[j], dst_slot(me, k)),
                    send_sem=sems[2 * t], recv_sem=sems[2 * t + 1], device_id=_peer(k),
                    device_id_type=MESH_IDS).start()
        refs[-1][...] = jnp.zeros(refs[-1].shape, F32)

    hbm = pl.BlockSpec(memory_space=pltpu.HBM)
    bufs = [pltpu.with_memory_space_constraint(a, pltpu.HBM) for a in list(srcs) + list(lands)]
    n_sems = 2 * len(items)
    out = pl.pallas_call(
        body, name=name, in_specs=[hbm] * (ns + nl),
        out_specs=(*[SEM_SPEC] * n_sems, *[hbm] * (ns + nl), pl.BlockSpec(memory_space=pltpu.VMEM)),
        out_shape=(*[pltpu.SemaphoreType.DMA(())] * n_sems, *[pltpu.HBM(a.shape, a.dtype) for a in bufs],
                   _sds((8, LANES))),
        input_output_aliases={i: n_sems + i for i in range(ns + nl)},
        compiler_params=pltpu.CompilerParams(has_side_effects=DATAFLOW),
    )(*bufs)
    sems, t = [], 0
    for group in groups:
        sems.append([(out[2 * (t + u)], out[2 * (t + u) + 1]) for u in range(len(group))])
        t += len(group)
    return sems, list(out[n_sems:n_sems + ns]), list(out[n_sems + ns:n_sems + ns + nl]), out[-1]


def _copies_wait(name, keep, lands, sems, group, after):
    nk, nl, n = len(keep), len(lands), len(group)

    def body(*refs):
        land_refs = refs[nk:nk + nl]
        sem_refs = refs[nk + nl:nk + nl + 2 * n]
        me = _my_index()
        copies = []
        for t, (_, _, j, _, seven) in enumerate(group):
            blocks = _at(land_refs[j], seven(me))
            copies.append(pltpu.make_async_remote_copy(src_ref=blocks, dst_ref=blocks, send_sem=sem_refs[2 * t],
                                                       recv_sem=sem_refs[2 * t + 1], device_id=_peer(1),
                                                       device_id_type=MESH_IDS))
        for cp in copies:
            cp.wait_recv()
        for cp in copies:
            cp.wait_send()

    hbm = pl.BlockSpec(memory_space=pltpu.HBM)
    bufs = list(keep) + list(lands)
    out = pl.pallas_call(
        body, name=name, in_specs=[hbm] * (nk + nl) + [SEM_SPEC] * (2 * n) + [pl.BlockSpec(memory_space=pl.ANY)],
        out_specs=[hbm] * (nk + nl), out_shape=[pltpu.HBM(a.shape, a.dtype) for a in bufs],
        input_output_aliases={i: i for i in range(nk + nl)},
        compiler_params=pltpu.CompilerParams(has_side_effects=DATAFLOW),
    )(*bufs, *[s for pair in sems for s in pair], after)
    return list(out[nk:])


def _allreduce_small(buf, name):
    def body(in_ref, all_ref, sum_ref, send_sems, recv_sems):
        me = _my_index()
        all_ref[me] = in_ref[...]

        def copy(k, slot):
            return pltpu.make_async_remote_copy(
                src_ref=in_ref, dst_ref=all_ref.at[slot], send_sem=send_sems.at[k - 1], recv_sem=recv_sems.at[k - 1],
                device_id=_peer(k), device_id_type=MESH_IDS)

        for k in range(1, N_DEV):
            copy(k, me).start()
        for k in range(1, N_DEV):
            copy(k, jnp.bitwise_xor(me, k)).wait_recv()
        for k in range(1, N_DEV):
            copy(k, me).wait_send()
        acc = all_ref[0]
        for j in range(1, N_DEV):
            acc = acc + all_ref[j]
        sum_ref[...] = acc

    vmem = pl.BlockSpec(memory_space=pltpu.VMEM)
    return pl.pallas_call(
        body, name=name, in_specs=[vmem], out_specs=[vmem, vmem],
        out_shape=[_sds((N_DEV,) + buf.shape), _sds(buf.shape)],
        scratch_shapes=[pltpu.SemaphoreType.DMA((N_DEV - 1,)), pltpu.SemaphoreType.DMA((N_DEV - 1,))],
        compiler_params=pltpu.CompilerParams(vmem_limit_bytes=VMEM_LIMIT_BYTES),
    )(buf)[1]


def _adamw_math(w, g, m, v):
    m = ADAM_B1 * m + (1.0 - ADAM_B1) * g
    v = ADAM_B2 * v + (1.0 - ADAM_B2) * (g * g)
    m_hat = m / (1.0 - ADAM_B1 ** ADAM_STEP)
    v_hat = v / (1.0 - ADAM_B2 ** ADAM_STEP)
    return -ADAM_LR * (m_hat / (jnp.sqrt(v_hat) + ADAM_EPS) + ADAM_WD * w), m, v


def _row_tile(rows, cap=256, mult=16):
    best = None
    for t in range(mult, min(rows, cap) + 1, mult):
        if rows % t == 0:
            best = t
    assert best is not None, rows
    return best


def _adamw_sharded(w, m, v, partials, name):
    layers, rows, cols = w.shape
    tr = _row_tile(rows)

    def body(w_ref, m_ref, v_ref, p_ref, g_ref, d_ref, nm_ref, nv_ref):
        g = p_ref[0].astype(F32)
        for j in range(1, N_DEV):
            g = g + p_ref[j].astype(F32)
        delta, m_new, v_new = _adamw_math(w_ref[...], g, m_ref[...], v_ref[...])
        g_ref[...], d_ref[...], nm_ref[...], nv_ref[...] = g, delta, m_new, v_new

    blk = pl.BlockSpec((None, tr, cols), lambda l, i: (l, i, 0))
    return pl.pallas_call(
        body, name=name, grid=(layers, rows // tr),
        in_specs=[blk, blk, blk, pl.BlockSpec((N_DEV, None, tr, cols), lambda l, i: (0, l, i, 0))],
        out_specs=[blk] * 4, out_shape=[_sds(w.shape)] * 4, compiler_params=_params("parallel", "parallel"),
    )(w, m, v, partials)


def _adamw_small(w, g, m, v, name):
    def body(w_ref, g_ref, m_ref, v_ref, d_ref, nm_ref, nv_ref):
        d_ref[...], nm_ref[...], nv_ref[...] = _adamw_math(w_ref[...], g_ref[...], m_ref[...], v_ref[...])

    return pl.pallas_call(body, name=name, out_shape=[_sds(w.shape)] * 3,
                          compiler_params=_params())(w, g, m, v)


def _pack(arrays):
    flat = jnp.concatenate([a.reshape(-1).astype(F32) for a in arrays])
    pad = -flat.shape[0] % (8 * LANES)
    return jnp.pad(flat, (0, pad)).reshape(-1, LANES)


def _unpack(buf, shapes):
    flat, out, off = buf.reshape(-1), [], 0
    for shp in shapes:
        size = math.prod(shp)
        out.append(flat[off:off + size].reshape(shp))
        off += size
    return out


WEIGHTS = ["mix_norm_g", "ffn_norm_g", "ssd_w_in", "ssd_conv_w", "ssd_conv_b", "ssd_dt_bias", "ssd_a_log", "ssd_d",
           "ssd_norm_g", "ssd_w_out", "fox_w_in", "fox_b_f", "fox_q_norm_g", "fox_k_norm_g", "fox_w_out", "ffn_w_up",
           "ffn_conv_w", "ffn_conv_b", "ffn_w_down", "final_norm_g"]
BIG = ["ssd_w_in", "ssd_w_out", "fox_w_in", "fox_w_out", "ffn_w_up", "ffn_w_down"]
COLUMN_SHARDED = ["ssd_w_in", "fox_w_in", "ffn_w_up"]
CONV = ["ssd_conv_w", "ffn_conv_w"]
REPLICATED = [n for n in WEIGHTS if n not in BIG + CONV]
DEPTH = 4
KEPT_TRANSPOSED = ["ffn_w_up"]
ADAMW_ORDER = ["fox_w_out", "fox_w_in", "ffn_w_down", "ffn_w_up", "ssd_w_out", "ssd_w_in"]
LAYER_SHARDED = (["ssd_w_in", "ssd_conv_w", "ssd_w_out", "ffn_w_up", "ffn_conv_w", "ffn_w_down"],
                 ["fox_w_in", "fox_w_out", "ffn_w_up", "ffn_conv_w", "ffn_w_down"])


def _to_shards(full, on_columns):
    nl, r, c = full.shape
    if on_columns:
        return full.reshape(nl, r, N_DEV, c // N_DEV).transpose(2, 0, 1, 3)
    return full.reshape(nl, N_DEV, r // N_DEV, c).transpose(1, 0, 2, 3)


def _pad_cols(w):
    return jnp.pad(w, ((0, 0), (0, LANES - w.shape[1])))


def kernel(x, mix_norm_g, ffn_norm_g, ssd_w_in, ssd_conv_w, ssd_conv_b, ssd_dt_bias, ssd_a_log, ssd_d, ssd_norm_g, ssd_w_out, fox_w_in, fox_b_f, fox_q_norm_g, fox_k_norm_g, fox_w_out, ffn_w_up, ffn_conv_w, ffn_conv_b, ffn_w_down, final_norm_g, loss_target, m_mix_norm_g, m_ffn_norm_g, m_ssd_w_in, m_ssd_conv_w, m_ssd_conv_b, m_ssd_dt_bias, m_ssd_a_log, m_ssd_d, m_ssd_norm_g, m_ssd_w_out, m_fox_w_in, m_fox_b_f, m_fox_q_norm_g, m_fox_k_norm_g, m_fox_w_out, m_ffn_w_up, m_ffn_conv_w, m_ffn_conv_b, m_ffn_w_down, m_final_norm_g, v_mix_norm_g, v_ffn_norm_g, v_ssd_w_in, v_ssd_conv_w, v_ssd_conv_b, v_ssd_dt_bias, v_ssd_a_log, v_ssd_d, v_ssd_norm_g, v_ssd_w_out, v_fox_w_in, v_fox_b_f, v_fox_q_norm_g, v_fox_k_norm_g, v_fox_w_out, v_ffn_w_up, v_ffn_conv_w, v_ffn_conv_b, v_ffn_w_down, v_final_norm_g):
    given = dict(locals())
    w = {n: given[n] for n in WEIGHTS}
    mom = {n: given["m_" + n] for n in WEIGHTS}
    var = {n: given["v_" + n] for n in WEIGHTS}
    me = _my_index()

    sharded = BIG + CONV
    shards = [w[n].astype(MXU_DT) if n in BIG else w[n] for n in sharded]
    zones = [(n, i if n.startswith("ffn") else i // 2) for i in range(DEPTH) for n in LAYER_SHARDED[i % 2]]
    items = [[(sharded.index(n), functools.partial(lambda me, k, layer: (layer,), layer=layer), z,
               lambda me, k: (me,), lambda me: (pl.ds(0, N_DEV - 1),))] for z, (n, layer) in enumerate(zones)]
    empty = [lax.empty((N_DEV,) + shards[sharded.index(n)].shape[1:], shards[sharded.index(n)].dtype) for n, _ in zones]
    gather_sems, shards_kept, landing, _ = _copies_start("gather_start", shards, empty, items)

    def arrived(n, layer, after):
        z = zones.index((n, layer))
        item = items[z][0]
        stack, = _copies_wait(f"gather_wait_{n}_{layer}", shards_kept if z == len(zones) - 1 else [], [landing[z]],
                              gather_sems[z], [item[:2] + (0,) + item[3:]], after)
        stack = lax.dynamic_update_index_in_dim(stack, shards[sharded.index(n)][layer], me, 0)
        _, r, c = stack.shape
        on_columns = n in COLUMN_SHARDED or n in CONV
        return stack.transpose(1, 0, 2).reshape(r, N_DEV * c) if on_columns else stack.reshape(N_DEV * r, c)

    def mixer_params(i):
        j = i // 2
        if i % 2 == 0:
            return dict(norm_g=w["mix_norm_g"][i], conv_b=w["ssd_conv_b"][j], dt_bias=w["ssd_dt_bias"][j],
                        a_log=w["ssd_a_log"][j], d=w["ssd_d"][j], gnorm_g=w["ssd_norm_g"][j])
        return dict(norm_g=w["mix_norm_g"][i], b_f=w["fox_b_f"][j], q_norm_g=w["fox_q_norm_g"][j],
                    k_norm_g=w["fox_k_norm_g"][j])

    h = x[0]
    tape = []
    for i in range(DEPTH):
        big = _Weights(functools.partial(lambda n, after, i: arrived(n, i if n.startswith("ffn") else i // 2, after), i=i))
        mp, fp = mixer_params(i), dict(norm_g=w["ffn_norm_g"][i], conv_b=w["ffn_conv_b"][i])
        h_mid, mix_saved = (_ssd_fwd if i % 2 == 0 else _fox_fwd)(h, mp, big)
        h_out, ffn_saved = _ffn_fwd(h_mid, fp, big)
        tape.append((h, mp, mix_saved, h_mid, fp, ffn_saved))
        h = h_out
    *dh, dfinal_g, loss_part = _loss_head(h, w["final_norm_g"], loss_target[0], "loss_head")

    grads = {n: [None] * w[n].shape[0] for n in WEIGHTS if n not in BIG + ["final_norm_g"]}
    shard_view = lambda n, a: a.swapaxes(-1, -2) if n in KEPT_TRANSPOSED else a
    partials = {n: lax.empty((N_DEV,) + shard_view(n, w[n]).shape, MXU_DT) for n in BIG}
    in_flight = {n: [] for n in BIG}

    def send_partial(n, layer, grad):
        slots = shard_view(n, _to_shards(grad[None], n in COLUMN_SHARDED)[:, 0])
        mine = lax.dynamic_index_in_dim(slots, me, 0, keepdims=False)
        zone = lax.dynamic_update_slice(partials[n], mine[None, None], (me, layer, 0, 0))
        item = (0, lambda me, k: (jnp.bitwise_xor(me, k),), 0,
                functools.partial(lambda me, k, layer: (me, layer), layer=layer),
                functools.partial(lambda me, layer: (pl.ds(0, N_DEV - 1), layer), layer=layer))
        sems, kept, (partials[n],), token = _copies_start(f"scatter_start_{n}_{layer}", [slots], [zone], [[item]])
        in_flight[n].append((layer, sems[0], kept, item))
        return token

    for i in reversed(range(DEPTH)):
        j = i // 2
        h_in, mp, mix_saved, h_mid, fp, ffn_saved = tape[i]
        out = _Gradients(functools.partial(lambda n, grad, i: send_partial(n, i if n.startswith("ffn") else i // 2, grad),
                                           i=i))
        dh, g = _ffn_bwd(h_mid, dh, ffn_saved, fp, out)
        grads["ffn_norm_g"][i], grads["ffn_conv_w"][i], grads["ffn_conv_b"][i] = g["norm_g"][0], g["conv_w"], g["conv_b"][0]
        if i % 2 == 0:
            dh, g = _ssd_bwd(h_in, dh, mix_saved, mp, out)
            grads["ssd_conv_w"][j] = g["conv_w"]
            for key, name in (("conv_b", "ssd_conv_b"), ("dt_bias", "ssd_dt_bias"), ("a_log", "ssd_a_log"),
                              ("d", "ssd_d"), ("gnorm_g", "ssd_norm_g")):
                grads[name][j] = g[key][0]
        else:
            dh, g = _fox_bwd(h_in, dh, mix_saved, mp, out)
            for key, name in (("b_f", "fox_b_f"), ("q_norm_g", "fox_q_norm_g"), ("k_norm_g", "fox_k_norm_g")):
                grads[name][j] = g[key][0]
        grads["mix_norm_g"][i] = g["norm_g"][0]
    grads = {n: jnp.stack(v) for n, v in grads.items()}
    grads["final_norm_g"] = dfinal_g[0]

    small_names = REPLICATED + CONV
    summed = _unpack(_allreduce_small(_pack([grads[n] for n in small_names] + [loss_part]), "allreduce_small"),
                     [grads[n].shape for n in small_names] + [(1, 1)])
    loss = summed[-1][0, 0]
    g_small = dict(zip(small_names, summed[:-1]))
    for n in CONV:
        width = w[n].shape[-1]
        g_small[n] = lax.dynamic_slice_in_dim(g_small[n], me * width, width, axis=2)
    pk = lambda d: _pack([d[n] for n in small_names])
    d_small, m_small, v_small = _adamw_small(pk(w), pk(g_small), pk(mom), pk(var), "adamw_small")
    shapes = [w[n].shape for n in small_names]
    out_g, out_d, out_m, out_v = dict(g_small), {}, {}, {}
    for dst, buf in ((out_d, d_small), (out_m, m_small), (out_v, v_small)):
        dst.update(zip(small_names, _unpack(buf, shapes)))

    after = d_small
    for n in ADAMW_ORDER:
        for layer, sems, kept, item in in_flight[n]:
            partials[n], = _copies_wait(f"scatter_wait_{n}_{layer}", kept, [partials[n]], sems, [item], after)
        res = _adamw_sharded(shard_view(n, w[n]), shard_view(n, mom[n]), shard_view(n, var[n]), partials[n], "adamw_" + n)
        out_g[n], out_d[n], out_m[n], out_v[n] = [shard_view(n, r) for r in res]
        after = out_v[n]

    return (loss, dh[0][None], *[out_g[n] for n in WEIGHTS], *[out_d[n] for n in WEIGHTS],
            *[out_m[n] for n in WEIGHTS], *[out_v[n] for n in WEIGHTS])
```

```python
import functools
import math

import jax
import jax.numpy as jnp
from jax import lax
from jax.experimental import pallas as pl
from jax.experimental.pallas import tpu as pltpu

F32 = jnp.float32
MXU_DT = jnp.bfloat16
VMEM_LIMIT_BYTES = 56 * 1024 * 1024
LANES = 128
N_DEV = 8
MESH_IDS = pl.DeviceIdType.MESH

EPS = 1e-6
D_MODEL = 1024
SSD_D_INNER = 2048
SSD_HEAD_DIM = 64
SSD_HEADS = 32
SSD_GROUPS = 4
SSD_HPG = 8
SSD_STATE = 128
SSD_CONV = 4
SSD_CHUNK = 128
SSD_CONV_DIM = 3072
SSD_ZX = SSD_D_INNER + SSD_CONV_DIM
FOX_HEAD_DIM = 64
FOX_HEADS = 16
FOX_D = 1024
D_FF = 2816
FFN_CONV = 3
ADAM_LR, ADAM_B1, ADAM_B2, ADAM_EPS, ADAM_WD, ADAM_STEP = 0.001, 0.9, 0.999, 1e-08, 0.01, 10


def _params(*sem):
    return pltpu.CompilerParams(dimension_semantics=sem or None, vmem_limit_bytes=VMEM_LIMIT_BYTES)


def _sds(shape, dtype=F32):
    return jax.ShapeDtypeStruct(tuple(shape), dtype)


def _col_tile(n, cap=1536):
    best = None
    for t in range(LANES, min(n, cap) + 1, LANES):
        if n % t == 0:
            best = t
    assert best is not None, n
    return best


def _sigmoid(x):
    return 0.5 * jnp.tanh(0.5 * x) + 0.5


def _matmul(a, b, *, ta=False, tb=False, res=None, out_dtype=F32, tm=512, tn=None, b_kblock=0, name):
    (kdim, m) = a.shape if ta else a.shape[::-1]
    (n, k2) = b.shape if tb else b.shape[::-1]
    assert kdim == k2 or (tb and k2 % kdim == 0), (a.shape, b.shape, ta, tb)
    tm = min(tm, m)
    if m % tm:
        tm = _col_tile(m, tm)
    tn = tn or _col_tile(n)
    assert m % tm == 0 and n % tn == 0, (m, tm, n, tn)
    dims = (((0 if ta else 1,), (1 if tb else 0,)), ((), ()))

    def body(*refs):
        a_ref, b_ref = refs[0], refs[1]
        o_ref = refs[-1]
        acc = lax.dot_general(a_ref[...].astype(MXU_DT), b_ref[...].astype(MXU_DT), dims,
                              preferred_element_type=F32)
        if res is not None:
            acc = acc + refs[2][...].astype(F32)
        o_ref[...] = acc.astype(o_ref.dtype)

    a_spec = pl.BlockSpec((kdim, tm), lambda i, j: (0, i)) if ta else pl.BlockSpec((tm, kdim), lambda i, j: (i, 0))
    b_spec = pl.BlockSpec((tn, kdim), lambda i, j: (j, b_kblock)) if tb else pl.BlockSpec((kdim, tn), lambda i, j: (0, j))
    o_spec = pl.BlockSpec((tm, tn), lambda i, j: (i, j))
    ins, specs = [a, b], [a_spec, b_spec]
    if res is not None:
        ins.append(res)
        specs.append(o_spec)
    return pl.pallas_call(
        body, name=name, grid=(m // tm, n // tn), in_specs=specs, out_specs=o_spec,
        out_shape=_sds((m, n), out_dtype), compiler_params=_params("parallel", "parallel"),
    )(*ins)


ROW_CHUNK = 64


def _rowwise(fn, rows, consts, out_rows, out_sums, *, tr=256, name):
    rows = [r if isinstance(r, tuple) else (r, 0, r.shape[1]) for r in rows]
    s = rows[0][0].shape[0]
    tr = min(tr, s)
    assert s % tr == 0
    n_in, n_c, n_or = len(rows), len(consts), len(out_rows)

    chunk = min(ROW_CHUNK, tr)

    def body(*refs):
        consts_v = [r[...] for r in refs[n_in:n_in + n_c]]
        o_refs = refs[n_in + n_c:]
        outs = None
        for r0 in range(0, tr, chunk):
            rows_v = [r[pl.ds(r0, chunk), :] for r in refs[:n_in]]
            part = fn(*rows_v, *consts_v)
            for o_ref, val in zip(o_refs[:n_or], part[:n_or]):
                o_ref[pl.ds(r0, chunk), :] = val.astype(o_ref.dtype)
            outs = part if outs is None else tuple(part[:n_or]) + tuple(a + b for a, b in zip(outs[n_or:], part[n_or:]))
        if out_sums:
            first = pl.program_id(0) == 0

            @pl.when(first)
            def _():
                for o_ref, val in zip(o_refs[n_or:], outs[n_or:]):
                    o_ref[...] = val.astype(o_ref.dtype)

            @pl.when(jnp.logical_not(first))
            def _():
                for o_ref, val in zip(o_refs[n_or:], outs[n_or:]):
                    o_ref[...] += val.astype(o_ref.dtype)

    in_specs = [pl.BlockSpec((tr, width), functools.partial(lambda i, cb: (i, cb), cb=cb)) for _, cb, width in rows]
    in_specs += [pl.BlockSpec(c.shape, lambda i: (0, 0)) for c in consts]
    out_specs = [pl.BlockSpec((tr, o.shape[1]), lambda i: (i, 0)) for o in out_rows]
    out_specs += [pl.BlockSpec(o.shape, lambda i: (0, 0)) for o in out_sums]
    return pl.pallas_call(
        body, name=name, grid=(s // tr,), in_specs=in_specs, out_specs=out_specs,
        out_shape=list(out_rows) + list(out_sums),
        compiler_params=_params("arbitrary" if out_sums else "parallel"),
    )(*[r[0] for r in rows], *consts)


def _rms_fwd(h, g, name):
    def fn(x, gv):
        r = lax.rsqrt(jnp.mean(x * x, axis=-1, keepdims=True) + EPS)
        return (x * r * gv,)
    return _rowwise(fn, [h], [g.reshape(1, -1)], [_sds(h.shape, MXU_DT)], [], name=name)[0]


def _rms_bwd(h, g, dy, dres, name):
    def fn(x, dyv, dr, gv):
        r = lax.rsqrt(jnp.mean(x * x, axis=-1, keepdims=True) + EPS)
        dyg = dyv * gv
        dx = dr + r * dyg - x * (r * r * r) * jnp.mean(x * dyg, axis=-1, keepdims=True)
        return dx, dx, jnp.sum(dyv * x * r, axis=0, keepdims=True)
    return _rowwise(fn, [h, dy, dres], [g.reshape(1, -1)], [_sds(h.shape), _sds(h.shape, MXU_DT)],
                    [_sds((1, h.shape[1]))], name=name)


def _loss_head(h, g, target, name):
    c = h.shape[1]

    def fn(x, t, gv):
        r = lax.rsqrt(jnp.mean(x * x, axis=-1, keepdims=True) + EPS)
        y = x * r * gv
        err = y - t
        dyv = err * (1.0 / c)
        dyg = dyv * gv
        dx = r * dyg - x * (r * r * r) * jnp.mean(x * dyg, axis=-1, keepdims=True)
        loss = 0.5 * jnp.sum(jnp.mean(err * err, axis=-1, keepdims=True), axis=0, keepdims=True)
        return dx, dx, jnp.sum(dyv * x * r, axis=0, keepdims=True), loss
    return _rowwise(fn, [h, target], [g.reshape(1, -1)], [_sds(h.shape), _sds(h.shape, MXU_DT)],
                    [_sds((1, c)), _sds((1, 1))], name=name)


PAD_ROWS = 8
ROW_TILE = 128


def _shifted_conv(xp_ref, w, r0, tr, kw):
    acc = None
    for k in range(kw):
        xk = xp_ref[pl.ds(PAD_ROWS + r0 - (kw - 1) + k, tr), :]
        term = xk * w[k:k + 1, :]
        acc = term if acc is None else acc + term
    return acc


def _convglu_fwd(u, conv_w, conv_b, name):
    s = u.shape[0]
    nt = D_FF // LANES
    tr = min(ROW_TILE, s)

    def body(ug_ref, uv_ref, w_ref, b_ref, act_ref, xp_ref):
        xp_ref[pl.ds(0, PAD_ROWS), :] = jnp.zeros((PAD_ROWS, LANES), F32)
        xp_ref[pl.ds(PAD_ROWS, s), :] = ug_ref[...]
        w = w_ref[...]
        b = b_ref[...]
        for r0 in range(0, s, tr):
            gate = _shifted_conv(xp_ref, w, r0, tr, FFN_CONV) + b
            act = gate * _sigmoid(gate) * uv_ref[pl.ds(r0, tr), :]
            act_ref[pl.ds(r0, tr), :] = act.astype(act_ref.dtype)

    return pl.pallas_call(
        body, name=name, grid=(nt,),
        in_specs=[pl.BlockSpec((s, LANES), lambda j: (0, j)), pl.BlockSpec((s, LANES), lambda j: (0, nt + j)),
                  pl.BlockSpec((FFN_CONV, LANES), lambda j: (0, j)), pl.BlockSpec((1, LANES), lambda j: (0, j))],
        out_specs=pl.BlockSpec((s, LANES), lambda j: (0, j)),
        out_shape=_sds((s, D_FF), MXU_DT),
        scratch_shapes=[pltpu.VMEM((s + PAD_ROWS, LANES), F32)],
        compiler_params=_params("parallel"),
    )(u, u, conv_w, conv_b.reshape(1, -1))


def _convglu_bwd(u, dact, conv_w, conv_b, name):
    s = u.shape[0]
    nt = D_FF // LANES
    tr = min(ROW_TILE, s)
    kw = FFN_CONV

    def body(ug_ref, uv_ref, da_ref, w_ref, b_ref, dug_ref, duv_ref, dw_ref, db_ref, xp_ref, dgp_ref):
        xp_ref[pl.ds(0, PAD_ROWS), :] = jnp.zeros((PAD_ROWS, LANES), F32)
        xp_ref[pl.ds(PAD_ROWS, s), :] = ug_ref[...]
        dgp_ref[pl.ds(s, PAD_ROWS), :] = jnp.zeros((PAD_ROWS, LANES), F32)
        w = w_ref[...]
        b = b_ref[...]
        dw = [jnp.zeros((1, LANES), F32) for _ in range(kw)]
        db = jnp.zeros((1, LANES), F32)
        for r0 in range(0, s, tr):
            gate = _shifted_conv(xp_ref, w, r0, tr, kw) + b
            sg = _sigmoid(gate)
            da = da_ref[pl.ds(r0, tr), :].astype(F32)
            duv_ref[pl.ds(r0, tr), :] = (da * gate * sg).astype(duv_ref.dtype)
            dgate = da * uv_ref[pl.ds(r0, tr), :] * (sg * (1.0 + gate * (1.0 - sg)))
            dgp_ref[pl.ds(r0, tr), :] = dgate
            db = db + jnp.sum(dgate, axis=0, keepdims=True)
            for k in range(kw):
                xk = xp_ref[pl.ds(PAD_ROWS + r0 - (kw - 1) + k, tr), :]
                dw[k] = dw[k] + jnp.sum(dgate * xk, axis=0, keepdims=True)
        for r0 in range(0, s, tr):
            acc = None
            for k in range(kw):
                term = dgp_ref[pl.ds(r0 + (kw - 1) - k, tr), :] * w[k:k + 1, :]
                acc = term if acc is None else acc + term
            dug_ref[pl.ds(r0, tr), :] = acc.astype(dug_ref.dtype)
        for k in range(kw):
            dw_ref[pl.ds(k, 1), :] = dw[k]
        db_ref[...] = db

    col = lambda j: (0, j)
    return pl.pallas_call(
        body, name=name, grid=(nt,),
        in_specs=[pl.BlockSpec((s, LANES), col), pl.BlockSpec((s, LANES), lambda j: (0, nt + j)),
                  pl.BlockSpec((s, LANES), col), pl.BlockSpec((kw, LANES), col), pl.BlockSpec((1, LANES), col)],
        out_specs=[pl.BlockSpec((s, LANES), col), pl.BlockSpec((s, LANES), col),
                   pl.BlockSpec((kw, LANES), col), pl.BlockSpec((1, LANES), col)],
        out_shape=[_sds((s, D_FF), MXU_DT), _sds((s, D_FF), MXU_DT), _sds((kw, D_FF)), _sds((1, D_FF))],
        scratch_shapes=[pltpu.VMEM((s + PAD_ROWS, LANES), F32), pltpu.VMEM((s + PAD_ROWS, LANES), F32)],
        compiler_params=_params("parallel"),
    )(u, u, dact, conv_w, conv_b.reshape(1, -1))


class _Weights:
    def __init__(self, fetch):
        self._fetch, self._got = fetch, {}

    def whole(self, name, after):
        if name not in self._got:
            self._got[name] = self._fetch(name, after)
        return self._got[name]


class _Gradients:
    def __init__(self, start):
        self._start, self._tokens = start, []

    def send(self, name, grad):
        token = self._start(name, grad)
        if token is not None:
            self._tokens.append(token)

    def tie(self, x):
        for token in self._tokens:
            x = x + token[0, 0]
        self._tokens = []
        return x


def _ffn_fwd(h, p, big):
    hf = _rms_fwd(h, p["norm_g"], "ffn_norm_fwd")
    w_up = big.whole("ffn_w_up", hf)
    u = _matmul(hf, w_up, tm=1024, name="ffn_up")
    conv_w = big.whole("ffn_conv_w", u)
    act = _convglu_fwd(u, conv_w, p["conv_b"], "convglu_fwd")
    w_down = big.whole("ffn_w_down", act)
    h_out = _matmul(act, w_down, res=h, name="ffn_down")
    return h_out, (hf, u, act, w_up, conv_w, w_down)


def _ffn_bwd(h, dh, saved, p, out):
    hf, u, act, w_up, conv_w, w_down = saved
    dh, dh_b = dh
    dact = _matmul(dh_b, w_down, tb=True, tm=1024, name="ffn_down_dx")
    out.send("ffn_w_down", _matmul(act, dh_b, ta=True, out_dtype=MXU_DT, name="ffn_down_dw"))
    dug, duv, dconv_w, dconv_b = _convglu_bwd(u, dact, conv_w, out.tie(p["conv_b"]), "convglu_bwd")
    dhf = _matmul(dug, w_up, tb=True, b_kblock=0, name="ffn_up_dx_gate")
    dhf = _matmul(duv, w_up, tb=True, b_kblock=1, res=dhf, name="ffn_up_dx_val")
    out.send("ffn_w_up", jnp.concatenate([_matmul(hf, dug, ta=True, out_dtype=MXU_DT, name="ffn_up_dw_gate"),
                                          _matmul(hf, duv, ta=True, out_dtype=MXU_DT, name="ffn_up_dw_val")], axis=1))
    *dh_in, dnorm_g = _rms_bwd(h, out.tie(p["norm_g"]), dhf, dh, "ffn_norm_bwd")
    return dh_in, dict(norm_g=dnorm_g, conv_w=dconv_w, conv_b=dconv_b)


def _dwconv_silu_fwd(proj, col0, n_ch, conv_w, conv_b, name):
    s = proj.shape[0]
    nt, t0, kw = n_ch // LANES, col0 // LANES, conv_w.shape[0]
    tr = min(ROW_TILE, s)

    def body(x_ref, w_ref, b_ref, o_ref, xp_ref):
        xp_ref[pl.ds(0, PAD_ROWS), :] = jnp.zeros((PAD_ROWS, LANES), F32)
        xp_ref[pl.ds(PAD_ROWS, s), :] = x_ref[...]
        w = w_ref[...]
        b = b_ref[...]
        for r0 in range(0, s, tr):
            pre = _shifted_conv(xp_ref, w, r0, tr, kw) + b
            o_ref[pl.ds(r0, tr), :] = pre * _sigmoid(pre)

    col = lambda j: (0, j)
    return pl.pallas_call(
        body, name=name, grid=(nt,),
        in_specs=[pl.BlockSpec((s, LANES), lambda j: (0, t0 + j)), pl.BlockSpec((kw, LANES), col),
                  pl.BlockSpec((1, LANES), col)],
        out_specs=pl.BlockSpec((s, LANES), col), out_shape=_sds((s, n_ch)),
        scratch_shapes=[pltpu.VMEM((s + PAD_ROWS, LANES), F32)],
        compiler_params=_params("parallel"),
    )(proj, conv_w, conv_b.reshape(1, -1))


def _dwconv_silu_bwd(proj, col0, n_ch, dout, conv_w, conv_b, name):
    s = proj.shape[0]
    nt, t0, kw = n_ch // LANES, col0 // LANES, conv_w.shape[0]
    tr = min(ROW_TILE, s)

    def body(x_ref, do_ref, w_ref, b_ref, dx_ref, dw_ref, db_ref, xp_ref, dgp_ref):
        xp_ref[pl.ds(0, PAD_ROWS), :] = jnp.zeros((PAD_ROWS, LANES), F32)
        xp_ref[pl.ds(PAD_ROWS, s), :] = x_ref[...]
        dgp_ref[pl.ds(s, PAD_ROWS), :] = jnp.zeros((PAD_ROWS, LANES), F32)
        w = w_ref[...]
        b = b_ref[...]
        dw = [jnp.zeros((1, LANES), F32) for _ in range(kw)]
        db = jnp.zeros((1, LANES), F32)
        for r0 in range(0, s, tr):
            pre = _shifted_conv(xp_ref, w, r0, tr, kw) + b
            sg = _sigmoid(pre)
            dpre = do_ref[pl.ds(r0, tr), :] * (sg * (1.0 + pre * (1.0 - sg)))
            dgp_ref[pl.ds(r0, tr), :] = dpre
            db = db + jnp.sum(dpre, axis=0, keepdims=True)
            for k in range(kw):
                xk = xp_ref[pl.ds(PAD_ROWS + r0 - (kw - 1) + k, tr), :]
                dw[k] = dw[k] + jnp.sum(dpre * xk, axis=0, keepdims=True)
        for r0 in range(0, s, tr):
            acc = None
            for k in range(kw):
                term = dgp_ref[pl.ds(r0 + (kw - 1) - k, tr), :] * w[k:k + 1, :]
                acc = term if acc is None else acc + term
            dx_ref[pl.ds(r0, tr), :] = acc.astype(dx_ref.dtype)
        for k in range(kw):
            dw_ref[pl.ds(k, 1), :] = dw[k]
        db_ref[...] = db

    col = lambda j: (0, j)
    return pl.pallas_call(
        body, name=name, grid=(nt,),
        in_specs=[pl.BlockSpec((s, LANES), lambda j: (0, t0 + j)), pl.BlockSpec((s, LANES), col),
                  pl.BlockSpec((kw, LANES), col), pl.BlockSpec((1, LANES), col)],
        out_specs=[pl.BlockSpec((s, LANES), col), pl.BlockSpec((kw, LANES), col), pl.BlockSpec((1, LANES), col)],
        out_shape=[_sds((s, n_ch), MXU_DT), _sds((kw, n_ch)), _sds((1, n_ch))],
        scratch_shapes=[pltpu.VMEM((s + PAD_ROWS, LANES), F32), pltpu.VMEM((s + PAD_ROWS, LANES), F32)],
        compiler_params=_params("parallel"),
    )(proj, dout, conv_w, conv_b.reshape(1, -1))


HIGHEST = lax.Precision.HIGHEST
PAIRS = SSD_HPG // 2
PAIR_W = 2 * SSD_HEAD_DIM
GROUP_W = SSD_HPG * SSD_HEAD_DIM


def _iota2(shape, axis):
    return lax.broadcasted_iota(jnp.int32, shape, axis)


def _lane_pad(v):
    return jnp.pad(v.reshape(1, -1), ((0, 0), (0, LANES - v.shape[0])))


def _heads_to_groups(a):
    s = a.shape[0]
    return a[:, :SSD_HEADS].reshape(s, SSD_GROUPS, SSD_HPG).transpose(1, 0, 2)


def _groups_to_heads(a):
    s = a.shape[1]
    return jnp.pad(a.transpose(1, 0, 2).reshape(s, SSD_HEADS), ((0, 0), (0, LANES - SSD_HEADS)))


def _ssd_prep(dt_raw, dt_bias, a_log, name):
    s = dt_raw.shape[0]
    lc = SSD_CHUNK

    def body(x_ref, b_ref, al_ref, dt_ref, acs_ref, acst_ref):
        x = x_ref[...] + b_ref[...]
        dt = jnp.maximum(x, 0.0) + jnp.log1p(jnp.exp(-jnp.abs(x)))
        da = dt * (-jnp.exp(al_ref[...]))
        lower = (_iota2((lc, lc), 0) >= _iota2((lc, lc), 1)).astype(F32)
        upper = (_iota2((lc, lc), 0) <= _iota2((lc, lc), 1)).astype(F32)
        dt_ref[...] = dt
        acs_ref[...] = jnp.dot(lower, da, precision=HIGHEST, preferred_element_type=F32)
        acst_ref[...] = lax.dot_general(da, upper, (((0,), (0,)), ((), ())), precision=HIGHEST,
                                        preferred_element_type=F32)

    row = pl.BlockSpec((lc, LANES), lambda c: (c, 0))
    one = pl.BlockSpec((1, LANES), lambda c: (0, 0))
    return pl.pallas_call(
        body, name=name, grid=(s // lc,), in_specs=[row, one, one],
        out_specs=[row, row, pl.BlockSpec((LANES, lc), lambda c: (0, c))],
        out_shape=[_sds((s, LANES)), _sds((s, LANES)), _sds((LANES, s))],
        compiler_params=_params("parallel"),
    )(dt_raw, dt_bias, a_log)


def _pair_cols(v, p, lo):
    return jnp.where(lo, v[:, 2 * p:2 * p + 1], v[:, 2 * p + 1:2 * p + 2])


def _decay_matrix(acs, acst, h, tri):
    return jnp.exp(jnp.where(tri, acs[:, h:h + 1] - acst[h:h + 1, :], -jnp.inf))


def _dot(a, b, ca, cb):
    return lax.dot_general(a, b, (((ca,), (cb,)), ((), ())), preferred_element_type=F32)


def _ssd_scan_fwd(xbc, dtg, acsg, acstg, d_skip, name):
    s = xbc.shape[0]
    lc, nc = SSD_CHUNK, s // SSD_CHUNK
    xt, bt = GROUP_W // LANES, SSD_D_INNER // LANES

    def body(x_ref, b_ref, c_ref, dt_ref, acs_ref, acst_ref, dsk_ref, y_ref, hp_ref, st_ref):
        @pl.when(pl.program_id(1) == 0)
        def _():
            st_ref[...] = jnp.zeros(st_ref.shape, F32)

        bm, cmb = b_ref[...], c_ref[...].astype(MXU_DT)
        dt, acs, acst = dt_ref[...], acs_ref[...], acst_ref[...]
        cb = _dot(cmb, bm.astype(MXU_DT), 1, 1)
        tri = _iota2((lc, lc), 0) >= _iota2((lc, lc), 1)
        lo = _iota2((lc, PAIR_W), 1) < SSD_HEAD_DIM
        a_last = acs[lc - 1:lc, :]
        e_acs, e_ds, e_cd = jnp.exp(acs), jnp.exp(a_last - acs), jnp.exp(a_last)
        for p in range(PAIRS):
            sl = pl.ds(p * PAIR_W, PAIR_W)
            xp = x_ref[:, sl]
            ub = (xp * _pair_cols(dt, p, lo)).astype(MXU_DT)
            m0 = (cb * _decay_matrix(acs, acst, 2 * p, tri)).astype(MXU_DT)
            m1 = (cb * _decay_matrix(acs, acst, 2 * p + 1, tri)).astype(MXU_DT)
            ht = st_ref[p]
            hp_ref[p] = ht
            y = jnp.where(lo, _dot(m0, ub, 1, 0), _dot(m1, ub, 1, 0))
            y = y + _dot(cmb, ht.astype(MXU_DT), 1, 0) * _pair_cols(e_acs, p, lo)
            y_ref[:, sl] = y + xp * dsk_ref[:, sl]
            bd0 = (bm * e_ds[:, 2 * p:2 * p + 1]).astype(MXU_DT)
            bd1 = (bm * e_ds[:, 2 * p + 1:2 * p + 2]).astype(MXU_DT)
            st_ref[p] = ht * _pair_cols(e_cd, p, lo[:1]) + jnp.where(lo, _dot(bd0, ub, 0, 0), _dot(bd1, ub, 0, 0))

    small = pl.BlockSpec((None, lc, SSD_HPG), lambda g, c: (g, c, 0))
    return pl.pallas_call(
        body, name=name, grid=(SSD_GROUPS, nc),
        in_specs=[pl.BlockSpec((lc, GROUP_W), lambda g, c: (c, g)),
                  pl.BlockSpec((lc, LANES), lambda g, c: (c, bt + g)),
                  pl.BlockSpec((lc, LANES), lambda g, c: (c, bt + SSD_GROUPS + g)),
                  small, small, pl.BlockSpec((None, SSD_HPG, lc), lambda g, c: (g, 0, c)),
                  pl.BlockSpec((1, GROUP_W), lambda g, c: (0, g))],
        out_specs=[pl.BlockSpec((lc, GROUP_W), lambda g, c: (c, g)),
                   pl.BlockSpec((None, PAIRS, SSD_STATE, PAIR_W), lambda g, c: (c, g, 0, 0))],
        out_shape=[_sds((s, SSD_D_INNER)), _sds((nc, SSD_GROUPS * PAIRS, SSD_STATE, PAIR_W))],
        scratch_shapes=[pltpu.VMEM((PAIRS, SSD_STATE, PAIR_W), F32)],
        compiler_params=_params("parallel", "arbitrary"),
    )(xbc, xbc, xbc, dtg, acsg, acstg, d_skip)


def _ssd_scan_bwd(xbc, dtg, acsg, acstg, d_skip, dy, hprev, name):
    s = xbc.shape[0]
    lc, nc = SSD_CHUNK, s // SSD_CHUNK
    bt = SSD_D_INNER // LANES

    def body(x_ref, b_ref, c_ref, dt_ref, acs_ref, acst_ref, dsk_ref, dy_ref, hp_ref, hn_ref,
             dx_ref, db_ref, dc_ref, daq_ref, dar_ref, ddtx_ref, dd_ref, dst_ref, ta_ref, tx_ref):
        @pl.when(pl.program_id(1) == 0)
        def _():
            dst_ref[...] = jnp.zeros(dst_ref.shape, F32)
            dd_ref[...] = jnp.zeros(dd_ref.shape, F32)

        bm, cmb = b_ref[...], c_ref[...].astype(MXU_DT)
        bmb = bm.astype(MXU_DT)
        dt, acs, acst = dt_ref[...], acs_ref[...], acst_ref[...]
        cb = _dot(cmb, bmb, 1, 1)
        tri = _iota2((lc, lc), 0) >= _iota2((lc, lc), 1)
        lo = _iota2((lc, PAIR_W), 1) < SSD_HEAD_DIM
        a_last = acs[lc - 1:lc, :]
        e_acs, e_ds, e_cd = jnp.exp(acs), jnp.exp(a_last - acs), jnp.exp(a_last)
        dcb = jnp.zeros((lc, lc), F32)
        dc_x = jnp.zeros((lc, SSD_STATE), F32)
        db_x = jnp.zeros((lc, SSD_STATE), F32)
        da_in = jnp.zeros((lc, LANES), F32)
        da_out = jnp.zeros((SSD_HPG, lc), F32)
        head_col = _iota2((lc, LANES), 1)
        head_row = _iota2((SSD_HPG, lc), 0)
        last = _iota2((SSD_HPG, lc), 1) == lc - 1
        for p in range(PAIRS):
            sl = pl.ds(p * PAIR_W, PAIR_W)
            xp, dyp, dsk = x_ref[:, sl], dy_ref[:, sl], dsk_ref[:, sl]
            dtp = _pair_cols(dt, p, lo)
            u = xp * dtp
            ub, dyb = u.astype(MXU_DT), dyp.astype(MXU_DT)
            lmat = (_decay_matrix(acs, acst, 2 * p, tri), _decay_matrix(acs, acst, 2 * p + 1, tri))
            m0, m1 = (cb * lmat[0]).astype(MXU_DT), (cb * lmat[1]).astype(MXU_DT)
            ea, dsl = _pair_cols(e_acs, p, lo), _pair_cols(e_ds, p, lo)
            dht, ht = dst_ref[p], hp_ref[p]
            dhtb, htb = dht.astype(MXU_DT), ht.astype(MXU_DT)
            bd0 = (bm * e_ds[:, 2 * p:2 * p + 1]).astype(MXU_DT)
            bd1 = (bm * e_ds[:, 2 * p + 1:2 * p + 2]).astype(MXU_DT)
            du_state = jnp.where(lo, _dot(bd0, dhtb, 1, 0), _dot(bd1, dhtb, 1, 0))
            du = jnp.where(lo, _dot(m0, dyb, 0, 0), _dot(m1, dyb, 0, 0)) + du_state
            y_off = _dot(cmb, htb, 1, 0) * ea
            ta_ref[:, sl] = dyp * y_off - u * du_state
            tx_ref[:, sl] = du * xp
            dx_ref[:, sl] = dtp * du + dsk * dyp
            dd_ref[:, sl] += jnp.sum(dyp * xp, axis=0, keepdims=True)
            dy_h = (jnp.where(lo, dyp, 0.0).astype(MXU_DT), jnp.where(lo, 0.0, dyp).astype(MXU_DT))
            carry = jnp.sum(dht * hn_ref[p], axis=0, keepdims=True)
            for hh in range(2):
                h = 2 * p + hh
                dml = _dot(dy_h[hh], ub, 1, 1) * lmat[hh]
                dcb = dcb + dml
                flow = cb * dml
                da_in = da_in + jnp.where(head_col == h, jnp.sum(flow, axis=1, keepdims=True), 0.0)
                through = jnp.sum(jnp.where(lo[:1] == (hh == 0), carry, 0.0), axis=1, keepdims=True)
                da_out = da_out + jnp.where(head_row == h, jnp.sum(flow, axis=0, keepdims=True)
                                            - jnp.where(last, through, 0.0), 0.0)
            dye = (dyp * ea).astype(MXU_DT)
            dc_x = dc_x + _dot(dye, htb, 1, 1)
            db_x = db_x + _dot((u * dsl).astype(MXU_DT), dhtb, 1, 1)
            dst_ref[p] = dht * _pair_cols(e_cd, p, lo[:1]) + _dot(cmb, dye, 0, 0)
        dcbb = dcb.astype(MXU_DT)
        dc_ref[...] = _dot(dcbb, bmb, 1, 0) + dc_x
        db_ref[...] = _dot(dcbb, cmb, 0, 0) + db_x
        seg_lo = _iota2((GROUP_W, LANES), 1) * SSD_HEAD_DIM
        chan = _iota2((GROUP_W, LANES), 0)
        seg = jnp.logical_and(chan >= seg_lo, chan < seg_lo + SSD_HEAD_DIM).astype(F32)
        da_in = da_in + jnp.dot(ta_ref[...], seg, precision=HIGHEST, preferred_element_type=F32)
        daq_ref[...] = da_in[:, :SSD_HPG]
        dar_ref[...] = da_out
        ddtx_ref[...] = jnp.dot(tx_ref[...], seg, precision=HIGHEST, preferred_element_type=F32)[:, :SSD_HPG]

    rev = lambda c: nc - 1 - c
    small = pl.BlockSpec((None, lc, SSD_HPG), lambda g, c: (g, rev(c), 0))
    small_t = pl.BlockSpec((None, SSD_HPG, lc), lambda g, c: (g, 0, rev(c)))
    wide = pl.BlockSpec((lc, GROUP_W), lambda g, c: (rev(c), g))
    state = lambda at: pl.BlockSpec((None, PAIRS, SSD_STATE, PAIR_W), lambda g, c: (at(c), g, 0, 0))
    return pl.pallas_call(
        body, name=name, grid=(SSD_GROUPS, nc),
        in_specs=[pl.BlockSpec((lc, GROUP_W), lambda g, c: (rev(c), g)),
                  pl.BlockSpec((lc, LANES), lambda g, c: (rev(c), bt + g)),
                  pl.BlockSpec((lc, LANES), lambda g, c: (rev(c), bt + SSD_GROUPS + g)),
                  small, small, small_t, pl.BlockSpec((1, GROUP_W), lambda g, c: (0, g)), wide,
                  state(rev), state(lambda c: jnp.minimum(rev(c) + 1, nc - 1))],
        out_specs=[wide, pl.BlockSpec((lc, LANES), lambda g, c: (rev(c), g)),
                   pl.BlockSpec((lc, LANES), lambda g, c: (rev(c), g)), small, small_t, small,
                   pl.BlockSpec((1, GROUP_W), lambda g, c: (0, g))],
        out_shape=[_sds((s, SSD_D_INNER)), _sds((s, SSD_GROUPS * SSD_STATE)), _sds((s, SSD_GROUPS * SSD_STATE)),
                   _sds((SSD_GROUPS, s, SSD_HPG)), _sds((SSD_GROUPS, SSD_HPG, s)), _sds((SSD_GROUPS, s, SSD_HPG)),
                   _sds((1, SSD_D_INNER))],
        scratch_shapes=[pltpu.VMEM((PAIRS, SSD_STATE, PAIR_W), F32), pltpu.VMEM((lc, GROUP_W), F32),
                        pltpu.VMEM((lc, GROUP_W), F32)],
        compiler_params=_params("parallel", "arbitrary"),
    )(xbc, xbc, xbc, dtg, acsg, acstg, d_skip, dy, hprev, hprev)


def _ssd_post(da_in, da_out, ddtx, dt, dt_raw, dt_bias, a_log, name):
    s = da_in.shape[0]
    lc = SSD_CHUNK

    def body(dain_ref, daout_ref, ddtx_ref, dt_ref, x_ref, b_ref, al_ref, ddr_ref, dal_ref, dbias_ref):
        @pl.when(pl.program_id(0) == 0)
        def _():
            dal_ref[...] = jnp.zeros(dal_ref.shape, F32)
            dbias_ref[...] = jnp.zeros(dbias_ref.shape, F32)

        upper = (_iota2((lc, lc), 0) <= _iota2((lc, lc), 1)).astype(F32)
        dda = jnp.dot(upper, dain_ref[...] - daout_ref[...], precision=HIGHEST, preferred_element_type=F32)
        a = -jnp.exp(al_ref[...])
        ddt = dda * a + ddtx_ref[...]
        dal_ref[...] += jnp.sum(dda * dt_ref[...], axis=0, keepdims=True) * a
        ddr = ddt * _sigmoid(x_ref[...] + b_ref[...])
        ddr_ref[...] = ddr.astype(ddr_ref.dtype)
        dbias_ref[...] += jnp.sum(ddr, axis=0, keepdims=True)

    row = pl.BlockSpec((lc, LANES), lambda i: (i, 0))
    one = pl.BlockSpec((1, LANES), lambda i: (0, 0))
    return pl.pallas_call(
        body, name=name, grid=(s // lc,), in_specs=[row, row, row, row, row, one, one], out_specs=[row, one, one],
        out_shape=[_sds((s, LANES), MXU_DT), _sds((1, LANES)), _sds((1, LANES))],
        compiler_params=_params("arbitrary"),
    )(da_in, da_out, ddtx, dt, dt_raw, dt_bias, a_log)


NORM_GROUP_W = SSD_D_INNER // SSD_GROUPS


def _group_rstd(yz):
    return [lax.rsqrt(jnp.mean(jnp.square(yz[:, g * NORM_GROUP_W:(g + 1) * NORM_GROUP_W]), axis=-1, keepdims=True) + EPS)
            for g in range(SSD_GROUPS)]


def _gated_norm_fwd(y, proj, norm_g, name):
    def fn(yv, z, gv):
        yz = yv * (z * _sigmoid(z))
        parts = [yz[:, g * NORM_GROUP_W:(g + 1) * NORM_GROUP_W] * r for g, r in enumerate(_group_rstd(yz))]
        return (jnp.concatenate(parts, axis=1) * gv,)
    return _rowwise(fn, [y, (proj, 0, SSD_D_INNER)], [norm_g.reshape(1, -1)], [_sds(y.shape, MXU_DT)], [],
                    name=name)[0]


def _gated_norm_bwd(y, proj, norm_g, dout, name):
    def fn(yv, z, do, gv):
        sg = _sigmoid(z)
        sz = z * sg
        yz = yv * sz
        dog = do * gv
        dyz, dg = [], []
        for g, r in enumerate(_group_rstd(yz)):
            cols = slice(g * NORM_GROUP_W, (g + 1) * NORM_GROUP_W)
            yzg, dogg = yz[:, cols], dog[:, cols]
            dyz.append(r * dogg - yzg * (r * r * r) * jnp.mean(yzg * dogg, axis=-1, keepdims=True))
            dg.append(jnp.sum(do[:, cols] * yzg * r, axis=0, keepdims=True))
        dyz = jnp.concatenate(dyz, axis=1)
        return dyz * sz, dyz * yv * (sg * (1.0 + z * (1.0 - sg))), jnp.concatenate(dg, axis=1)
    return _rowwise(fn, [y, (proj, 0, SSD_D_INNER), dout], [norm_g.reshape(1, -1)],
                    [_sds(y.shape), _sds(y.shape, MXU_DT)], [_sds((1, y.shape[1]))], name=name)


def _ssd_fwd(h, p, big):
    hn = _rms_fwd(h, p["norm_g"], "mix_norm_fwd")
    w_in = big.whole("ssd_w_in", hn)
    w_zx, w_dt = w_in[:, :SSD_ZX], _pad_cols(w_in[:, SSD_ZX:])
    proj = _matmul(hn, w_zx, tm=1024, name="ssd_in_zx")
    dt_raw = _matmul(hn, w_dt, name="ssd_in_dt")
    conv_w = big.whole("ssd_conv_w", proj)
    xbc = _dwconv_silu_fwd(proj, SSD_D_INNER, SSD_CONV_DIM, conv_w, p["conv_b"], "ssd_conv_fwd")
    dt, acs, acst = _ssd_prep(dt_raw, _lane_pad(p["dt_bias"]), _lane_pad(p["a_log"]), "ssd_prep")
    dtg, acsg = _heads_to_groups(dt), _heads_to_groups(acs)
    acstg = acst[:SSD_HEADS].reshape(SSD_GROUPS, SSD_HPG, -1)
    d_skip = jnp.repeat(p["d"], SSD_HEAD_DIM).reshape(1, -1)
    y, hprev = _ssd_scan_fwd(xbc, dtg, acsg, acstg, d_skip, "ssd_scan_fwd")
    yn = _gated_norm_fwd(y, proj, p["gnorm_g"], "ssd_gnorm_fwd")
    w_out = big.whole("ssd_w_out", yn)
    h_out = _matmul(yn, w_out, res=h, name="ssd_out")
    return h_out, (hn, proj, dt_raw, xbc, dt, dtg, acsg, acstg, d_skip, y, hprev, yn, w_zx, w_dt, conv_w, w_out)


def _ssd_bwd(h, dh, saved, p, out):
    hn, proj, dt_raw, xbc, dt, dtg, acsg, acstg, d_skip, y, hprev, yn, w_zx, w_dt, conv_w, w_out = saved
    dh, dh_b = dh
    dyn = _matmul(dh_b, w_out, tb=True, name="ssd_out_dx")
    out.send("ssd_w_out", _matmul(yn, dh_b, ta=True, out_dtype=MXU_DT, name="ssd_out_dw"))
    dy, dz, dgnorm = _gated_norm_bwd(y, proj, out.tie(p["gnorm_g"]), dyn, "ssd_gnorm_bwd")
    dx, dbm, dcm, daq, dar, ddtx, dd = _ssd_scan_bwd(xbc, dtg, acsg, acstg, d_skip, dy, hprev, "ssd_scan_bwd")
    dxbc = jnp.concatenate([dx, dbm, dcm], axis=1)
    dpre, dconv_w, dconv_b = _dwconv_silu_bwd(proj, SSD_D_INNER, SSD_CONV_DIM, dxbc, conv_w, p["conv_b"],
                                              "ssd_conv_bwd")
    ddr, dalog, dbias = _ssd_post(_groups_to_heads(daq), _groups_to_heads(dar.transpose(0, 2, 1)),
                                  _groups_to_heads(ddtx), dt, dt_raw,
                                  _lane_pad(p["dt_bias"]), _lane_pad(p["a_log"]), "ssd_post")
    w_z, w_x = w_zx[:, :SSD_D_INNER], w_zx[:, SSD_D_INNER:]
    dhn = _matmul(dz, w_z, tb=True, name="ssd_in_dx_z")
    dhn = _matmul(dpre, w_x, tb=True, res=dhn, name="ssd_in_dx_x")
    dhn = _matmul(ddr, w_dt, tb=True, res=dhn, name="ssd_in_dx_dt")
    out.send("ssd_w_in", jnp.concatenate(
        [_matmul(hn, dz, ta=True, out_dtype=MXU_DT, name="ssd_in_dw_z"),
         _matmul(hn, dpre, ta=True, out_dtype=MXU_DT, name="ssd_in_dw_x"),
         _matmul(hn, ddr, ta=True, out_dtype=MXU_DT, name="ssd_in_dw_dt")[:, :SSD_HEADS]], axis=1))
    *dh_in, dnorm_g = _rms_bwd(h, out.tie(p["norm_g"]), dhn, dh, "mix_norm_bwd")
    grads = dict(norm_g=dnorm_g, conv_w=dconv_w, conv_b=dconv_b, dt_bias=dbias[:, :SSD_HEADS],
                 a_log=dalog[:, :SSD_HEADS], d=dd.reshape(SSD_HEADS, SSD_HEAD_DIM).sum(axis=1).reshape(1, -1),
                 gnorm_g=dgnorm)
    return dh_in, grads


FOX_PAIRS = FOX_HEADS // 2
ATT_TQ = 512
ATT_BWD_TQ = 1024
ATT_TK = 512
ATT_SUB = 512
FOX_PREP_ROWS = 256
NEG_BIG = -1e30
FOX_SCALE = FOX_HEAD_DIM ** -0.5
FOX_AUG_D = FOX_HEADS * LANES
QSIDE = FOX_HEAD_DIM
KSIDE = FOX_HEAD_DIM + 3


def _split3(x):
    a = x.astype(MXU_DT).astype(F32)
    b = (x - a).astype(MXU_DT).astype(F32)
    return a, b, (x - a - b).astype(MXU_DT).astype(F32)


def _head_tiles(pair_tile):
    return pair_tile, pltpu.roll(pair_tile, FOX_HEAD_DIM, 1)


def _fill_lanes(base, lane, first, values):
    for i, v in enumerate(values):
        base = jnp.where(lane == first + i, v, base)
    return base


def _pair_tile(lane, tile0, tile1):
    return jnp.where(lane < FOX_HEAD_DIM, tile0, pltpu.roll(tile1, FOX_HEAD_DIM, 1))


def _compact_heads(a, lane):
    return jnp.concatenate([_pair_tile(lane, a[:, 2 * j * LANES:(2 * j + 1) * LANES],
                                       a[:, (2 * j + 1) * LANES:(2 * j + 2) * LANES]) for j in range(FOX_PAIRS)], axis=1)


def _head_sum_matrix():
    return (_iota2((LANES, LANES), 0) < FOX_HEAD_DIM) == (_iota2((LANES, LANES), 1) < FOX_HEAD_DIM)


def _head_sums(x):
    bd = _head_sum_matrix().astype(F32)
    parts = [jnp.dot(x[:, j * LANES:(j + 1) * LANES], bd, precision=HIGHEST, preferred_element_type=F32)
             for j in range(x.shape[1] // LANES)]
    return parts[0] if len(parts) == 1 else jnp.concatenate(parts, axis=1)


def _fox_prep_fwd(proj, f_raw, qg, kg, b_f, name):
    s = proj.shape[0]
    tr = min(FOX_PREP_ROWS, s)

    def body(q_ref, k_ref, v_ref, f_ref, qg_ref, kg_ref, b_ref, qa_ref, ka_ref, va_ref, carry_ref):
        @pl.when(pl.program_id(0) == 0)
        def _():
            carry_ref[...] = jnp.zeros(carry_ref.shape, F32)

        normed = []
        for x_ref, g_ref in ((q_ref, qg_ref), (k_ref, kg_ref)):
            x = x_ref[...]
            r = lax.rsqrt(_head_sums(x * x) * (1.0 / FOX_HEAD_DIM) + EPS)
            normed.append(x * r * g_ref[...])
        qn, kn, v = normed[0] * FOX_SCALE, normed[1], v_ref[...]
        x = f_ref[...] + b_ref[...]
        lf = jnp.minimum(x, 0.0) - jnp.log1p(jnp.exp(-jnp.abs(x)))
        lower = (_iota2((tr, tr), 0) >= _iota2((tr, tr), 1)).astype(F32)
        cum = jnp.dot(lower, lf, precision=HIGHEST, preferred_element_type=F32) + carry_ref[...]
        carry_ref[...] += jnp.sum(lf, axis=0, keepdims=True)
        first = _iota2((LANES, FOX_D), 0) * FOX_HEAD_DIM
        chan = _iota2((LANES, FOX_D), 1)
        spread = jnp.logical_and(chan >= first, chan < first + FOX_HEAD_DIM).astype(F32)
        cum = jnp.dot(cum, spread, precision=HIGHEST, preferred_element_type=F32)
        lane = _iota2((tr, LANES), 1)
        ones = jnp.where(jnp.logical_and(lane >= QSIDE, lane < KSIDE + 3), 1.0, 0.0)
        for j in range(FOX_PAIRS):
            cols = slice(j * LANES, (j + 1) * LANES)
            tiles = zip(_head_tiles(qn[:, cols]), _head_tiles(kn[:, cols]), _head_tiles(v[:, cols]),
                        reversed(_head_tiles(cum[:, cols])))
            for hh, (qt, kt, vt, ct) in enumerate(tiles):
                out = slice((2 * j + hh) * LANES, (2 * j + hh + 1) * LANES)
                c3 = _split3(ct)
                head = lane < FOX_HEAD_DIM
                qa_ref[:, out] = _fill_lanes(jnp.where(head, qt, ones), lane, QSIDE, c3).astype(qa_ref.dtype)
                ka_ref[:, out] = _fill_lanes(jnp.where(head, kt, ones), lane, KSIDE, [-c for c in c3]).astype(ka_ref.dtype)
                va_ref[:, out] = jnp.where(head, vt, jnp.where(lane < KSIDE, 1.0, 0.0)).astype(va_ref.dtype)

    wide = lambda cb: pl.BlockSpec((tr, FOX_D), lambda i: (i, cb))
    aug = pl.BlockSpec((tr, FOX_AUG_D), lambda i: (i, 0))
    one = lambda n: pl.BlockSpec((1, n), lambda i: (0, 0))
    return pl.pallas_call(
        body, name=name, grid=(s // tr,),
        in_specs=[wide(0), wide(1), wide(2), pl.BlockSpec((tr, LANES), lambda i: (i, 0)), one(FOX_D), one(FOX_D),
                  one(LANES)],
        out_specs=[aug, aug, aug], out_shape=[_sds((s, FOX_AUG_D), MXU_DT)] * 3,
        scratch_shapes=[pltpu.VMEM((1, LANES), F32)],
        compiler_params=_params("arbitrary"),
    )(proj, proj, proj, f_raw, qg, kg, b_f)


def _fox_prep_bwd(proj, f_raw, qg, kg, b_f, dqa, dka, name):
    s = proj.shape[0]
    tr = min(FOX_PREP_ROWS, s)
    nb = s // tr

    def body(q_ref, k_ref, f_ref, qg_ref, kg_ref, b_ref, dqa_ref, dka_ref,
             dq_ref, dk_ref, df_ref, dqg_ref, dkg_ref, db_ref, carry_ref):
        @pl.when(pl.program_id(0) == 0)
        def _():
            carry_ref[...] = jnp.zeros(carry_ref.shape, F32)
            dqg_ref[...] = jnp.zeros(dqg_ref.shape, F32)
            dkg_ref[...] = jnp.zeros(dkg_ref.shape, F32)
            db_ref[...] = jnp.zeros(db_ref.shape, F32)

        lane = _iota2((tr, LANES), 1)
        for x_ref, g_ref, dt_ref, scale, dx_ref, dg_ref in ((q_ref, qg_ref, dqa_ref, FOX_SCALE, dq_ref, dqg_ref),
                                                            (k_ref, kg_ref, dka_ref, 1.0, dk_ref, dkg_ref)):
            x, dy = x_ref[...], _compact_heads(dt_ref[...], lane) * scale
            r = lax.rsqrt(_head_sums(x * x) * (1.0 / FOX_HEAD_DIM) + EPS)
            dyg = dy * g_ref[...]
            dx = r * dyg - x * (r * r * r) * (_head_sums(x * dyg) * (1.0 / FOX_HEAD_DIM))
            dx_ref[...] = dx.astype(dx_ref.dtype)
            dg_ref[...] += jnp.sum(dy * x * r, axis=0, keepdims=True)
        dc = jnp.zeros((tr, LANES), F32)
        for h in range(FOX_HEADS):
            sums = dqa_ref[:, pl.ds(h * LANES + QSIDE, 1)] - dka_ref[:, pl.ds(h * LANES + KSIDE, 1)]
            dc = jnp.where(lane == h, sums, dc)
        upper = (_iota2((tr, tr), 0) <= _iota2((tr, tr), 1)).astype(F32)
        dlf = jnp.dot(upper, dc, precision=HIGHEST, preferred_element_type=F32) + carry_ref[...]
        carry_ref[...] += jnp.sum(dc, axis=0, keepdims=True)
        df = dlf * _sigmoid(-(f_ref[...] + b_ref[...]))
        df_ref[...] = df.astype(df_ref.dtype)
        db_ref[...] += jnp.sum(df, axis=0, keepdims=True)

    wide = lambda cb: pl.BlockSpec((tr, FOX_D), lambda i: (nb - 1 - i, cb))
    aug = pl.BlockSpec((tr, FOX_AUG_D), lambda i: (nb - 1 - i, 0))
    row = pl.BlockSpec((tr, LANES), lambda i: (nb - 1 - i, 0))
    one = lambda n: pl.BlockSpec((1, n), lambda i: (0, 0))
    return pl.pallas_call(
        body, name=name, grid=(nb,),
        in_specs=[wide(0), wide(1), row, one(FOX_D), one(FOX_D), one(LANES), aug, aug],
        out_specs=[wide(0), wide(0), row, one(FOX_D), one(FOX_D), one(LANES)],
        out_shape=[_sds((s, FOX_D), MXU_DT), _sds((s, FOX_D), MXU_DT), _sds((s, LANES), MXU_DT),
                   _sds((1, FOX_D)), _sds((1, FOX_D)), _sds((1, LANES))],
        scratch_shapes=[pltpu.VMEM((1, LANES), F32)],
        compiler_params=_params("arbitrary"),
    )(proj, proj, f_raw, qg, kg, b_f, dqa, dka)


def _fox_attn_fwd(qa, ka, va, proj, name):
    s = qa.shape[0]
    tq, tk = min(ATT_TQ, s), min(ATT_TK, s)
    assert s % tq == 0 and s % tk == 0
    gt = 3 * FOX_D // LANES
    head_lanes = [slice(hh * LANES, (hh + 1) * LANES) for hh in range(2)]

    def body(qa_ref, ka_ref, va_ref, g_ref, o_ref, og_ref, qb_ref):
        qi = pl.program_id(1)
        sub = min(ATT_SUB, tq)
        lane = _iota2((sub, LANES), 1)
        ahead = _iota2((sub, tk), 0) - _iota2((sub, tk), 1)
        chains = [(hh, r0) for hh in range(2) for r0 in range(0, tq, sub)]
        q = [qa_ref[pl.ds(r0, sub), head_lanes[hh]] for hh, r0 in chains]

        def kv_step(j, carry, masked):
            rows = pl.ds(pl.multiple_of(j * tk, tk), tk)
            out = []
            for c, (hh, r0) in enumerate(chains):
                m, acc = carry[2 * c:2 * c + 2]
                sc = _dot(q[c], ka_ref[rows, head_lanes[hh]], 1, 1)
                if masked:
                    sc = jnp.where(ahead >= j * tk - qi * tq - r0, sc, NEG_BIG)
                m_new = jnp.maximum(m, jnp.max(sc, axis=1, keepdims=True))
                pr = jnp.exp(sc - m_new).astype(MXU_DT)
                out += [m_new, jnp.exp(m - m_new) * acc + _dot(pr, va_ref[rows, head_lanes[hh]], 1, 0)]
            return tuple(out)

        n_clear = lax.div(qi * tq, tk)
        n_all = lax.div((qi + 1) * tq + tk - 1, tk)
        init = (jnp.full((sub, 1), NEG_BIG, F32), jnp.zeros((sub, LANES), F32)) * len(chains)
        carry = lax.fori_loop(0, n_clear, functools.partial(kv_step, masked=False), init)
        carry = lax.fori_loop(n_clear, n_all, functools.partial(kv_step, masked=True), carry)
        heads = [[], []]
        for c, (hh, r0) in enumerate(chains):
            m, acc = carry[2 * c:2 * c + 2]
            l = acc[:, QSIDE:QSIDE + 1]
            heads[hh].append(acc / l)
            qf = q[c].astype(F32)
            bias = qf[:, QSIDE:QSIDE + 1] + qf[:, QSIDE + 1:QSIDE + 2] + qf[:, QSIDE + 2:QSIDE + 3]
            qb_ref[pl.ds(r0, sub), head_lanes[hh]] = _fill_lanes(
                qf, lane, QSIDE, _split3(bias - (m + jnp.log(l)))).astype(qb_ref.dtype)
        heads = [jnp.concatenate(h, axis=0) for h in heads]
        o = _pair_tile(_iota2((tq, LANES), 1), heads[0], heads[1])
        o_ref[...] = o
        og_ref[...] = (o * _sigmoid(g_ref[...])).astype(og_ref.dtype)

    blk2 = pl.BlockSpec((tq, 2 * LANES), lambda p, i: (i, p))
    seq2 = pl.BlockSpec((s, 2 * LANES), lambda p, i: (0, p))
    blk = pl.BlockSpec((tq, LANES), lambda p, i: (i, p))
    return pl.pallas_call(
        body, name=name, grid=(FOX_PAIRS, s // tq),
        in_specs=[blk2, seq2, seq2, pl.BlockSpec((tq, LANES), lambda p, i: (i, gt + p))],
        out_specs=[blk, blk, blk2],
        out_shape=[_sds((s, FOX_D)), _sds((s, FOX_D), MXU_DT), _sds((s, FOX_AUG_D), MXU_DT)],
        compiler_params=_params("parallel", "parallel"),
    )(qa, ka, va, proj)


def _fox_gate_bwd(dog, o, proj, name):
    def fn(dogv, ov, gate):
        sg = _sigmoid(gate)
        do = dogv * sg
        delta = _head_sums(do * ov)
        lane = _iota2((do.shape[0], LANES), 1)
        tiles = []
        for j in range(FOX_PAIRS):
            cols = slice(j * LANES, (j + 1) * LANES)
            for dt, dl in zip(_head_tiles(do[:, cols]), reversed(_head_tiles(delta[:, cols]))):
                tiles.append(_fill_lanes(jnp.where(lane < FOX_HEAD_DIM, dt, 0.0), lane, QSIDE,
                                         [-d for d in _split3(dl)]))
        return dogv * ov * sg * (1.0 - sg), jnp.concatenate(tiles, axis=1)
    return _rowwise(fn, [dog, o, (proj, 3, FOX_D)], [], [_sds(o.shape, MXU_DT), _sds((o.shape[0], FOX_AUG_D), MXU_DT)],
                    [], name=name)


def _fox_attn_bwd(qb, ka, va, doa, name):
    s = qb.shape[0]
    tq, tk = min(ATT_BWD_TQ, s), min(ATT_TK, s)
    nq, nk = s // tq, s // tk
    head_lanes = [slice(hh * LANES, (hh + 1) * LANES) for hh in range(2)]

    def body(qb_ref, doa_ref, ka_ref, va_ref, dqa_ref, dka_ref, dv_ref):
        kj = pl.program_id(1)

        @pl.when(kj == 0)
        def _():
            dqa_ref[...] = jnp.zeros(dqa_ref.shape, F32)

        ahead = _iota2((tq, tk), 0) - _iota2((tq, tk), 1)
        kb = [ka_ref[:, hs] for hs in head_lanes]
        vb = [va_ref[:, hs] for hs in head_lanes]

        def q_step(i, carry, masked):
            rows = pl.ds(pl.multiple_of(i * tq, tq), tq)
            out = []
            for hh, hs in enumerate(head_lanes):
                dk, dv = carry[2 * hh:2 * hh + 2]
                q, do = qb_ref[rows, hs], doa_ref[rows, hs]
                pr = jnp.exp(_dot(q, kb[hh], 1, 1))
                if masked:
                    pr = jnp.where(ahead >= kj * tk - i * tq, pr, 0.0)
                dv = dv + _dot(pr.astype(MXU_DT), do, 0, 0)
                ds = (pr * _dot(do, vb[hh], 1, 1)).astype(MXU_DT)
                dqa_ref[rows, hs] += _dot(ds, kb[hh], 1, 0)
                out += [dk + _dot(ds, q, 0, 0), dv]
            return tuple(out)

        first = lax.div(kj * tk, tq)
        n_masked = lax.div((kj + 1) * tk + tq - 1, tq)
        carry = lax.fori_loop(first, n_masked, functools.partial(q_step, masked=True),
                              (jnp.zeros((tk, LANES), F32),) * 4)
        dk0, dv0, dk1, dv1 = lax.fori_loop(n_masked, nq, functools.partial(q_step, masked=False), carry)
        dka_ref[:, head_lanes[0]] = dk0
        dka_ref[:, head_lanes[1]] = dk1
        dv_ref[...] = _pair_tile(_iota2((tk, LANES), 1), dv0, dv1).astype(dv_ref.dtype)

    seq2 = pl.BlockSpec((s, 2 * LANES), lambda p, j: (0, p))
    blk2 = pl.BlockSpec((tk, 2 * LANES), lambda p, j: (j, p))
    return pl.pallas_call(
        body, name=name, grid=(FOX_PAIRS, nk), in_specs=[seq2, seq2, blk2, blk2],
        out_specs=[seq2, blk2, pl.BlockSpec((tk, LANES), lambda p, j: (j, p))],
        out_shape=[_sds((s, FOX_AUG_D)), _sds((s, FOX_AUG_D)), _sds((s, FOX_D), MXU_DT)],
        compiler_params=_params("parallel", "arbitrary"),
    )(qb, doa, ka, va)


def _fox_fwd(h, p, big):
    hn = _rms_fwd(h, p["norm_g"], "mix_norm_fwd")
    w_in = big.whole("fox_w_in", hn)
    w_qkvg, w_f = w_in[:, :4 * FOX_D], _pad_cols(w_in[:, 4 * FOX_D:])
    proj = _matmul(hn, w_qkvg, tm=1024, name="fox_in_qkvg")
    f_raw = _matmul(hn, w_f, name="fox_in_f")
    qg = jnp.tile(p["q_norm_g"], FOX_HEADS).reshape(1, -1)
    kg = jnp.tile(p["k_norm_g"], FOX_HEADS).reshape(1, -1)
    qa, ka, va = _fox_prep_fwd(proj, f_raw, qg, kg, _lane_pad(p["b_f"]), "fox_prep_fwd")
    o, og, qb = _fox_attn_fwd(qa, ka, va, proj, "fox_attn_fwd")
    w_out = big.whole("fox_w_out", og)
    h_out = _matmul(og, w_out, res=h, name="fox_out")
    return h_out, (hn, proj, f_raw, qg, kg, ka, va, qb, o, og, w_qkvg, w_f, w_out)


def _fox_bwd(h, dh, saved, p, out):
    hn, proj, f_raw, qg, kg, ka, va, qb, o, og, w_qkvg, w_f, w_out = saved
    s = h.shape[0]
    dh, dh_b = dh
    dog = _matmul(dh_b, w_out, tb=True, name="fox_out_dx")
    out.send("fox_w_out", _matmul(og, dh_b, ta=True, out_dtype=MXU_DT, name="fox_out_dw"))
    dgate, doa = _fox_gate_bwd(dog, o, proj, "fox_gate_bwd")
    dqa, dka, dv = _fox_attn_bwd(qb, ka, va, doa, "fox_attn_bwd")
    dq, dk, df, dqg, dkg, dbf = _fox_prep_bwd(proj, f_raw, qg, kg, out.tie(_lane_pad(p["b_f"])), dqa, dka,
                                              "fox_prep_bwd")
    dproj = jnp.concatenate([dq, dk, dv, dgate], axis=1)
    dhn = _matmul(dproj, w_qkvg, tb=True, name="fox_in_dx_qkvg")
    dhn = _matmul(df, w_f, tb=True, res=dhn, name="fox_in_dx_f")
    out.send("fox_w_in", jnp.concatenate(
        [_matmul(hn, dproj, ta=True, out_dtype=MXU_DT, name="fox_in_dw_qkvg"),
         _matmul(hn, df, ta=True, out_dtype=MXU_DT, name="fox_in_dw_f")[:, :FOX_HEADS]], axis=1))
    *dh_in, dnorm_g = _rms_bwd(h, out.tie(p["norm_g"]), dhn, dh, "mix_norm_bwd")
    fold = lambda g: g.reshape(FOX_HEADS, FOX_HEAD_DIM).sum(axis=0).reshape(1, -1)
    grads = dict(norm_g=dnorm_g, b_f=dbf[:, :FOX_HEADS], q_norm_g=fold(dqg), k_norm_g=fold(dkg))
    return dh_in, grads


def _my_index():
    return 4 * lax.axis_index("x") + 2 * lax.axis_index("y") + lax.axis_index("c")


def _peer(k):
    x, y, c = lax.axis_index("x"), lax.axis_index("y"), lax.axis_index("c")
    flip = lambda v, bit: 1 - v if bit else v
    return (flip(x, k & 4), flip(y, k & 2), flip(c, k & 1))


SEM_SPEC = pl.BlockSpec(memory_space=pltpu.SEMAPHORE)
DATAFLOW = pltpu.SideEffectType.DATAFLOW_SIDE_EFFECTING


def _at(ref, idx):
    return ref.at[tuple(idx)] if idx else ref


def _copies_start(name, srcs, lands, groups):
    ns, nl = len(srcs), len(lands)
    items = [item for group in groups for item in group]

    def body(*refs):
        src_refs, land_refs = refs[:ns], refs[ns:ns + nl]
        sems = refs[ns + nl:ns + nl + 2 * len(items)]
        me = _my_index()
        for t, (i, src_slot, j, dst_slot, _) in enumerate(items):
            for k in range(1, N_DEV):
                pltpu.make_async_remote_copy(
                    src_ref=_at(src_refs[i], src_slot(me, k)), dst_ref=_at(land_refs[j], dst_slot(me, k)),
                    send_sem=sems[2 * t], recv_sem=sems[2 * t + 1], device_id=_peer(k),
                    device_id_type=MESH_IDS).start()
        refs[-1][...] = jnp.zeros(refs[-1].shape, F32)

    hbm = pl.BlockSpec(memory_space=pltpu.HBM)
    bufs = [pltpu.with_memory_space_constraint(a, pltpu.HBM) for a in list(srcs) + list(lands)]
    n_sems = 2 * len(items)
    out = pl.pallas_call(
        body, name=name, in_specs=[hbm] * (ns + nl),
        out_specs=(*[SEM_SPEC] * n_sems, *[hbm] * (ns + nl), pl.BlockSpec(memory_space=pltpu.VMEM)),
        out_shape=(*[pltpu.SemaphoreType.DMA(())] * n_sems, *[pltpu.HBM(a.shape, a.dtype) for a in bufs],
                   _sds((8, LANES))),
        input_output_aliases={i: n_sems + i for i in range(ns + nl)},
        compiler_params=pltpu.CompilerParams(has_side_effects=DATAFLOW),
    )(*bufs)
    sems, t = [], 0
    for group in groups:
        sems.append([(out[2 * (t + u)], out[2 * (t + u) + 1]) for u in range(len(group))])
        t += len(group)
    return sems, list(out[n_sems:n_sems + ns]), list(out[n_sems + ns:n_sems + ns + nl]), out[-1]


def _copies_wait(name, keep, lands, sems, group, after):
    nk, nl, n = len(keep), len(lands), len(group)

    def body(*refs):
        land_refs = refs[nk:nk + nl]
        sem_refs = refs[nk + nl:nk + nl + 2 * n]
        me = _my_index()
        copies = []
        for t, (_, _, j, _, seven) in enumerate(group):
            blocks = _at(land_refs[j], seven(me))
            copies.append(pltpu.make_async_remote_copy(src_ref=blocks, dst_ref=blocks, send_sem=sem_refs[2 * t],
                                                       recv_sem=sem_refs[2 * t + 1], device_id=_peer(1),
                                                       device_id_type=MESH_IDS))
        for cp in copies:
            cp.wait_recv()
        for cp in copies:
            cp.wait_send()

    hbm = pl.BlockSpec(memory_space=pltpu.HBM)
    bufs = list(keep) + list(lands)
    out = pl.pallas_call(
        body, name=name, in_specs=[hbm] * (nk + nl) + [SEM_SPEC] * (2 * n) + [pl.BlockSpec(memory_space=pl.ANY)],
        out_specs=[hbm] * (nk + nl), out_shape=[pltpu.HBM(a.shape, a.dtype) for a in bufs],
        input_output_aliases={i: i for i in range(nk + nl)},
        compiler_params=pltpu.CompilerParams(has_side_effects=DATAFLOW),
    )(*bufs, *[s for pair in sems for s in pair], after)
    return list(out[nk:])


def _allreduce_small(buf, name):
    def body(in_ref, all_ref, sum_ref, send_sems, recv_sems):
        me = _my_index()
        all_ref[me] = in_ref[...]

        def copy(k, slot):
            return pltpu.make_async_remote_copy(
                src_ref=in_ref, dst_ref=all_ref.at[slot], send_sem=send_sems.at[k - 1], recv_sem=recv_sems.at[k - 1],
                device_id=_peer(k), device_id_type=MESH_IDS)

        for k in range(1, N_DEV):
            copy(k, me).start()
        for k in range(1, N_DEV):
            copy(k, jnp.bitwise_xor(me, k)).wait_recv()
        for k in range(1, N_DEV):
            copy(k, me).wait_send()
        acc = all_ref[0]
        for j in range(1, N_DEV):
            acc = acc + all_ref[j]
        sum_ref[...] = acc

    vmem = pl.BlockSpec(memory_space=pltpu.VMEM)
    return pl.pallas_call(
        body, name=name, in_specs=[vmem], out_specs=[vmem, vmem],
        out_shape=[_sds((N_DEV,) + buf.shape), _sds(buf.shape)],
        scratch_shapes=[pltpu.SemaphoreType.DMA((N_DEV - 1,)), pltpu.SemaphoreType.DMA((N_DEV - 1,))],
        compiler_params=pltpu.CompilerParams(vmem_limit_bytes=VMEM_LIMIT_BYTES),
    )(buf)[1]


def _adamw_math(w, g, m, v):
    m = ADAM_B1 * m + (1.0 - ADAM_B1) * g
    v = ADAM_B2 * v + (1.0 - ADAM_B2) * (g * g)
    m_hat = m / (1.0 - ADAM_B1 ** ADAM_STEP)
    v_hat = v / (1.0 - ADAM_B2 ** ADAM_STEP)
    return -ADAM_LR * (m_hat / (jnp.sqrt(v_hat) + ADAM_EPS) + ADAM_WD * w), m, v


def _row_tile(rows, cap=256, mult=16):
    best = None
    for t in range(mult, min(rows, cap) + 1, mult):
        if rows % t == 0:
            best = t
    assert best is not None, rows
    return best


def _adamw_sharded(w, m, v, partials, name):
    layers, rows, cols = w.shape
    tr = _row_tile(rows)

    def body(w_ref, m_ref, v_ref, p_ref, g_ref, d_ref, nm_ref, nv_ref):
        g = p_ref[0].astype(F32)
        for j in range(1, N_DEV):
            g = g + p_ref[j].astype(F32)
        delta, m_new, v_new = _adamw_math(w_ref[...], g, m_ref[...], v_ref[...])
        g_ref[...], d_ref[...], nm_ref[...], nv_ref[...] = g, delta, m_new, v_new

    blk = pl.BlockSpec((None, tr, cols), lambda l, i: (l, i, 0))
    return pl.pallas_call(
        body, name=name, grid=(layers, rows // tr),
        in_specs=[blk, blk, blk, pl.BlockSpec((N_DEV, None, tr, cols), lambda l, i: (0, l, i, 0))],
        out_specs=[blk] * 4, out_shape=[_sds(w.shape)] * 4, compiler_params=_params("parallel", "parallel"),
    )(w, m, v, partials)


def _adamw_small(w, g, m, v, name):
    def body(w_ref, g_ref, m_ref, v_ref, d_ref, nm_ref, nv_ref):
        d_ref[...], nm_ref[...], nv_ref[...] = _adamw_math(w_ref[...], g_ref[...], m_ref[...], v_ref[...])

    return pl.pallas_call(body, name=name, out_shape=[_sds(w.shape)] * 3,
                          compiler_params=_params())(w, g, m, v)


def _pack(arrays):
    flat = jnp.concatenate([a.reshape(-1).astype(F32) for a in arrays])
    pad = -flat.shape[0] % (8 * LANES)
    return jnp.pad(flat, (0, pad)).reshape(-1, LANES)


def _unpack(buf, shapes):
    flat, out, off = buf.reshape(-1), [], 0
    for shp in shapes:
        size = math.prod(shp)
        out.append(flat[off:off + size].reshape(shp))
        off += size
    return out


WEIGHTS = ["mix_norm_g", "ffn_norm_g", "ssd_w_in", "ssd_conv_w", "ssd_conv_b", "ssd_dt_bias", "ssd_a_log", "ssd_d",
           "ssd_norm_g", "ssd_w_out", "fox_w_in", "fox_b_f", "fox_q_norm_g", "fox_k_norm_g", "fox_w_out", "ffn_w_up",
           "ffn_conv_w", "ffn_conv_b", "ffn_w_down", "final_norm_g"]
BIG = ["ssd_w_in", "ssd_w_out", "fox_w_in", "fox_w_out", "ffn_w_up", "ffn_w_down"]
COLUMN_SHARDED = ["ssd_w_in", "fox_w_in", "ffn_w_up"]
CONV = ["ssd_conv_w", "ffn_conv_w"]
REPLICATED = [n for n in WEIGHTS if n not in BIG + CONV]
DEPTH = 4
KEPT_TRANSPOSED = ["ffn_w_up"]
ADAMW_ORDER = ["fox_w_out", "fox_w_in", "ffn_w_down", "ffn_w_up", "ssd_w_out", "ssd_w_in"]
LAYER_SHARDED = (["ssd_w_in", "ssd_conv_w", "ssd_w_out", "ffn_w_up", "ffn_conv_w", "ffn_w_down"],
                 ["fox_w_in", "fox_w_out", "ffn_w_up", "ffn_conv_w", "ffn_w_down"])


def _to_shards(full, on_columns):
    nl, r, c = full.shape
    if on_columns:
        return full.reshape(nl, r, N_DEV, c // N_DEV).transpose(2, 0, 1, 3)
    return full.reshape(nl, N_DEV, r // N_DEV, c).transpose(1, 0, 2, 3)


def _pad_cols(w):
    return jnp.pad(w, ((0, 0), (0, LANES - w.shape[1])))


def kernel(x, mix_norm_g, ffn_norm_g, ssd_w_in, ssd_conv_w, ssd_conv_b, ssd_dt_bias, ssd_a_log, ssd_d, ssd_norm_g, ssd_w_out, fox_w_in, fox_b_f, fox_q_norm_g, fox_k_norm_g, fox_w_out, ffn_w_up, ffn_conv_w, ffn_conv_b, ffn_w_down, final_norm_g, loss_target, m_mix_norm_g, m_ffn_norm_g, m_ssd_w_in, m_ssd_conv_w, m_ssd_conv_b, m_ssd_dt_bias, m_ssd_a_log, m_ssd_d, m_ssd_norm_g, m_ssd_w_out, m_fox_w_in, m_fox_b_f, m_fox_q_norm_g, m_fox_k_norm_g, m_fox_w_out, m_ffn_w_up, m_ffn_conv_w, m_ffn_conv_b, m_ffn_w_down, m_final_norm_g, v_mix_norm_g, v_ffn_norm_g, v_ssd_w_in, v_ssd_conv_w, v_ssd_conv_b, v_ssd_dt_bias, v_ssd_a_log, v_ssd_d, v_ssd_norm_g, v_ssd_w_out, v_fox_w_in, v_fox_b_f, v_fox_q_norm_g, v_fox_k_norm_g, v_fox_w_out, v_ffn_w_up, v_ffn_conv_w, v_ffn_conv_b, v_ffn_w_down, v_final_norm_g):
    given = dict(locals())
    w = {n: given[n] for n in WEIGHTS}
    mom = {n: given["m_" + n] for n in WEIGHTS}
    var = {n: given["v_" + n] for n in WEIGHTS}
    me = _my_index()

    sharded = BIG + CONV
    shards = [w[n].astype(MXU_DT) if n in BIG else w[n] for n in sharded]
    zones = [(n, i if n.startswith("ffn") else i // 2) for i in range(DEPTH) for n in LAYER_SHARDED[i % 2]]
    items = [[(sharded.index(n), functools.partial(lambda me, k, layer: (layer,), layer=layer), z,
               lambda me, k: (me,), lambda me: (pl.ds(0, N_DEV - 1),))] for z, (n, layer) in enumerate(zones)]
    empty = [lax.empty((N_DEV,) + shards[sharded.index(n)].shape[1:], shards[sharded.index(n)].dtype) for n, _ in zones]
    gather_sems, shards_kept, landing, _ = _copies_start("gather_start", shards, empty, items)

    def arrived(n, layer, after):
        z = zones.index((n, layer))
        item = items[z][0]
        stack, = _copies_wait(f"gather_wait_{n}_{layer}", shards_kept if z == len(zones) - 1 else [], [landing[z]],
                              gather_sems[z], [item[:2] + (0,) + item[3:]], after)
        stack = lax.dynamic_update_index_in_dim(stack, shards[sharded.index(n)][layer], me, 0)
        _, r, c = stack.shape
        on_columns = n in COLUMN_SHARDED or n in CONV
        return stack.transpose(1, 0, 2).reshape(r, N_DEV * c) if on_columns else stack.reshape(N_DEV * r, c)

    def mixer_params(i):
        j = i // 2
        if i % 2 == 0:
            return dict(norm_g=w["mix_norm_g"][i], conv_b=w["ssd_conv_b"][j], dt_bias=w["ssd_dt_bias"][j],
                        a_log=w["ssd_a_log"][j], d=w["ssd_d"][j], gnorm_g=w["ssd_norm_g"][j])
        return dict(norm_g=w["mix_norm_g"][i], b_f=w["fox_b_f"][j], q_norm_g=w["fox_q_norm_g"][j],
                    k_norm_g=w["fox_k_norm_g"][j])

    h = x[0]
    tape = []
    for i in range(DEPTH):
        big = _Weights(functools.partial(lambda n, after, i: arrived(n, i if n.startswith("ffn") else i // 2, after), i=i))
        mp, fp = mixer_params(i), dict(norm_g=w["ffn_norm_g"][i], conv_b=w["ffn_conv_b"][i])
        h_mid, mix_saved = (_ssd_fwd if i % 2 == 0 else _fox_fwd)(h, mp, big)
        h_out, ffn_saved = _ffn_fwd(h_mid, fp, big)
        tape.append((h, mp, mix_saved, h_mid, fp, ffn_saved))
        h = h_out
    *dh, dfinal_g, loss_part = _loss_head(h, w["final_norm_g"], loss_target[0], "loss_head")

    grads = {n: [None] * w[n].shape[0] for n in WEIGHTS if n not in BIG + ["final_norm_g"]}
    shard_view = lambda n, a: a.swapaxes(-1, -2) if n in KEPT_TRANSPOSED else a
    partials = {n: lax.empty((N_DEV,) + shard_view(n, w[n]).shape, MXU_DT) for n in BIG}
    in_flight = {n: [] for n in BIG}

    def send_partial(n, layer, grad):
        slots = shard_view(n, _to_shards(grad[None], n in COLUMN_SHARDED)[:, 0])
        mine = lax.dynamic_index_in_dim(slots, me, 0, keepdims=False)
        zone = lax.dynamic_update_slice(partials[n], mine[None, None], (me, layer, 0, 0))
        item = (0, lambda me, k: (jnp.bitwise_xor(me, k),), 0,
                functools.partial(lambda me, k, layer: (me, layer), layer=layer),
                functools.partial(lambda me, layer: (pl.ds(0, N_DEV - 1), layer), layer=layer))
        sems, kept, (partials[n],), token = _copies_start(f"scatter_start_{n}_{layer}", [slots], [zone], [[item]])
        in_flight[n].append((layer, sems[0], kept, item))
        return token

    for i in reversed(range(DEPTH)):
        j = i // 2
        h_in, mp, mix_saved, h_mid, fp, ffn_saved = tape[i]
        out = _Gradients(functools.partial(lambda n, grad, i: send_partial(n, i if n.startswith("ffn") else i // 2, grad),
                                           i=i))
        dh, g = _ffn_bwd(h_mid, dh, ffn_saved, fp, out)
        grads["ffn_norm_g"][i], grads["ffn_conv_w"][i], grads["ffn_conv_b"][i] = g["norm_g"][0], g["conv_w"], g["conv_b"][0]
        if i % 2 == 0:
            dh, g = _ssd_bwd(h_in, dh, mix_saved, mp, out)
            grads["ssd_conv_w"][j] = g["conv_w"]
            for key, name in (("conv_b", "ssd_conv_b"), ("dt_bias", "ssd_dt_bias"), ("a_log", "ssd_a_log"),
                              ("d", "ssd_d"), ("gnorm_g", "ssd_norm_g")):
                grads[name][j] = g[key][0]
        else:
            dh, g = _fox_bwd(h_in, dh, mix_saved, mp, out)
            for key, name in (("b_f", "fox_b_f"), ("q_norm_g", "fox_q_norm_g"), ("k_norm_g", "fox_k_norm_g")):
                grads[name][j] = g[key][0]
        grads["mix_norm_g"][i] = g["norm_g"][0]
    grads = {n: jnp.stack(v) for n, v in grads.items()}
    grads["final_norm_g"] = dfinal_g[0]

    small_names = REPLICATED + CONV
    summed = _unpack(_allreduce_small(_pack([grads[n] for n in small_names] + [loss_part]), "allreduce_small"),
                     [grads[n].shape for n in small_names] + [(1, 1)])
    loss = summed[-1][0, 0]
    g_small = dict(zip(small_names, summed[:-1]))
    for n in CONV:
        width = w[n].shape[-1]
        g_small[n] = lax.dynamic_slice_in_dim(g_small[n], me * width, width, axis=2)
    pk = lambda d: _pack([d[n] for n in small_names])
    d_small, m_small, v_small = _adamw_small(pk(w), pk(g_small), pk(mom), pk(var), "adamw_small")
    shapes = [w[n].shape for n in small_names]
    out_g, out_d, out_m, out_v = dict(g_small), {}, {}, {}
    for dst, buf in ((out_d, d_small), (out_m, m_small), (out_v, v_small)):
        dst.update(zip(small_names, _unpack(buf, shapes)))

    after = d_small
    for n in ADAMW_ORDER:
        for layer, sems, kept, item in in_flight[n]:
            partials[n], = _copies_wait(f"scatter_wait_{n}_{layer}", kept, [partials[n]], sems, [item], after)
        res = _adamw_sharded(shard_view(n, w[n]), shard_view(n, mom[n]), shard_view(n, var[n]), partials[n], "adamw_" + n)
        out_g[n], out_d[n], out_m[n], out_v[n] = [shard_view(n, r) for r in res]
        after = out_v[n]

    return (loss, dh[0][None], *[out_g[n] for n in WEIGHTS], *[out_d[n] for n in WEIGHTS],
            *[out_m[n] for n in WEIGHTS], *[out_v[n] for n in WEIGHTS])
```
